```python
import math
import jax, jax.numpy as jnp
from jax import lax
import numpy as np

D_MODEL = 1024
BATCH = 16
SEQ = 256
DEPTH = 4
DEC_BATCH = 4
DEC_SEQ = 1024
PAST_LEN = 256

GRID_W = 64
N_MIXERS = 2
N_HYENA = (DEPTH + 1) // 2
N_ATTN = DEPTH // 2
EPS = 1e-6
HY_ORDER = 2
HY_BANDS = 16
HY_EMB = 1 + 2 * HY_BANDS
HY_FILTER_HIDDEN = 64
HY_FAST_DECAY = 0.3
HY_SLOW_DECAY = 1.5
HY_TARGET = 1e-2
N_HEADS = 8
HEAD_DIM = 64
ROT_AXIS = HEAD_DIM // 2
ROPE_BASE = 10000.0
Q_BLOCK = 128
N_GROUPS = 4
EXPERTS_PER_GROUP = 8
N_EXPERTS = N_GROUPS * EXPERTS_PER_GROUP
TOP_K = 2
D_EXPERT = D_MODEL // 4

kernel_name = "hyena_diffattn_hmoe_diffusion_step"

F32 = jnp.float32


def rms_norm(x, g):
    xf = x.astype(F32)
    y = xf * lax.rsqrt(jnp.mean(xf * xf, axis=-1, keepdims=True) + EPS)
    return (y * g.astype(F32)).astype(x.dtype)


def adaln(cond, w, b):
    m = jax.nn.silu(cond) @ w + b
    return jnp.split(m[:, None, :], 6, axis=-1)


def modulate(x, g, shift, scale):
    return rms_norm(x, g) * (1.0 + scale) + shift


def short_conv(u, w, b):
    up = jnp.pad(u, ((0, 0), (1, 1), (0, 0)))
    return up[:, :-2] * w[0] + up[:, 1:-1] * w[1] + up[:, 2:] * w[2] + b


def hyena_filters(L, w1, b1, w2, b2, w3, b3, freq):
    pos = jnp.arange(L, dtype=F32)
    t = pos / (L - 1)
    bands = jnp.linspace(1e-4, HY_BANDS - 1, HY_BANDS, dtype=F32)
    ang = (2.0 * math.pi / L) * pos[:, None] * bands[None, :]
    feats = jnp.concatenate([t[:, None], jnp.cos(ang), -jnp.sin(ang)], axis=-1)
    h = jnp.sin(freq[0].astype(F32) * (feats @ w1.astype(F32) + b1.astype(F32)))
    h = jnp.sin(freq[1].astype(F32) * (h @ w2.astype(F32) + b2.astype(F32)))
    h = h @ w3.astype(F32) + b3.astype(F32)
    deltas = jnp.abs(jnp.linspace(math.log(HY_TARGET) / HY_SLOW_DECAY,
                                  math.log(HY_TARGET) / HY_FAST_DECAY, D_MODEL, dtype=F32))
    window = jnp.exp(-t[:, None] * deltas[None, :])
    return h.reshape(L, HY_ORDER, 2, D_MODEL) * window[:, None, None, :]


def long_conv(u, h_fwd, h_bwd, bias):
    L = u.shape[1]
    kern = jnp.concatenate([h_fwd, jnp.zeros((1, D_MODEL), F32), h_bwd[1:][::-1]], axis=0)
    uf = u.astype(F32)
    spec = jnp.fft.rfft(uf, n=2 * L, axis=1) * jnp.fft.rfft(kern, n=2 * L, axis=0)[None]
    y = jnp.fft.irfft(spec, n=2 * L, axis=1)[:, :L]
    return (y + uf * bias.astype(F32)).astype(u.dtype)


def hyena_mixer(h, w_in, conv_w, conv_b, f_w1, f_b1, f_w2, f_b2, f_w3, f_b3, f_freq, bias, w_out):
    L = h.shape[1]
    z = short_conv(h @ w_in, conv_w, conv_b)
    v, x1, x2 = jnp.split(z, 3, axis=-1)
    filt = hyena_filters(L, f_w1, f_b1, f_w2, f_b2, f_w3, f_b3, f_freq)
    y = x1 * long_conv(v, filt[:, 0, 0], filt[:, 0, 1], bias[0])
    y = x2 * long_conv(y, filt[:, 1, 0], filt[:, 1, 1], bias[1])
    return y @ w_out


def rope_axis(x, ang):
    cos = jnp.cos(ang)[None, :, None, None, :]
    sin = jnp.sin(ang)[None, :, None, None, :]
    half = x.shape[-1] // 2
    x1, x2 = x[..., :half], x[..., half:]
    return jnp.concatenate([x1 * cos - x2 * sin, x1 * sin + x2 * cos], axis=-1)


def axial_rope(x):
    L = x.shape[1]
    rows = L // GRID_W
    r = jnp.repeat(jnp.arange(rows, dtype=F32), GRID_W)
    col = jnp.tile(jnp.arange(GRID_W, dtype=F32), rows)
    inv = ROPE_BASE ** (-jnp.arange(ROT_AXIS // 2, dtype=F32) / (ROT_AXIS // 2))
    xf = x.astype(F32)
    out = jnp.concatenate([rope_axis(xf[..., :ROT_AXIS], r[:, None] * inv),
                           rope_axis(xf[..., ROT_AXIS:], col[:, None] * inv)], axis=-1)
    return out.astype(x.dtype)


def diff_attention(q, k, v, lam):
    B, Lq = q.shape[0], q.shape[1]
    nb = Lq // Q_BLOCK
    qb = q.reshape(B, nb, Q_BLOCK, N_HEADS, 2, HEAD_DIM).transpose(1, 0, 2, 3, 4, 5)
    scale = HEAD_DIM ** -0.5

    def block(qi):
        s = jnp.einsum('bqhcd,bkhcd->bhcqk', qi, k).astype(F32) * scale
        p = jax.nn.softmax(s, axis=-1)
        a = p[:, :, 0] - lam.astype(F32) * p[:, :, 1]
        return jnp.einsum('bhqk,bkhe->bqhe', a.astype(v.dtype), v)

    o = lax.map(block, qb)
    return o.transpose(1, 0, 2, 3, 4).reshape(B, Lq, N_HEADS, 2 * HEAD_DIM)


def diff_attn_mixer(h, w_qkv, lam, lam_init, subln_g, w_o, ctx_k=None, ctx_v=None):
    B, L, _ = h.shape
    q, k, v = jnp.split(h @ w_qkv, 3, axis=-1)
    q = q.reshape(B, L, N_HEADS, 2, HEAD_DIM)
    k = k.reshape(B, L, N_HEADS, 2, HEAD_DIM)
    v = v.reshape(B, L, N_HEADS, 2 * HEAD_DIM)
    if ctx_k is None:
        o = diff_attention(q, k, k if False else k, lam) if False else diff_attention(q, k, v, lam)
        k_out = k.reshape(B, L, N_HEADS, 2 * HEAD_DIM)
        v_out = v
    else:
        Lc = ctx_k.shape[1]
        q = axial_rope(q)
        k = axial_rope(k)
        keys = jnp.concatenate([ctx_k.reshape(B, Lc, N_HEADS, 2, HEAD_DIM).astype(k.dtype), k], axis=1)
        vals = jnp.concatenate([ctx_v.astype(v.dtype), v], axis=1)
        o = diff_attention(q, keys, vals, lam)
        k_out, v_out = ctx_k, ctx_v
    o = rms_norm(o, subln_g) * (1.0 - lam_init)
    return o.reshape(B, L, D_MODEL) @ w_o, k_out, v_out


def hier_moe(h, w_group, b_group, w_expert, b_expert, w1, w3, w2):
    B, L, D = h.shape
    t = h.reshape(B * L, D)
    T = t.shape[0]
    pg = jax.nn.softmax((t @ w_group + b_group).astype(F32), axis=-1)
    g_idx = jnp.argmax(pg, axis=-1)
    g_gate = jnp.max(pg, axis=-1)
    le = (t @ w_expert + b_expert).astype(F32).reshape(T, N_GROUPS, EXPERTS_PER_GROUP)
    le_sel = jnp.take_along_axis(le, g_idx[:, None, None], axis=1)[:, 0]
    pe = jax.nn.softmax(le_sel, axis=-1)
    top_p, top_i = lax.top_k(pe, TOP_K)
    top_p = top_p / jnp.sum(top_p, axis=-1, keepdims=True)
    w_fine = jnp.sum(jax.nn.one_hot(top_i, EXPERTS_PER_GROUP, dtype=F32) * top_p[..., None], axis=1)
    gates = (jax.nn.one_hot(g_idx, N_GROUPS, dtype=F32)[:, :, None]
             * (g_gate[:, None] * w_fine)[:, None, :]).reshape(T, N_EXPERTS)
    hid = jax.nn.silu(jnp.einsum('td,edf->tef', t, w1)) * jnp.einsum('td,edf->tef', t, w3)
    y = jnp.einsum('tef,efd->td', hid * gates.astype(hid.dtype)[..., None], w2)
    return y.reshape(B, L, D)


def setup_inputs(seed: int = 0) -> dict:
    key = jax.random.key(seed)
    ks = jax.random.split(key, 40)
    n = lambda i, shape, s: jax.random.normal(ks[i], shape, F32) * s
    D = D_MODEL
    FH = HY_FILTER_HIDDEN
    return {
        "x_prompt": n(0, (BATCH, SEQ, D), 1.0),
        "x_sample": n(1, (DEC_BATCH, DEC_SEQ, D), 1.0),
        "cache_k": n(2, (DEC_BATCH, N_ATTN, PAST_LEN, N_HEADS, 2 * HEAD_DIM), 1.0),
        "cache_v": n(3, (DEC_BATCH, N_ATTN, PAST_LEN, N_HEADS, 2 * HEAD_DIM), 1.0),
        "c": n(4, (DEC_BATCH, D), 1.0),
        "c_ctx": n(5, (D,), 1.0),
        "ada_w": n(6, (DEPTH, D, 6 * D), 0.5 * D ** -0.5),
        "ada_b": n(7, (DEPTH, 6 * D), 0.02),
        "norm1_g": 1.0 + n(8, (DEPTH, D), 0.02),
        "norm2_g": 1.0 + n(9, (DEPTH, D), 0.02),
        "final_g": 1.0 + n(10, (D,), 0.02),
        "hy_w_in": n(11, (N_HYENA, D, 3 * D), D ** -0.5),
        "hy_conv_w": n(12, (N_HYENA, 3, 3 * D), 0.5),
        "hy_conv_b": n(13, (N_HYENA, 3 * D), 0.02),
        "hy_f_w1": n(14, (N_HYENA, HY_EMB, FH), HY_EMB ** -0.5),
        "hy_f_b1": n(15, (N_HYENA, FH), 0.1),
        "hy_f_w2": n(16, (N_HYENA, FH, FH), FH ** -0.5),
        "hy_f_b2": n(17, (N_HYENA, FH), 0.1),
        "hy_f_w3": n(18, (N_HYENA, FH, HY_ORDER * 2 * D), 0.03 * FH ** -0.5),
        "hy_f_b3": n(19, (N_HYENA, HY_ORDER * 2 * D), 0.005),
        "hy_f_freq": 1.0 + n(20, (N_HYENA, 2, FH), 0.1),
        "hy_bias": n(21, (N_HYENA, HY_ORDER, D), 0.1),
        "hy_w_out": n(22, (N_HYENA, D, D), D ** -0.5),
        "at_w_qkv": n(23, (N_ATTN, D, 3 * D), D ** -0.5),
        "at_lam_q1": n(24, (N_ATTN, HEAD_DIM), 0.1),
        "at_lam_k1": n(25, (N_ATTN, HEAD_DIM), 0.1),
        "at_lam_q2": n(26, (N_ATTN, HEAD_DIM), 0.1),
        "at_lam_k2": n(27, (N_ATTN, HEAD_DIM), 0.1),
        "at_subln_g": 1.0 + n(28, (N_ATTN, 2 * HEAD_DIM), 0.02),
        "at_w_o": n(29, (N_ATTN, D, D), D ** -0.5),
        "moe_w_group": n(30, (DEPTH, D, N_GROUPS), D ** -0.5),
        "moe_b_group": n(31, (DEPTH, N_GROUPS), 0.01),
        "moe_w_expert": n(32, (DEPTH, D, N_EXPERTS), D ** -0.5),
        "moe_b_expert": n(33, (DEPTH, N_EXPERTS), 0.01),
        "moe_w1": n(34, (DEPTH, N_EXPERTS, D, D_EXPERT), D ** -0.5),
        "moe_w3": n(35, (DEPTH, N_EXPERTS, D, D_EXPERT), D ** -0.5),
        "moe_w2": n(36, (DEPTH, N_EXPERTS, D_EXPERT, D), D_EXPERT ** -0.5),
    }


def reference(x_prompt, x_sample, cache_k, cache_v, c, c_ctx, ada_w, ada_b, norm1_g, norm2_g, final_g,
              hy_w_in, hy_conv_w, hy_conv_b, hy_f_w1, hy_f_b1, hy_f_w2, hy_f_b2, hy_f_w3, hy_f_b3,
              hy_f_freq, hy_bias, hy_w_out, at_w_qkv, at_lam_q1, at_lam_k1, at_lam_q2, at_lam_k2,
              at_subln_g, at_w_o, moe_w_group, moe_b_group, moe_w_expert, moe_b_expert,
              moe_w1, moe_w3, moe_w2):
    xp, xs = x_prompt, x_sample
    new_k, new_v = [], []
    for l in range(DEPTH):
        j = l // N_MIXERS
        mp = adaln(c_ctx[None, :], ada_w[l], ada_b[l])
        ms = adaln(c, ada_w[l], ada_b[l])
        hp = modulate(xp, norm1_g[l], mp[0], mp[1])
        hs = modulate(xs, norm1_g[l], ms[0], ms[1])
        if l % N_MIXERS == 0:
            hy = (hy_w_in[j], hy_conv_w[j], hy_conv_b[j], hy_f_w1[j], hy_f_b1[j], hy_f_w2[j],
                  hy_f_b2[j], hy_f_w3[j], hy_f_b3[j], hy_f_freq[j], hy_bias[j], hy_w_out[j])
            yp = hyena_mixer(hp, *hy)
            ys = hyena_mixer(hs, *hy)
        else:
            lam_init = 0.8 - 0.6 * math.exp(-0.3 * l)
            lam = (jnp.exp(jnp.sum(at_lam_q1[j] * at_lam_k1[j]))
                   - jnp.exp(jnp.sum(at_lam_q2[j] * at_lam_k2[j])) + lam_init)
            yp, kp, vp = diff_attn_mixer(hp, at_w_qkv[j], lam, lam_init, at_subln_g[j], at_w_o[j])
            ys, _, _ = diff_attn_mixer(hs, at_w_qkv[j], lam, lam_init, at_subln_g[j], at_w_o[j],
                                       ctx_k=cache_k[:, j], ctx_v=cache_v[:, j])
            new_k.append(kp)
            new_v.append(vp)
        xp = xp + mp[2] * yp
        xs = xs + ms[2] * ys
        moe = (moe_w_group[l], moe_b_group[l], moe_w_expert[l], moe_b_expert[l],
               moe_w1[l], moe_w3[l], moe_w2[l])
        xp = xp + mp[5] * hier_moe(modulate(xp, norm2_g[l], mp[3], mp[4]), *moe)
        xs = xs + ms[5] * hier_moe(modulate(xs, norm2_g[l], ms[3], ms[4]), *moe)
    y_prompt = rms_norm(xp, final_g)
    y_sample = rms_norm(xs, final_g)
    new_cache_k = jnp.stack(new_k, axis=1)
    new_cache_v = jnp.stack(new_v, axis=1)
    return (y_prompt, y_sample, new_cache_k, new_cache_v)
```

```python
import functools
import math

import numpy as np
import jax
import jax.numpy as jnp
from jax import lax
from jax.experimental import pallas as pl
from jax.experimental.pallas import tpu as pltpu

F32 = jnp.float32
BF16 = jnp.bfloat16

D_MODEL = 1024
BATCH = 16
SEQ = 256
DEPTH = 4
DEC_BATCH = 4
DEC_SEQ = 1024
PAST_LEN = 256
GRID_W = 64
EPS = 1e-6
HY_BANDS = 16
HY_EMB = 1 + 2 * HY_BANDS
HY_FILTER_HIDDEN = 64
HY_FAST_DECAY = 0.3
HY_SLOW_DECAY = 1.5
HY_TARGET = 1e-2
N_HEADS = 8
HEAD_DIM = 64
ROT_AXIS = HEAD_DIM // 2
ROPE_BASE = 10000.0
N_GROUPS = 4
EXPERTS_PER_GROUP = 8
N_EXPERTS = N_GROUPS * EXPERTS_PER_GROUP
D_EXPERT = D_MODEL // 4

N_PROMPT = BATCH * SEQ
N_SAMPLE = DEC_BATCH * DEC_SEQ
N_TOK = N_PROMPT + N_SAMPLE
MOD_ROWS = 8
LANES = 128
HEAD_W = 2 * HEAD_DIM
EXPERT_TILE = 256
MAX_PAIRS = 2 * N_TOK + N_EXPERTS * EXPERT_TILE
VMEM_LIMIT = 56 * 1024 * 1024


def _cparams(sem):
    return pltpu.CompilerParams(dimension_semantics=sem, vmem_limit_bytes=VMEM_LIMIT)


def _mod_row(blk, tm):
    start = blk * tm
    return jnp.where(start < N_PROMPT, 0, 1 + (start - N_PROMPT) // DEC_SEQ)


def _mod_index(layer, which, tm):
    def index_map(i, *_):
        return ((layer * MOD_ROWS + _mod_row(i, tm)) * 6 + which, 0, 0)
    return index_map


@functools.lru_cache(maxsize=None)
def _dft_mats(L):
    k = np.arange(L, dtype=np.float64)[:, None]
    j = np.arange(L, dtype=np.float64)[None, :]
    ang = np.pi * ((k * j) % (2 * L)) / L
    c = np.cos(ang)
    s = np.sin(ang)
    s[0, :] = np.where(np.arange(L) % 2 == 0, 1.0, -1.0)
    fwd = np.concatenate([c, s], axis=0)
    scale = np.full((2 * L,), 1.0 / L)
    scale[0] = scale[L] = 0.5 / L
    inv = fwd.T * scale[None, :]
    return fwd.astype(np.float32), inv.astype(np.float32)


@functools.lru_cache(maxsize=None)
def _filter_feats(L):
    pos = np.arange(L, dtype=np.float64)
    t = pos / (L - 1)
    bands = np.linspace(1e-4, HY_BANDS - 1, HY_BANDS)
    ang = (2.0 * math.pi / L) * pos[:, None] * bands[None, :]
    feats = np.concatenate([t[:, None], np.cos(ang), -np.sin(ang)], axis=-1)
    feats = np.pad(feats, ((0, 0), (0, LANES - HY_EMB)))
    deltas = np.abs(np.linspace(math.log(HY_TARGET) / HY_SLOW_DECAY,
                                math.log(HY_TARGET) / HY_FAST_DECAY, D_MODEL))
    window = np.exp(-t[:, None] * deltas[None, :])
    alt = np.where(np.arange(L) % 2 == 0, 1.0, -1.0)[:, None] * np.ones((1, LANES))
    return feats.astype(np.float32), window.astype(np.float32), alt.astype(np.float32)


@functools.lru_cache(maxsize=None)
def _rope_tables():
    pos = np.arange(DEC_SEQ)
    row = (pos // GRID_W).astype(np.float64)
    col = (pos % GRID_W).astype(np.float64)
    lane = np.arange(HEAD_W)
    d = lane % HEAD_DIM
    axis = d // ROT_AXIS
    n = d % ROT_AXIS
    half = n // (ROT_AXIS // 2)
    f = n % (ROT_AXIS // 2)
    inv = ROPE_BASE ** (-f.astype(np.float64) / (ROT_AXIS // 2))
    p = np.where(axis[None, :] == 0, row[:, None], col[:, None])
    ang = p * inv[None, :]
    sign = np.where(half == 0, -1.0, 1.0)[None, :]
    return np.cos(ang).astype(np.float32), (np.sin(ang) * sign).astype(np.float32)


def _split_bf16(a):
    hi = a.astype(BF16)
    lo = (a - hi.astype(F32)).astype(BF16)
    return hi, lo


def _dot(a, b):
    return jnp.dot(a, b, preferred_element_type=F32)


def _dot3(a_hi, a_lo, b_hi, b_lo):
    return _dot(a_hi, b_hi) + (_dot(a_hi, b_lo) + _dot(a_lo, b_hi))


def _ada_kernel(c_ref, w_ref, b_ref, o_ref):
    c = c_ref[...]
    s = c * jax.nn.sigmoid(c)
    o_ref[0] = _dot(s.astype(BF16), w_ref[0].astype(BF16)) + b_ref[0]


def _ada_table(cond, ada_w, ada_b):
    tn = 1536
    n = 6 * D_MODEL
    return pl.pallas_call(
        _ada_kernel,
        grid=(DEPTH, n // tn),
        in_specs=[
            pl.BlockSpec((MOD_ROWS, D_MODEL), lambda l, j: (0, 0)),
            pl.BlockSpec((1, D_MODEL, tn), lambda l, j: (l, 0, j)),
            pl.BlockSpec((1, 1, tn), lambda l, j: (l, 0, j)),
        ],
        out_specs=pl.BlockSpec((1, MOD_ROWS, tn), lambda l, j: (l, 0, j)),
        out_shape=jax.ShapeDtypeStruct((DEPTH, MOD_ROWS, n), F32),
        compiler_params=_cparams(("arbitrary", "arbitrary")),
        name="ada_table",
    )(cond, ada_w, ada_b.reshape(DEPTH, 1, n))


def _modulate(x, g, scale, shift):
    ms = jnp.mean(x * x, axis=-1, keepdims=True)
    y = (x * lax.rsqrt(ms + EPS)) * g
    return y * (1.0 + scale) + shift


def _normmm_kernel(x_ref, g_ref, sh_ref, sc_ref, w_ref, cos_ref, sin_ref, o_ref, h_scr, *,
                   rope, tm, tn):
    i = pl.program_id(0)
    j = pl.program_id(1)

    @pl.when(j == 0)
    def _():
        h_scr[...] = _modulate(x_ref[...], g_ref[...], sc_ref[0], sh_ref[0]).astype(BF16)

    acc = _dot(h_scr[...], w_ref[...].astype(BF16))
    if not rope:
        o_ref[...] = acc.astype(o_ref.dtype)
        return

    is_rope = jnp.logical_and(i >= N_PROMPT // tm, j < 2 * D_MODEL // tn)

    @pl.when(is_rope)
    def _():
        reps = tn // HEAD_W
        cos = jnp.tile(cos_ref[...], (1, reps))
        sin = jnp.tile(sin_ref[...], (1, reps))
        lane = lax.broadcasted_iota(jnp.int32, (tm, tn), 1)
        first_half = (lane % ROT_AXIS) < (ROT_AXIS // 2)
        partner = jnp.where(first_half,
                            pltpu.roll(acc, tn - ROT_AXIS // 2, 1),
                            pltpu.roll(acc, ROT_AXIS // 2, 1))
        o_ref[...] = (acc * cos + partner * sin).astype(o_ref.dtype)

    @pl.when(jnp.logical_not(is_rope))
    def _():
        o_ref[...] = acc.astype(o_ref.dtype)


def _norm_matmul(x, g, mods, layer, which_shift, which_scale, w, *, rope=False, out_dtype=F32,
                 tm=1024, tn=512):
    n = w.shape[1]
    cos, sin = _rope_tables()
    rope_blocks = DEC_SEQ // tm

    def rope_idx(i, j):
        return (jnp.maximum(i - N_PROMPT // tm, 0) % rope_blocks, 0)

    return pl.pallas_call(
        functools.partial(_normmm_kernel, rope=rope, tm=tm, tn=tn),
        grid=(N_TOK // tm, n // tn),
        in_specs=[
            pl.BlockSpec((tm, D_MODEL), lambda i, j: (i, 0)),
            pl.BlockSpec((1, D_MODEL), lambda i, j: (0, 0)),
            pl.BlockSpec((1, 1, D_MODEL), _mod_index(layer, which_shift, tm)),
            pl.BlockSpec((1, 1, D_MODEL), _mod_index(layer, which_scale, tm)),
            pl.BlockSpec((D_MODEL, tn), lambda i, j: (0, j)),
            pl.BlockSpec((tm, HEAD_W), rope_idx),
            pl.BlockSpec((tm, HEAD_W), rope_idx),
        ],
        out_specs=pl.BlockSpec((tm, tn), lambda i, j: (i, j)),
        out_shape=jax.ShapeDtypeStruct((N_TOK, n), out_dtype),
        scratch_shapes=[pltpu.VMEM((tm, D_MODEL), BF16)],
        compiler_params=_cparams(("arbitrary", "arbitrary")),
        name="norm_matmul_rope" if rope else "norm_matmul",
    )(x, g.reshape(1, D_MODEL), mods, mods, w, jnp.asarray(cos), jnp.asarray(sin))


def _mmres_kernel(a_ref, w_ref, x_ref, gate_ref, o_ref):
    acc = _dot(a_ref[...], w_ref[...].astype(BF16))
    o_ref[...] = x_ref[...] + gate_ref[0] * acc


def _matmul_residual(a, w, x, mods, layer, which_gate, *, tm=1024, tn=512):
    k = a.shape[1]
    n = w.shape[1]

    def gate_idx(i, j):
        return ((layer * MOD_ROWS + _mod_row(i, tm)) * 6 + which_gate, 0, j)

    return pl.pallas_call(
        _mmres_kernel,
        grid=(N_TOK // tm, n // tn),
        in_specs=[
            pl.BlockSpec((tm, k), lambda i, j: (i, 0)),
            pl.BlockSpec((k, tn), lambda i, j: (0, j)),
            pl.BlockSpec((tm, tn), lambda i, j: (i, j)),
            pl.BlockSpec((1, 1, tn), gate_idx),
        ],
        out_specs=pl.BlockSpec((tm, tn), lambda i, j: (i, j)),
        out_shape=jax.ShapeDtypeStruct((N_TOK, n), F32),
        compiler_params=_cparams(("arbitrary", "arbitrary")),
        name="matmul_residual",
    )(a, w, x, mods)


def _filter_kernel(feats_ref, win_ref, alt_ref, w1_ref, b1_ref, w2_ref, b2_ref, fq_ref,
                   w3f_ref, b3f_ref, w3b_ref, b3b_ref, fhi_ref, flo_ref, o_ref, *, L):
    def dense(a, w_ref, b_ref):
        a_hi, a_lo = _split_bf16(a)
        w_hi, w_lo = _split_bf16(w_ref[...])
        return _dot3(a_hi, a_lo, w_hi, w_lo) + b_ref[...]

    fq = fq_ref[...]
    h = jnp.sin(fq[0:1, :] * dense(feats_ref[...], w1_ref, b1_ref))
    h = jnp.sin(fq[1:2, :] * dense(h, w2_ref, b2_ref))
    win = win_ref[...]
    hf = dense(h, w3f_ref, b3f_ref) * win
    hb = dense(h, w3b_ref, b3b_ref) * win
    row = lax.broadcasted_iota(jnp.int32, hf.shape, 0)
    hb = jnp.where(row == 0, 0.0, hb)
    s_hi, s_lo = _split_bf16(hf + hb)
    d_hi, d_lo = _split_bf16(hf - hb)
    kr = _dot3(fhi_ref[0:L, :], flo_ref[0:L, :], s_hi, s_lo)
    ks = _dot3(fhi_ref[L:2 * L, :], flo_ref[L:2 * L, :], d_hi, d_lo)
    k_nyq = jnp.sum((hf + hb) * alt_ref[:, 0:1], axis=0, keepdims=True)
    o_ref[0, 0] = kr
    o_ref[0, 1] = jnp.where(row == 0, k_nyq, kr)
    o_ref[0, 2] = jnp.where(row == 0, 0.0, ks)


def _filter_spectra(L, w1, b1, w2, b2, w3, b3, freq, f_hi, f_lo, *, dt=256):
    feats, window, alt = _filter_feats(L)
    fh = HY_FILTER_HIDDEN
    nd = D_MODEL // dt
    w1p = jnp.pad(w1, ((0, LANES - HY_EMB), (0, 0)))
    const = lambda shape: pl.BlockSpec(shape, lambda o, c: tuple(0 for _ in shape))
    return pl.pallas_call(
        functools.partial(_filter_kernel, L=L),
        grid=(2, nd),
        in_specs=[
            const((L, LANES)),
            pl.BlockSpec((L, dt), lambda o, c: (0, c)),
            const((L, LANES)),
            const((LANES, fh)), const((1, fh)), const((fh, fh)), const((1, fh)), const((2, fh)),
            pl.BlockSpec((fh, dt), lambda o, c: (0, (2 * o) * nd + c)),
            pl.BlockSpec((1, dt), lambda o, c: (0, (2 * o) * nd + c)),
            pl.BlockSpec((fh, dt), lambda o, c: (0, (2 * o + 1) * nd + c)),
            pl.BlockSpec((1, dt), lambda o, c: (0, (2 * o + 1) * nd + c)),
            const((2 * L, L)), const((2 * L, L)),
        ],
        out_specs=pl.BlockSpec((1, 3, L, dt), lambda o, c: (o, 0, 0, c)),
        out_shape=jax.ShapeDtypeStruct((2, 3, L, D_MODEL), F32),
        compiler_params=_cparams(("arbitrary", "arbitrary")),
        name=f"hyena_filter_spectra_{L}",
    )(jnp.asarray(feats), jnp.asarray(window), jnp.asarray(alt), w1p, b1.reshape(1, fh), w2,
      b2.reshape(1, fh), freq, w3, b3.reshape(1, -1), w3, b3.reshape(1, -1), f_hi, f_lo)


def _conv_kernel(v_ref, x1_ref, x2_ref, cwv_ref, cw1_ref, cw2_ref, cbv_ref, cb1_ref, cb2_ref,
                 f_ref, finv_ref, kc_ref, bias_ref, o_ref, *, L):
    row = lax.broadcasted_iota(jnp.int32, v_ref.shape, 0)

    def short_conv(u_ref, w_ref, b_ref):
        u = u_ref[...]
        prev = jnp.where(row == 0, 0.0, pltpu.roll(u, 1, 0))
        nxt = jnp.where(row == L - 1, 0.0, pltpu.roll(u, L - 1, 0))
        w = w_ref[...]
        return prev * w[0:1, :] + u * w[1:2, :] + nxt * w[2:3, :] + b_ref[...]

    def long_conv(u, order):
        spec = _dot(f_ref[...], u.astype(BF16))
        a = spec[0:L, :]
        b = spec[L:2 * L, :]
        kra = kc_ref[order, 0]
        krb = kc_ref[order, 1]
        ks = kc_ref[order, 2]
        prod = jnp.concatenate([a * kra - b * ks, a * ks + b * krb], axis=0)
        return _dot(finv_ref[...], prod.astype(BF16)) + u * bias_ref[order:order + 1, :]

    v = short_conv(v_ref, cwv_ref, cbv_ref)
    y = short_conv(x1_ref, cw1_ref, cb1_ref) * long_conv(v, 0)
    y = short_conv(x2_ref, cw2_ref, cb2_ref) * long_conv(y, 1)
    o_ref[...] = y.astype(o_ref.dtype)


def _hyena_conv(z, conv_w, conv_b, f_bf, finv_bf, kc, bias, *, L, row_block0, n_seq, dt):
    nd = D_MODEL // dt
    cb = conv_b.reshape(1, 3 * D_MODEL)
    seg = lambda s: pl.BlockSpec((L, dt), lambda c, b: (row_block0 + b, s * nd + c))
    cw = lambda s: pl.BlockSpec((3, dt), lambda c, b: (0, s * nd + c))
    cbs = lambda s: pl.BlockSpec((1, dt), lambda c, b: (0, s * nd + c))
    return pl.pallas_call(
        functools.partial(_conv_kernel, L=L),
        grid=(nd, n_seq),
        in_specs=[
            seg(0), seg(1), seg(2), cw(0), cw(1), cw(2), cbs(0), cbs(1), cbs(2),
            pl.BlockSpec((2 * L, L), lambda c, b: (0, 0)),
            pl.BlockSpec((L, 2 * L), lambda c, b: (0, 0)),
            pl.BlockSpec((2, 3, L, dt), lambda c, b: (0, 0, 0, c)),
            pl.BlockSpec((2, dt), lambda c, b: (0, c)),
        ],
        out_specs=pl.BlockSpec((L, dt), lambda c, b: (b, c)),
        out_shape=jax.ShapeDtypeStruct((n_seq * L, D_MODEL), BF16),
        compiler_params=_cparams(("arbitrary", "arbitrary")),
        name=f"hyena_conv_{L}",
    )(z, z, z, conv_w, conv_w, conv_w, cb, cb, cb, f_bf, finv_bf, kc, bias)


def _attn_kernel(lq1_ref, lk1_ref, lq2_ref, lk2_ref, g_ref, q_ref, k_ref, v_ref, *rest,
                 lam_init, has_ctx, tq):
    if has_ctx:
        ck_ref, cv_ref, o_ref = rest
    else:
        (o_ref,) = rest
    lam = (jnp.exp(jnp.sum(lq1_ref[...] * lk1_ref[...], axis=-1, keepdims=True))
           - jnp.exp(jnp.sum(lq2_ref[...] * lk2_ref[...], axis=-1, keepdims=True)) + lam_init)
    scale = HEAD_DIM ** -0.5
    lane = lax.broadcasted_iota(jnp.int32, (tq, HEAD_W), 1)
    nt = (((1,), (1,)), ((), ()))
    for h in range(N_HEADS):
        cols = slice(h * HEAD_W, (h + 1) * HEAD_W)
        q = q_ref[:, cols]
        q2 = jnp.concatenate([jnp.where(lane < HEAD_DIM, q, 0.0),
                              jnp.where(lane >= HEAD_DIM, q, 0.0)], axis=0).astype(BF16)
        s = lax.dot_general(q2, k_ref[:, cols].astype(BF16), nt, preferred_element_type=F32) * scale
        m = jnp.max(s, axis=-1, keepdims=True)
        if has_ctx:
            sc = lax.dot_general(q2, ck_ref[0, 0, :, cols].astype(BF16), nt,
                                 preferred_element_type=F32) * scale
            m = jnp.maximum(m, jnp.max(sc, axis=-1, keepdims=True))
            ec = jnp.exp(sc - m)
        e = jnp.exp(s - m)
        den = jnp.sum(e, axis=-1, keepdims=True)
        if has_ctx:
            den = den + jnp.sum(ec, axis=-1, keepdims=True)
        r = 1.0 / den
        r1 = r[0:tq]
        r2 = r[tq:2 * tq] * lam
        a = (e[0:tq] * r1 - e[tq:2 * tq] * r2).astype(BF16)
        o = _dot(a, v_ref[:, cols].astype(BF16))
        if has_ctx:
            ac = (ec[0:tq] * r1 - ec[tq:2 * tq] * r2).astype(BF16)
            o = o + _dot(ac, cv_ref[0, 0, :, cols].astype(BF16))
        ms = jnp.mean(o * o, axis=-1, keepdims=True)
        o = (o * lax.rsqrt(ms + EPS)) * g_ref[...] * (1.0 - lam_init)
        o_ref[:, cols] = o.astype(o_ref.dtype)


def _attention(qkv, lam_params, subln_g, lam_init, *, L, row_block0, n_seq, tq, ctx=None, j=0):
    nq = L // tq
    rb = row_block0 * nq
    small = pl.BlockSpec((1, HEAD_DIM), lambda b, i: (0, 0))
    in_specs = [small, small, small, small,
                pl.BlockSpec((1, HEAD_W), lambda b, i: (0, 0)),
                pl.BlockSpec((tq, D_MODEL), lambda b, i: (rb + b * nq + i, 0)),
                pl.BlockSpec((L, D_MODEL), lambda b, i: (row_block0 + b, 1)),
                pl.BlockSpec((L, D_MODEL), lambda b, i: (row_block0 + b, 2))]
    args = [p.reshape(1, HEAD_DIM) for p in lam_params] + [subln_g.reshape(1, HEAD_W), qkv, qkv, qkv]
    if ctx is not None:
        ctx_spec = pl.BlockSpec((1, 1, PAST_LEN, D_MODEL), lambda b, i: (b, j, 0, 0))
        in_specs += [ctx_spec, ctx_spec]
        args += list(ctx)
    return pl.pallas_call(
        functools.partial(_attn_kernel, lam_init=lam_init, has_ctx=ctx is not None, tq=tq),
        grid=(n_seq, nq),
        in_specs=in_specs,
        out_specs=pl.BlockSpec((tq, D_MODEL), lambda b, i: (b * nq + i, 0)),
        out_shape=jax.ShapeDtypeStruct((n_seq * L, D_MODEL), BF16),
        compiler_params=_cparams(("arbitrary", "arbitrary")),
        name="diff_attention_ctx" if ctx is not None else "diff_attention",
    )(*args)


def _router_kernel(x_ref, g_ref, sh_ref, sc_ref, wr_ref, br_ref, h_ref, gate_ref, rank_ref, cnt_ref,
                   carry, *, tm):
    @pl.when(pl.program_id(0) == 0)
    def _():
        carry[...] = jnp.zeros_like(carry)

    h = _modulate(x_ref[...], g_ref[...], sc_ref[0], sh_ref[0])
    h_ref[...] = h
    logits = _dot(h.astype(BF16), wr_ref[...].astype(BF16)) + br_ref[...]
    lane = lax.broadcasted_iota(jnp.int32, (tm, LANES), 1)
    neg = -jnp.inf
    is_grp = jnp.logical_and(lane >= N_EXPERTS, lane < N_EXPERTS + N_GROUPS)
    lg = jnp.where(is_grp, logits, neg)
    mg = jnp.max(lg, axis=-1, keepdims=True)
    g_gate = 1.0 / jnp.sum(jnp.exp(lg - mg), axis=-1, keepdims=True)
    g_idx = jnp.min(jnp.where(lg == mg, lane - N_EXPERTS, N_GROUPS), axis=-1, keepdims=True)
    in_grp = jnp.logical_and(lane < N_EXPERTS, lane // EXPERTS_PER_GROUP == g_idx)
    le = jnp.where(in_grp, logits, neg)
    m1 = jnp.max(le, axis=-1, keepdims=True)
    i1 = jnp.min(jnp.where(le == m1, lane, LANES), axis=-1, keepdims=True)
    le2 = jnp.where(lane == i1, neg, le)
    m2 = jnp.max(le2, axis=-1, keepdims=True)
    i2 = jnp.min(jnp.where(le2 == m2, lane, LANES), axis=-1, keepdims=True)
    e2 = jnp.exp(m2 - m1)
    inv = 1.0 / (1.0 + e2)
    gate_ref[...] = jnp.where(lane == i1, g_gate * inv, jnp.where(lane == i2, g_gate * (e2 * inv), 0.0))
    sel = jnp.logical_or(lane == i1, lane == i2)
    onehot = jnp.where(sel, 1.0, 0.0)
    r = lax.broadcasted_iota(jnp.int32, (tm, tm), 0)
    c = lax.broadcasted_iota(jnp.int32, (tm, tm), 1)
    below = jnp.where(c < r, 1.0, 0.0).astype(BF16)
    rank = _dot(below, onehot.astype(BF16)) + carry[...]
    rank_ref[...] = jnp.where(sel, rank, -1.0)
    carry[...] = carry[...] + jnp.sum(onehot, axis=0, keepdims=True)
    cnt_ref[...] = carry[...]


def _router(x, g, mods, layer, wr, br, *, tm=512):
    return pl.pallas_call(
        functools.partial(_router_kernel, tm=tm),
        grid=(N_TOK // tm,),
        in_specs=[
            pl.BlockSpec((tm, D_MODEL), lambda i: (i, 0)),
            pl.BlockSpec((1, D_MODEL), lambda i: (0, 0)),
            pl.BlockSpec((1, 1, D_MODEL), _mod_index(layer, 3, tm)),
            pl.BlockSpec((1, 1, D_MODEL), _mod_index(layer, 4, tm)),
            pl.BlockSpec((D_MODEL, LANES), lambda i: (0, 0)),
            pl.BlockSpec((1, LANES), lambda i: (0, 0)),
        ],
        out_specs=[
            pl.BlockSpec((tm, D_MODEL), lambda i: (i, 0)),
            pl.BlockSpec((tm, LANES), lambda i: (i, 0)),
            pl.BlockSpec((tm, LANES), lambda i: (i, 0)),
            pl.BlockSpec((1, LANES), lambda i: (0, 0)),
        ],
        out_shape=[
            jax.ShapeDtypeStruct((N_TOK, D_MODEL), F32),
            jax.ShapeDtypeStruct((N_TOK, LANES), F32),
            jax.ShapeDtypeStruct((N_TOK, LANES), F32),
            jax.ShapeDtypeStruct((1, LANES), F32),
        ],
        scratch_shapes=[pltpu.VMEM((1, LANES), F32)],
        compiler_params=_cparams(("arbitrary",)),
        name="moe_router",
    )(x, g.reshape(1, D_MODEL), mods, mods, wr, br)


def _routing_tables(gate, rank, counts):
    cnt = counts[0, :N_EXPERTS].astype(jnp.int32)
    padded = ((cnt + EXPERT_TILE - 1) // EXPERT_TILE) * EXPERT_TILE
    ends = jnp.cumsum(padded)
    offs = ends - padded
    offs_l = jnp.zeros((LANES,), jnp.int32).at[:N_EXPERTS].set(offs)
    ranki = rank.astype(jnp.int32)
    slot = jnp.where(ranki >= 0, ranki + offs_l[None, :], -1)
    slot_a = jnp.max(slot, axis=1)
    slot_b = jnp.min(jnp.where(slot >= 0, slot, MAX_PAIRS), axis=1)
    gate_a = jnp.sum(jnp.where(slot == slot_a[:, None], gate, 0.0), axis=1)
    gate_b = jnp.sum(jnp.where(slot == slot_b[:, None], gate, 0.0), axis=1)
    n_tiles = MAX_PAIRS // EXPERT_TILE
    tile_start = jnp.arange(n_tiles, dtype=jnp.int32) * EXPERT_TILE
    tile_expert = jnp.sum(tile_start[:, None] >= ends[None, :], axis=1).astype(jnp.int32)
    n_used = (ends[-1] // EXPERT_TILE).astype(jnp.int32)
    tile_expert = jnp.minimum(tile_expert, N_EXPERTS - 1)
    slots = jnp.stack([slot_a, slot_b], axis=1).reshape(-1).astype(jnp.int32)
    gates = jnp.stack([gate_a, gate_b], axis=1)
    return slots, gates, tile_expert, n_used.reshape(1)


def _dispatch_kernel(slots_ref, h_ref, xs_in_ref, xs_ref, sem, *, chunk):
    del xs_in_ref
    base = pl.program_id(0) * chunk

    def row_copy(t, k):
        return pltpu.make_async_copy(h_ref.at[pl.ds(t, 1)],
                                     xs_ref.at[pl.ds(slots_ref[2 * t + k], 1)], sem)

    def issue(t, _):
        row_copy(base + t, 0).start()
        row_copy(base + t, 1).start()
        return 0

    lax.fori_loop(0, chunk, issue, 0)

    def drain(t, _):
        row_copy(base + t, 0).wait()
        row_copy(base + t, 1).wait()
        return 0

    lax.fori_loop(0, chunk, drain, 0)


def _dispatch(slots, h, *, chunk=1024):
    xs0 = jnp.zeros((MAX_PAIRS, D_MODEL), F32)
    return pl.pallas_call(
        functools.partial(_dispatch_kernel, chunk=chunk),
        grid_spec=pltpu.PrefetchScalarGridSpec(
            num_scalar_prefetch=1,
            grid=(N_TOK // chunk,),
            in_specs=[pl.BlockSpec(memory_space=pl.ANY), pl.BlockSpec(memory_space=pl.ANY)],
            out_specs=pl.BlockSpec(memory_space=pl.ANY),
            scratch_shapes=[pltpu.SemaphoreType.DMA(())],
        ),
        out_shape=jax.ShapeDtypeStruct((MAX_PAIRS, D_MODEL), F32),
        input_output_aliases={2: 0},
        compiler_params=_cparams(("arbitrary",)),
        name="moe_dispatch",
    )(slots, h, xs0)


def _expert_kernel(te_ref, nu_ref, xs_ref, w1_ref, w3_ref, w2_ref, y_ref):
    used = pl.program_id(0) < nu_ref[0]

    @pl.when(used)
    def _():
        x = xs_ref[...].astype(BF16)
        a = _dot(x, w1_ref[0].astype(BF16))
        b = _dot(x, w3_ref[0].astype(BF16))
        hid = (a * jax.nn.sigmoid(a)) * b
        y_ref[...] = _dot(hid.astype(BF16), w2_ref[0].astype(BF16))

    @pl.when(jnp.logical_not(used))
    def _():
        y_ref[...] = jnp.zeros_like(y_ref)


def _experts(tile_expert, n_used, xs, w1, w3, w2):
    n_tiles = MAX_PAIRS // EXPERT_TILE
    row = lambda i, te, nu: (jnp.minimum(i, nu[0] - 1), 0)
    wsel = lambda i, te, nu: (te[jnp.minimum(i, nu[0] - 1)], 0, 0)
    return pl.pallas_call(
        _expert_kernel,
        grid_spec=pltpu.PrefetchScalarGridSpec(
            num_scalar_prefetch=2,
            grid=(n_tiles,),
            in_specs=[
                pl.BlockSpec((EXPERT_TILE, D_MODEL), row),
                pl.BlockSpec((1, D_MODEL, D_EXPERT), wsel),
                pl.BlockSpec((1, D_MODEL, D_EXPERT), wsel),
                pl.BlockSpec((1, D_EXPERT, D_MODEL), wsel),
            ],
            out_specs=pl.BlockSpec((EXPERT_TILE, D_MODEL), lambda i, te, nu: (i, 0)),
        ),
        out_shape=jax.ShapeDtypeStruct((MAX_PAIRS, D_MODEL), F32),
        compiler_params=_cparams(("arbitrary",)),
        name="moe_experts",
    )(tile_expert, n_used, xs, w1, w3, w2)


def _combine_kernel(slots_ref, y_ref, x_ref, gates_ref, mg_ref, o_ref, ya, yb, sem, *, tm):
    base = pl.program_id(0) * tm

    def row_copy(t, k, buf):
        return pltpu.make_async_copy(y_ref.at[pl.ds(slots_ref[2 * (base + t) + k], 1)],
                                     buf.at[pl.ds(t, 1)], sem)

    def issue(t, _):
        row_copy(t, 0, ya).start()
        row_copy(t, 1, yb).start()
        return 0

    lax.fori_loop(0, tm, issue, 0)

    def drain(t, _):
        row_copy(t, 0, ya).wait()
        row_copy(t, 1, yb).wait()
        return 0

    lax.fori_loop(0, tm, drain, 0)
    g = gates_ref[...]
    moe = g[:, 0:1] * ya[...] + g[:, 1:2] * yb[...]
    o_ref[...] = x_ref[...] + mg_ref[0] * moe


def _combine(slots, y, x, gates, mods, layer, *, tm=256):
    def gate_idx(i, s):
        return ((layer * MOD_ROWS + _mod_row(i, tm)) * 6 + 5, 0, 0)

    return pl.pallas_call(
        functools.partial(_combine_kernel, tm=tm),
        grid_spec=pltpu.PrefetchScalarGridSpec(
            num_scalar_prefetch=1,
            grid=(N_TOK // tm,),
            in_specs=[
                pl.BlockSpec(memory_space=pl.ANY),
                pl.BlockSpec((tm, D_MODEL), lambda i, s: (i, 0)),
                pl.BlockSpec((tm, 2), lambda i, s: (i, 0)),
                pl.BlockSpec((1, 1, D_MODEL), gate_idx),
            ],
            out_specs=pl.BlockSpec((tm, D_MODEL), lambda i, s: (i, 0)),
            scratch_shapes=[pltpu.VMEM((tm, D_MODEL), F32), pltpu.VMEM((tm, D_MODEL), F32),
                            pltpu.SemaphoreType.DMA(())],
        ),
        out_shape=jax.ShapeDtypeStruct((N_TOK, D_MODEL), F32),
        compiler_params=_cparams(("arbitrary",)),
        name="moe_combine",
    )(slots, y, x, gates, mods)


def _moe(x, g, mods, layer, w_group, b_group, w_expert, b_expert, w1, w3, w2):
    pad = LANES - N_EXPERTS - N_GROUPS
    wr = jnp.concatenate([w_expert, w_group, jnp.zeros((D_MODEL, pad), F32)], axis=1)
    br = jnp.concatenate([b_expert, b_group, jnp.zeros((pad,), F32)]).reshape(1, LANES)
    h, gate, rank, counts = _router(x, g, mods, layer, wr, br)
    slots, gates, tile_expert, n_used = _routing_tables(gate, rank, counts)
    xs = _dispatch(slots, h)
    y = _experts(tile_expert, n_used, xs, w1, w3, w2)
    return _combine(slots, y, x, gates, mods, layer)


def _final_kernel(x_ref, g_ref, o_ref):
    x = x_ref[...]
    ms = jnp.mean(x * x, axis=-1, keepdims=True)
    o_ref[...] = (x * lax.rsqrt(ms + EPS)) * g_ref[...]


def _final_norm(x, g, *, tm=1024):
    return pl.pallas_call(
        _final_kernel,
        grid=(N_TOK // tm,),
        in_specs=[pl.BlockSpec((tm, D_MODEL), lambda i: (i, 0)),
                  pl.BlockSpec((1, D_MODEL), lambda i: (0, 0))],
        out_specs=pl.BlockSpec((tm, D_MODEL), lambda i: (i, 0)),
        out_shape=jax.ShapeDtypeStruct((N_TOK, D_MODEL), F32),
        compiler_params=_cparams(("arbitrary",)),
        name="final_norm",
    )(x, g.reshape(1, D_MODEL))


def kernel(x_prompt, x_sample, cache_k, cache_v, c, c_ctx, ada_w, ada_b, norm1_g, norm2_g, final_g, hy_w_in, hy_conv_w, hy_conv_b, hy_f_w1, hy_f_b1, hy_f_w2, hy_f_b2, hy_f_w3, hy_f_b3, hy_f_freq, hy_bias, hy_w_out, at_w_qkv, at_lam_q1, at_lam_k1, at_lam_q2, at_lam_k2, at_subln_g, at_w_o, moe_w_group, moe_b_group, moe_w_expert, moe_b_expert, moe_w1, moe_w3, moe_w2):
    x = jnp.concatenate([x_prompt.reshape(N_PROMPT, D_MODEL), x_sample.reshape(N_SAMPLE, D_MODEL)], axis=0)
    cond = jnp.concatenate([c_ctx[None, :], c, jnp.zeros((MOD_ROWS - 1 - DEC_BATCH, D_MODEL), F32)], axis=0)
    mods = _ada_table(cond, ada_w, ada_b).reshape(DEPTH * MOD_ROWS * 6, 1, D_MODEL)
    ctx_k = cache_k.reshape(DEC_BATCH, DEPTH // 2, PAST_LEN, D_MODEL)
    ctx_v = cache_v.reshape(DEC_BATCH, DEPTH // 2, PAST_LEN, D_MODEL)

    dft = {}
    for L in (SEQ, DEC_SEQ):
        fwd, inv = _dft_mats(L)
        fwd = jnp.asarray(fwd)
        f_hi, f_lo = _split_bf16(fwd)
        dft[L] = (f_hi, f_lo, jnp.asarray(inv).astype(BF16))

    new_k, new_v = [], []
    for l in range(DEPTH):
        j = l // 2
        if l % 2 == 0:
            z = _norm_matmul(x, norm1_g[l], mods, l, 0, 1, hy_w_in[j])
            ys = []
            for L, rb0, n_seq, dt in ((SEQ, 0, BATCH, 512), (DEC_SEQ, N_PROMPT // DEC_SEQ, DEC_BATCH, 256)):
                f_hi, f_lo, finv = dft[L]
                kc = _filter_spectra(L, hy_f_w1[j], hy_f_b1[j], hy_f_w2[j], hy_f_b2[j], hy_f_w3[j],
                                     hy_f_b3[j], hy_f_freq[j], f_hi, f_lo)
                ys.append(_hyena_conv(z, hy_conv_w[j], hy_conv_b[j], f_hi, finv, kc, hy_bias[j],
                                      L=L, row_block0=rb0, n_seq=n_seq, dt=dt))
            y = jnp.concatenate(ys, axis=0)
            x = _matmul_residual(y, hy_w_out[j], x, mods, l, 2)
        else:
            lam_init = 0.8 - 0.6 * math.exp(-0.3 * l)
            qkv = _norm_matmul(x, norm1_g[l], mods, l, 0, 1, at_w_qkv[j], rope=True)
            lam_params = (at_lam_q1[j], at_lam_k1[j], at_lam_q2[j], at_lam_k2[j])
            op = _attention(qkv, lam_params, at_subln_g[j], lam_init, L=SEQ, row_block0=0,
                            n_seq=BATCH, tq=SEQ)
            os_ = _attention(qkv, lam_params, at_subln_g[j], lam_init, L=DEC_SEQ,
                             row_block0=N_PROMPT // DEC_SEQ, n_seq=DEC_BATCH, tq=256,
                             ctx=(ctx_k, ctx_v), j=j)
            o = jnp.concatenate([op, os_], axis=0)
            new_k.append(qkv[:N_PROMPT, D_MODEL:2 * D_MODEL].reshape(BATCH, SEQ, N_HEADS, HEAD_W))
            new_v.append(qkv[:N_PROMPT, 2 * D_MODEL:].reshape(BATCH, SEQ, N_HEADS, HEAD_W))
            x = _matmul_residual(o, at_w_o[j], x, mods, l, 2)
        x = _moe(x, norm2_g[l], mods, l, moe_w_group[l], moe_b_group[l], moe_w_expert[l],
                 moe_b_expert[l], moe_w1[l], moe_w3[l], moe_w2[l])
    out = _final_norm(x, final_g)
    y_prompt = out[:N_PROMPT].reshape(BATCH, SEQ, D_MODEL)
    y_sample = out[N_PROMPT:].reshape(DEC_BATCH, DEC_SEQ, D_MODEL)
    return (y_prompt, y_sample, jnp.stack(new_k, axis=1), jnp.stack(new_v, axis=1))
```

```python
import functools
import math

import numpy as np
import jax
import jax.numpy as jnp
from jax import lax
from jax.experimental import pallas as pl
from jax.experimental.pallas import tpu as pltpu

F32 = jnp.float32
BF16 = jnp.bfloat16

D_MODEL = 1024
BATCH = 16
SEQ = 256
DEPTH = 4
DEC_BATCH = 4
DEC_SEQ = 1024
PAST_LEN = 256
GRID_W = 64
EPS = 1e-6
HY_BANDS = 16
HY_EMB = 1 + 2 * HY_BANDS
HY_FILTER_HIDDEN = 64
HY_FAST_DECAY = 0.3
HY_SLOW_DECAY = 1.5
HY_TARGET = 1e-2
N_HEADS = 8
HEAD_DIM = 64
ROT_AXIS = HEAD_DIM // 2
ROPE_BASE = 10000.0
N_GROUPS = 4
EXPERTS_PER_GROUP = 8
N_EXPERTS = N_GROUPS * EXPERTS_PER_GROUP
D_EXPERT = D_MODEL // 4

N_PROMPT = BATCH * SEQ
N_SAMPLE = DEC_BATCH * DEC_SEQ
N_TOK = N_PROMPT + N_SAMPLE
MOD_ROWS = 8
LANES = 128
SUBLANES = 8
HEAD_W = 2 * HEAD_DIM
VMEM_LIMIT = 56 * 1024 * 1024

MOE_BLOCK = 256
N_BLOCKS = N_TOK // MOE_BLOCK
CHUNK = SUBLANES
BLOCK_ROWS = 2 * MOE_BLOCK + N_EXPERTS * CHUNK
BLOCK_CHUNKS = BLOCK_ROWS // CHUNK
EXPERT_TILE = 256
TILE_CHUNKS = EXPERT_TILE // CHUNK
N_TILES = (2 * N_TOK + N_BLOCKS * N_EXPERTS * (CHUNK - 1)) // EXPERT_TILE + N_EXPERTS


def _cparams(sem):
    return pltpu.CompilerParams(dimension_semantics=sem, vmem_limit_bytes=VMEM_LIMIT)


def _mod_row(blk, tm):
    start = blk * tm
    return jnp.where(start < N_PROMPT, 0, 1 + (start - N_PROMPT) // DEC_SEQ)


def _mod_index(layer, which, tm):
    def index_map(i, *_):
        return ((layer * MOD_ROWS + _mod_row(i, tm)) * 6 + which, 0, 0)
    return index_map


@functools.lru_cache(maxsize=None)
def _dft_mats(L):
    k = np.arange(L, dtype=np.float64)[:, None]
    j = np.arange(L, dtype=np.float64)[None, :]
    ang = np.pi * ((k * j) % (2 * L)) / L
    c = np.cos(ang)
    s = np.sin(ang)
    s[0, :] = np.where(np.arange(L) % 2 == 0, 1.0, -1.0)
    fwd = np.concatenate([c, s], axis=0)
    scale = np.full((2 * L,), 1.0 / L)
    scale[0] = scale[L] = 0.5 / L
    inv = fwd.T * scale[None, :]
    return fwd.astype(np.float32), inv.astype(np.float32)


@functools.lru_cache(maxsize=None)
def _filter_feats(L):
    pos = np.arange(L, dtype=np.float64)
    t = pos / (L - 1)
    bands = np.linspace(1e-4, HY_BANDS - 1, HY_BANDS)
    ang = (2.0 * math.pi / L) * pos[:, None] * bands[None, :]
    feats = np.concatenate([t[:, None], np.cos(ang), -np.sin(ang)], axis=-1)
    feats = np.pad(feats, ((0, 0), (0, LANES - HY_EMB)))
    deltas = np.abs(np.linspace(math.log(HY_TARGET) / HY_SLOW_DECAY,
                                math.log(HY_TARGET) / HY_FAST_DECAY, D_MODEL))
    window = np.exp(-t[:, None] * deltas[None, :])
    alt = np.where(np.arange(L) % 2 == 0, 1.0, -1.0)[:, None] * np.ones((1, LANES))
    return feats.astype(np.float32), window.astype(np.float32), alt.astype(np.float32)


@functools.lru_cache(maxsize=None)
def _rope_tables():
    pos = np.arange(DEC_SEQ)
    row = (pos // GRID_W).astype(np.float64)
    col = (pos % GRID_W).astype(np.float64)
    lane = np.arange(HEAD_W)
    d = lane % HEAD_DIM
    axis = d // ROT_AXIS
    n = d % ROT_AXIS
    half = n // (ROT_AXIS // 2)
    f = n % (ROT_AXIS // 2)
    inv = ROPE_BASE ** (-f.astype(np.float64) / (ROT_AXIS // 2))
    p = np.where(axis[None, :] == 0, row[:, None], col[:, None])
    ang = p * inv[None, :]
    sign = np.where(half == 0, -1.0, 1.0)[None, :]
    return np.cos(ang).astype(np.float32), (np.sin(ang) * sign).astype(np.float32)


def _split_bf16(a):
    hi = a.astype(BF16)
    lo = (a - hi.astype(F32)).astype(BF16)
    return hi, lo


def _dot(a, b):
    return jnp.dot(a, b, preferred_element_type=F32)


def _dot3(a_hi, a_lo, b_hi, b_lo):
    return _dot(a_hi, b_hi) + (_dot(a_hi, b_lo) + _dot(a_lo, b_hi))


def _ada_kernel(c_ref, w_ref, b_ref, o_ref):
    c = c_ref[...]
    s = c * jax.nn.sigmoid(c)
    o_ref[0] = _dot(s.astype(BF16), w_ref[0].astype(BF16)) + b_ref[0]


def _ada_table(cond, ada_w, ada_b):
    tn = 1536
    n = 6 * D_MODEL
    return pl.pallas_call(
        _ada_kernel,
        grid=(DEPTH, n // tn),
        in_specs=[
            pl.BlockSpec((MOD_ROWS, D_MODEL), lambda l, j: (0, 0)),
            pl.BlockSpec((1, D_MODEL, tn), lambda l, j: (l, 0, j)),
            pl.BlockSpec((1, 1, tn), lambda l, j: (l, 0, j)),
        ],
        out_specs=pl.BlockSpec((1, MOD_ROWS, tn), lambda l, j: (l, 0, j)),
        out_shape=jax.ShapeDtypeStruct((DEPTH, MOD_ROWS, n), F32),
        compiler_params=_cparams(("arbitrary", "arbitrary")),
        name="ada_table",
    )(cond, ada_w, ada_b.reshape(DEPTH, 1, n))


def _modulate(x, g, scale, shift):
    ms = jnp.mean(x * x, axis=-1, keepdims=True)
    y = (x * lax.rsqrt(ms + EPS)) * g
    return y * (1.0 + scale) + shift


def _normmm_kernel(x_ref, g_ref, sh_ref, sc_ref, w_ref, *rest, rope, tm, tn):
    if rope:
        cos_ref, sin_ref, o_ref, h_scr = rest
    else:
        o_ref, h_scr = rest
    i = pl.program_id(0)
    j = pl.program_id(1)

    @pl.when(j == 0)
    def _():
        h_scr[...] = _modulate(x_ref[...], g_ref[...], sc_ref[0], sh_ref[0]).astype(BF16)

    acc = _dot(h_scr[...], w_ref[0].astype(BF16))
    if not rope:
        o_ref[...] = acc.astype(o_ref.dtype)
        return

    is_rope = jnp.logical_and(i >= N_PROMPT // tm, j < 2 * D_MODEL // tn)

    @pl.when(is_rope)
    def _():
        reps = tn // HEAD_W
        cos = jnp.tile(cos_ref[...], (1, reps))
        sin = jnp.tile(sin_ref[...], (1, reps))
        lane = lax.broadcasted_iota(jnp.int32, (tm, tn), 1)
        first_half = (lane % ROT_AXIS) < (ROT_AXIS // 2)
        partner = jnp.where(first_half,
                            pltpu.roll(acc, tn - ROT_AXIS // 2, 1),
                            pltpu.roll(acc, ROT_AXIS // 2, 1))
        o_ref[...] = (acc * cos + partner * sin).astype(o_ref.dtype)

    @pl.when(jnp.logical_not(is_rope))
    def _():
        o_ref[...] = acc.astype(o_ref.dtype)


def _norm_matmul(x, g, mods, layer, which_shift, which_scale, w, widx, *, rope=False,
                 out_dtype=F32, tm=1024, tn=512):
    n = w.shape[2]
    rope_blocks = DEC_SEQ // tm

    def rope_idx(i, j):
        return (jnp.maximum(i - N_PROMPT // tm, 0) % rope_blocks, 0)

    in_specs = [
        pl.BlockSpec((tm, D_MODEL), lambda i, j: (i, 0)),
        pl.BlockSpec((1, D_MODEL), lambda i, j: (0, 0)),
        pl.BlockSpec((1, 1, D_MODEL), _mod_index(layer, which_shift, tm)),
        pl.BlockSpec((1, 1, D_MODEL), _mod_index(layer, which_scale, tm)),
        pl.BlockSpec((1, D_MODEL, tn), lambda i, j: (widx, 0, j)),
    ]
    args = [x, g.reshape(1, D_MODEL), mods, mods, w]
    if rope:
        cos, sin = _rope_tables()
        in_specs += [pl.BlockSpec((tm, HEAD_W), rope_idx), pl.BlockSpec((tm, HEAD_W), rope_idx)]
        args += [jnp.asarray(cos), jnp.asarray(sin)]
    return pl.pallas_call(
        functools.partial(_normmm_kernel, rope=rope, tm=tm, tn=tn),
        grid=(N_TOK // tm, n // tn),
        in_specs=in_specs,
        out_specs=pl.BlockSpec((tm, tn), lambda i, j: (i, j)),
        out_shape=jax.ShapeDtypeStruct((N_TOK, n), out_dtype),
        scratch_shapes=[pltpu.VMEM((tm, D_MODEL), BF16)],
        compiler_params=_cparams(("arbitrary", "arbitrary")),
        name="norm_matmul_rope" if rope else "norm_matmul",
    )(*args)


def _mmres_kernel(a_ref, w_ref, x_ref, gate_ref, o_ref):
    acc = _dot(a_ref[...], w_ref[0].astype(BF16))
    o_ref[...] = x_ref[...] + gate_ref[0] * acc


def _matmul_residual(a, w, widx, x, mods, layer, which_gate, *, tm=1024, tn=512):
    k = a.shape[1]
    n = w.shape[2]

    def gate_idx(i, j):
        return ((layer * MOD_ROWS + _mod_row(i, tm)) * 6 + which_gate, 0, j)

    return pl.pallas_call(
        _mmres_kernel,
        grid=(N_TOK // tm, n // tn),
        in_specs=[
            pl.BlockSpec((tm, k), lambda i, j: (i, 0)),
            pl.BlockSpec((1, k, tn), lambda i, j: (widx, 0, j)),
            pl.BlockSpec((tm, tn), lambda i, j: (i, j)),
            pl.BlockSpec((1, 1, tn), gate_idx),
        ],
        out_specs=pl.BlockSpec((tm, tn), lambda i, j: (i, j)),
        out_shape=jax.ShapeDtypeStruct((N_TOK, n), F32),
        compiler_params=_cparams(("arbitrary", "arbitrary")),
        name="matmul_residual",
    )(a, w, x, mods)


def _filter_kernel(feats_ref, win_ref, alt_ref, w1_ref, b1_ref, w2_ref, b2_ref, fq_ref,
                   w3f_ref, b3f_ref, w3b_ref, b3b_ref, fhi_ref, flo_ref, o_ref, *, L):
    def dense(a, w_ref, b_ref):
        a_hi, a_lo = _split_bf16(a)
        w_hi, w_lo = _split_bf16(w_ref[...])
        return _dot3(a_hi, a_lo, w_hi, w_lo) + b_ref[...]

    fq = fq_ref[...]
    h = jnp.sin(fq[0:1, :] * dense(feats_ref[...], w1_ref, b1_ref))
    h = jnp.sin(fq[1:2, :] * dense(h, w2_ref, b2_ref))
    win = win_ref[...]
    hf = dense(h, w3f_ref, b3f_ref) * win
    hb = dense(h, w3b_ref, b3b_ref) * win
    row = lax.broadcasted_iota(jnp.int32, hf.shape, 0)
    hb = jnp.where(row == 0, 0.0, hb)
    s_hi, s_lo = _split_bf16(hf + hb)
    d_hi, d_lo = _split_bf16(hf - hb)
    kr = _dot3(fhi_ref[0:L, :], flo_ref[0:L, :], s_hi, s_lo)
    ks = _dot3(fhi_ref[L:2 * L, :], flo_ref[L:2 * L, :], d_hi, d_lo)
    k_nyq = jnp.sum((hf + hb) * alt_ref[:, 0:1], axis=0, keepdims=True)
    o_ref[0, 0] = kr
    o_ref[0, 1] = jnp.where(row == 0, k_nyq, kr)
    o_ref[0, 2] = jnp.where(row == 0, 0.0, ks)


def _filter_spectra(L, w1, b1, w2, b2, w3, b3, freq, f_hi, f_lo, *, dt=256):
    feats, window, alt = _filter_feats(L)
    fh = HY_FILTER_HIDDEN
    nd = D_MODEL // dt
    w1p = jnp.pad(w1, ((0, LANES - HY_EMB), (0, 0)))
    const = lambda shape: pl.BlockSpec(shape, lambda o, c: tuple(0 for _ in shape))
    return pl.pallas_call(
        functools.partial(_filter_kernel, L=L),
        grid=(2, nd),
        in_specs=[
            const((L, LANES)),
            pl.BlockSpec((L, dt), lambda o, c: (0, c)),
            const((L, LANES)),
            const((LANES, fh)), const((1, fh)), const((fh, fh)), const((1, fh)), const((2, fh)),
            pl.BlockSpec((fh, dt), lambda o, c: (0, (2 * o) * nd + c)),
            pl.BlockSpec((1, dt), lambda o, c: (0, (2 * o) * nd + c)),
            pl.BlockSpec((fh, dt), lambda o, c: (0, (2 * o + 1) * nd + c)),
            pl.BlockSpec((1, dt), lambda o, c: (0, (2 * o + 1) * nd + c)),
            const((2 * L, L)), const((2 * L, L)),
        ],
        out_specs=pl.BlockSpec((1, 3, L, dt), lambda o, c: (o, 0, 0, c)),
        out_shape=jax.ShapeDtypeStruct((2, 3, L, D_MODEL), F32),
        compiler_params=_cparams(("arbitrary", "arbitrary")),
        name=f"hyena_filter_spectra_{L}",
    )(jnp.asarray(feats), jnp.asarray(window), jnp.asarray(alt), w1p, b1.reshape(1, fh), w2,
      b2.reshape(1, fh), freq, w3, b3.reshape(1, -1), w3, b3.reshape(1, -1), f_hi, f_lo)


def _conv_kernel(v_ref, x1_ref, x2_ref, cwv_ref, cw1_ref, cw2_ref, cbv_ref, cb1_ref, cb2_ref,
                 f_ref, finv_ref, kc_ref, bias_ref, o_ref, *, L):
    row = lax.broadcasted_iota(jnp.int32, v_ref.shape, 0)

    def short_conv(u_ref, w_ref, b_ref):
        u = u_ref[...]
        prev = jnp.where(row == 0, 0.0, pltpu.roll(u, 1, 0))
        nxt = jnp.where(row == L - 1, 0.0, pltpu.roll(u, L - 1, 0))
        w = w_ref[...]
        return prev * w[0:1, :] + u * w[1:2, :] + nxt * w[2:3, :] + b_ref[...]

    def long_conv(u, order):
        spec = _dot(f_ref[...], u.astype(BF16))
        a = spec[0:L, :]
        b = spec[L:2 * L, :]
        kra = kc_ref[order, 0]
        krb = kc_ref[order, 1]
        ks = kc_ref[order, 2]
        prod = jnp.concatenate([a * kra - b * ks, a * ks + b * krb], axis=0)
        return _dot(finv_ref[...], prod.astype(BF16)) + u * bias_ref[order:order + 1, :]

    v = short_conv(v_ref, cwv_ref, cbv_ref)
    y = short_conv(x1_ref, cw1_ref, cb1_ref) * long_conv(v, 0)
    y = short_conv(x2_ref, cw2_ref, cb2_ref) * long_conv(y, 1)
    o_ref[...] = y.astype(o_ref.dtype)


def _hyena_conv(z, conv_w, conv_b, f_bf, finv_bf, kc, bias, *, L, row_block0, n_seq, dt):
    nd = D_MODEL // dt
    cb = conv_b.reshape(1, 3 * D_MODEL)
    seg = lambda s: pl.BlockSpec((L, dt), lambda c, b: (row_block0 + b, s * nd + c))
    cw = lambda s: pl.BlockSpec((3, dt), lambda c, b: (0, s * nd + c))
    cbs = lambda s: pl.BlockSpec((1, dt), lambda c, b: (0, s * nd + c))
    return pl.pallas_call(
        functools.partial(_conv_kernel, L=L),
        grid=(nd, n_seq),
        in_specs=[
            seg(0), seg(1), seg(2), cw(0), cw(1), cw(2), cbs(0), cbs(1), cbs(2),
            pl.BlockSpec((2 * L, L), lambda c, b: (0, 0)),
            pl.BlockSpec((L, 2 * L), lambda c, b: (0, 0)),
            pl.BlockSpec((2, 3, L, dt), lambda c, b: (0, 0, 0, c)),
            pl.BlockSpec((2, dt), lambda c, b: (0, c)),
        ],
        out_specs=pl.BlockSpec((L, dt), lambda c, b: (b, c)),
        out_shape=jax.ShapeDtypeStruct((n_seq * L, D_MODEL), BF16),
        compiler_params=_cparams(("arbitrary", "arbitrary")),
        name=f"hyena_conv_{L}",
    )(z, z, z, conv_w, conv_w, conv_w, cb, cb, cb, f_bf, finv_bf, kc, bias)


def _attn_kernel(lq1_ref, lk1_ref, lq2_ref, lk2_ref, g_ref, q_ref, k_ref, v_ref, *rest,
                 lam_init, has_ctx, tq):
    if has_ctx:
        ck_ref, cv_ref, o_ref = rest
    else:
        (o_ref,) = rest
    lam = (jnp.exp(jnp.sum(lq1_ref[...] * lk1_ref[...], axis=-1, keepdims=True))
           - jnp.exp(jnp.sum(lq2_ref[...] * lk2_ref[...], axis=-1, keepdims=True)) + lam_init)
    scale = HEAD_DIM ** -0.5
    lane = lax.broadcasted_iota(jnp.int32, (tq, HEAD_W), 1)
    nt = (((1,), (1,)), ((), ()))
    for h in range(N_HEADS):
        cols = slice(h * HEAD_W, (h + 1) * HEAD_W)
        q = q_ref[:, cols]
        q2 = jnp.concatenate([jnp.where(lane < HEAD_DIM, q, 0.0),
                              jnp.where(lane >= HEAD_DIM, q, 0.0)], axis=0).astype(BF16)
        s = lax.dot_general(q2, k_ref[:, cols].astype(BF16), nt, preferred_element_type=F32) * scale
        m = jnp.max(s, axis=-1, keepdims=True)
        if has_ctx:
            sc = lax.dot_general(q2, ck_ref[0, 0, :, cols].astype(BF16), nt,
                                 preferred_element_type=F32) * scale
            m = jnp.maximum(m, jnp.max(sc, axis=-1, keepdims=True))
            ec = jnp.exp(sc - m)
        e = jnp.exp(s - m)
        den = jnp.sum(e, axis=-1, keepdims=True)
        if has_ctx:
            den = den + jnp.sum(ec, axis=-1, keepdims=True)
        r = 1.0 / den
        r1 = r[0:tq]
        r2 = r[tq:2 * tq] * lam
        a = (e[0:tq] * r1 - e[tq:2 * tq] * r2).astype(BF16)
        o = _dot(a, v_ref[:, cols].astype(BF16))
        if has_ctx:
            ac = (ec[0:tq] * r1 - ec[tq:2 * tq] * r2).astype(BF16)
            o = o + _dot(ac, cv_ref[0, 0, :, cols].astype(BF16))
        ms = jnp.mean(o * o, axis=-1, keepdims=True)
        o = (o * lax.rsqrt(ms + EPS)) * g_ref[...] * (1.0 - lam_init)
        o_ref[:, cols] = o.astype(o_ref.dtype)


def _attention(qkv, lam_params, subln_g, lam_init, *, L, row_block0, n_seq, tq, ctx=None, j=0):
    nq = L // tq
    rb = row_block0 * nq
    small = pl.BlockSpec((1, HEAD_DIM), lambda b, i: (0, 0))
    in_specs = [small, small, small, small,
                pl.BlockSpec((1, HEAD_W), lambda b, i: (0, 0)),
                pl.BlockSpec((tq, D_MODEL), lambda b, i: (rb + b * nq + i, 0)),
                pl.BlockSpec((L, D_MODEL), lambda b, i: (row_block0 + b, 1)),
                pl.BlockSpec((L, D_MODEL), lambda b, i: (row_block0 + b, 2))]
    args = [p.reshape(1, HEAD_DIM) for p in lam_params] + [subln_g.reshape(1, HEAD_W), qkv, qkv, qkv]
    if ctx is not None:
        ctx_spec = pl.BlockSpec((1, 1, PAST_LEN, D_MODEL), lambda b, i: (b, j, 0, 0))
        in_specs += [ctx_spec, ctx_spec]
        args += list(ctx)
    return pl.pallas_call(
        functools.partial(_attn_kernel, lam_init=lam_init, has_ctx=ctx is not None, tq=tq),
        grid=(n_seq, nq),
        in_specs=in_specs,
        out_specs=pl.BlockSpec((tq, D_MODEL), lambda b, i: (b * nq + i, 0)),
        out_shape=jax.ShapeDtypeStruct((n_seq * L, D_MODEL), BF16),
        compiler_params=_cparams(("arbitrary", "arbitrary")),
        name="diff_attention_ctx" if ctx is not None else "diff_attention",
    )(*args)


def _row_slots(pos, sel):
    big = float(BLOCK_ROWS)
    pos_a = jnp.max(pos, axis=-1, keepdims=True)
    pos_b = jnp.min(jnp.where(sel, pos, big), axis=-1, keepdims=True)
    return pos_a, pos_b


def _router_kernel(x_ref, g_ref, sh_ref, sc_ref, wr_ref, br_ref, xb_ref, pos_ref, gate_ref, cnt_ref,
                   *, tm):
    h = _modulate(x_ref[...], g_ref[...], sc_ref[0], sh_ref[0]).astype(BF16)
    logits = _dot(h, wr_ref[...].astype(BF16)) + br_ref[...]
    lane = lax.broadcasted_iota(jnp.int32, (tm, LANES), 1)
    neg = -jnp.inf
    is_grp = jnp.logical_and(lane >= N_EXPERTS, lane < N_EXPERTS + N_GROUPS)
    lg = jnp.where(is_grp, logits, neg)
    mg = jnp.max(lg, axis=-1, keepdims=True)
    g_gate = 1.0 / jnp.sum(jnp.exp(lg - mg), axis=-1, keepdims=True)
    g_idx = jnp.min(jnp.where(lg == mg, lane - N_EXPERTS, N_GROUPS), axis=-1, keepdims=True)
    in_grp = jnp.logical_and(lane < N_EXPERTS, lane // EXPERTS_PER_GROUP == g_idx)
    le = jnp.where(in_grp, logits, neg)
    m1 = jnp.max(le, axis=-1, keepdims=True)
    i1 = jnp.min(jnp.where(le == m1, lane, LANES), axis=-1, keepdims=True)
    le2 = jnp.where(lane == i1, neg, le)
    m2 = jnp.max(le2, axis=-1, keepdims=True)
    i2 = jnp.min(jnp.where(le2 == m2, lane, LANES), axis=-1, keepdims=True)
    e2 = jnp.exp(m2 - m1)
    inv = 1.0 / (1.0 + e2)
    gate_ref[...] = jnp.where(lane == i1, g_gate * inv, jnp.where(lane == i2, g_gate * (e2 * inv), 0.0))
    sel = jnp.logical_or(lane == i1, lane == i2)
    onehot = jnp.where(sel, 1.0, 0.0)
    r = lax.broadcasted_iota(jnp.int32, (tm, tm), 0)
    c = lax.broadcasted_iota(jnp.int32, (tm, tm), 1)
    rank = _dot(jnp.where(c < r, 1.0, 0.0).astype(BF16), onehot.astype(BF16))
    count = jnp.sum(onehot, axis=0, keepdims=True)
    cnt_ref[0] = count
    padded = jnp.ceil(count * (1.0 / CHUNK)) * CHUNK
    er = lax.broadcasted_iota(jnp.int32, (LANES, LANES), 0)
    ec = lax.broadcasted_iota(jnp.int32, (LANES, LANES), 1)
    before = jnp.where(er < ec, 1.0, 0.0).astype(BF16)
    start = _dot(jnp.broadcast_to(padded, (SUBLANES, LANES)).astype(BF16), before)[0:1, :]
    pos = jnp.where(sel, start + rank, -1.0)
    pos_ref[...] = pos
    pos_a, pos_b = _row_slots(pos, sel)
    slot = lax.broadcasted_iota(jnp.int32, (tm, BLOCK_ROWS), 1).astype(F32)
    pick = jnp.where(jnp.logical_or(slot == pos_a, slot == pos_b), 1.0, 0.0).astype(BF16)
    xb_ref[...] = lax.dot_general(pick, h, (((0,), (0,)), ((), ())), preferred_element_type=F32)


def _router(x, g, mods, layer, wr, br):
    tm = MOE_BLOCK
    return pl.pallas_call(
        functools.partial(_router_kernel, tm=tm),
        grid=(N_BLOCKS,),
        in_specs=[
            pl.BlockSpec((tm, D_MODEL), lambda i: (i, 0)),
            pl.BlockSpec((1, D_MODEL), lambda i: (0, 0)),
            pl.BlockSpec((1, 1, D_MODEL), _mod_index(layer, 3, tm)),
            pl.BlockSpec((1, 1, D_MODEL), _mod_index(layer, 4, tm)),
            pl.BlockSpec((D_MODEL, LANES), lambda i: (0, 0)),
            pl.BlockSpec((1, LANES), lambda i: (0, 0)),
        ],
        out_specs=[
            pl.BlockSpec((BLOCK_ROWS, D_MODEL), lambda i: (i, 0)),
            pl.BlockSpec((tm, LANES), lambda i: (i, 0)),
            pl.BlockSpec((tm, LANES), lambda i: (i, 0)),
            pl.BlockSpec((1, 1, LANES), lambda i: (i, 0, 0)),
        ],
        out_shape=[
            jax.ShapeDtypeStruct((N_BLOCKS * BLOCK_ROWS, D_MODEL), F32),
            jax.ShapeDtypeStruct((N_TOK, LANES), F32),
            jax.ShapeDtypeStruct((N_TOK, LANES), F32),
            jax.ShapeDtypeStruct((N_BLOCKS, 1, LANES), F32),
        ],
        compiler_params=_cparams(("arbitrary",)),
        name="moe_router",
    )(x, g.reshape(1, D_MODEL), mods, mods, wr, br)


def _chunk_tables(counts):
    n = counts[:, 0, :N_EXPERTS].astype(jnp.int32)
    m = (n + CHUNK - 1) // CHUNK
    m_incl_e = jnp.cumsum(m, axis=1)
    bstart = m_incl_e - m
    m_incl_b = jnp.cumsum(m, axis=0)
    cm = m_incl_b - m
    total = m_incl_b[-1]
    tiles = (total + TILE_CHUNKS - 1) // TILE_CHUNKS
    tile_end = jnp.cumsum(tiles)
    choff = (tile_end - tiles) * TILE_CHUNKS
    n_used = tile_end[-1]
    tile_ids = jnp.arange(N_TILES, dtype=jnp.int32)
    tile_expert = jnp.minimum(jnp.sum(tile_ids[:, None] >= tile_end[None, :], axis=1), N_EXPERTS - 1)
    g = jnp.arange(N_TILES * TILE_CHUNKS, dtype=jnp.int32)
    e = jnp.repeat(tile_expert, TILE_CHUNKS)
    q = g - choff[e]
    blk = jnp.minimum(jnp.sum(q[:, None] >= m_incl_b.T[e], axis=1), N_BLOCKS - 1)
    src = (blk * BLOCK_CHUNKS + bstart[blk, e] + (q - cm[blk, e])) * CHUNK
    valid = jnp.logical_and(q < total[e], g < n_used * TILE_CHUNKS)
    disp_src = jnp.where(valid, src, -1).astype(jnp.int32)
    c = jnp.arange(BLOCK_CHUNKS, dtype=jnp.int32)
    eb = jnp.minimum(jnp.sum(c[None, :, None] >= m_incl_e[:, None, :], axis=2), N_EXPERTS - 1)
    bidx = jnp.arange(N_BLOCKS, dtype=jnp.int32)[:, None]
    back = (choff[eb] + cm[bidx, eb] + (c[None, :] - bstart[bidx, eb])) * CHUNK
    comb_src = jnp.where(c[None, :] < m_incl_e[:, -1:], back, -1).astype(jnp.int32).reshape(-1)
    return tile_expert.astype(jnp.int32), n_used.astype(jnp.int32).reshape(1), disp_src, comb_src


def _chunk_copies(table_ref, base, n_chunks, src_hbm, dst, sem, wait):
    for c in range(n_chunks):
        row = table_ref[base + c]

        @pl.when(row >= 0)
        def _():
            cp = pltpu.make_async_copy(src_hbm.at[pl.ds(pl.multiple_of(row, CHUNK), CHUNK)],
                                       dst.at[pl.ds(c * CHUNK, CHUNK)], sem)
            if wait:
                cp.wait()
            else:
                cp.start()


def _expert_kernel(te_ref, nu_ref, src_ref, xb_hbm, w1_ref, w3_ref, w2_ref, y_ref, xbuf, sem):
    i = pl.program_id(0)
    n_used = nu_ref[0]
    slot = i % 2

    def fetch(tile, s, wait):
        _chunk_copies(src_ref, tile * TILE_CHUNKS, TILE_CHUNKS, xb_hbm, xbuf.at[s], sem.at[s], wait)

    @pl.when(i == 0)
    def _():
        xbuf[...] = jnp.zeros_like(xbuf)
        fetch(0, 0, False)

    @pl.when(i + 1 < n_used)
    def _():
        fetch(i + 1, 1 - slot, False)

    @pl.when(i < n_used)
    def _():
        fetch(i, slot, True)
        x = xbuf[slot].astype(BF16)
        a = _dot(x, w1_ref[0, 0].astype(BF16))
        b = _dot(x, w3_ref[0, 0].astype(BF16))
        hid = (a * jax.nn.sigmoid(a)) * b
        y_ref[...] = _dot(hid.astype(BF16), w2_ref[0, 0].astype(BF16))

    @pl.when(i >= n_used)
    def _():
        y_ref[...] = jnp.zeros_like(y_ref)


def _experts(tile_expert, n_used, disp_src, xb, w1, w3, w2, layer):
    wsel = lambda i, te, nu, src: (layer, te[jnp.minimum(i, nu[0] - 1)], 0, 0)
    return pl.pallas_call(
        _expert_kernel,
        grid_spec=pltpu.PrefetchScalarGridSpec(
            num_scalar_prefetch=3,
            grid=(N_TILES,),
            in_specs=[
                pl.BlockSpec(memory_space=pl.ANY),
                pl.BlockSpec((1, 1, D_MODEL, D_EXPERT), wsel),
                pl.BlockSpec((1, 1, D_MODEL, D_EXPERT), wsel),
                pl.BlockSpec((1, 1, D_EXPERT, D_MODEL), wsel),
            ],
            out_specs=pl.BlockSpec((EXPERT_TILE, D_MODEL), lambda i, te, nu, src: (i, 0)),
            scratch_shapes=[pltpu.VMEM((2, EXPERT_TILE, D_MODEL), F32),
                            pltpu.SemaphoreType.DMA((2,))],
        ),
        out_shape=jax.ShapeDtypeStruct((N_TILES * EXPERT_TILE, D_MODEL), F32),
        compiler_params=_cparams(("arbitrary",)),
        name="moe_experts",
    )(tile_expert, n_used, disp_src, xb, w1, w3, w2)


def _combine_kernel(src_ref, y_hbm, x_ref, pos_ref, gate_ref, mg_ref, o_ref, ybuf, sem, *, tm):
    i = pl.program_id(0)
    slot = i % 2

    def fetch(blk, s, wait):
        _chunk_copies(src_ref, blk * BLOCK_CHUNKS, BLOCK_CHUNKS, y_hbm, ybuf.at[s], sem.at[s], wait)

    @pl.when(i == 0)
    def _():
        ybuf[...] = jnp.zeros_like(ybuf)
        fetch(0, 0, False)

    @pl.when(i + 1 < N_BLOCKS)
    def _():
        fetch(i + 1, 1 - slot, False)

    fetch(i, slot, True)
    pos = pos_ref[...]
    sel = pos >= 0.0
    gate = gate_ref[...]
    pos_a, pos_b = _row_slots(pos, sel)
    gate_a = jnp.sum(jnp.where(pos == pos_a, gate, 0.0), axis=-1, keepdims=True)
    gate_b = jnp.sum(jnp.where(jnp.logical_and(sel, pos == pos_b), gate, 0.0), axis=-1, keepdims=True)
    row = lax.broadcasted_iota(jnp.int32, (tm, BLOCK_ROWS), 1).astype(F32)
    weights = jnp.where(row == pos_a, gate_a, jnp.where(row == pos_b, gate_b, 0.0))
    moe = _dot(weights.astype(BF16), ybuf[slot].astype(BF16))
    o_ref[...] = x_ref[...] + mg_ref[0] * moe


def _combine(comb_src, y, x, pos, gate, mods, layer):
    tm = MOE_BLOCK

    def gate_idx(i, s):
        return ((layer * MOD_ROWS + _mod_row(i, tm)) * 6 + 5, 0, 0)

    return pl.pallas_call(
        functools.partial(_combine_kernel, tm=tm),
        grid_spec=pltpu.PrefetchScalarGridSpec(
            num_scalar_prefetch=1,
            grid=(N_BLOCKS,),
            in_specs=[
                pl.BlockSpec(memory_space=pl.ANY),
                pl.BlockSpec((tm, D_MODEL), lambda i, s: (i, 0)),
                pl.BlockSpec((tm, LANES), lambda i, s: (i, 0)),
                pl.BlockSpec((tm, LANES), lambda i, s: (i, 0)),
                pl.BlockSpec((1, 1, D_MODEL), gate_idx),
            ],
            out_specs=pl.BlockSpec((tm, D_MODEL), lambda i, s: (i, 0)),
            scratch_shapes=[pltpu.VMEM((2, BLOCK_ROWS, D_MODEL), F32),
                            pltpu.SemaphoreType.DMA((2,))],
        ),
        out_shape=jax.ShapeDtypeStruct((N_TOK, D_MODEL), F32),
        compiler_params=_cparams(("arbitrary",)),
        name="moe_combine",
    )(comb_src, y, x, pos, gate, mods)


def _moe(x, g, mods, layer, w_group, b_group, w_expert, b_expert, w1, w3, w2):
    pad = LANES - N_EXPERTS - N_GROUPS
    wr = jnp.concatenate([w_expert, w_group, jnp.zeros((D_MODEL, pad), F32)], axis=1)
    br = jnp.concatenate([b_expert, b_group, jnp.zeros((pad,), F32)]).reshape(1, LANES)
    xb, pos, gate, counts = _router(x, g, mods, layer, wr, br)
    tile_expert, n_used, disp_src, comb_src = _chunk_tables(counts)
    y = _experts(tile_expert, n_used, disp_src, xb, w1, w3, w2, layer)
    return _combine(comb_src, y, x, pos, gate, mods, layer)


def _final_kernel(x_ref, g_ref, o_ref):
    x = x_ref[...]
    ms = jnp.mean(x * x, axis=-1, keepdims=True)
    o_ref[...] = (x * lax.rsqrt(ms + EPS)) * g_ref[...]


def _final_norm(x, g, *, tm=1024):
    return pl.pallas_call(
        _final_kernel,
        grid=(N_TOK // tm,),
        in_specs=[pl.BlockSpec((tm, D_MODEL), lambda i: (i, 0)),
                  pl.BlockSpec((1, D_MODEL), lambda i: (0, 0))],
        out_specs=pl.BlockSpec((tm, D_MODEL), lambda i: (i, 0)),
        out_shape=jax.ShapeDtypeStruct((N_TOK, D_MODEL), F32),
        compiler_params=_cparams(("arbitrary",)),
        name="final_norm",
    )(x, g.reshape(1, D_MODEL))


def kernel(x_prompt, x_sample, cache_k, cache_v, c, c_ctx, ada_w, ada_b, norm1_g, norm2_g, final_g, hy_w_in, hy_conv_w, hy_conv_b, hy_f_w1, hy_f_b1, hy_f_w2, hy_f_b2, hy_f_w3, hy_f_b3, hy_f_freq, hy_bias, hy_w_out, at_w_qkv, at_lam_q1, at_lam_k1, at_lam_q2, at_lam_k2, at_subln_g, at_w_o, moe_w_group, moe_b_group, moe_w_expert, moe_b_expert, moe_w1, moe_w3, moe_w2):
    x = jnp.concatenate([x_prompt.reshape(N_PROMPT, D_MODEL), x_sample.reshape(N_SAMPLE, D_MODEL)], axis=0)
    cond = jnp.concatenate([c_ctx[None, :], c, jnp.zeros((MOD_ROWS - 1 - DEC_BATCH, D_MODEL), F32)], axis=0)
    mods = _ada_table(cond, ada_w, ada_b).reshape(DEPTH * MOD_ROWS * 6, 1, D_MODEL)
    ctx_k = cache_k.reshape(DEC_BATCH, DEPTH // 2, PAST_LEN, D_MODEL)
    ctx_v = cache_v.reshape(DEC_BATCH, DEPTH // 2, PAST_LEN, D_MODEL)

    dft = {}
    for L in (SEQ, DEC_SEQ):
        fwd, inv = _dft_mats(L)
        fwd = jnp.asarray(fwd)
        f_hi, f_lo = _split_bf16(fwd)
        dft[L] = (f_hi, f_lo, jnp.asarray(inv).astype(BF16))

    new_k, new_v = [], []
    for l in range(DEPTH):
        j = l // 2
        if l % 2 == 0:
            z = _norm_matmul(x, norm1_g[l], mods, l, 0, 1, hy_w_in, j)
            ys = []
            for L, rb0, n_seq, dt in ((SEQ, 0, BATCH, 512), (DEC_SEQ, N_PROMPT // DEC_SEQ, DEC_BATCH, 256)):
                f_hi, f_lo, finv = dft[L]
                kc = _filter_spectra(L, hy_f_w1[j], hy_f_b1[j], hy_f_w2[j], hy_f_b2[j], hy_f_w3[j],
                                     hy_f_b3[j], hy_f_freq[j], f_hi, f_lo)
                ys.append(_hyena_conv(z, hy_conv_w[j], hy_conv_b[j], f_hi, finv, kc, hy_bias[j],
                                      L=L, row_block0=rb0, n_seq=n_seq, dt=dt))
            y = jnp.concatenate(ys, axis=0)
            x = _matmul_residual(y, hy_w_out, j, x, mods, l, 2)
        else:
            lam_init = 0.8 - 0.6 * math.exp(-0.3 * l)
            qkv = _norm_matmul(x, norm1_g[l], mods, l, 0, 1, at_w_qkv, j, rope=True)
            lam_params = (at_lam_q1[j], at_lam_k1[j], at_lam_q2[j], at_lam_k2[j])
            op = _attention(qkv, lam_params, at_subln_g[j], lam_init, L=SEQ, row_block0=0,
                            n_seq=BATCH, tq=SEQ)
            os_ = _attention(qkv, lam_params, at_subln_g[j], lam_init, L=DEC_SEQ,
                             row_block0=N_PROMPT // DEC_SEQ, n_seq=DEC_BATCH, tq=256,
                             ctx=(ctx_k, ctx_v), j=j)
            o = jnp.concatenate([op, os_], axis=0)
            new_k.append(qkv[:N_PROMPT, D_MODEL:2 * D_MODEL].reshape(BATCH, SEQ, N_HEADS, HEAD_W))
            new_v.append(qkv[:N_PROMPT, 2 * D_MODEL:].reshape(BATCH, SEQ, N_HEADS, HEAD_W))
            x = _matmul_residual(o, at_w_o, j, x, mods, l, 2)
        x = _moe(x, norm2_g[l], mods, l, moe_w_group[l], moe_b_group[l], moe_w_expert[l],
                 moe_b_expert[l], moe_w1, moe_w3, moe_w2)
    out = _final_norm(x, final_g)
    y_prompt = out[:N_PROMPT].reshape(BATCH, SEQ, D_MODEL)
    y_sample = out[N_PROMPT:].reshape(DEC_BATCH, DEC_SEQ, D_MODEL)
    return (y_prompt, y_sample, jnp.stack(new_k, axis=1), jnp.stack(new_v, axis=1))
```

```python
import functools
import math

import numpy as np
import jax
import jax.numpy as jnp
from jax import lax
from jax.experimental import pallas as pl
from jax.experimental.pallas import tpu as pltpu

F32 = jnp.float32
BF16 = jnp.bfloat16

D_MODEL = 1024
BATCH = 16
SEQ = 256
DEPTH = 4
DEC_BATCH = 4
DEC_SEQ = 1024
PAST_LEN = 256
GRID_W = 64
EPS = 1e-6
HY_BANDS = 16
HY_EMB = 1 + 2 * HY_BANDS
HY_FILTER_HIDDEN = 64
HY_FAST_DECAY = 0.3
HY_SLOW_DECAY = 1.5
HY_TARGET = 1e-2
N_HEADS = 8
HEAD_DIM = 64
ROT_AXIS = HEAD_DIM // 2
ROPE_BASE = 10000.0
N_GROUPS = 4
EXPERTS_PER_GROUP = 8
N_EXPERTS = N_GROUPS * EXPERTS_PER_GROUP
D_EXPERT = D_MODEL // 4

N_PROMPT = BATCH * SEQ
N_SAMPLE = DEC_BATCH * DEC_SEQ
N_TOK = N_PROMPT + N_SAMPLE
MOD_ROWS = 8
LANES = 128
SUBLANES = 8
HEAD_W = 2 * HEAD_DIM
VMEM_LIMIT = 56 * 1024 * 1024

MOE_BLOCK = 256
N_BLOCKS = N_TOK // MOE_BLOCK
CHUNK = 2 * SUBLANES
BLOCK_ROWS = 2 * MOE_BLOCK + N_EXPERTS * CHUNK
BLOCK_CHUNKS = BLOCK_ROWS // CHUNK
EXPERT_TILE = 256
TILE_CHUNKS = EXPERT_TILE // CHUNK
N_TILES = (2 * N_TOK + N_BLOCKS * N_EXPERTS * (CHUNK - 1)) // EXPERT_TILE + N_EXPERTS


def _cparams(sem):
    return pltpu.CompilerParams(dimension_semantics=sem, vmem_limit_bytes=VMEM_LIMIT)


def _mod_row(blk, tm):
    start = blk * tm
    return jnp.where(start < N_PROMPT, 0, 1 + (start - N_PROMPT) // DEC_SEQ)


def _mod_index(layer, which, tm):
    def index_map(i, *_):
        return ((layer * MOD_ROWS + _mod_row(i, tm)) * 6 + which, 0, 0)
    return index_map


@functools.lru_cache(maxsize=None)
def _dft_mats(L):
    k = np.arange(L, dtype=np.float64)[:, None]
    j = np.arange(L, dtype=np.float64)[None, :]
    ang = np.pi * ((k * j) % (2 * L)) / L
    c = np.cos(ang)
    s = np.sin(ang)
    s[0, :] = np.where(np.arange(L) % 2 == 0, 1.0, -1.0)
    fwd = np.concatenate([c, s], axis=0)
    scale = np.full((2 * L,), 1.0 / L)
    scale[0] = scale[L] = 0.5 / L
    inv = fwd.T * scale[None, :]
    return fwd.astype(np.float32), inv.astype(np.float32)


@functools.lru_cache(maxsize=None)
def _filter_feats(L):
    pos = np.arange(L, dtype=np.float64)
    t = pos / (L - 1)
    bands = np.linspace(1e-4, HY_BANDS - 1, HY_BANDS)
    ang = (2.0 * math.pi / L) * pos[:, None] * bands[None, :]
    feats = np.concatenate([t[:, None], np.cos(ang), -np.sin(ang)], axis=-1)
    feats = np.pad(feats, ((0, 0), (0, LANES - HY_EMB)))
    deltas = np.abs(np.linspace(math.log(HY_TARGET) / HY_SLOW_DECAY,
                                math.log(HY_TARGET) / HY_FAST_DECAY, D_MODEL))
    window = np.exp(-t[:, None] * deltas[None, :])
    alt = np.where(np.arange(L) % 2 == 0, 1.0, -1.0)[:, None] * np.ones((1, LANES))
    return feats.astype(np.float32), window.astype(np.float32), alt.astype(np.float32)


@functools.lru_cache(maxsize=None)
def _rope_tables():
    pos = np.arange(DEC_SEQ)
    row = (pos // GRID_W).astype(np.float64)
    col = (pos % GRID_W).astype(np.float64)
    lane = np.arange(HEAD_W)
    d = lane % HEAD_DIM
    axis = d // ROT_AXIS
    n = d % ROT_AXIS
    half = n // (ROT_AXIS // 2)
    f = n % (ROT_AXIS // 2)
    inv = ROPE_BASE ** (-f.astype(np.float64) / (ROT_AXIS // 2))
    p = np.where(axis[None, :] == 0, row[:, None], col[:, None])
    ang = p * inv[None, :]
    sign = np.where(half == 0, -1.0, 1.0)[None, :]
    return np.cos(ang).astype(np.float32), (np.sin(ang) * sign).astype(np.float32)


def _split_bf16(a):
    hi = a.astype(BF16)
    lo = (a - hi.astype(F32)).astype(BF16)
    return hi, lo


def _dot(a, b):
    return jnp.dot(a, b, preferred_element_type=F32)


def _dot3(a_hi, a_lo, b_hi, b_lo):
    return _dot(a_hi, b_hi) + (_dot(a_hi, b_lo) + _dot(a_lo, b_hi))


def _ada_kernel(c_ref, w_ref, b_ref, o_ref):
    c = c_ref[...]
    s = c * jax.nn.sigmoid(c)
    o_ref[0] = _dot(s.astype(BF16), w_ref[0].astype(BF16)) + b_ref[0]


def _ada_table(cond, ada_w, ada_b):
    tn = 1536
    n = 6 * D_MODEL
    return pl.pallas_call(
        _ada_kernel,
        grid=(DEPTH, n // tn),
        in_specs=[
            pl.BlockSpec((MOD_ROWS, D_MODEL), lambda l, j: (0, 0)),
            pl.BlockSpec((1, D_MODEL, tn), lambda l, j: (l, 0, j)),
            pl.BlockSpec((1, 1, tn), lambda l, j: (l, 0, j)),
        ],
        out_specs=pl.BlockSpec((1, MOD_ROWS, tn), lambda l, j: (l, 0, j)),
        out_shape=jax.ShapeDtypeStruct((DEPTH, MOD_ROWS, n), F32),
        compiler_params=_cparams(("arbitrary", "arbitrary")),
        name="ada_table",
    )(cond, ada_w, ada_b.reshape(DEPTH, 1, n))


def _modulate(x, g, scale, shift):
    ms = jnp.mean(x * x, axis=-1, keepdims=True)
    y = (x * lax.rsqrt(ms + EPS)) * g
    return y * (1.0 + scale) + shift


MM_COLS = 512


def _normmm_kernel(x_ref, g_ref, sh_ref, sc_ref, w_ref, *rest, rope, tm):
    if rope:
        cos_ref, sin_ref, o_ref, w_scr = rest
    else:
        o_ref, w_scr = rest
    i = pl.program_id(0)

    @pl.when(i == 0)
    def _():
        w_scr[...] = w_ref[0].astype(BF16)

    h = _modulate(x_ref[...], g_ref[...], sc_ref[0], sh_ref[0]).astype(BF16)
    n = w_scr.shape[1]
    for j in range(n // MM_COLS):
        cols = slice(j * MM_COLS, (j + 1) * MM_COLS)
        acc = _dot(h, w_scr[:, cols])
        if not (rope and j < 2 * D_MODEL // MM_COLS):
            o_ref[:, cols] = acc.astype(o_ref.dtype)
            continue

        @pl.when(i >= N_PROMPT // tm)
        def _():
            reps = MM_COLS // HEAD_W
            cos = jnp.tile(cos_ref[...], (1, reps))
            sin = jnp.tile(sin_ref[...], (1, reps))
            lane = lax.broadcasted_iota(jnp.int32, (tm, MM_COLS), 1)
            first_half = (lane % ROT_AXIS) < (ROT_AXIS // 2)
            partner = jnp.where(first_half,
                                pltpu.roll(acc, MM_COLS - ROT_AXIS // 2, 1),
                                pltpu.roll(acc, ROT_AXIS // 2, 1))
            o_ref[:, cols] = (acc * cos + partner * sin).astype(o_ref.dtype)

        @pl.when(i < N_PROMPT // tm)
        def _():
            o_ref[:, cols] = acc.astype(o_ref.dtype)


def _norm_matmul(x, g, mods, layer, which_shift, which_scale, w, widx, *, rope=False, tm=512):
    n = w.shape[2]
    rope_blocks = DEC_SEQ // tm

    def rope_idx(i):
        return (jnp.maximum(i - N_PROMPT // tm, 0) % rope_blocks, 0)

    in_specs = [
        pl.BlockSpec((tm, D_MODEL), lambda i: (i, 0)),
        pl.BlockSpec((1, D_MODEL), lambda i: (0, 0)),
        pl.BlockSpec((1, 1, D_MODEL), _mod_index(layer, which_shift, tm)),
        pl.BlockSpec((1, 1, D_MODEL), _mod_index(layer, which_scale, tm)),
        pl.BlockSpec((1, D_MODEL, n), lambda i: (widx, 0, 0), pipeline_mode=pl.Buffered(1)),
    ]
    args = [x, g.reshape(1, D_MODEL), mods, mods, w]
    if rope:
        cos, sin = _rope_tables()
        in_specs += [pl.BlockSpec((tm, HEAD_W), rope_idx), pl.BlockSpec((tm, HEAD_W), rope_idx)]
        args += [jnp.asarray(cos), jnp.asarray(sin)]
    return pl.pallas_call(
        functools.partial(_normmm_kernel, rope=rope, tm=tm),
        grid=(N_TOK // tm,),
        in_specs=in_specs,
        out_specs=pl.BlockSpec((tm, n), lambda i: (i, 0)),
        out_shape=jax.ShapeDtypeStruct((N_TOK, n), BF16),
        scratch_shapes=[pltpu.VMEM((D_MODEL, n), BF16)],
        compiler_params=_cparams(("arbitrary",)),
        name="norm_matmul_rope" if rope else "norm_matmul",
    )(*args)


def _mmres_kernel(a_ref, w_ref, x_ref, gate_ref, o_ref, w_scr):
    @pl.when(pl.program_id(0) == 0)
    def _():
        w_scr[...] = w_ref[0].astype(BF16)

    a = a_ref[...]
    gate = gate_ref[0]
    for j in range(w_scr.shape[1] // MM_COLS):
        cols = slice(j * MM_COLS, (j + 1) * MM_COLS)
        o_ref[:, cols] = x_ref[:, cols] + gate[:, cols] * _dot(a, w_scr[:, cols])


def _matmul_residual(a, w, widx, x, mods, layer, which_gate, *, tm=512):
    k = a.shape[1]
    n = w.shape[2]
    return pl.pallas_call(
        _mmres_kernel,
        grid=(N_TOK // tm,),
        in_specs=[
            pl.BlockSpec((tm, k), lambda i: (i, 0)),
            pl.BlockSpec((1, k, n), lambda i: (widx, 0, 0), pipeline_mode=pl.Buffered(1)),
            pl.BlockSpec((tm, n), lambda i: (i, 0)),
            pl.BlockSpec((1, 1, n), _mod_index(layer, which_gate, tm)),
        ],
        out_specs=pl.BlockSpec((tm, n), lambda i: (i, 0)),
        out_shape=jax.ShapeDtypeStruct((N_TOK, n), F32),
        scratch_shapes=[pltpu.VMEM((k, n), BF16)],
        compiler_params=_cparams(("arbitrary",)),
        name="matmul_residual",
    )(a, w, x, mods)


def _filter_kernel(feats_ref, win_ref, alt_ref, w1_ref, b1_ref, w2_ref, b2_ref, fq_ref,
                   w3f_ref, b3f_ref, w3b_ref, b3b_ref, fhi_ref, flo_ref, o_ref, *, L):
    def dense(a, w_ref, b_ref):
        a_hi, a_lo = _split_bf16(a)
        w_hi, w_lo = _split_bf16(w_ref[...])
        return _dot3(a_hi, a_lo, w_hi, w_lo) + b_ref[...]

    fq = fq_ref[...]
    h = jnp.sin(fq[0:1, :] * dense(feats_ref[...], w1_ref, b1_ref))
    h = jnp.sin(fq[1:2, :] * dense(h, w2_ref, b2_ref))
    win = win_ref[...]
    hf = dense(h, w3f_ref, b3f_ref) * win
    hb = dense(h, w3b_ref, b3b_ref) * win
    row = lax.broadcasted_iota(jnp.int32, hf.shape, 0)
    hb = jnp.where(row == 0, 0.0, hb)
    s_hi, s_lo = _split_bf16(hf + hb)
    d_hi, d_lo = _split_bf16(hf - hb)
    kr = _dot3(fhi_ref[0:L, :], flo_ref[0:L, :], s_hi, s_lo)
    ks = _dot3(fhi_ref[L:2 * L, :], flo_ref[L:2 * L, :], d_hi, d_lo)
    k_nyq = jnp.sum((hf + hb) * alt_ref[:, 0:1], axis=0, keepdims=True)
    o_ref[0, 0] = kr
    o_ref[0, 1] = jnp.where(row == 0, k_nyq, kr)
    o_ref[0, 2] = jnp.where(row == 0, 0.0, ks)


def _filter_spectra(L, w1, b1, w2, b2, w3, b3, freq, f_hi, f_lo, *, dt=256):
    feats, window, alt = _filter_feats(L)
    fh = HY_FILTER_HIDDEN
    nd = D_MODEL // dt
    w1p = jnp.pad(w1, ((0, LANES - HY_EMB), (0, 0)))
    const = lambda shape: pl.BlockSpec(shape, lambda o, c: tuple(0 for _ in shape))
    return pl.pallas_call(
        functools.partial(_filter_kernel, L=L),
        grid=(2, nd),
        in_specs=[
            const((L, LANES)),
            pl.BlockSpec((L, dt), lambda o, c: (0, c)),
            const((L, LANES)),
            const((LANES, fh)), const((1, fh)), const((fh, fh)), const((1, fh)), const((2, fh)),
            pl.BlockSpec((fh, dt), lambda o, c: (0, (2 * o) * nd + c)),
            pl.BlockSpec((1, dt), lambda o, c: (0, (2 * o) * nd + c)),
            pl.BlockSpec((fh, dt), lambda o, c: (0, (2 * o + 1) * nd + c)),
            pl.BlockSpec((1, dt), lambda o, c: (0, (2 * o + 1) * nd + c)),
            const((2 * L, L)), const((2 * L, L)),
        ],
        out_specs=pl.BlockSpec((1, 3, L, dt), lambda o, c: (o, 0, 0, c)),
        out_shape=jax.ShapeDtypeStruct((2, 3, L, D_MODEL), F32),
        compiler_params=_cparams(("arbitrary", "arbitrary")),
        name=f"hyena_filter_spectra_{L}",
    )(jnp.asarray(feats), jnp.asarray(window), jnp.asarray(alt), w1p, b1.reshape(1, fh), w2,
      b2.reshape(1, fh), freq, w3, b3.reshape(1, -1), w3, b3.reshape(1, -1), f_hi, f_lo)


def _conv_kernel(v_ref, x1_ref, x2_ref, cwv_ref, cw1_ref, cw2_ref, cbv_ref, cb1_ref, cb2_ref,
                 f_ref, finv_ref, kc_ref, bias_ref, o_ref, *, L):
    row = lax.broadcasted_iota(jnp.int32, v_ref.shape, 0)

    def short_conv(u_ref, w_ref, b_ref):
        u = u_ref[...].astype(F32)
        prev = jnp.where(row == 0, 0.0, pltpu.roll(u, 1, 0))
        nxt = jnp.where(row == L - 1, 0.0, pltpu.roll(u, L - 1, 0))
        w = w_ref[...]
        return prev * w[0:1, :] + u * w[1:2, :] + nxt * w[2:3, :] + b_ref[...]

    def long_conv(u, order):
        spec = _dot(f_ref[...], u.astype(BF16))
        a = spec[0:L, :]
        b = spec[L:2 * L, :]
        kra = kc_ref[order, 0]
        krb = kc_ref[order, 1]
        ks = kc_ref[order, 2]
        prod = jnp.concatenate([a * kra - b * ks, a * ks + b * krb], axis=0)
        return _dot(finv_ref[...], prod.astype(BF16)) + u * bias_ref[order:order + 1, :]

    v = short_conv(v_ref, cwv_ref, cbv_ref)
    y = short_conv(x1_ref, cw1_ref, cb1_ref) * long_conv(v, 0)
    y = short_conv(x2_ref, cw2_ref, cb2_ref) * long_conv(y, 1)
    o_ref[...] = y.astype(o_ref.dtype)


def _hyena_conv(z, conv_w, conv_b, f_bf, finv_bf, kc, bias, *, L, row_block0, n_seq, dt):
    nd = D_MODEL // dt
    cb = conv_b.reshape(1, 3 * D_MODEL)
    seg = lambda s: pl.BlockSpec((L, dt), lambda c, b: (row_block0 + b, s * nd + c))
    cw = lambda s: pl.BlockSpec((3, dt), lambda c, b: (0, s * nd + c))
    cbs = lambda s: pl.BlockSpec((1, dt), lambda c, b: (0, s * nd + c))
    return pl.pallas_call(
        functools.partial(_conv_kernel, L=L),
        grid=(nd, n_seq),
        in_specs=[
            seg(0), seg(1), seg(2), cw(0), cw(1), cw(2), cbs(0), cbs(1), cbs(2),
            pl.BlockSpec((2 * L, L), lambda c, b: (0, 0)),
            pl.BlockSpec((L, 2 * L), lambda c, b: (0, 0)),
            pl.BlockSpec((2, 3, L, dt), lambda c, b: (0, 0, 0, c)),
            pl.BlockSpec((2, dt), lambda c, b: (0, c)),
        ],
        out_specs=pl.BlockSpec((L, dt), lambda c, b: (b, c)),
        out_shape=jax.ShapeDtypeStruct((n_seq * L, D_MODEL), BF16),
        compiler_params=_cparams(("arbitrary", "arbitrary")),
        name=f"hyena_conv_{L}",
    )(z, z, z, conv_w, conv_w, conv_w, cb, cb, cb, f_bf, finv_bf, kc, bias)


def _attn_kernel(lq1_ref, lk1_ref, lq2_ref, lk2_ref, g_ref, q_ref, k_ref, v_ref, *rest,
                 lam_init, has_ctx, tq, nq):
    if has_ctx:
        ck_ref, cv_ref, o_ref = rest
    else:
        (o_ref,) = rest
    lam = (jnp.exp(jnp.sum(lq1_ref[...] * lk1_ref[...], axis=-1, keepdims=True))
           - jnp.exp(jnp.sum(lq2_ref[...] * lk2_ref[...], axis=-1, keepdims=True)) + lam_init)
    first_map = lax.broadcasted_iota(jnp.int32, (tq, HEAD_W), 1) < HEAD_DIM
    nt = (((1,), (1,)), ((), ()))

    def q_block(rows):
        for h in range(N_HEADS):
            cols = slice(h * HEAD_W, (h + 1) * HEAD_W)
            q = q_ref[rows, cols] * (HEAD_DIM ** -0.5)
            zero = jnp.zeros_like(q)
            q2 = jnp.concatenate([jnp.where(first_map, q, zero), jnp.where(first_map, zero, q)], axis=0)
            s = lax.dot_general(q2, k_ref[:, cols], nt, preferred_element_type=F32)
            m = jnp.max(s, axis=-1, keepdims=True)
            if has_ctx:
                sc = lax.dot_general(q2, ck_ref[0, 0, :, cols].astype(BF16), nt,
                                     preferred_element_type=F32)
                m = jnp.maximum(m, jnp.max(sc, axis=-1, keepdims=True))
                ec = jnp.exp(sc - m)
            e = jnp.exp(s - m)
            den = jnp.sum(e, axis=-1, keepdims=True)
            if has_ctx:
                den = den + jnp.sum(ec, axis=-1, keepdims=True)
            den1 = den[0:tq]
            rho = lam * den1 / den[tq:2 * tq]
            o = _dot((e[0:tq] - rho * e[tq:2 * tq]).astype(BF16), v_ref[:, cols])
            if has_ctx:
                o = o + _dot((ec[0:tq] - rho * ec[tq:2 * tq]).astype(BF16),
                             cv_ref[0, 0, :, cols].astype(BF16))
            o = o * (1.0 / den1)
            ms = jnp.mean(o * o, axis=-1, keepdims=True)
            o = (o * lax.rsqrt(ms + EPS)) * g_ref[...] * (1.0 - lam_init)
            o_ref[rows, cols] = o.astype(o_ref.dtype)

    if nq == 1:
        q_block(slice(0, tq))
    else:
        def body(qi, carry):
            q_block(pl.ds(pl.multiple_of(qi * tq, tq), tq))
            return carry
        lax.fori_loop(0, nq, body, 0)


def _attention(qkv, lam_params, subln_g, lam_init, *, L, row_block0, n_seq, tq, ctx=None, j=0):
    small = pl.BlockSpec((1, HEAD_DIM), lambda b: (0, 0))
    in_specs = [small, small, small, small,
                pl.BlockSpec((1, HEAD_W), lambda b: (0, 0)),
                pl.BlockSpec((L, D_MODEL), lambda b: (row_block0 + b, 0)),
                pl.BlockSpec((L, D_MODEL), lambda b: (row_block0 + b, 1)),
                pl.BlockSpec((L, D_MODEL), lambda b: (row_block0 + b, 2))]
    args = [p.reshape(1, HEAD_DIM) for p in lam_params] + [subln_g.reshape(1, HEAD_W), qkv, qkv, qkv]
    if ctx is not None:
        ctx_spec = pl.BlockSpec((1, 1, PAST_LEN, D_MODEL), lambda b: (b, j, 0, 0))
        in_specs += [ctx_spec, ctx_spec]
        args += list(ctx)
    return pl.pallas_call(
        functools.partial(_attn_kernel, lam_init=lam_init, has_ctx=ctx is not None, tq=tq, nq=L // tq),
        grid=(n_seq,),
        in_specs=in_specs,
        out_specs=pl.BlockSpec((L, D_MODEL), lambda b: (b, 0)),
        out_shape=jax.ShapeDtypeStruct((n_seq * L, D_MODEL), BF16),
        compiler_params=_cparams(("arbitrary",)),
        name="diff_attention_ctx" if ctx is not None else "diff_attention",
    )(*args)


def _row_slots(pos, sel):
    big = float(BLOCK_ROWS)
    pos_a = jnp.max(pos, axis=-1, keepdims=True)
    pos_b = jnp.min(jnp.where(sel, pos, big), axis=-1, keepdims=True)
    return pos_a, pos_b


def _router_kernel(x_ref, g_ref, sh_ref, sc_ref, wr_ref, br_ref, xb_ref, pos_ref, gate_ref, cnt_ref,
                   *, tm):
    h = _modulate(x_ref[...], g_ref[...], sc_ref[0], sh_ref[0]).astype(BF16)
    logits = _dot(h, wr_ref[...].astype(BF16)) + br_ref[...]
    lane = lax.broadcasted_iota(jnp.int32, (tm, LANES), 1)
    neg = -jnp.inf
    is_grp = jnp.logical_and(lane >= N_EXPERTS, lane < N_EXPERTS + N_GROUPS)
    lg = jnp.where(is_grp, logits, neg)
    mg = jnp.max(lg, axis=-1, keepdims=True)
    g_gate = 1.0 / jnp.sum(jnp.exp(lg - mg), axis=-1, keepdims=True)
    g_idx = jnp.min(jnp.where(lg == mg, lane - N_EXPERTS, N_GROUPS), axis=-1, keepdims=True)
    in_grp = jnp.logical_and(lane < N_EXPERTS, lane // EXPERTS_PER_GROUP == g_idx)
    le = jnp.where(in_grp, logits, neg)
    m1 = jnp.max(le, axis=-1, keepdims=True)
    i1 = jnp.min(jnp.where(le == m1, lane, LANES), axis=-1, keepdims=True)
    le2 = jnp.where(lane == i1, neg, le)
    m2 = jnp.max(le2, axis=-1, keepdims=True)
    i2 = jnp.min(jnp.where(le2 == m2, lane, LANES), axis=-1, keepdims=True)
    e2 = jnp.exp(m2 - m1)
    inv = 1.0 / (1.0 + e2)
    gate_ref[...] = jnp.where(lane == i1, g_gate * inv, jnp.where(lane == i2, g_gate * (e2 * inv), 0.0))
    sel = jnp.logical_or(lane == i1, lane == i2)
    onehot = jnp.where(sel, 1.0, 0.0)
    r = lax.broadcasted_iota(jnp.int32, (tm, tm), 0)
    c = lax.broadcasted_iota(jnp.int32, (tm, tm), 1)
    rank = _dot(jnp.where(c < r, 1.0, 0.0).astype(BF16), onehot.astype(BF16))
    count = jnp.sum(onehot, axis=0, keepdims=True)
    cnt_ref[0] = count.astype(jnp.int32)
    padded = jnp.ceil(count * (1.0 / CHUNK)) * CHUNK
    er = lax.broadcasted_iota(jnp.int32, (LANES, LANES), 0)
    ec = lax.broadcasted_iota(jnp.int32, (LANES, LANES), 1)
    before = jnp.where(er < ec, 1.0, 0.0).astype(BF16)
    start = _dot(jnp.broadcast_to(padded, (SUBLANES, LANES)).astype(BF16), before)[0:1, :]
    pos = jnp.where(sel, start + rank, -1.0)
    pos_ref[...] = pos
    pos_a, pos_b = _row_slots(pos, sel)
    slot = lax.broadcasted_iota(jnp.int32, (tm, BLOCK_ROWS), 1).astype(F32)
    pick = jnp.where(jnp.logical_or(slot == pos_a, slot == pos_b), 1.0, 0.0).astype(BF16)
    xb_ref[...] = lax.dot_general(pick, h, (((0,), (0,)), ((), ())),
                                  preferred_element_type=F32).astype(BF16)


def _router(x, g, mods, layer, wr, br):
    tm = MOE_BLOCK
    return pl.pallas_call(
        functools.partial(_router_kernel, tm=tm),
        grid=(N_BLOCKS,),
        in_specs=[
            pl.BlockSpec((tm, D_MODEL), lambda i: (i, 0)),
            pl.BlockSpec((1, D_MODEL), lambda i: (0, 0)),
            pl.BlockSpec((1, 1, D_MODEL), _mod_index(layer, 3, tm)),
            pl.BlockSpec((1, 1, D_MODEL), _mod_index(layer, 4, tm)),
            pl.BlockSpec((D_MODEL, LANES), lambda i: (0, 0)),
            pl.BlockSpec((1, LANES), lambda i: (0, 0)),
        ],
        out_specs=[
            pl.BlockSpec((BLOCK_ROWS, D_MODEL), lambda i: (i, 0)),
            pl.BlockSpec((tm, LANES), lambda i: (i, 0)),
            pl.BlockSpec((tm, LANES), lambda i: (i, 0)),
            pl.BlockSpec((1, 1, LANES), lambda i: (i, 0, 0)),
        ],
        out_shape=[
            jax.ShapeDtypeStruct((N_BLOCKS * BLOCK_ROWS, D_MODEL), BF16),
            jax.ShapeDtypeStruct((N_TOK, LANES), F32),
            jax.ShapeDtypeStruct((N_TOK, LANES), F32),
            jax.ShapeDtypeStruct((N_BLOCKS, 1, LANES), jnp.int32),
        ],
        compiler_params=_cparams(("arbitrary",)),
        name="moe_router",
    )(x, g.reshape(1, D_MODEL), mods, mods, wr, br)


def _tables_kernel(cnt_ref, te_ref, nu_ref, disp_ref, comb_ref, cur_ref):
    def fill(ref, n, value):
        def body(i, carry):
            ref[i] = value
            return carry
        lax.fori_loop(0, n, body, 0)

    fill(disp_ref, N_TILES * TILE_CHUNKS, -1)
    fill(comb_ref, N_BLOCKS * BLOCK_CHUNKS, -1)
    fill(te_ref, N_TILES, N_EXPERTS - 1)
    fill(cur_ref, N_BLOCKS, 0)

    def expert_body(e, tile_cursor):
        q0 = tile_cursor * TILE_CHUNKS

        def block_body(b, q):
            m = (cnt_ref[b, e] + (CHUNK - 1)) // CHUNK
            cur = cur_ref[b]
            first = b * BLOCK_CHUNKS + cur

            def chunk_body(k, carry):
                disp_ref[q + k] = (first + k) * CHUNK
                comb_ref[first + k] = (q + k) * CHUNK
                return carry

            lax.fori_loop(0, m, chunk_body, 0)
            cur_ref[b] = cur + m
            return q + m

        q_end = lax.fori_loop(0, N_BLOCKS, block_body, q0)
        tiles = (q_end - q0 + (TILE_CHUNKS - 1)) // TILE_CHUNKS

        def tile_body(t, carry):
            te_ref[tile_cursor + t] = e
            return carry

        lax.fori_loop(0, tiles, tile_body, 0)
        return tile_cursor + tiles

    nu_ref[0] = lax.fori_loop(0, N_EXPERTS, expert_body, 0)


def _chunk_tables(counts):
    smem = pl.BlockSpec(memory_space=pltpu.SMEM)
    return pl.pallas_call(
        _tables_kernel,
        in_specs=[smem],
        out_specs=[smem, smem, smem, smem],
        out_shape=[
            jax.ShapeDtypeStruct((N_TILES,), jnp.int32),
            jax.ShapeDtypeStruct((1,), jnp.int32),
            jax.ShapeDtypeStruct((N_TILES * TILE_CHUNKS,), jnp.int32),
            jax.ShapeDtypeStruct((N_BLOCKS * BLOCK_CHUNKS,), jnp.int32),
        ],
        scratch_shapes=[pltpu.SMEM((N_BLOCKS,), jnp.int32)],
        name="moe_chunk_tables",
    )(counts.reshape(N_BLOCKS, LANES))


def _chunk_copies(table_ref, base, n_chunks, src_hbm, dst, sem, wait):
    for c in range(n_chunks):
        row = table_ref[base + c]

        @pl.when(row >= 0)
        def _():
            cp = pltpu.make_async_copy(src_hbm.at[pl.ds(pl.multiple_of(row, CHUNK), CHUNK)],
                                       dst.at[pl.ds(c * CHUNK, CHUNK)], sem)
            if wait:
                cp.wait()
            else:
                cp.start()


def _expert_kernel(te_ref, nu_ref, src_ref, xb_hbm, w1_ref, w3_ref, w2_ref, y_ref, xbuf, sem):
    i = pl.program_id(0)
    n_used = nu_ref[0]
    slot = i % 2

    def fetch(tile, s, wait):
        _chunk_copies(src_ref, tile * TILE_CHUNKS, TILE_CHUNKS, xb_hbm, xbuf.at[s], sem.at[s], wait)

    @pl.when(i == 0)
    def _():
        xbuf[...] = jnp.zeros_like(xbuf)
        fetch(0, 0, False)

    @pl.when(i + 1 < n_used)
    def _():
        fetch(i + 1, 1 - slot, False)

    @pl.when(i < n_used)
    def _():
        fetch(i, slot, True)
        x = xbuf[slot]
        a = _dot(x, w1_ref[0, 0].astype(BF16))
        b = _dot(x, w3_ref[0, 0].astype(BF16))
        hid = (a * jax.nn.sigmoid(a)) * b
        y_ref[...] = _dot(hid.astype(BF16), w2_ref[0, 0].astype(BF16)).astype(y_ref.dtype)

    @pl.when(i >= n_used)
    def _():
        y_ref[...] = jnp.zeros_like(y_ref)


def _experts(tile_expert, n_used, disp_src, xb, w1, w3, w2, layer):
    wsel = lambda i, te, nu, src: (layer, te[jnp.minimum(i, nu[0] - 1)], 0, 0)
    return pl.pallas_call(
        _expert_kernel,
        grid_spec=pltpu.PrefetchScalarGridSpec(
            num_scalar_prefetch=3,
            grid=(N_TILES,),
            in_specs=[
                pl.BlockSpec(memory_space=pl.ANY),
                pl.BlockSpec((1, 1, D_MODEL, D_EXPERT), wsel),
                pl.BlockSpec((1, 1, D_MODEL, D_EXPERT), wsel),
                pl.BlockSpec((1, 1, D_EXPERT, D_MODEL), wsel),
            ],
            out_specs=pl.BlockSpec((EXPERT_TILE, D_MODEL), lambda i, te, nu, src: (i, 0)),
            scratch_shapes=[pltpu.VMEM((2, EXPERT_TILE, D_MODEL), BF16),
                            pltpu.SemaphoreType.DMA((2,))],
        ),
        out_shape=jax.ShapeDtypeStruct((N_TILES * EXPERT_TILE, D_MODEL), BF16),
        compiler_params=_cparams(("arbitrary",)),
        name="moe_experts",
    )(tile_expert, n_used, disp_src, xb, w1, w3, w2)


def _combine_kernel(src_ref, y_hbm, x_ref, pos_ref, gate_ref, mg_ref, o_ref, ybuf, sem, *, tm):
    i = pl.program_id(0)
    slot = i % 2

    def fetch(blk, s, wait):
        _chunk_copies(src_ref, blk * BLOCK_CHUNKS, BLOCK_CHUNKS, y_hbm, ybuf.at[s], sem.at[s], wait)

    @pl.when(i == 0)
    def _():
        ybuf[...] = jnp.zeros_like(ybuf)
        fetch(0, 0, False)

    @pl.when(i + 1 < N_BLOCKS)
    def _():
        fetch(i + 1, 1 - slot, False)

    fetch(i, slot, True)
    pos = pos_ref[...]
    sel = pos >= 0.0
    gate = gate_ref[...]
    pos_a, pos_b = _row_slots(pos, sel)
    gate_a = jnp.sum(jnp.where(pos == pos_a, gate, 0.0), axis=-1, keepdims=True)
    gate_b = jnp.sum(jnp.where(jnp.logical_and(sel, pos == pos_b), gate, 0.0), axis=-1, keepdims=True)
    row = lax.broadcasted_iota(jnp.int32, (tm, BLOCK_ROWS), 1).astype(F32)
    weights = jnp.where(row == pos_a, gate_a, jnp.where(row == pos_b, gate_b, 0.0))
    moe = _dot(weights.astype(BF16), ybuf[slot])
    o_ref[...] = x_ref[...] + mg_ref[0] * moe


def _combine(comb_src, y, x, pos, gate, mods, layer):
    tm = MOE_BLOCK

    def gate_idx(i, s):
        return ((layer * MOD_ROWS + _mod_row(i, tm)) * 6 + 5, 0, 0)

    return pl.pallas_call(
        functools.partial(_combine_kernel, tm=tm),
        grid_spec=pltpu.PrefetchScalarGridSpec(
            num_scalar_prefetch=1,
            grid=(N_BLOCKS,),
            in_specs=[
                pl.BlockSpec(memory_space=pl.ANY),
                pl.BlockSpec((tm, D_MODEL), lambda i, s: (i, 0)),
                pl.BlockSpec((tm, LANES), lambda i, s: (i, 0)),
                pl.BlockSpec((tm, LANES), lambda i, s: (i, 0)),
                pl.BlockSpec((1, 1, D_MODEL), gate_idx),
            ],
            out_specs=pl.BlockSpec((tm, D_MODEL), lambda i, s: (i, 0)),
            scratch_shapes=[pltpu.VMEM((2, BLOCK_ROWS, D_MODEL), BF16),
                            pltpu.SemaphoreType.DMA((2,))],
        ),
        out_shape=jax.ShapeDtypeStruct((N_TOK, D_MODEL), F32),
        compiler_params=_cparams(("arbitrary",)),
        name="moe_combine",
    )(comb_src, y, x, pos, gate, mods)


def _moe(x, g, mods, layer, w_group, b_group, w_expert, b_expert, w1, w3, w2):
    pad = LANES - N_EXPERTS - N_GROUPS
    wr = jnp.concatenate([w_expert, w_group, jnp.zeros((D_MODEL, pad), F32)], axis=1)
    br = jnp.concatenate([b_expert, b_group, jnp.zeros((pad,), F32)]).reshape(1, LANES)
    xb, pos, gate, counts = _router(x, g, mods, layer, wr, br)
    tile_expert, n_used, disp_src, comb_src = _chunk_tables(counts)
    y = _experts(tile_expert, n_used, disp_src, xb, w1, w3, w2, layer)
    return _combine(comb_src, y, x, pos, gate, mods, layer)


def _final_kernel(x_ref, g_ref, o_ref):
    x = x_ref[...]
    ms = jnp.mean(x * x, axis=-1, keepdims=True)
    o_ref[...] = (x * lax.rsqrt(ms + EPS)) * g_ref[...]


def _final_norm(x, g, *, tm=1024):
    return pl.pallas_call(
        _final_kernel,
        grid=(N_TOK // tm,),
        in_specs=[pl.BlockSpec((tm, D_MODEL), lambda i: (i, 0)),
                  pl.BlockSpec((1, D_MODEL), lambda i: (0, 0))],
        out_specs=pl.BlockSpec((tm, D_MODEL), lambda i: (i, 0)),
        out_shape=jax.ShapeDtypeStruct((N_TOK, D_MODEL), F32),
        compiler_params=_cparams(("arbitrary",)),
        name="final_norm",
    )(x, g.reshape(1, D_MODEL))


def kernel(x_prompt, x_sample, cache_k, cache_v, c, c_ctx, ada_w, ada_b, norm1_g, norm2_g, final_g, hy_w_in, hy_conv_w, hy_conv_b, hy_f_w1, hy_f_b1, hy_f_w2, hy_f_b2, hy_f_w3, hy_f_b3, hy_f_freq, hy_bias, hy_w_out, at_w_qkv, at_lam_q1, at_lam_k1, at_lam_q2, at_lam_k2, at_subln_g, at_w_o, moe_w_group, moe_b_group, moe_w_expert, moe_b_expert, moe_w1, moe_w3, moe_w2):
    x = jnp.concatenate([x_prompt.reshape(N_PROMPT, D_MODEL), x_sample.reshape(N_SAMPLE, D_MODEL)], axis=0)
    cond = jnp.concatenate([c_ctx[None, :], c, jnp.zeros((MOD_ROWS - 1 - DEC_BATCH, D_MODEL), F32)], axis=0)
    mods = _ada_table(cond, ada_w, ada_b).reshape(DEPTH * MOD_ROWS * 6, 1, D_MODEL)
    ctx_k = cache_k.reshape(DEC_BATCH, DEPTH // 2, PAST_LEN, D_MODEL)
    ctx_v = cache_v.reshape(DEC_BATCH, DEPTH // 2, PAST_LEN, D_MODEL)

    dft = {}
    for L in (SEQ, DEC_SEQ):
        fwd, inv = _dft_mats(L)
        fwd = jnp.asarray(fwd)
        f_hi, f_lo = _split_bf16(fwd)
        dft[L] = (f_hi, f_lo, jnp.asarray(inv).astype(BF16))

    new_k, new_v = [], []
    for l in range(DEPTH):
        j = l // 2
        if l % 2 == 0:
            z = _norm_matmul(x, norm1_g[l], mods, l, 0, 1, hy_w_in, j)
            ys = []
            for L, rb0, n_seq, dt in ((SEQ, 0, BATCH, 512), (DEC_SEQ, N_PROMPT // DEC_SEQ, DEC_BATCH, 256)):
                f_hi, f_lo, finv = dft[L]
                kc = _filter_spectra(L, hy_f_w1[j], hy_f_b1[j], hy_f_w2[j], hy_f_b2[j], hy_f_w3[j],
                                     hy_f_b3[j], hy_f_freq[j], f_hi, f_lo)
                ys.append(_hyena_conv(z, hy_conv_w[j], hy_conv_b[j], f_hi, finv, kc, hy_bias[j],
                                      L=L, row_block0=rb0, n_seq=n_seq, dt=dt))
            y = jnp.concatenate(ys, axis=0)
            x = _matmul_residual(y, hy_w_out, j, x, mods, l, 2)
        else:
            lam_init = 0.8 - 0.6 * math.exp(-0.3 * l)
            qkv = _norm_matmul(x, norm1_g[l], mods, l, 0, 1, at_w_qkv, j, rope=True)
            lam_params = (at_lam_q1[j], at_lam_k1[j], at_lam_q2[j], at_lam_k2[j])
            op = _attention(qkv, lam_params, at_subln_g[j], lam_init, L=SEQ, row_block0=0,
                            n_seq=BATCH, tq=SEQ)
            os_ = _attention(qkv, lam_params, at_subln_g[j], lam_init, L=DEC_SEQ,
                             row_block0=N_PROMPT // DEC_SEQ, n_seq=DEC_BATCH, tq=256,
                             ctx=(ctx_k, ctx_v), j=j)
            o = jnp.concatenate([op, os_], axis=0)
            new_k.append(qkv[:N_PROMPT, D_MODEL:2 * D_MODEL].astype(F32).reshape(BATCH, SEQ, N_HEADS, HEAD_W))
            new_v.append(qkv[:N_PROMPT, 2 * D_MODEL:].astype(F32).reshape(BATCH, SEQ, N_HEADS, HEAD_W))
            x = _matmul_residual(o, at_w_o, j, x, mods, l, 2)
        x = _moe(x, norm2_g[l], mods, l, moe_w_group[l], moe_b_group[l], moe_w_expert[l],
                 moe_b_expert[l], moe_w1, moe_w3, moe_w2)
    out = _final_norm(x, final_g)
    y_prompt = out[:N_PROMPT].reshape(BATCH, SEQ, D_MODEL)
    y_sample = out[N_PROMPT:].reshape(DEC_BATCH, DEC_SEQ, D_MODEL)
    return (y_prompt, y_sample, jnp.stack(new_k, axis=1), jnp.stack(new_v, axis=1))
```

```python
import functools
import math

import numpy as np
import jax
import jax.numpy as jnp
from jax import lax
from jax.experimental import pallas as pl
from jax.experimental.pallas import tpu as pltpu

F32 = jnp.float32
BF16 = jnp.bfloat16

D_MODEL = 1024
BATCH = 16
SEQ = 256
DEPTH = 4
DEC_BATCH = 4
DEC_SEQ = 1024
PAST_LEN = 256
GRID_W = 64
EPS = 1e-6
HY_BANDS = 16
HY_EMB = 1 + 2 * HY_BANDS
HY_FILTER_HIDDEN = 64
HY_FAST_DECAY = 0.3
HY_SLOW_DECAY = 1.5
HY_TARGET = 1e-2
N_HEADS = 8
HEAD_DIM = 64
ROT_AXIS = HEAD_DIM // 2
ROPE_BASE = 10000.0
N_GROUPS = 4
EXPERTS_PER_GROUP = 8
N_EXPERTS = N_GROUPS * EXPERTS_PER_GROUP
D_EXPERT = D_MODEL // 4

N_PROMPT = BATCH * SEQ
N_SAMPLE = DEC_BATCH * DEC_SEQ
N_TOK = N_PROMPT + N_SAMPLE
MOD_ROWS = 8
LANES = 128
SUBLANES = 8
HEAD_W = 2 * HEAD_DIM
VMEM_LIMIT = 56 * 1024 * 1024

MOE_BLOCK = 256
N_BLOCKS = N_TOK // MOE_BLOCK
CHUNK = 2 * SUBLANES
BLOCK_ROWS = 2 * MOE_BLOCK + N_EXPERTS * CHUNK
BLOCK_CHUNKS = BLOCK_ROWS // CHUNK
EXPERT_TILE = 256
TILE_CHUNKS = EXPERT_TILE // CHUNK
N_TILES = (2 * N_TOK + N_BLOCKS * N_EXPERTS * (CHUNK - 1)) // EXPERT_TILE + N_EXPERTS


def _cparams(sem):
    return pltpu.CompilerParams(dimension_semantics=sem, vmem_limit_bytes=VMEM_LIMIT)


def _mod_row(blk, tm):
    start = blk * tm
    return jnp.where(start < N_PROMPT, 0, 1 + (start - N_PROMPT) // DEC_SEQ)


def _mod_index(layer, which, tm):
    def index_map(i, *_):
        return ((layer * MOD_ROWS + _mod_row(i, tm)) * 6 + which, 0, 0)
    return index_map


@functools.lru_cache(maxsize=None)
def _dft_mats(L):
    k = np.arange(L, dtype=np.float64)[:, None]
    j = np.arange(L, dtype=np.float64)[None, :]
    ang = np.pi * ((k * j) % (2 * L)) / L
    c = np.cos(ang)
    s = np.sin(ang)
    s[0, :] = np.where(np.arange(L) % 2 == 0, 1.0, -1.0)
    fwd = np.concatenate([c, s], axis=0)
    scale = np.full((2 * L,), 1.0 / L)
    scale[0] = scale[L] = 0.5 / L
    inv = fwd.T * scale[None, :]
    return fwd.astype(np.float32), inv.astype(np.float32)


@functools.lru_cache(maxsize=None)
def _filter_feats(L):
    pos = np.arange(L, dtype=np.float64)
    t = pos / (L - 1)
    bands = np.linspace(1e-4, HY_BANDS - 1, HY_BANDS)
    ang = (2.0 * math.pi / L) * pos[:, None] * bands[None, :]
    feats = np.concatenate([t[:, None], np.cos(ang), -np.sin(ang)], axis=-1)
    feats = np.pad(feats, ((0, 0), (0, LANES - HY_EMB)))
    deltas = np.abs(np.linspace(math.log(HY_TARGET) / HY_SLOW_DECAY,
                                math.log(HY_TARGET) / HY_FAST_DECAY, D_MODEL))
    window = np.exp(-t[:, None] * deltas[None, :])
    alt = np.where(np.arange(L) % 2 == 0, 1.0, -1.0)[:, None] * np.ones((1, LANES))
    return feats.astype(np.float32), window.astype(np.float32), alt.astype(np.float32)


@functools.lru_cache(maxsize=None)
def _rope_tables():
    pos = np.arange(DEC_SEQ)
    row = (pos // GRID_W).astype(np.float64)
    col = (pos % GRID_W).astype(np.float64)
    lane = np.arange(HEAD_W)
    d = lane % HEAD_DIM
    axis = d // ROT_AXIS
    n = d % ROT_AXIS
    half = n // (ROT_AXIS // 2)
    f = n % (ROT_AXIS // 2)
    inv = ROPE_BASE ** (-f.astype(np.float64) / (ROT_AXIS // 2))
    p = np.where(axis[None, :] == 0, row[:, None], col[:, None])
    ang = p * inv[None, :]
    sign = np.where(half == 0, -1.0, 1.0)[None, :]
    return np.cos(ang).astype(np.float32), (np.sin(ang) * sign).astype(np.float32)


def _split_bf16(a):
    hi = a.astype(BF16)
    lo = (a - hi.astype(F32)).astype(BF16)
    return hi, lo


def _dot(a, b):
    return jnp.dot(a, b, preferred_element_type=F32)


def _dot3(a_hi, a_lo, b_hi, b_lo):
    return _dot(a_hi, b_hi) + (_dot(a_hi, b_lo) + _dot(a_lo, b_hi))


def _ada_kernel(c_ref, w_ref, b_ref, o_ref):
    c = c_ref[...]
    s = c * jax.nn.sigmoid(c)
    o_ref[0] = _dot(s.astype(BF16), w_ref[0].astype(BF16)) + b_ref[0]


def _ada_table(cond, ada_w, ada_b):
    tn = 1536
    n = 6 * D_MODEL
    return pl.pallas_call(
        _ada_kernel,
        grid=(DEPTH, n // tn),
        in_specs=[
            pl.BlockSpec((MOD_ROWS, D_MODEL), lambda l, j: (0, 0)),
            pl.BlockSpec((1, D_MODEL, tn), lambda l, j: (l, 0, j)),
            pl.BlockSpec((1, 1, tn), lambda l, j: (l, 0, j)),
        ],
        out_specs=pl.BlockSpec((1, MOD_ROWS, tn), lambda l, j: (l, 0, j)),
        out_shape=jax.ShapeDtypeStruct((DEPTH, MOD_ROWS, n), F32),
        compiler_params=_cparams(("arbitrary", "arbitrary")),
        name="ada_table",
    )(cond, ada_w, ada_b.reshape(DEPTH, 1, n))


def _modulate(x, g, scale, shift):
    ms = jnp.mean(x * x, axis=-1, keepdims=True)
    y = (x * lax.rsqrt(ms + EPS)) * g
    return y * (1.0 + scale) + shift


MM_COLS = 512


def _normmm_kernel(x_ref, g_ref, sh_ref, sc_ref, w_ref, *rest, rope, tm):
    if rope:
        cos_ref, sin_ref, o_ref, w_scr = rest
    else:
        o_ref, w_scr = rest
    i = pl.program_id(0)

    @pl.when(i == 0)
    def _():
        w_scr[...] = w_ref[0].astype(BF16)

    h = _modulate(x_ref[...], g_ref[...], sc_ref[0], sh_ref[0]).astype(BF16)
    n = w_scr.shape[1]
    for j in range(n // MM_COLS):
        cols = slice(j * MM_COLS, (j + 1) * MM_COLS)
        acc = _dot(h, w_scr[:, cols])
        if not (rope and j < 2 * D_MODEL // MM_COLS):
            o_ref[:, cols] = acc.astype(o_ref.dtype)
            continue

        @pl.when(i >= N_PROMPT // tm)
        def _():
            reps = MM_COLS // HEAD_W
            cos = jnp.tile(cos_ref[...], (1, reps))
            sin = jnp.tile(sin_ref[...], (1, reps))
            lane = lax.broadcasted_iota(jnp.int32, (tm, MM_COLS), 1)
            first_half = (lane % ROT_AXIS) < (ROT_AXIS // 2)
            partner = jnp.where(first_half,
                                pltpu.roll(acc, MM_COLS - ROT_AXIS // 2, 1),
                                pltpu.roll(acc, ROT_AXIS // 2, 1))
            o_ref[:, cols] = (acc * cos + partner * sin).astype(o_ref.dtype)

        @pl.when(i < N_PROMPT // tm)
        def _():
            o_ref[:, cols] = acc.astype(o_ref.dtype)


def _norm_matmul(x, g, mods, layer, which_shift, which_scale, w, widx, *, rope=False, tm=512):
    n = w.shape[2]
    rope_blocks = DEC_SEQ // tm

    def rope_idx(i):
        return (jnp.maximum(i - N_PROMPT // tm, 0) % rope_blocks, 0)

    in_specs = [
        pl.BlockSpec((tm, D_MODEL), lambda i: (i, 0)),
        pl.BlockSpec((1, D_MODEL), lambda i: (0, 0)),
        pl.BlockSpec((1, 1, D_MODEL), _mod_index(layer, which_shift, tm)),
        pl.BlockSpec((1, 1, D_MODEL), _mod_index(layer, which_scale, tm)),
        pl.BlockSpec((1, D_MODEL, n), lambda i: (widx, 0, 0), pipeline_mode=pl.Buffered(1)),
    ]
    args = [x, g.reshape(1, D_MODEL), mods, mods, w]
    if rope:
        cos, sin = _rope_tables()
        in_specs += [pl.BlockSpec((tm, HEAD_W), rope_idx), pl.BlockSpec((tm, HEAD_W), rope_idx)]
        args += [jnp.asarray(cos), jnp.asarray(sin)]
    return pl.pallas_call(
        functools.partial(_normmm_kernel, rope=rope, tm=tm),
        grid=(N_TOK // tm,),
        in_specs=in_specs,
        out_specs=pl.BlockSpec((tm, n), lambda i: (i, 0)),
        out_shape=jax.ShapeDtypeStruct((N_TOK, n), BF16),
        scratch_shapes=[pltpu.VMEM((D_MODEL, n), BF16)],
        compiler_params=_cparams(("arbitrary",)),
        name="norm_matmul_rope" if rope else "norm_matmul",
    )(*args)


def _mmres_kernel(a_ref, w_ref, x_ref, gate_ref, o_ref, w_scr):
    @pl.when(pl.program_id(0) == 0)
    def _():
        w_scr[...] = w_ref[0].astype(BF16)

    a = a_ref[...]
    gate = gate_ref[0]
    for j in range(w_scr.shape[1] // MM_COLS):
        cols = slice(j * MM_COLS, (j + 1) * MM_COLS)
        o_ref[:, cols] = x_ref[:, cols] + gate[:, cols] * _dot(a, w_scr[:, cols])


def _matmul_residual(a, w, widx, x, mods, layer, which_gate, *, tm=512):
    k = a.shape[1]
    n = w.shape[2]
    return pl.pallas_call(
        _mmres_kernel,
        grid=(N_TOK // tm,),
        in_specs=[
            pl.BlockSpec((tm, k), lambda i: (i, 0)),
            pl.BlockSpec((1, k, n), lambda i: (widx, 0, 0), pipeline_mode=pl.Buffered(1)),
            pl.BlockSpec((tm, n), lambda i: (i, 0)),
            pl.BlockSpec((1, 1, n), _mod_index(layer, which_gate, tm)),
        ],
        out_specs=pl.BlockSpec((tm, n), lambda i: (i, 0)),
        out_shape=jax.ShapeDtypeStruct((N_TOK, n), F32),
        scratch_shapes=[pltpu.VMEM((k, n), BF16)],
        compiler_params=_cparams(("arbitrary",)),
        name="matmul_residual",
    )(a, w, x, mods)


def _filter_kernel(feats_ref, win_ref, alt_ref, w1_ref, b1_ref, w2_ref, b2_ref, fq_ref,
                   w3f_ref, b3f_ref, w3b_ref, b3b_ref, fhi_ref, flo_ref, o_ref, *, L):
    def dense(a, w_ref, b_ref):
        a_hi, a_lo = _split_bf16(a)
        w_hi, w_lo = _split_bf16(w_ref[...])
        return _dot3(a_hi, a_lo, w_hi, w_lo) + b_ref[...]

    fq = fq_ref[...]
    h = jnp.sin(fq[0:1, :] * dense(feats_ref[...], w1_ref, b1_ref))
    h = jnp.sin(fq[1:2, :] * dense(h, w2_ref, b2_ref))
    win = win_ref[...]
    hf = dense(h, w3f_ref, b3f_ref) * win
    hb = dense(h, w3b_ref, b3b_ref) * win
    row = lax.broadcasted_iota(jnp.int32, hf.shape, 0)
    hb = jnp.where(row == 0, 0.0, hb)
    s_hi, s_lo = _split_bf16(hf + hb)
    d_hi, d_lo = _split_bf16(hf - hb)
    kr = _dot3(fhi_ref[0:L, :], flo_ref[0:L, :], s_hi, s_lo)
    ks = _dot3(fhi_ref[L:2 * L, :], flo_ref[L:2 * L, :], d_hi, d_lo)
    k_nyq = jnp.sum((hf + hb) * alt_ref[:, 0:1], axis=0, keepdims=True)
    o_ref[0, 0] = kr
    o_ref[0, 1] = jnp.where(row == 0, k_nyq, kr)
    o_ref[0, 2] = jnp.where(row == 0, 0.0, ks)


def _filter_spectra(L, w1, b1, w2, b2, w3, b3, freq, f_hi, f_lo, *, dt=256):
    feats, window, alt = _filter_feats(L)
    fh = HY_FILTER_HIDDEN
    nd = D_MODEL // dt
    w1p = jnp.pad(w1, ((0, LANES - HY_EMB), (0, 0)))
    const = lambda shape: pl.BlockSpec(shape, lambda o, c: tuple(0 for _ in shape))
    return pl.pallas_call(
        functools.partial(_filter_kernel, L=L),
        grid=(2, nd),
        in_specs=[
            const((L, LANES)),
            pl.BlockSpec((L, dt), lambda o, c: (0, c)),
            const((L, LANES)),
            const((LANES, fh)), const((1, fh)), const((fh, fh)), const((1, fh)), const((2, fh)),
            pl.BlockSpec((fh, dt), lambda o, c: (0, (2 * o) * nd + c)),
            pl.BlockSpec((1, dt), lambda o, c: (0, (2 * o) * nd + c)),
            pl.BlockSpec((fh, dt), lambda o, c: (0, (2 * o + 1) * nd + c)),
            pl.BlockSpec((1, dt), lambda o, c: (0, (2 * o + 1) * nd + c)),
            const((2 * L, L)), const((2 * L, L)),
        ],
        out_specs=pl.BlockSpec((1, 3, L, dt), lambda o, c: (o, 0, 0, c)),
        out_shape=jax.ShapeDtypeStruct((2, 3, L, D_MODEL), F32),
        compiler_params=_cparams(("arbitrary", "arbitrary")),
        name=f"hyena_filter_spectra_{L}",
    )(jnp.asarray(feats), jnp.asarray(window), jnp.asarray(alt), w1p, b1.reshape(1, fh), w2,
      b2.reshape(1, fh), freq, w3, b3.reshape(1, -1), w3, b3.reshape(1, -1), f_hi, f_lo)


def _conv_kernel(v_ref, x1_ref, x2_ref, cwv_ref, cw1_ref, cw2_ref, cbv_ref, cb1_ref, cb2_ref,
                 f_ref, finv_ref, kc_ref, bias_ref, o_ref, *, L):
    row = lax.broadcasted_iota(jnp.int32, v_ref.shape, 0)

    def short_conv(u_ref, w_ref, b_ref):
        u = u_ref[...].astype(F32)
        prev = jnp.where(row == 0, 0.0, pltpu.roll(u, 1, 0))
        nxt = jnp.where(row == L - 1, 0.0, pltpu.roll(u, L - 1, 0))
        w = w_ref[...]
        return prev * w[0:1, :] + u * w[1:2, :] + nxt * w[2:3, :] + b_ref[...]

    def long_conv(u, order):
        spec = _dot(f_ref[...], u.astype(BF16))
        a = spec[0:L, :]
        b = spec[L:2 * L, :]
        kra = kc_ref[order, 0]
        krb = kc_ref[order, 1]
        ks = kc_ref[order, 2]
        prod = jnp.concatenate([a * kra - b * ks, a * ks + b * krb], axis=0)
        return _dot(finv_ref[...], prod.astype(BF16)) + u * bias_ref[order:order + 1, :]

    v = short_conv(v_ref, cwv_ref, cbv_ref)
    y = short_conv(x1_ref, cw1_ref, cb1_ref) * long_conv(v, 0)
    y = short_conv(x2_ref, cw2_ref, cb2_ref) * long_conv(y, 1)
    o_ref[...] = y.astype(o_ref.dtype)


def _hyena_conv(z, conv_w, conv_b, f_bf, finv_bf, kc, bias, *, L, row_block0, n_seq, dt):
    nd = D_MODEL // dt
    cb = conv_b.reshape(1, 3 * D_MODEL)
    seg = lambda s: pl.BlockSpec((L, dt), lambda c, b: (row_block0 + b, s * nd + c))
    cw = lambda s: pl.BlockSpec((3, dt), lambda c, b: (0, s * nd + c))
    cbs = lambda s: pl.BlockSpec((1, dt), lambda c, b: (0, s * nd + c))
    return pl.pallas_call(
        functools.partial(_conv_kernel, L=L),
        grid=(nd, n_seq),
        in_specs=[
            seg(0), seg(1), seg(2), cw(0), cw(1), cw(2), cbs(0), cbs(1), cbs(2),
            pl.BlockSpec((2 * L, L), lambda c, b: (0, 0)),
            pl.BlockSpec((L, 2 * L), lambda c, b: (0, 0)),
            pl.BlockSpec((2, 3, L, dt), lambda c, b: (0, 0, 0, c)),
            pl.BlockSpec((2, dt), lambda c, b: (0, c)),
        ],
        out_specs=pl.BlockSpec((L, dt), lambda c, b: (b, c)),
        out_shape=jax.ShapeDtypeStruct((n_seq * L, D_MODEL), BF16),
        compiler_params=_cparams(("arbitrary", "arbitrary")),
        name=f"hyena_conv_{L}",
    )(z, z, z, conv_w, conv_w, conv_w, cb, cb, cb, f_bf, finv_bf, kc, bias)


def _attn_kernel(lq1_ref, lk1_ref, lq2_ref, lk2_ref, g_ref, q_ref, k_ref, v_ref, *rest,
                 lam_init, has_ctx, tq, nq):
    if has_ctx:
        ck_ref, cv_ref, o_ref = rest
    else:
        (o_ref,) = rest
    lam = (jnp.exp(jnp.sum(lq1_ref[...] * lk1_ref[...], axis=-1, keepdims=True))
           - jnp.exp(jnp.sum(lq2_ref[...] * lk2_ref[...], axis=-1, keepdims=True)) + lam_init)
    first_map = lax.broadcasted_iota(jnp.int32, (tq, HEAD_W), 1) < HEAD_DIM
    nt = (((1,), (1,)), ((), ()))

    def q_block(rows):
        for h in range(N_HEADS):
            cols = slice(h * HEAD_W, (h + 1) * HEAD_W)
            q = q_ref[rows, cols] * (HEAD_DIM ** -0.5)
            zero = jnp.zeros_like(q)
            q2 = jnp.concatenate([jnp.where(first_map, q, zero), jnp.where(first_map, zero, q)], axis=0)
            s = lax.dot_general(q2, k_ref[:, cols], nt, preferred_element_type=F32)
            m = jnp.max(s, axis=-1, keepdims=True)
            if has_ctx:
                sc = lax.dot_general(q2, ck_ref[0, 0, :, cols].astype(BF16), nt,
                                     preferred_element_type=F32)
                m = jnp.maximum(m, jnp.max(sc, axis=-1, keepdims=True))
                ec = jnp.exp(sc - m)
            e = jnp.exp(s - m)
            den = jnp.sum(e, axis=-1, keepdims=True)
            if has_ctx:
                den = den + jnp.sum(ec, axis=-1, keepdims=True)
            ov = _dot(e.astype(BF16), v_ref[:, cols])
            if has_ctx:
                ov = ov + _dot(ec.astype(BF16), cv_ref[0, 0, :, cols].astype(BF16))
            inv = 1.0 / den
            o = ov[0:tq] * inv[0:tq] - ov[tq:2 * tq] * (lam * inv[tq:2 * tq])
            ms = jnp.mean(o * o, axis=-1, keepdims=True)
            o = (o * lax.rsqrt(ms + EPS)) * g_ref[...] * (1.0 - lam_init)
            o_ref[rows, cols] = o.astype(o_ref.dtype)

    if nq == 1:
        q_block(slice(0, tq))
    else:
        def body(qi, carry):
            q_block(pl.ds(pl.multiple_of(qi * tq, tq), tq))
            return carry
        lax.fori_loop(0, nq, body, 0)


def _attention(qkv, lam_params, subln_g, lam_init, *, L, row_block0, n_seq, tq, ctx=None, j=0):
    small = pl.BlockSpec((1, HEAD_DIM), lambda b: (0, 0))
    in_specs = [small, small, small, small,
                pl.BlockSpec((1, HEAD_W), lambda b: (0, 0)),
                pl.BlockSpec((L, D_MODEL), lambda b: (row_block0 + b, 0)),
                pl.BlockSpec((L, D_MODEL), lambda b: (row_block0 + b, 1)),
                pl.BlockSpec((L, D_MODEL), lambda b: (row_block0 + b, 2))]
    args = [p.reshape(1, HEAD_DIM) for p in lam_params] + [subln_g.reshape(1, HEAD_W), qkv, qkv, qkv]
    if ctx is not None:
        ctx_spec = pl.BlockSpec((1, 1, PAST_LEN, D_MODEL), lambda b: (b, j, 0, 0))
        in_specs += [ctx_spec, ctx_spec]
        args += list(ctx)
    return pl.pallas_call(
        functools.partial(_attn_kernel, lam_init=lam_init, has_ctx=ctx is not None, tq=tq, nq=L // tq),
        grid=(n_seq,),
        in_specs=in_specs,
        out_specs=pl.BlockSpec((L, D_MODEL), lambda b: (b, 0)),
        out_shape=jax.ShapeDtypeStruct((n_seq * L, D_MODEL), BF16),
        compiler_params=_cparams(("arbitrary",)),
        name="diff_attention_ctx" if ctx is not None else "diff_attention",
    )(*args)


def _row_slots(pos, sel):
    big = float(BLOCK_ROWS)
    pos_a = jnp.max(pos, axis=-1, keepdims=True)
    pos_b = jnp.min(jnp.where(sel, pos, big), axis=-1, keepdims=True)
    return pos_a, pos_b


def _router_kernel(x_ref, g_ref, sh_ref, sc_ref, wr_ref, br_ref, xb_ref, pos_ref, gate_ref, cnt_ref,
                   *, tm):
    h = _modulate(x_ref[...], g_ref[...], sc_ref[0], sh_ref[0]).astype(BF16)
    logits = _dot(h, wr_ref[...].astype(BF16)) + br_ref[...]
    lane = lax.broadcasted_iota(jnp.int32, (tm, LANES), 1)
    neg = -jnp.inf
    is_grp = jnp.logical_and(lane >= N_EXPERTS, lane < N_EXPERTS + N_GROUPS)
    lg = jnp.where(is_grp, logits, neg)
    mg = jnp.max(lg, axis=-1, keepdims=True)
    g_gate = 1.0 / jnp.sum(jnp.exp(lg - mg), axis=-1, keepdims=True)
    g_idx = jnp.min(jnp.where(lg == mg, lane - N_EXPERTS, N_GROUPS), axis=-1, keepdims=True)
    in_grp = jnp.logical_and(lane < N_EXPERTS, lane // EXPERTS_PER_GROUP == g_idx)
    le = jnp.where(in_grp, logits, neg)
    m1 = jnp.max(le, axis=-1, keepdims=True)
    i1 = jnp.min(jnp.where(le == m1, lane, LANES), axis=-1, keepdims=True)
    le2 = jnp.where(lane == i1, neg, le)
    m2 = jnp.max(le2, axis=-1, keepdims=True)
    i2 = jnp.min(jnp.where(le2 == m2, lane, LANES), axis=-1, keepdims=True)
    e2 = jnp.exp(m2 - m1)
    inv = 1.0 / (1.0 + e2)
    gate_ref[...] = jnp.where(lane == i1, g_gate * inv, jnp.where(lane == i2, g_gate * (e2 * inv), 0.0))
    sel = jnp.logical_or(lane == i1, lane == i2)
    onehot = jnp.where(sel, 1.0, 0.0)
    r = lax.broadcasted_iota(jnp.int32, (tm, tm), 0)
    c = lax.broadcasted_iota(jnp.int32, (tm, tm), 1)
    rank = _dot(jnp.where(c < r, 1.0, 0.0).astype(BF16), onehot.astype(BF16))
    count = jnp.sum(onehot, axis=0, keepdims=True)
    cnt_ref[0] = count.astype(jnp.int32)
    padded = jnp.ceil(count * (1.0 / CHUNK)) * CHUNK
    er = lax.broadcasted_iota(jnp.int32, (LANES, LANES), 0)
    ec = lax.broadcasted_iota(jnp.int32, (LANES, LANES), 1)
    before = jnp.where(er < ec, 1.0, 0.0).astype(BF16)
    start = _dot(jnp.broadcast_to(padded, (SUBLANES, LANES)).astype(BF16), before)[0:1, :]
    pos = jnp.where(sel, start + rank, -1.0)
    pos_ref[...] = pos
    pos_a, pos_b = _row_slots(pos, sel)
    slot = lax.broadcasted_iota(jnp.int32, (tm, BLOCK_ROWS), 1).astype(F32)
    pick = jnp.where(jnp.logical_or(slot == pos_a, slot == pos_b), 1.0, 0.0).astype(BF16)
    xb_ref[...] = lax.dot_general(pick, h, (((0,), (0,)), ((), ())),
                                  preferred_element_type=F32).astype(BF16)


def _router(x, g, mods, layer, wr, br):
    tm = MOE_BLOCK
    return pl.pallas_call(
        functools.partial(_router_kernel, tm=tm),
        grid=(N_BLOCKS,),
        in_specs=[
            pl.BlockSpec((tm, D_MODEL), lambda i: (i, 0)),
            pl.BlockSpec((1, D_MODEL), lambda i: (0, 0)),
            pl.BlockSpec((1, 1, D_MODEL), _mod_index(layer, 3, tm)),
            pl.BlockSpec((1, 1, D_MODEL), _mod_index(layer, 4, tm)),
            pl.BlockSpec((D_MODEL, LANES), lambda i: (0, 0)),
            pl.BlockSpec((1, LANES), lambda i: (0, 0)),
        ],
        out_specs=[
            pl.BlockSpec((BLOCK_ROWS, D_MODEL), lambda i: (i, 0)),
            pl.BlockSpec((tm, LANES), lambda i: (i, 0)),
            pl.BlockSpec((tm, LANES), lambda i: (i, 0)),
            pl.BlockSpec((1, 1, LANES), lambda i: (i, 0, 0)),
        ],
        out_shape=[
            jax.ShapeDtypeStruct((N_BLOCKS * BLOCK_ROWS, D_MODEL), BF16),
            jax.ShapeDtypeStruct((N_TOK, LANES), F32),
            jax.ShapeDtypeStruct((N_TOK, LANES), F32),
            jax.ShapeDtypeStruct((N_BLOCKS, 1, LANES), jnp.int32),
        ],
        compiler_params=_cparams(("arbitrary",)),
        name="moe_router",
    )(x, g.reshape(1, D_MODEL), mods, mods, wr, br)


TABLE_ROWS = -(-(N_TILES * TILE_CHUNKS) // LANES)
TILE_ROWS = -(-N_TILES // LANES)


def _tables_kernel(cnt_ref, te_ref, nu_ref, disp_ref, comb_ref):
    ne = N_EXPERTS
    cnt = jnp.concatenate([cnt_ref[...].astype(F32), jnp.zeros((LANES - N_BLOCKS, LANES), F32)], axis=0)
    m = jnp.ceil(cnt * (1.0 / CHUNK))
    r = lax.broadcasted_iota(jnp.int32, (LANES, LANES), 0)
    c = lax.broadcasted_iota(jnp.int32, (LANES, LANES), 1)
    mb = m.astype(BF16)
    bstart = _dot(mb, jnp.where(r < c, 1.0, 0.0).astype(BF16))
    cm = _dot(jnp.where(c < r, 1.0, 0.0).astype(BF16), mb)
    total = jnp.sum(m, axis=0, keepdims=True)
    tiles = jnp.ceil(total * (1.0 / TILE_CHUNKS))
    tile_end = _dot(jnp.broadcast_to(tiles, (SUBLANES, LANES)).astype(BF16),
                    jnp.where(r <= c, 1.0, 0.0).astype(BF16))[0:1, :]
    choff = (tile_end - tiles) * TILE_CHUNKS
    nu_ref[...] = jnp.broadcast_to(tile_end[:, ne - 1:ne], (1, LANES)).astype(jnp.int32)

    tid = (lax.broadcasted_iota(jnp.int32, (TILE_ROWS, LANES), 0) * LANES
           + lax.broadcasted_iota(jnp.int32, (TILE_ROWS, LANES), 1)).astype(F32)
    te = jnp.zeros((TILE_ROWS, LANES), F32)
    for e in range(ne):
        te = te + jnp.where(tid >= tile_end[:, e:e + 1], 1.0, 0.0)
    te_ref[...] = jnp.minimum(te, ne - 1.0).astype(jnp.int32)

    start_t = (choff + cm).T
    m_t = m.T
    bstart_t = bstart.T
    lane = lax.broadcasted_iota(jnp.int32, (ne, LANES), 1).astype(F32)
    acc = [jnp.zeros((ne, LANES), F32) for _ in range(TABLE_ROWS)]
    for b in range(N_BLOCKS):
        lo = start_t[0:ne, b:b + 1]
        hi = lo + m_t[0:ne, b:b + 1]
        bs = bstart_t[0:ne, b:b + 1]
        fwd = (b * BLOCK_CHUNKS) + bs - lo
        for j in range(TABLE_ROWS):
            g = lane + float(j * LANES)
            acc[j] = acc[j] + jnp.where(g >= lo, jnp.where(g < hi, fwd + g, 0.0), 0.0)
        back = jnp.where(lane >= bs, jnp.where(lane < hi - lo + bs, lo - bs + lane, 0.0), 0.0)
        comb_ref[b:b + 1, :] = (jnp.sum(back, axis=0, keepdims=True) * CHUNK).astype(jnp.int32)
    for j in range(TABLE_ROWS):
        disp_ref[j:j + 1, :] = (jnp.sum(acc[j], axis=0, keepdims=True) * CHUNK).astype(jnp.int32)


def _chunk_tables(counts):
    return pl.pallas_call(
        _tables_kernel,
        out_shape=[
            jax.ShapeDtypeStruct((TILE_ROWS, LANES), jnp.int32),
            jax.ShapeDtypeStruct((1, LANES), jnp.int32),
            jax.ShapeDtypeStruct((TABLE_ROWS, LANES), jnp.int32),
            jax.ShapeDtypeStruct((N_BLOCKS, LANES), jnp.int32),
        ],
        name="moe_chunk_tables",
    )(counts.reshape(N_BLOCKS, LANES))


def _chunk_copies(read_row, n_chunks, src_hbm, dst, sem, wait):
    for c in range(n_chunks):
        cp = pltpu.make_async_copy(src_hbm.at[pl.ds(pl.multiple_of(read_row(c), CHUNK), CHUNK)],
                                   dst.at[pl.ds(c * CHUNK, CHUNK)], sem)
        if wait:
            cp.wait()
        else:
            cp.start()


def _expert_kernel(te_ref, nu_ref, src_ref, xb_hbm, w1_ref, w3_ref, w2_ref, y_ref, xbuf, sem):
    i = pl.program_id(0)
    n_used = nu_ref[0, 0]
    slot = i % 2
    tiles_per_row = LANES // TILE_CHUNKS

    def fetch(tile, s, wait):
        row = tile // tiles_per_row
        col = (tile % tiles_per_row) * TILE_CHUNKS
        _chunk_copies(lambda c: src_ref[row, col + c], TILE_CHUNKS, xb_hbm, xbuf.at[s], sem.at[s], wait)

    @pl.when(i == 0)
    def _():
        fetch(0, 0, False)

    @pl.when(i + 1 < n_used)
    def _():
        fetch(i + 1, 1 - slot, False)

    @pl.when(i < n_used)
    def _():
        fetch(i, slot, True)
        x = xbuf[slot]
        a = _dot(x, w1_ref[0, 0].astype(BF16))
        b = _dot(x, w3_ref[0, 0].astype(BF16))
        hid = (a * jax.nn.sigmoid(a)) * b
        y_ref[...] = _dot(hid.astype(BF16), w2_ref[0, 0].astype(BF16)).astype(y_ref.dtype)

    @pl.when(i >= n_used)
    def _():
        y_ref[...] = jnp.zeros_like(y_ref)


def _experts(tile_expert, n_used, disp_src, xb, w1, w3, w2, layer):
    def wsel(i, te, nu, src):
        t = jnp.minimum(i, nu[0, 0] - 1)
        return (layer, te[t // LANES, t % LANES], 0, 0)

    return pl.pallas_call(
        _expert_kernel,
        grid_spec=pltpu.PrefetchScalarGridSpec(
            num_scalar_prefetch=3,
            grid=(N_TILES,),
            in_specs=[
                pl.BlockSpec(memory_space=pl.ANY),
                pl.BlockSpec((1, 1, D_MODEL, D_EXPERT), wsel),
                pl.BlockSpec((1, 1, D_MODEL, D_EXPERT), wsel),
                pl.BlockSpec((1, 1, D_EXPERT, D_MODEL), wsel),
            ],
            out_specs=pl.BlockSpec((EXPERT_TILE, D_MODEL), lambda i, te, nu, src: (i, 0)),
            scratch_shapes=[pltpu.VMEM((2, EXPERT_TILE, D_MODEL), BF16),
                            pltpu.SemaphoreType.DMA((2,))],
        ),
        out_shape=jax.ShapeDtypeStruct((N_TILES * EXPERT_TILE, D_MODEL), BF16),
        compiler_params=_cparams(("arbitrary",)),
        name="moe_experts",
    )(tile_expert, n_used, disp_src, xb, w1, w3, w2)


def _combine_kernel(src_ref, y_hbm, x_ref, pos_ref, gate_ref, mg_ref, o_ref, ybuf, sem, *, tm):
    i = pl.program_id(0)
    slot = i % 2

    def fetch(blk, s, wait):
        _chunk_copies(lambda c: src_ref[blk, c], BLOCK_CHUNKS, y_hbm, ybuf.at[s], sem.at[s], wait)

    @pl.when(i == 0)
    def _():
        fetch(0, 0, False)

    @pl.when(i + 1 < N_BLOCKS)
    def _():
        fetch(i + 1, 1 - slot, False)

    fetch(i, slot, True)
    pos = pos_ref[...]
    sel = pos >= 0.0
    gate = gate_ref[...]
    pos_a, pos_b = _row_slots(pos, sel)
    gate_a = jnp.sum(jnp.where(pos == pos_a, gate, 0.0), axis=-1, keepdims=True)
    gate_b = jnp.sum(jnp.where(jnp.logical_and(sel, pos == pos_b), gate, 0.0), axis=-1, keepdims=True)
    row = lax.broadcasted_iota(jnp.int32, (tm, BLOCK_ROWS), 1).astype(F32)
    weights = jnp.where(row == pos_a, gate_a, jnp.where(row == pos_b, gate_b, 0.0))
    moe = _dot(weights.astype(BF16), ybuf[slot])
    o_ref[...] = x_ref[...] + mg_ref[0] * moe


def _combine(comb_src, y, x, pos, gate, mods, layer):
    tm = MOE_BLOCK

    def gate_idx(i, s):
        return ((layer * MOD_ROWS + _mod_row(i, tm)) * 6 + 5, 0, 0)

    return pl.pallas_call(
        functools.partial(_combine_kernel, tm=tm),
        grid_spec=pltpu.PrefetchScalarGridSpec(
            num_scalar_prefetch=1,
            grid=(N_BLOCKS,),
            in_specs=[
                pl.BlockSpec(memory_space=pl.ANY),
                pl.BlockSpec((tm, D_MODEL), lambda i, s: (i, 0)),
                pl.BlockSpec((tm, LANES), lambda i, s: (i, 0)),
                pl.BlockSpec((tm, LANES), lambda i, s: (i, 0)),
                pl.BlockSpec((1, 1, D_MODEL), gate_idx),
            ],
            out_specs=pl.BlockSpec((tm, D_MODEL), lambda i, s: (i, 0)),
            scratch_shapes=[pltpu.VMEM((2, BLOCK_ROWS, D_MODEL), BF16),
                            pltpu.SemaphoreType.DMA((2,))],
        ),
        out_shape=jax.ShapeDtypeStruct((N_TOK, D_MODEL), F32),
        compiler_params=_cparams(("arbitrary",)),
        name="moe_combine",
    )(comb_src, y, x, pos, gate, mods)


def _moe(x, g, mods, layer, w_group, b_group, w_expert, b_expert, w1, w3, w2):
    pad = LANES - N_EXPERTS - N_GROUPS
    wr = jnp.concatenate([w_expert, w_group, jnp.zeros((D_MODEL, pad), F32)], axis=1)
    br = jnp.concatenate([b_expert, b_group, jnp.zeros((pad,), F32)]).reshape(1, LANES)
    xb, pos, gate, counts = _router(x, g, mods, layer, wr, br)
    tile_expert, n_used, disp_src, comb_src = _chunk_tables(counts)
    y = _experts(tile_expert, n_used, disp_src, xb, w1, w3, w2, layer)
    return _combine(comb_src, y, x, pos, gate, mods, layer)


def _final_kernel(x_ref, g_ref, o_ref):
    x = x_ref[...]
    ms = jnp.mean(x * x, axis=-1, keepdims=True)
    o_ref[...] = (x * lax.rsqrt(ms + EPS)) * g_ref[...]


def _final_norm(x, g, *, tm=1024):
    return pl.pallas_call(
        _final_kernel,
        grid=(N_TOK // tm,),
        in_specs=[pl.BlockSpec((tm, D_MODEL), lambda i: (i, 0)),
                  pl.BlockSpec((1, D_MODEL), lambda i: (0, 0))],
        out_specs=pl.BlockSpec((tm, D_MODEL), lambda i: (i, 0)),
        out_shape=jax.ShapeDtypeStruct((N_TOK, D_MODEL), F32),
        compiler_params=_cparams(("arbitrary",)),
        name="final_norm",
    )(x, g.reshape(1, D_MODEL))


def kernel(x_prompt, x_sample, cache_k, cache_v, c, c_ctx, ada_w, ada_b, norm1_g, norm2_g, final_g, hy_w_in, hy_conv_w, hy_conv_b, hy_f_w1, hy_f_b1, hy_f_w2, hy_f_b2, hy_f_w3, hy_f_b3, hy_f_freq, hy_bias, hy_w_out, at_w_qkv, at_lam_q1, at_lam_k1, at_lam_q2, at_lam_k2, at_subln_g, at_w_o, moe_w_group, moe_b_group, moe_w_expert, moe_b_expert, moe_w1, moe_w3, moe_w2):
    x = jnp.concatenate([x_prompt.reshape(N_PROMPT, D_MODEL), x_sample.reshape(N_SAMPLE, D_MODEL)], axis=0)
    cond = jnp.concatenate([c_ctx[None, :], c, jnp.zeros((MOD_ROWS - 1 - DEC_BATCH, D_MODEL), F32)], axis=0)
    mods = _ada_table(cond, ada_w, ada_b).reshape(DEPTH * MOD_ROWS * 6, 1, D_MODEL)
    ctx_k = cache_k.reshape(DEC_BATCH, DEPTH // 2, PAST_LEN, D_MODEL)
    ctx_v = cache_v.reshape(DEC_BATCH, DEPTH // 2, PAST_LEN, D_MODEL)

    dft = {}
    for L in (SEQ, DEC_SEQ):
        fwd, inv = _dft_mats(L)
        fwd = jnp.asarray(fwd)
        f_hi, f_lo = _split_bf16(fwd)
        dft[L] = (f_hi, f_lo, jnp.asarray(inv).astype(BF16))

    new_k, new_v = [], []
    for l in range(DEPTH):
        j = l // 2
        if l % 2 == 0:
            z = _norm_matmul(x, norm1_g[l], mods, l, 0, 1, hy_w_in, j)
            ys = []
            for L, rb0, n_seq, dt in ((SEQ, 0, BATCH, 512), (DEC_SEQ, N_PROMPT // DEC_SEQ, DEC_BATCH, 256)):
                f_hi, f_lo, finv = dft[L]
                kc = _filter_spectra(L, hy_f_w1[j], hy_f_b1[j], hy_f_w2[j], hy_f_b2[j], hy_f_w3[j],
                                     hy_f_b3[j], hy_f_freq[j], f_hi, f_lo)
                ys.append(_hyena_conv(z, hy_conv_w[j], hy_conv_b[j], f_hi, finv, kc, hy_bias[j],
                                      L=L, row_block0=rb0, n_seq=n_seq, dt=dt))
            y = jnp.concatenate(ys, axis=0)
            x = _matmul_residual(y, hy_w_out, j, x, mods, l, 2)
        else:
            lam_init = 0.8 - 0.6 * math.exp(-0.3 * l)
            qkv = _norm_matmul(x, norm1_g[l], mods, l, 0, 1, at_w_qkv, j, rope=True)
            lam_params = (at_lam_q1[j], at_lam_k1[j], at_lam_q2[j], at_lam_k2[j])
            op = _attention(qkv, lam_params, at_subln_g[j], lam_init, L=SEQ, row_block0=0,
                            n_seq=BATCH, tq=SEQ)
            os_ = _attention(qkv, lam_params, at_subln_g[j], lam_init, L=DEC_SEQ,
                             row_block0=N_PROMPT // DEC_SEQ, n_seq=DEC_BATCH, tq=256,
                             ctx=(ctx_k, ctx_v), j=j)
            o = jnp.concatenate([op, os_], axis=0)
            new_k.append(qkv[:N_PROMPT, D_MODEL:2 * D_MODEL].astype(F32).reshape(BATCH, SEQ, N_HEADS, HEAD_W))
            new_v.append(qkv[:N_PROMPT, 2 * D_MODEL:].astype(F32).reshape(BATCH, SEQ, N_HEADS, HEAD_W))
            x = _matmul_residual(o, at_w_o, j, x, mods, l, 2)
        x = _moe(x, norm2_g[l], mods, l, moe_w_group[l], moe_b_group[l], moe_w_expert[l],
                 moe_b_expert[l], moe_w1, moe_w3, moe_w2)
    out = _final_norm(x, final_g)
    y_prompt = out[:N_PROMPT].reshape(BATCH, SEQ, D_MODEL)
    y_sample = out[N_PROMPT:].reshape(DEC_BATCH, DEC_SEQ, D_MODEL)
    return (y_prompt, y_sample, jnp.stack(new_k, axis=1), jnp.stack(new_v, axis=1))
```

```python
import functools
import math

import numpy as np
import jax
import jax.numpy as jnp
from jax import lax
from jax.experimental import pallas as pl
from jax.experimental.pallas import tpu as pltpu

F32 = jnp.float32
BF16 = jnp.bfloat16

D_MODEL = 1024
BATCH = 16
SEQ = 256
DEPTH = 4
DEC_BATCH = 4
DEC_SEQ = 1024
PAST_LEN = 256
GRID_W = 64
EPS = 1e-6
HY_BANDS = 16
HY_EMB = 1 + 2 * HY_BANDS
HY_FILTER_HIDDEN = 64
HY_FAST_DECAY = 0.3
HY_SLOW_DECAY = 1.5
HY_TARGET = 1e-2
N_HEADS = 8
HEAD_DIM = 64
ROT_AXIS = HEAD_DIM // 2
ROPE_BASE = 10000.0
N_GROUPS = 4
EXPERTS_PER_GROUP = 8
N_EXPERTS = N_GROUPS * EXPERTS_PER_GROUP
D_EXPERT = D_MODEL // 4

N_PROMPT = BATCH * SEQ
N_SAMPLE = DEC_BATCH * DEC_SEQ
N_TOK = N_PROMPT + N_SAMPLE
MOD_ROWS = 8
LANES = 128
SUBLANES = 8
HEAD_W = 2 * HEAD_DIM
VMEM_LIMIT = 56 * 1024 * 1024

MOE_BLOCK = 256
N_BLOCKS = N_TOK // MOE_BLOCK
CHUNK = 2 * SUBLANES
BLOCK_ROWS = 2 * MOE_BLOCK + N_EXPERTS * CHUNK
BLOCK_CHUNKS = BLOCK_ROWS // CHUNK
EXPERT_TILE = 256
TILE_CHUNKS = EXPERT_TILE // CHUNK
N_TILES = (2 * N_TOK + N_BLOCKS * N_EXPERTS * (CHUNK - 1)) // EXPERT_TILE + N_EXPERTS


def _cparams(sem):
    return pltpu.CompilerParams(dimension_semantics=sem, vmem_limit_bytes=VMEM_LIMIT)


def _mod_row(blk, tm):
    start = blk * tm
    return jnp.where(start < N_PROMPT, 0, 1 + (start - N_PROMPT) // DEC_SEQ)


def _mod_index(layer, which, tm):
    def index_map(i, *_):
        return ((layer * MOD_ROWS + _mod_row(i, tm)) * 6 + which, 0, 0)
    return index_map


@functools.lru_cache(maxsize=None)
def _dft_mats(L):
    k = np.arange(L, dtype=np.float64)[:, None]
    j = np.arange(L, dtype=np.float64)[None, :]
    ang = np.pi * ((k * j) % (2 * L)) / L
    c = np.cos(ang)
    s = np.sin(ang)
    s[0, :] = np.where(np.arange(L) % 2 == 0, 1.0, -1.0)
    fwd = np.concatenate([c, s], axis=0)
    scale = np.full((2 * L,), 1.0 / L)
    scale[0] = scale[L] = 0.5 / L
    inv = fwd.T * scale[None, :]
    return fwd.astype(np.float32), inv.astype(np.float32)


@functools.lru_cache(maxsize=None)
def _filter_feats(L):
    pos = np.arange(L, dtype=np.float64)
    t = pos / (L - 1)
    bands = np.linspace(1e-4, HY_BANDS - 1, HY_BANDS)
    ang = (2.0 * math.pi / L) * pos[:, None] * bands[None, :]
    feats = np.concatenate([t[:, None], np.cos(ang), -np.sin(ang)], axis=-1)
    feats = np.pad(feats, ((0, 0), (0, LANES - HY_EMB)))
    deltas = np.abs(np.linspace(math.log(HY_TARGET) / HY_SLOW_DECAY,
                                math.log(HY_TARGET) / HY_FAST_DECAY, D_MODEL))
    window = np.exp(-t[:, None] * deltas[None, :])
    alt = np.where(np.arange(L) % 2 == 0, 1.0, -1.0)[:, None] * np.ones((1, LANES))
    return feats.astype(np.float32), window.astype(np.float32), alt.astype(np.float32)


@functools.lru_cache(maxsize=None)
def _rope_tables():
    pos = np.arange(DEC_SEQ)
    row = (pos // GRID_W).astype(np.float64)
    col = (pos % GRID_W).astype(np.float64)
    lane = np.arange(HEAD_W)
    d = lane % HEAD_DIM
    axis = d // ROT_AXIS
    n = d % ROT_AXIS
    half = n // (ROT_AXIS // 2)
    f = n % (ROT_AXIS // 2)
    inv = ROPE_BASE ** (-f.astype(np.float64) / (ROT_AXIS // 2))
    p = np.where(axis[None, :] == 0, row[:, None], col[:, None])
    ang = p * inv[None, :]
    sign = np.where(half == 0, -1.0, 1.0)[None, :]
    return np.cos(ang).astype(np.float32), (np.sin(ang) * sign).astype(np.float32)


def _split_bf16(a):
    hi = a.astype(BF16)
    lo = (a - hi.astype(F32)).astype(BF16)
    return hi, lo


def _dot(a, b):
    return jnp.dot(a, b, preferred_element_type=F32)


def _dot3(a_hi, a_lo, b_hi, b_lo):
    return _dot(a_hi, b_hi) + (_dot(a_hi, b_lo) + _dot(a_lo, b_hi))


def _ada_kernel(c_ref, w_ref, b_ref, o_ref):
    c = c_ref[...]
    s = c * jax.nn.sigmoid(c)
    o_ref[0] = _dot(s.astype(BF16), w_ref[0].astype(BF16)) + b_ref[0]


def _ada_table(cond, ada_w, ada_b):
    tn = 1536
    n = 6 * D_MODEL
    return pl.pallas_call(
        _ada_kernel,
        grid=(DEPTH, n // tn),
        in_specs=[
            pl.BlockSpec((MOD_ROWS, D_MODEL), lambda l, j: (0, 0)),
            pl.BlockSpec((1, D_MODEL, tn), lambda l, j: (l, 0, j)),
            pl.BlockSpec((1, 1, tn), lambda l, j: (l, 0, j)),
        ],
        out_specs=pl.BlockSpec((1, MOD_ROWS, tn), lambda l, j: (l, 0, j)),
        out_shape=jax.ShapeDtypeStruct((DEPTH, MOD_ROWS, n), F32),
        compiler_params=_cparams(("arbitrary", "arbitrary")),
        name="ada_table",
    )(cond, ada_w, ada_b.reshape(DEPTH, 1, n))


def _modulate(x, g, scale, shift):
    ms = jnp.mean(x * x, axis=-1, keepdims=True)
    y = (x * lax.rsqrt(ms + EPS)) * g
    return y * (1.0 + scale) + shift


MM_COLS = 512


def _normmm_kernel(x_ref, g_ref, sh_ref, sc_ref, w_ref, *rest, rope, tm):
    if rope:
        cos_ref, sin_ref, o_ref, w_scr = rest
    else:
        o_ref, w_scr = rest
    i = pl.program_id(0)

    @pl.when(i == 0)
    def _():
        w_scr[...] = w_ref[0].astype(BF16)

    h = _modulate(x_ref[...], g_ref[...], sc_ref[0], sh_ref[0]).astype(BF16)
    n = w_scr.shape[1]
    for j in range(n // MM_COLS):
        cols = slice(j * MM_COLS, (j + 1) * MM_COLS)
        acc = _dot(h, w_scr[:, cols])
        if not (rope and j < 2 * D_MODEL // MM_COLS):
            o_ref[:, cols] = acc.astype(o_ref.dtype)
            continue

        @pl.when(i >= N_PROMPT // tm)
        def _():
            reps = MM_COLS // HEAD_W
            cos = jnp.tile(cos_ref[...], (1, reps))
            sin = jnp.tile(sin_ref[...], (1, reps))
            lane = lax.broadcasted_iota(jnp.int32, (tm, MM_COLS), 1)
            first_half = (lane % ROT_AXIS) < (ROT_AXIS // 2)
            partner = jnp.where(first_half,
                                pltpu.roll(acc, MM_COLS - ROT_AXIS // 2, 1),
                                pltpu.roll(acc, ROT_AXIS // 2, 1))
            o_ref[:, cols] = (acc * cos + partner * sin).astype(o_ref.dtype)

        @pl.when(i < N_PROMPT // tm)
        def _():
            o_ref[:, cols] = acc.astype(o_ref.dtype)


def _norm_matmul(x, g, mods, layer, which_shift, which_scale, w, widx, *, rope=False, tm=512):
    n = w.shape[2]
    rope_blocks = DEC_SEQ // tm

    def rope_idx(i):
        return (jnp.maximum(i - N_PROMPT // tm, 0) % rope_blocks, 0)

    in_specs = [
        pl.BlockSpec((tm, D_MODEL), lambda i: (i, 0)),
        pl.BlockSpec((1, D_MODEL), lambda i: (0, 0)),
        pl.BlockSpec((1, 1, D_MODEL), _mod_index(layer, which_shift, tm)),
        pl.BlockSpec((1, 1, D_MODEL), _mod_index(layer, which_scale, tm)),
        pl.BlockSpec((1, D_MODEL, n), lambda i: (widx, 0, 0), pipeline_mode=pl.Buffered(1)),
    ]
    args = [x, g.reshape(1, D_MODEL), mods, mods, w]
    if rope:
        cos, sin = _rope_tables()
        in_specs += [pl.BlockSpec((tm, HEAD_W), rope_idx), pl.BlockSpec((tm, HEAD_W), rope_idx)]
        args += [jnp.asarray(cos), jnp.asarray(sin)]
    return pl.pallas_call(
        functools.partial(_normmm_kernel, rope=rope, tm=tm),
        grid=(N_TOK // tm,),
        in_specs=in_specs,
        out_specs=pl.BlockSpec((tm, n), lambda i: (i, 0)),
        out_shape=jax.ShapeDtypeStruct((N_TOK, n), BF16),
        scratch_shapes=[pltpu.VMEM((D_MODEL, n), BF16)],
        compiler_params=_cparams(("arbitrary",)),
        name="norm_matmul_rope" if rope else "norm_matmul",
    )(*args)


def _mmres_kernel(a_ref, w_ref, x_ref, gate_ref, o_ref, w_scr):
    @pl.when(pl.program_id(0) == 0)
    def _():
        w_scr[...] = w_ref[0].astype(BF16)

    a = a_ref[...]
    gate = gate_ref[0]
    for j in range(w_scr.shape[1] // MM_COLS):
        cols = slice(j * MM_COLS, (j + 1) * MM_COLS)
        o_ref[:, cols] = x_ref[:, cols] + gate[:, cols] * _dot(a, w_scr[:, cols])


def _matmul_residual(a, w, widx, x, mods, layer, which_gate, *, tm=512):
    k = a.shape[1]
    n = w.shape[2]
    return pl.pallas_call(
        _mmres_kernel,
        grid=(N_TOK // tm,),
        in_specs=[
            pl.BlockSpec((tm, k), lambda i: (i, 0)),
            pl.BlockSpec((1, k, n), lambda i: (widx, 0, 0), pipeline_mode=pl.Buffered(1)),
            pl.BlockSpec((tm, n), lambda i: (i, 0)),
            pl.BlockSpec((1, 1, n), _mod_index(layer, which_gate, tm)),
        ],
        out_specs=pl.BlockSpec((tm, n), lambda i: (i, 0)),
        out_shape=jax.ShapeDtypeStruct((N_TOK, n), F32),
        scratch_shapes=[pltpu.VMEM((k, n), BF16)],
        compiler_params=_cparams(("arbitrary",)),
        name="matmul_residual",
    )(a, w, x, mods)


def _filter_kernel(feats_ref, win_ref, alt_ref, w1_ref, b1_ref, w2_ref, b2_ref, fq_ref,
                   w3f_ref, b3f_ref, w3b_ref, b3b_ref, fhi_ref, flo_ref, o_ref, *, L):
    def dense(a, w_ref, b_ref):
        a_hi, a_lo = _split_bf16(a)
        w_hi, w_lo = _split_bf16(w_ref[...])
        return _dot3(a_hi, a_lo, w_hi, w_lo) + b_ref[...]

    fq = fq_ref[...]
    h = jnp.sin(fq[0:1, :] * dense(feats_ref[...], w1_ref, b1_ref))
    h = jnp.sin(fq[1:2, :] * dense(h, w2_ref, b2_ref))
    win = win_ref[...]
    hf = dense(h, w3f_ref, b3f_ref) * win
    hb = dense(h, w3b_ref, b3b_ref) * win
    row = lax.broadcasted_iota(jnp.int32, hf.shape, 0)
    hb = jnp.where(row == 0, 0.0, hb)
    s_hi, s_lo = _split_bf16(hf + hb)
    d_hi, d_lo = _split_bf16(hf - hb)
    kr = _dot3(fhi_ref[0:L, :], flo_ref[0:L, :], s_hi, s_lo)
    ks = _dot3(fhi_ref[L:2 * L, :], flo_ref[L:2 * L, :], d_hi, d_lo)
    k_nyq = jnp.sum((hf + hb) * alt_ref[:, 0:1], axis=0, keepdims=True)
    o_ref[0, 0] = kr
    o_ref[0, 1] = jnp.where(row == 0, k_nyq, kr)
    o_ref[0, 2] = jnp.where(row == 0, 0.0, ks)


def _filter_spectra(L, w1, b1, w2, b2, w3, b3, freq, f_hi, f_lo, *, dt=256):
    feats, window, alt = _filter_feats(L)
    fh = HY_FILTER_HIDDEN
    nd = D_MODEL // dt
    w1p = jnp.pad(w1, ((0, LANES - HY_EMB), (0, 0)))
    const = lambda shape: pl.BlockSpec(shape, lambda o, c: tuple(0 for _ in shape))
    return pl.pallas_call(
        functools.partial(_filter_kernel, L=L),
        grid=(2, nd),
        in_specs=[
            const((L, LANES)),
            pl.BlockSpec((L, dt), lambda o, c: (0, c)),
            const((L, LANES)),
            const((LANES, fh)), const((1, fh)), const((fh, fh)), const((1, fh)), const((2, fh)),
            pl.BlockSpec((fh, dt), lambda o, c: (0, (2 * o) * nd + c)),
            pl.BlockSpec((1, dt), lambda o, c: (0, (2 * o) * nd + c)),
            pl.BlockSpec((fh, dt), lambda o, c: (0, (2 * o + 1) * nd + c)),
            pl.BlockSpec((1, dt), lambda o, c: (0, (2 * o + 1) * nd + c)),
            const((2 * L, L)), const((2 * L, L)),
        ],
        out_specs=pl.BlockSpec((1, 3, L, dt), lambda o, c: (o, 0, 0, c)),
        out_shape=jax.ShapeDtypeStruct((2, 3, L, D_MODEL), F32),
        compiler_params=_cparams(("arbitrary", "arbitrary")),
        name=f"hyena_filter_spectra_{L}",
    )(jnp.asarray(feats), jnp.asarray(window), jnp.asarray(alt), w1p, b1.reshape(1, fh), w2,
      b2.reshape(1, fh), freq, w3, b3.reshape(1, -1), w3, b3.reshape(1, -1), f_hi, f_lo)


def _conv_kernel(v_ref, x1_ref, x2_ref, cwv_ref, cw1_ref, cw2_ref, cbv_ref, cb1_ref, cb2_ref,
                 f_ref, finv_ref, kc_ref, bias_ref, o_ref, *, L):
    row = lax.broadcasted_iota(jnp.int32, v_ref.shape, 0)

    def short_conv(u_ref, w_ref, b_ref):
        u = u_ref[...].astype(F32)
        prev = jnp.where(row == 0, 0.0, pltpu.roll(u, 1, 0))
        nxt = jnp.where(row == L - 1, 0.0, pltpu.roll(u, L - 1, 0))
        w = w_ref[...]
        return prev * w[0:1, :] + u * w[1:2, :] + nxt * w[2:3, :] + b_ref[...]

    def long_conv(u, order):
        spec = _dot(f_ref[...], u.astype(BF16))
        a = spec[0:L, :]
        b = spec[L:2 * L, :]
        kra = kc_ref[order, 0]
        krb = kc_ref[order, 1]
        ks = kc_ref[order, 2]
        prod = jnp.concatenate([a * kra - b * ks, a * ks + b * krb], axis=0)
        return _dot(finv_ref[...], prod.astype(BF16)) + u * bias_ref[order:order + 1, :]

    v = short_conv(v_ref, cwv_ref, cbv_ref)
    y = short_conv(x1_ref, cw1_ref, cb1_ref) * long_conv(v, 0)
    y = short_conv(x2_ref, cw2_ref, cb2_ref) * long_conv(y, 1)
    o_ref[...] = y.astype(o_ref.dtype)


def _hyena_conv(z, conv_w, conv_b, f_bf, finv_bf, kc, bias, *, L, row_block0, n_seq, dt):
    nd = D_MODEL // dt
    cb = conv_b.reshape(1, 3 * D_MODEL)
    seg = lambda s: pl.BlockSpec((L, dt), lambda c, b: (row_block0 + b, s * nd + c))
    cw = lambda s: pl.BlockSpec((3, dt), lambda c, b: (0, s * nd + c))
    cbs = lambda s: pl.BlockSpec((1, dt), lambda c, b: (0, s * nd + c))
    return pl.pallas_call(
        functools.partial(_conv_kernel, L=L),
        grid=(nd, n_seq),
        in_specs=[
            seg(0), seg(1), seg(2), cw(0), cw(1), cw(2), cbs(0), cbs(1), cbs(2),
            pl.BlockSpec((2 * L, L), lambda c, b: (0, 0)),
            pl.BlockSpec((L, 2 * L), lambda c, b: (0, 0)),
            pl.BlockSpec((2, 3, L, dt), lambda c, b: (0, 0, 0, c)),
            pl.BlockSpec((2, dt), lambda c, b: (0, c)),
        ],
        out_specs=pl.BlockSpec((L, dt), lambda c, b: (b, c)),
        out_shape=jax.ShapeDtypeStruct((n_seq * L, D_MODEL), BF16),
        compiler_params=_cparams(("arbitrary", "arbitrary")),
        name=f"hyena_conv_{L}",
    )(z, z, z, conv_w, conv_w, conv_w, cb, cb, cb, f_bf, finv_bf, kc, bias)


def _attn_kernel(lq1_ref, lk1_ref, lq2_ref, lk2_ref, g_ref, q_ref, k_ref, v_ref, *rest,
                 lam_init, has_ctx, tq, nq):
    if has_ctx:
        ck_ref, cv_ref, o_ref = rest
    else:
        (o_ref,) = rest
    lam = (jnp.exp(jnp.sum(lq1_ref[...] * lk1_ref[...], axis=-1, keepdims=True))
           - jnp.exp(jnp.sum(lq2_ref[...] * lk2_ref[...], axis=-1, keepdims=True)) + lam_init)
    first_map = lax.broadcasted_iota(jnp.int32, (tq, HEAD_W), 1) < HEAD_DIM
    nt = (((1,), (1,)), ((), ()))

    def q_block(rows):
        for h in range(N_HEADS):
            cols = slice(h * HEAD_W, (h + 1) * HEAD_W)
            q = q_ref[rows, cols] * (HEAD_DIM ** -0.5)
            zero = jnp.zeros_like(q)
            q2 = jnp.concatenate([jnp.where(first_map, q, zero), jnp.where(first_map, zero, q)], axis=0)
            s = lax.dot_general(q2, k_ref[:, cols], nt, preferred_element_type=F32)
            m = jnp.max(s, axis=-1, keepdims=True)
            if has_ctx:
                sc = lax.dot_general(q2, ck_ref[0, 0, :, cols].astype(BF16), nt,
                                     preferred_element_type=F32)
                m = jnp.maximum(m, jnp.max(sc, axis=-1, keepdims=True))
                ec = jnp.exp(sc - m)
            e = jnp.exp(s - m)
            den = jnp.sum(e, axis=-1, keepdims=True)
            if has_ctx:
                den = den + jnp.sum(ec, axis=-1, keepdims=True)
            ov = _dot(e.astype(BF16), v_ref[:, cols])
            if has_ctx:
                ov = ov + _dot(ec.astype(BF16), cv_ref[0, 0, :, cols].astype(BF16))
            inv = 1.0 / den
            o = ov[0:tq] * inv[0:tq] - ov[tq:2 * tq] * (lam * inv[tq:2 * tq])
            ms = jnp.mean(o * o, axis=-1, keepdims=True)
            o = (o * lax.rsqrt(ms + EPS)) * g_ref[...] * (1.0 - lam_init)
            o_ref[rows, cols] = o.astype(o_ref.dtype)

    if nq == 1:
        q_block(slice(0, tq))
    else:
        def body(qi, carry):
            q_block(pl.ds(pl.multiple_of(qi * tq, tq), tq))
            return carry
        lax.fori_loop(0, nq, body, 0)


def _attention(qkv, lam_params, subln_g, lam_init, *, L, row_block0, n_seq, tq, ctx=None, j=0):
    small = pl.BlockSpec((1, HEAD_DIM), lambda b: (0, 0))
    in_specs = [small, small, small, small,
                pl.BlockSpec((1, HEAD_W), lambda b: (0, 0)),
                pl.BlockSpec((L, D_MODEL), lambda b: (row_block0 + b, 0)),
                pl.BlockSpec((L, D_MODEL), lambda b: (row_block0 + b, 1)),
                pl.BlockSpec((L, D_MODEL), lambda b: (row_block0 + b, 2))]
    args = [p.reshape(1, HEAD_DIM) for p in lam_params] + [subln_g.reshape(1, HEAD_W), qkv, qkv, qkv]
    if ctx is not None:
        ctx_spec = pl.BlockSpec((1, 1, PAST_LEN, D_MODEL), lambda b: (b, j, 0, 0))
        in_specs += [ctx_spec, ctx_spec]
        args += list(ctx)
    return pl.pallas_call(
        functools.partial(_attn_kernel, lam_init=lam_init, has_ctx=ctx is not None, tq=tq, nq=L // tq),
        grid=(n_seq,),
        in_specs=in_specs,
        out_specs=pl.BlockSpec((L, D_MODEL), lambda b: (b, 0)),
        out_shape=jax.ShapeDtypeStruct((n_seq * L, D_MODEL), BF16),
        compiler_params=_cparams(("arbitrary",)),
        name="diff_attention_ctx" if ctx is not None else "diff_attention",
    )(*args)


def _row_slots(pos, sel):
    big = float(BLOCK_ROWS)
    pos_a = jnp.max(pos, axis=-1, keepdims=True)
    pos_b = jnp.min(jnp.where(sel, pos, big), axis=-1, keepdims=True)
    return pos_a, pos_b


def _router_kernel(x_ref, g_ref, sh_ref, sc_ref, wr_ref, br_ref, xb_ref, pos_ref, gate_ref, cnt_ref,
                   *, tm):
    h = _modulate(x_ref[...], g_ref[...], sc_ref[0], sh_ref[0]).astype(BF16)
    logits = _dot(h, wr_ref[...].astype(BF16)) + br_ref[...]
    lane = lax.broadcasted_iota(jnp.int32, (tm, LANES), 1)
    neg = -jnp.inf
    is_grp = jnp.logical_and(lane >= N_EXPERTS, lane < N_EXPERTS + N_GROUPS)
    lg = jnp.where(is_grp, logits, neg)
    mg = jnp.max(lg, axis=-1, keepdims=True)
    g_gate = 1.0 / jnp.sum(jnp.exp(lg - mg), axis=-1, keepdims=True)
    g_idx = jnp.min(jnp.where(lg == mg, lane - N_EXPERTS, N_GROUPS), axis=-1, keepdims=True)
    in_grp = jnp.logical_and(lane < N_EXPERTS, lane // EXPERTS_PER_GROUP == g_idx)
    le = jnp.where(in_grp, logits, neg)
    m1 = jnp.max(le, axis=-1, keepdims=True)
    i1 = jnp.min(jnp.where(le == m1, lane, LANES), axis=-1, keepdims=True)
    le2 = jnp.where(lane == i1, neg, le)
    m2 = jnp.max(le2, axis=-1, keepdims=True)
    i2 = jnp.min(jnp.where(le2 == m2, lane, LANES), axis=-1, keepdims=True)
    e2 = jnp.exp(m2 - m1)
    inv = 1.0 / (1.0 + e2)
    gate_ref[...] = jnp.where(lane == i1, g_gate * inv, jnp.where(lane == i2, g_gate * (e2 * inv), 0.0))
    sel = jnp.logical_or(lane == i1, lane == i2)
    onehot = jnp.where(sel, 1.0, 0.0)
    r = lax.broadcasted_iota(jnp.int32, (tm, tm), 0)
    c = lax.broadcasted_iota(jnp.int32, (tm, tm), 1)
    rank = _dot(jnp.where(c < r, 1.0, 0.0).astype(BF16), onehot.astype(BF16))
    count = jnp.sum(onehot, axis=0, keepdims=True)
    cnt_ref[0] = count.astype(jnp.int32)
    padded = jnp.ceil(count * (1.0 / CHUNK)) * CHUNK
    er = lax.broadcasted_iota(jnp.int32, (LANES, LANES), 0)
    ec = lax.broadcasted_iota(jnp.int32, (LANES, LANES), 1)
    before = jnp.where(er < ec, 1.0, 0.0).astype(BF16)
    start = _dot(jnp.broadcast_to(padded, (SUBLANES, LANES)).astype(BF16), before)[0:1, :]
    pos = jnp.where(sel, start + rank, -1.0)
    pos_ref[...] = pos
    pos_a, pos_b = _row_slots(pos, sel)
    slot = lax.broadcasted_iota(jnp.int32, (tm, BLOCK_ROWS), 1).astype(F32)
    pick = jnp.where(jnp.logical_or(slot == pos_a, slot == pos_b), 1.0, 0.0).astype(BF16)
    xb_ref[...] = lax.dot_general(pick, h, (((0,), (0,)), ((), ())),
                                  preferred_element_type=F32).astype(BF16)


def _router(x, g, mods, layer, wr, br):
    tm = MOE_BLOCK
    return pl.pallas_call(
        functools.partial(_router_kernel, tm=tm),
        grid=(N_BLOCKS,),
        in_specs=[
            pl.BlockSpec((tm, D_MODEL), lambda i: (i, 0)),
            pl.BlockSpec((1, D_MODEL), lambda i: (0, 0)),
            pl.BlockSpec((1, 1, D_MODEL), _mod_index(layer, 3, tm)),
            pl.BlockSpec((1, 1, D_MODEL), _mod_index(layer, 4, tm)),
            pl.BlockSpec((D_MODEL, LANES), lambda i: (0, 0)),
            pl.BlockSpec((1, LANES), lambda i: (0, 0)),
        ],
        out_specs=[
            pl.BlockSpec((BLOCK_ROWS, D_MODEL), lambda i: (i, 0)),
            pl.BlockSpec((tm, LANES), lambda i: (i, 0)),
            pl.BlockSpec((tm, LANES), lambda i: (i, 0)),
            pl.BlockSpec((1, 1, LANES), lambda i: (i, 0, 0)),
        ],
        out_shape=[
            jax.ShapeDtypeStruct((N_BLOCKS * BLOCK_ROWS, D_MODEL), BF16),
            jax.ShapeDtypeStruct((N_TOK, LANES), F32),
            jax.ShapeDtypeStruct((N_TOK, LANES), F32),
            jax.ShapeDtypeStruct((N_BLOCKS, 1, LANES), jnp.int32),
        ],
        compiler_params=_cparams(("arbitrary",)),
        name="moe_router",
    )(x, g.reshape(1, D_MODEL), mods, mods, wr, br)


TABLE_ROWS = -(-(N_TILES * TILE_CHUNKS) // LANES)


def _tables_kernel(cnt_ref, tend_ref, disp_ref, comb_ref):
    ne = N_EXPERTS
    cnt = jnp.concatenate([cnt_ref[...].astype(F32), jnp.zeros((LANES - N_BLOCKS, LANES), F32)], axis=0)
    m = jnp.ceil(cnt * (1.0 / CHUNK))
    r = lax.broadcasted_iota(jnp.int32, (LANES, LANES), 0)
    c = lax.broadcasted_iota(jnp.int32, (LANES, LANES), 1)
    mb = m.astype(BF16)
    bstart = _dot(mb, jnp.where(r < c, 1.0, 0.0).astype(BF16))
    cm = _dot(jnp.where(c < r, 1.0, 0.0).astype(BF16), mb)
    total = jnp.sum(m, axis=0, keepdims=True)
    tiles = jnp.ceil(total * (1.0 / TILE_CHUNKS))
    tile_end = _dot(jnp.broadcast_to(tiles, (SUBLANES, LANES)).astype(BF16),
                    jnp.where(r <= c, 1.0, 0.0).astype(BF16))[0:1, :]
    choff = (tile_end - tiles) * TILE_CHUNKS
    tend_ref[...] = tile_end.astype(jnp.int32)

    start_t = (choff + cm).T
    m_t = m.T
    bstart_t = bstart.T
    lane = lax.broadcasted_iota(jnp.int32, (ne, LANES), 1).astype(F32)
    acc = [jnp.zeros((ne, LANES), F32) for _ in range(TABLE_ROWS)]
    for b in range(N_BLOCKS):
        lo = start_t[0:ne, b:b + 1]
        hi = lo + m_t[0:ne, b:b + 1]
        bs = bstart_t[0:ne, b:b + 1]
        fwd = (b * BLOCK_CHUNKS) + bs - lo
        for j in range(TABLE_ROWS):
            g = lane + float(j * LANES)
            acc[j] = acc[j] + jnp.where(g >= lo, jnp.where(g < hi, fwd + g, 0.0), 0.0)
        back = jnp.where(lane >= bs, jnp.where(lane < hi - lo + bs, (lo - bs + lane) * CHUNK + 1.0, 0.0), 0.0)
        comb_ref[b:b + 1, :] = (jnp.sum(back, axis=0, keepdims=True) - 1.0).astype(jnp.int32)
    for j in range(TABLE_ROWS):
        disp_ref[j:j + 1, :] = (jnp.sum(acc[j], axis=0, keepdims=True) * CHUNK).astype(jnp.int32)


def _chunk_tables(counts):
    return pl.pallas_call(
        _tables_kernel,
        out_shape=[
            jax.ShapeDtypeStruct((1, LANES), jnp.int32),
            jax.ShapeDtypeStruct((TABLE_ROWS, LANES), jnp.int32),
            jax.ShapeDtypeStruct((N_BLOCKS, LANES), jnp.int32),
        ],
        name="moe_chunk_tables",
    )(counts.reshape(N_BLOCKS, LANES))


def _chunk_copies(read_row, n_chunks, src_hbm, dst, sem, wait, skip_negative=False):
    for c in range(n_chunks):
        row = read_row(c)

        def copy(row=row, c=c):
            cp = pltpu.make_async_copy(src_hbm.at[pl.ds(pl.multiple_of(row, CHUNK), CHUNK)],
                                       dst.at[pl.ds(c * CHUNK, CHUNK)], sem)
            if wait:
                cp.wait()
            else:
                cp.start()

        if skip_negative:
            pl.when(row >= 0)(copy)
        else:
            copy()


def _expert_kernel(tend_ref, src_ref, xb_hbm, w1_ref, w3_ref, w2_ref, y_hbm,
                   xbuf, ybuf, w1_scr, w3_scr, w2_scr, in_sem, out_sem):
    e = pl.program_id(0)
    first = jnp.where(e == 0, 0, tend_ref[0, jnp.maximum(e - 1, 0)])
    last = tend_ref[0, e]
    tiles_per_row = LANES // TILE_CHUNKS

    def fetch(tile, s, wait):
        row = tile // tiles_per_row
        col = (tile % tiles_per_row) * TILE_CHUNKS
        _chunk_copies(lambda c: src_ref[row, col + c], TILE_CHUNKS, xb_hbm, xbuf.at[s], in_sem.at[s], wait)

    def store(tile, s):
        rows = pl.ds(pl.multiple_of(tile * EXPERT_TILE, EXPERT_TILE), EXPERT_TILE)
        return pltpu.make_async_copy(ybuf.at[s], y_hbm.at[rows], out_sem.at[s])

    @pl.when(last > first)
    def _():
        fetch(first, 0, False)
        w1_scr[...] = w1_ref[0, 0].astype(BF16)
        w3_scr[...] = w3_ref[0, 0].astype(BF16)
        w2_scr[...] = w2_ref[0, 0].astype(BF16)

    def tile_body(t, carry):
        k = t - first
        s = k % 2

        @pl.when(t + 1 < last)
        def _():
            fetch(t + 1, 1 - s, False)

        fetch(t, s, True)
        x = xbuf[s]
        a = _dot(x, w1_scr[...])
        b = _dot(x, w3_scr[...])
        hid = (a * jax.nn.sigmoid(a)) * b
        y = _dot(hid.astype(BF16), w2_scr[...]).astype(BF16)

        @pl.when(k >= 2)
        def _():
            store(t, s).wait()

        ybuf[s] = y
        store(t, s).start()
        return carry

    lax.fori_loop(first, last, tile_body, 0)
    n = last - first

    @pl.when(n >= 1)
    def _():
        store(first, (n - 1) % 2).wait()

    @pl.when(n >= 2)
    def _():
        store(first, n % 2).wait()

    @pl.when(e == N_EXPERTS - 1)
    def _():
        ybuf[0] = jnp.zeros((EXPERT_TILE, D_MODEL), BF16)

        def start(t, carry):
            store(t, 0).start()
            return carry

        def drain(t, carry):
            store(t, 0).wait()
            return carry

        lax.fori_loop(last, N_TILES, start, 0)
        lax.fori_loop(last, N_TILES, drain, 0)


def _experts(tile_end, disp_src, xb, w1, w3, w2, layer):
    wsel = lambda e, tend, src: (layer, e, 0, 0)
    return pl.pallas_call(
        _expert_kernel,
        grid_spec=pltpu.PrefetchScalarGridSpec(
            num_scalar_prefetch=2,
            grid=(N_EXPERTS,),
            in_specs=[
                pl.BlockSpec(memory_space=pl.ANY),
                pl.BlockSpec((1, 1, D_MODEL, D_EXPERT), wsel),
                pl.BlockSpec((1, 1, D_MODEL, D_EXPERT), wsel),
                pl.BlockSpec((1, 1, D_EXPERT, D_MODEL), wsel),
            ],
            out_specs=pl.BlockSpec(memory_space=pl.ANY),
            scratch_shapes=[pltpu.VMEM((2, EXPERT_TILE, D_MODEL), BF16),
                            pltpu.VMEM((2, EXPERT_TILE, D_MODEL), BF16),
                            pltpu.VMEM((D_MODEL, D_EXPERT), BF16),
                            pltpu.VMEM((D_MODEL, D_EXPERT), BF16),
                            pltpu.VMEM((D_EXPERT, D_MODEL), BF16),
                            pltpu.SemaphoreType.DMA((2,)),
                            pltpu.SemaphoreType.DMA((2,))],
        ),
        out_shape=jax.ShapeDtypeStruct((N_TILES * EXPERT_TILE, D_MODEL), BF16),
        compiler_params=_cparams(("arbitrary",)),
        name="moe_experts",
    )(tile_end, disp_src, xb, w1, w3, w2)


def _combine_kernel(src_ref, y_hbm, x_ref, pos_ref, gate_ref, mg_ref, o_ref, ybuf, sem, *, tm):
    i = pl.program_id(0)
    slot = i % 2

    def fetch(blk, s, wait):
        _chunk_copies(lambda c: src_ref[blk, c], BLOCK_CHUNKS, y_hbm, ybuf.at[s], sem.at[s], wait,
                      skip_negative=True)

    @pl.when(i == 0)
    def _():
        ybuf[...] = jnp.zeros_like(ybuf)
        fetch(0, 0, False)

    @pl.when(i + 1 < N_BLOCKS)
    def _():
        fetch(i + 1, 1 - slot, False)

    fetch(i, slot, True)
    pos = pos_ref[...]
    sel = pos >= 0.0
    gate = gate_ref[...]
    pos_a, pos_b = _row_slots(pos, sel)
    gate_a = jnp.sum(jnp.where(pos == pos_a, gate, 0.0), axis=-1, keepdims=True)
    gate_b = jnp.sum(jnp.where(jnp.logical_and(sel, pos == pos_b), gate, 0.0), axis=-1, keepdims=True)
    row = lax.broadcasted_iota(jnp.int32, (tm, BLOCK_ROWS), 1).astype(F32)
    weights = jnp.where(row == pos_a, gate_a, jnp.where(row == pos_b, gate_b, 0.0))
    moe = _dot(weights.astype(BF16), ybuf[slot])
    o_ref[...] = x_ref[...] + mg_ref[0] * moe


def _combine(comb_src, y, x, pos, gate, mods, layer):
    tm = MOE_BLOCK

    def gate_idx(i, s):
        return ((layer * MOD_ROWS + _mod_row(i, tm)) * 6 + 5, 0, 0)

    return pl.pallas_call(
        functools.partial(_combine_kernel, tm=tm),
        grid_spec=pltpu.PrefetchScalarGridSpec(
            num_scalar_prefetch=1,
            grid=(N_BLOCKS,),
            in_specs=[
                pl.BlockSpec(memory_space=pl.ANY),
                pl.BlockSpec((tm, D_MODEL), lambda i, s: (i, 0)),
                pl.BlockSpec((tm, LANES), lambda i, s: (i, 0)),
                pl.BlockSpec((tm, LANES), lambda i, s: (i, 0)),
                pl.BlockSpec((1, 1, D_MODEL), gate_idx),
            ],
            out_specs=pl.BlockSpec((tm, D_MODEL), lambda i, s: (i, 0)),
            scratch_shapes=[pltpu.VMEM((2, BLOCK_ROWS, D_MODEL), BF16),
                            pltpu.SemaphoreType.DMA((2,))],
        ),
        out_shape=jax.ShapeDtypeStruct((N_TOK, D_MODEL), F32),
        compiler_params=_cparams(("arbitrary",)),
        name="moe_combine",
    )(comb_src, y, x, pos, gate, mods)


def _moe(x, g, mods, layer, w_group, b_group, w_expert, b_expert, w1, w3, w2):
    pad = LANES - N_EXPERTS - N_GROUPS
    wr = jnp.concatenate([w_expert, w_group, jnp.zeros((D_MODEL, pad), F32)], axis=1)
    br = jnp.concatenate([b_expert, b_group, jnp.zeros((pad,), F32)]).reshape(1, LANES)
    xb, pos, gate, counts = _router(x, g, mods, layer, wr, br)
    tile_end, disp_src, comb_src = _chunk_tables(counts)
    y = _experts(tile_end, disp_src, xb, w1, w3, w2, layer)
    return _combine(comb_src, y, x, pos, gate, mods, layer)


def _final_kernel(x_ref, g_ref, o_ref):
    x = x_ref[...]
    ms = jnp.mean(x * x, axis=-1, keepdims=True)
    o_ref[...] = (x * lax.rsqrt(ms + EPS)) * g_ref[...]


def _final_norm(x, g, *, tm=1024):
    return pl.pallas_call(
        _final_kernel,
        grid=(N_TOK // tm,),
        in_specs=[pl.BlockSpec((tm, D_MODEL), lambda i: (i, 0)),
                  pl.BlockSpec((1, D_MODEL), lambda i: (0, 0))],
        out_specs=pl.BlockSpec((tm, D_MODEL), lambda i: (i, 0)),
        out_shape=jax.ShapeDtypeStruct((N_TOK, D_MODEL), F32),
        compiler_params=_cparams(("arbitrary",)),
        name="final_norm",
    )(x, g.reshape(1, D_MODEL))


def kernel(x_prompt, x_sample, cache_k, cache_v, c, c_ctx, ada_w, ada_b, norm1_g, norm2_g, final_g, hy_w_in, hy_conv_w, hy_conv_b, hy_f_w1, hy_f_b1, hy_f_w2, hy_f_b2, hy_f_w3, hy_f_b3, hy_f_freq, hy_bias, hy_w_out, at_w_qkv, at_lam_q1, at_lam_k1, at_lam_q2, at_lam_k2, at_subln_g, at_w_o, moe_w_group, moe_b_group, moe_w_expert, moe_b_expert, moe_w1, moe_w3, moe_w2):
    x = jnp.concatenate([x_prompt.reshape(N_PROMPT, D_MODEL), x_sample.reshape(N_SAMPLE, D_MODEL)], axis=0)
    cond = jnp.concatenate([c_ctx[None, :], c, jnp.zeros((MOD_ROWS - 1 - DEC_BATCH, D_MODEL), F32)], axis=0)
    mods = _ada_table(cond, ada_w, ada_b).reshape(DEPTH * MOD_ROWS * 6, 1, D_MODEL)
    ctx_k = cache_k.reshape(DEC_BATCH, DEPTH // 2, PAST_LEN, D_MODEL)
    ctx_v = cache_v.reshape(DEC_BATCH, DEPTH // 2, PAST_LEN, D_MODEL)

    dft = {}
    for L in (SEQ, DEC_SEQ):
        fwd, inv = _dft_mats(L)
        fwd = jnp.asarray(fwd)
        f_hi, f_lo = _split_bf16(fwd)
        dft[L] = (f_hi, f_lo, jnp.asarray(inv).astype(BF16))

    new_k, new_v = [], []
    for l in range(DEPTH):
        j = l // 2
        if l % 2 == 0:
            z = _norm_matmul(x, norm1_g[l], mods, l, 0, 1, hy_w_in, j)
            ys = []
            for L, rb0, n_seq, dt in ((SEQ, 0, BATCH, 512), (DEC_SEQ, N_PROMPT // DEC_SEQ, DEC_BATCH, 256)):
                f_hi, f_lo, finv = dft[L]
                kc = _filter_spectra(L, hy_f_w1[j], hy_f_b1[j], hy_f_w2[j], hy_f_b2[j], hy_f_w3[j],
                                     hy_f_b3[j], hy_f_freq[j], f_hi, f_lo)
                ys.append(_hyena_conv(z, hy_conv_w[j], hy_conv_b[j], f_hi, finv, kc, hy_bias[j],
                                      L=L, row_block0=rb0, n_seq=n_seq, dt=dt))
            y = jnp.concatenate(ys, axis=0)
            x = _matmul_residual(y, hy_w_out, j, x, mods, l, 2)
        else:
            lam_init = 0.8 - 0.6 * math.exp(-0.3 * l)
            qkv = _norm_matmul(x, norm1_g[l], mods, l, 0, 1, at_w_qkv, j, rope=True)
            lam_params = (at_lam_q1[j], at_lam_k1[j], at_lam_q2[j], at_lam_k2[j])
            op = _attention(qkv, lam_params, at_subln_g[j], lam_init, L=SEQ, row_block0=0,
                            n_seq=BATCH, tq=SEQ)
            os_ = _attention(qkv, lam_params, at_subln_g[j], lam_init, L=DEC_SEQ,
                             row_block0=N_PROMPT // DEC_SEQ, n_seq=DEC_BATCH, tq=256,
                             ctx=(ctx_k, ctx_v), j=j)
            o = jnp.concatenate([op, os_], axis=0)
            new_k.append(qkv[:N_PROMPT, D_MODEL:2 * D_MODEL].astype(F32).reshape(BATCH, SEQ, N_HEADS, HEAD_W))
            new_v.append(qkv[:N_PROMPT, 2 * D_MODEL:].astype(F32).reshape(BATCH, SEQ, N_HEADS, HEAD_W))
            x = _matmul_residual(o, at_w_o, j, x, mods, l, 2)
        x = _moe(x, norm2_g[l], mods, l, moe_w_group[l], moe_b_group[l], moe_w_expert[l],
                 moe_b_expert[l], moe_w1, moe_w3, moe_w2)
    out = _final_norm(x, final_g)
    y_prompt = out[:N_PROMPT].reshape(BATCH, SEQ, D_MODEL)
    y_sample = out[N_PROMPT:].reshape(DEC_BATCH, DEC_SEQ, D_MODEL)
    return (y_prompt, y_sample, jnp.stack(new_k, axis=1), jnp.stack(new_v, axis=1))
```

```python
import functools
import math

import numpy as np
import jax
import jax.numpy as jnp
from jax import lax
from jax.experimental import pallas as pl
from jax.experimental.pallas import tpu as pltpu

F32 = jnp.float32
BF16 = jnp.bfloat16

D_MODEL = 1024
BATCH = 16
SEQ = 256
DEPTH = 4
DEC_BATCH = 4
DEC_SEQ = 1024
PAST_LEN = 256
GRID_W = 64
EPS = 1e-6
HY_BANDS = 16
HY_EMB = 1 + 2 * HY_BANDS
HY_FILTER_HIDDEN = 64
HY_FAST_DECAY = 0.3
HY_SLOW_DECAY = 1.5
HY_TARGET = 1e-2
N_HEADS = 8
HEAD_DIM = 64
ROT_AXIS = HEAD_DIM // 2
ROPE_BASE = 10000.0
N_GROUPS = 4
EXPERTS_PER_GROUP = 8
N_EXPERTS = N_GROUPS * EXPERTS_PER_GROUP
D_EXPERT = D_MODEL // 4

N_PROMPT = BATCH * SEQ
N_SAMPLE = DEC_BATCH * DEC_SEQ
N_TOK = N_PROMPT + N_SAMPLE
MOD_ROWS = 8
LANES = 128
SUBLANES = 8
HEAD_W = 2 * HEAD_DIM
VMEM_LIMIT = 56 * 1024 * 1024

MOE_BLOCK = 256
N_BLOCKS = N_TOK // MOE_BLOCK
CHUNK = 2 * SUBLANES
BLOCK_ROWS = 2 * MOE_BLOCK + N_EXPERTS * CHUNK
BLOCK_CHUNKS = BLOCK_ROWS // CHUNK
EXPERT_TILE = 256
TILE_CHUNKS = EXPERT_TILE // CHUNK
N_TILES = (2 * N_TOK + N_BLOCKS * N_EXPERTS * (CHUNK - 1)) // EXPERT_TILE + N_EXPERTS


def _cparams(sem):
    return pltpu.CompilerParams(dimension_semantics=sem, vmem_limit_bytes=VMEM_LIMIT)


def _mod_row(blk, tm):
    start = blk * tm
    return jnp.where(start < N_PROMPT, 0, 1 + (start - N_PROMPT) // DEC_SEQ)


def _mod_index(layer, which, tm):
    def index_map(i, *_):
        return ((layer * MOD_ROWS + _mod_row(i, tm)) * 6 + which, 0, 0)
    return index_map


@functools.lru_cache(maxsize=None)
def _dft_mats(L):
    k = np.arange(L, dtype=np.float64)[:, None]
    j = np.arange(L, dtype=np.float64)[None, :]
    ang = np.pi * ((k * j) % (2 * L)) / L
    c = np.cos(ang)
    s = np.sin(ang)
    s[0, :] = np.where(np.arange(L) % 2 == 0, 1.0, -1.0)
    fwd = np.concatenate([c, s], axis=0)
    scale = np.full((2 * L,), 1.0 / L)
    scale[0] = scale[L] = 0.5 / L
    inv = fwd.T * scale[None, :]
    return fwd.astype(np.float32), inv.astype(np.float32)


@functools.lru_cache(maxsize=None)
def _filter_feats(L):
    pos = np.arange(L, dtype=np.float64)
    t = pos / (L - 1)
    bands = np.linspace(1e-4, HY_BANDS - 1, HY_BANDS)
    ang = (2.0 * math.pi / L) * pos[:, None] * bands[None, :]
    feats = np.concatenate([t[:, None], np.cos(ang), -np.sin(ang)], axis=-1)
    feats = np.pad(feats, ((0, 0), (0, LANES - HY_EMB)))
    deltas = np.abs(np.linspace(math.log(HY_TARGET) / HY_SLOW_DECAY,
                                math.log(HY_TARGET) / HY_FAST_DECAY, D_MODEL))
    window = np.exp(-t[:, None] * deltas[None, :])
    alt = np.where(np.arange(L) % 2 == 0, 1.0, -1.0)[:, None] * np.ones((1, LANES))
    return feats.astype(np.float32), window.astype(np.float32), alt.astype(np.float32)


@functools.lru_cache(maxsize=None)
def _rope_tables():
    pos = np.arange(DEC_SEQ)
    row = (pos // GRID_W).astype(np.float64)
    col = (pos % GRID_W).astype(np.float64)
    lane = np.arange(HEAD_W)
    d = lane % HEAD_DIM
    axis = d // ROT_AXIS
    n = d % ROT_AXIS
    half = n // (ROT_AXIS // 2)
    f = n % (ROT_AXIS // 2)
    inv = ROPE_BASE ** (-f.astype(np.float64) / (ROT_AXIS // 2))
    p = np.where(axis[None, :] == 0, row[:, None], col[:, None])
    ang = p * inv[None, :]
    sign = np.where(half == 0, -1.0, 1.0)[None, :]
    return np.cos(ang).astype(np.float32), (np.sin(ang) * sign).astype(np.float32)


def _split_bf16(a):
    hi = a.astype(BF16)
    lo = (a - hi.astype(F32)).astype(BF16)
    return hi, lo


def _dot(a, b):
    return jnp.dot(a, b, preferred_element_type=F32)


def _dot3(a_hi, a_lo, b_hi, b_lo):
    return _dot(a_hi, b_hi) + (_dot(a_hi, b_lo) + _dot(a_lo, b_hi))


def _ada_kernel(c_ref, w_ref, b_ref, o_ref):
    c = c_ref[...]
    s = c * jax.nn.sigmoid(c)
    o_ref[0] = _dot(s.astype(BF16), w_ref[0].astype(BF16)) + b_ref[0]


def _ada_table(cond, ada_w, ada_b):
    tn = 1536
    n = 6 * D_MODEL
    return pl.pallas_call(
        _ada_kernel,
        grid=(DEPTH, n // tn),
        in_specs=[
            pl.BlockSpec((MOD_ROWS, D_MODEL), lambda l, j: (0, 0)),
            pl.BlockSpec((1, D_MODEL, tn), lambda l, j: (l, 0, j)),
            pl.BlockSpec((1, 1, tn), lambda l, j: (l, 0, j)),
        ],
        out_specs=pl.BlockSpec((1, MOD_ROWS, tn), lambda l, j: (l, 0, j)),
        out_shape=jax.ShapeDtypeStruct((DEPTH, MOD_ROWS, n), F32),
        compiler_params=_cparams(("arbitrary", "arbitrary")),
        name="ada_table",
    )(cond, ada_w, ada_b.reshape(DEPTH, 1, n))


def _modulate(x, g, scale, shift):
    ms = jnp.mean(x * x, axis=-1, keepdims=True)
    y = (x * lax.rsqrt(ms + EPS)) * g
    return y * (1.0 + scale) + shift


MM_COLS = 512


def _normmm_kernel(x_ref, g_ref, sh_ref, sc_ref, w_ref, *rest, rope, tm):
    if rope:
        cos_ref, sin_ref, o_ref, w_scr = rest
    else:
        o_ref, w_scr = rest
    i = pl.program_id(0)

    @pl.when(i == 0)
    def _():
        w_scr[...] = w_ref[0].astype(BF16)

    h = _modulate(x_ref[...], g_ref[...], sc_ref[0], sh_ref[0]).astype(BF16)
    n = w_scr.shape[1]
    for j in range(n // MM_COLS):
        cols = slice(j * MM_COLS, (j + 1) * MM_COLS)
        acc = _dot(h, w_scr[:, cols])
        if not (rope and j < 2 * D_MODEL // MM_COLS):
            o_ref[:, cols] = acc.astype(o_ref.dtype)
            continue

        @pl.when(i >= N_PROMPT // tm)
        def _():
            reps = MM_COLS // HEAD_W
            cos = jnp.tile(cos_ref[...], (1, reps))
            sin = jnp.tile(sin_ref[...], (1, reps))
            lane = lax.broadcasted_iota(jnp.int32, (tm, MM_COLS), 1)
            first_half = (lane % ROT_AXIS) < (ROT_AXIS // 2)
            partner = jnp.where(first_half,
                                pltpu.roll(acc, MM_COLS - ROT_AXIS // 2, 1),
                                pltpu.roll(acc, ROT_AXIS // 2, 1))
            o_ref[:, cols] = (acc * cos + partner * sin).astype(o_ref.dtype)

        @pl.when(i < N_PROMPT // tm)
        def _():
            o_ref[:, cols] = acc.astype(o_ref.dtype)


def _norm_matmul(x, g, mods, layer, which_shift, which_scale, w, widx, *, rope=False, tm=512):
    n = w.shape[2]
    rope_blocks = DEC_SEQ // tm

    def rope_idx(i):
        return (jnp.maximum(i - N_PROMPT // tm, 0) % rope_blocks, 0)

    in_specs = [
        pl.BlockSpec((tm, D_MODEL), lambda i: (i, 0)),
        pl.BlockSpec((1, D_MODEL), lambda i: (0, 0)),
        pl.BlockSpec((1, 1, D_MODEL), _mod_index(layer, which_shift, tm)),
        pl.BlockSpec((1, 1, D_MODEL), _mod_index(layer, which_scale, tm)),
        pl.BlockSpec((1, D_MODEL, n), lambda i: (widx, 0, 0), pipeline_mode=pl.Buffered(1)),
    ]
    args = [x, g.reshape(1, D_MODEL), mods, mods, w]
    if rope:
        cos, sin = _rope_tables()
        in_specs += [pl.BlockSpec((tm, HEAD_W), rope_idx), pl.BlockSpec((tm, HEAD_W), rope_idx)]
        args += [jnp.asarray(cos), jnp.asarray(sin)]
    return pl.pallas_call(
        functools.partial(_normmm_kernel, rope=rope, tm=tm),
        grid=(N_TOK // tm,),
        in_specs=in_specs,
        out_specs=pl.BlockSpec((tm, n), lambda i: (i, 0)),
        out_shape=jax.ShapeDtypeStruct((N_TOK, n), BF16),
        scratch_shapes=[pltpu.VMEM((D_MODEL, n), BF16)],
        compiler_params=_cparams(("arbitrary",)),
        name="norm_matmul_rope" if rope else "norm_matmul",
    )(*args)


def _mmres_kernel(a_ref, w_ref, x_ref, gate_ref, o_ref, w_scr):
    @pl.when(pl.program_id(0) == 0)
    def _():
        w_scr[...] = w_ref[0].astype(BF16)

    a = a_ref[...]
    gate = gate_ref[0]
    for j in range(w_scr.shape[1] // MM_COLS):
        cols = slice(j * MM_COLS, (j + 1) * MM_COLS)
        o_ref[:, cols] = x_ref[:, cols] + gate[:, cols] * _dot(a, w_scr[:, cols])


def _matmul_residual(a, w, widx, x, mods, layer, which_gate, *, tm=512):
    k = a.shape[1]
    n = w.shape[2]
    return pl.pallas_call(
        _mmres_kernel,
        grid=(N_TOK // tm,),
        in_specs=[
            pl.BlockSpec((tm, k), lambda i: (i, 0)),
            pl.BlockSpec((1, k, n), lambda i: (widx, 0, 0), pipeline_mode=pl.Buffered(1)),
            pl.BlockSpec((tm, n), lambda i: (i, 0)),
            pl.BlockSpec((1, 1, n), _mod_index(layer, which_gate, tm)),
        ],
        out_specs=pl.BlockSpec((tm, n), lambda i: (i, 0)),
        out_shape=jax.ShapeDtypeStruct((N_TOK, n), F32),
        scratch_shapes=[pltpu.VMEM((k, n), BF16)],
        compiler_params=_cparams(("arbitrary",)),
        name="matmul_residual",
    )(a, w, x, mods)


def _filter_kernel(feats_ref, win_ref, alt_ref, w1_ref, b1_ref, w2_ref, b2_ref, fq_ref,
                   w3f_ref, b3f_ref, w3b_ref, b3b_ref, fhi_ref, flo_ref, o_ref, *, L):
    def dense(a, w_ref, b_ref):
        a_hi, a_lo = _split_bf16(a)
        w_hi, w_lo = _split_bf16(w_ref[...])
        return _dot3(a_hi, a_lo, w_hi, w_lo) + b_ref[...]

    fq = fq_ref[...]
    h = jnp.sin(fq[0:1, :] * dense(feats_ref[...], w1_ref, b1_ref))
    h = jnp.sin(fq[1:2, :] * dense(h, w2_ref, b2_ref))
    win = win_ref[...]
    hf = dense(h, w3f_ref, b3f_ref) * win
    hb = dense(h, w3b_ref, b3b_ref) * win
    row = lax.broadcasted_iota(jnp.int32, hf.shape, 0)
    hb = jnp.where(row == 0, 0.0, hb)
    s_hi, s_lo = _split_bf16(hf + hb)
    d_hi, d_lo = _split_bf16(hf - hb)
    kr = _dot3(fhi_ref[0:L, :], flo_ref[0:L, :], s_hi, s_lo)
    ks = _dot3(fhi_ref[L:2 * L, :], flo_ref[L:2 * L, :], d_hi, d_lo)
    k_nyq = jnp.sum((hf + hb) * alt_ref[:, 0:1], axis=0, keepdims=True)
    o_ref[0, 0] = kr
    o_ref[0, 1] = jnp.where(row == 0, k_nyq, kr)
    o_ref[0, 2] = jnp.where(row == 0, 0.0, ks)


def _filter_spectra(L, w1, b1, w2, b2, w3, b3, freq, f_hi, f_lo, *, dt=256):
    feats, window, alt = _filter_feats(L)
    fh = HY_FILTER_HIDDEN
    nd = D_MODEL // dt
    w1p = jnp.pad(w1, ((0, LANES - HY_EMB), (0, 0)))
    const = lambda shape: pl.BlockSpec(shape, lambda o, c: tuple(0 for _ in shape))
    return pl.pallas_call(
        functools.partial(_filter_kernel, L=L),
        grid=(2, nd),
        in_specs=[
            const((L, LANES)),
            pl.BlockSpec((L, dt), lambda o, c: (0, c)),
            const((L, LANES)),
            const((LANES, fh)), const((1, fh)), const((fh, fh)), const((1, fh)), const((2, fh)),
            pl.BlockSpec((fh, dt), lambda o, c: (0, (2 * o) * nd + c)),
            pl.BlockSpec((1, dt), lambda o, c: (0, (2 * o) * nd + c)),
            pl.BlockSpec((fh, dt), lambda o, c: (0, (2 * o + 1) * nd + c)),
            pl.BlockSpec((1, dt), lambda o, c: (0, (2 * o + 1) * nd + c)),
            const((2 * L, L)), const((2 * L, L)),
        ],
        out_specs=pl.BlockSpec((1, 3, L, dt), lambda o, c: (o, 0, 0, c)),
        out_shape=jax.ShapeDtypeStruct((2, 3, L, D_MODEL), F32),
        compiler_params=_cparams(("arbitrary", "arbitrary")),
        name=f"hyena_filter_spectra_{L}",
    )(jnp.asarray(feats), jnp.asarray(window), jnp.asarray(alt), w1p, b1.reshape(1, fh), w2,
      b2.reshape(1, fh), freq, w3, b3.reshape(1, -1), w3, b3.reshape(1, -1), f_hi, f_lo)


def _conv_kernel(v_ref, x1_ref, x2_ref, cwv_ref, cw1_ref, cw2_ref, cbv_ref, cb1_ref, cb2_ref,
                 f_ref, finv_ref, kc_ref, bias_ref, o_ref, *, L):
    row = lax.broadcasted_iota(jnp.int32, v_ref.shape, 0)

    def short_conv(u_ref, w_ref, b_ref):
        u = u_ref[...].astype(F32)
        prev = jnp.where(row == 0, 0.0, pltpu.roll(u, 1, 0))
        nxt = jnp.where(row == L - 1, 0.0, pltpu.roll(u, L - 1, 0))
        w = w_ref[...]
        return prev * w[0:1, :] + u * w[1:2, :] + nxt * w[2:3, :] + b_ref[...]

    def long_conv(u, order):
        spec = _dot(f_ref[...], u.astype(BF16))
        a = spec[0:L, :]
        b = spec[L:2 * L, :]
        kra = kc_ref[order, 0]
        krb = kc_ref[order, 1]
        ks = kc_ref[order, 2]
        prod = jnp.concatenate([a * kra - b * ks, a * ks + b * krb], axis=0)
        return _dot(finv_ref[...], prod.astype(BF16)) + u * bias_ref[order:order + 1, :]

    v = short_conv(v_ref, cwv_ref, cbv_ref)
    y = short_conv(x1_ref, cw1_ref, cb1_ref) * long_conv(v, 0)
    y = short_conv(x2_ref, cw2_ref, cb2_ref) * long_conv(y, 1)
    o_ref[...] = y.astype(o_ref.dtype)


def _hyena_conv(z, conv_w, conv_b, f_bf, finv_bf, kc, bias, *, L, row_block0, n_seq, dt):
    nd = D_MODEL // dt
    cb = conv_b.reshape(1, 3 * D_MODEL)
    seg = lambda s: pl.BlockSpec((L, dt), lambda c, b: (row_block0 + b, s * nd + c))
    cw = lambda s: pl.BlockSpec((3, dt), lambda c, b: (0, s * nd + c))
    cbs = lambda s: pl.BlockSpec((1, dt), lambda c, b: (0, s * nd + c))
    return pl.pallas_call(
        functools.partial(_conv_kernel, L=L),
        grid=(nd, n_seq),
        in_specs=[
            seg(0), seg(1), seg(2), cw(0), cw(1), cw(2), cbs(0), cbs(1), cbs(2),
            pl.BlockSpec((2 * L, L), lambda c, b: (0, 0)),
            pl.BlockSpec((L, 2 * L), lambda c, b: (0, 0)),
            pl.BlockSpec((2, 3, L, dt), lambda c, b: (0, 0, 0, c)),
            pl.BlockSpec((2, dt), lambda c, b: (0, c)),
        ],
        out_specs=pl.BlockSpec((L, dt), lambda c, b: (b, c)),
        out_shape=jax.ShapeDtypeStruct((n_seq * L, D_MODEL), BF16),
        compiler_params=_cparams(("arbitrary", "arbitrary")),
        name=f"hyena_conv_{L}",
    )(z, z, z, conv_w, conv_w, conv_w, cb, cb, cb, f_bf, finv_bf, kc, bias)


def _attn_kernel(lq1_ref, lk1_ref, lq2_ref, lk2_ref, g_ref, q_ref, k_ref, v_ref, *rest,
                 lam_init, has_ctx, tq, nq):
    if has_ctx:
        ck_ref, cv_ref, o_ref = rest
    else:
        (o_ref,) = rest
    lam = (jnp.exp(jnp.sum(lq1_ref[...] * lk1_ref[...], axis=-1, keepdims=True))
           - jnp.exp(jnp.sum(lq2_ref[...] * lk2_ref[...], axis=-1, keepdims=True)) + lam_init)
    first_map = lax.broadcasted_iota(jnp.int32, (tq, HEAD_W), 1) < HEAD_DIM
    nt = (((1,), (1,)), ((), ()))

    def q_block(rows):
        for h in range(N_HEADS):
            cols = slice(h * HEAD_W, (h + 1) * HEAD_W)
            q = q_ref[rows, cols] * (HEAD_DIM ** -0.5)
            zero = jnp.zeros_like(q)
            q2 = jnp.concatenate([jnp.where(first_map, q, zero), jnp.where(first_map, zero, q)], axis=0)
            s = lax.dot_general(q2, k_ref[:, cols], nt, preferred_element_type=F32)
            m = jnp.max(s, axis=-1, keepdims=True)
            if has_ctx:
                sc = lax.dot_general(q2, ck_ref[0, 0, :, cols].astype(BF16), nt,
                                     preferred_element_type=F32)
                m = jnp.maximum(m, jnp.max(sc, axis=-1, keepdims=True))
                ec = jnp.exp(sc - m)
            e = jnp.exp(s - m)
            den = jnp.sum(e, axis=-1, keepdims=True)
            if has_ctx:
                den = den + jnp.sum(ec, axis=-1, keepdims=True)
            ov = _dot(e.astype(BF16), v_ref[:, cols])
            if has_ctx:
                ov = ov + _dot(ec.astype(BF16), cv_ref[0, 0, :, cols].astype(BF16))
            inv = 1.0 / den
            o = ov[0:tq] * inv[0:tq] - ov[tq:2 * tq] * (lam * inv[tq:2 * tq])
            ms = jnp.mean(o * o, axis=-1, keepdims=True)
            o = (o * lax.rsqrt(ms + EPS)) * g_ref[...] * (1.0 - lam_init)
            o_ref[rows, cols] = o.astype(o_ref.dtype)

    if nq == 1:
        q_block(slice(0, tq))
    else:
        def body(qi, carry):
            q_block(pl.ds(pl.multiple_of(qi * tq, tq), tq))
            return carry
        lax.fori_loop(0, nq, body, 0)


def _attention(qkv, lam_params, subln_g, lam_init, *, L, row_block0, n_seq, tq, ctx=None, j=0):
    small = pl.BlockSpec((1, HEAD_DIM), lambda b: (0, 0))
    in_specs = [small, small, small, small,
                pl.BlockSpec((1, HEAD_W), lambda b: (0, 0)),
                pl.BlockSpec((L, D_MODEL), lambda b: (row_block0 + b, 0)),
                pl.BlockSpec((L, D_MODEL), lambda b: (row_block0 + b, 1)),
                pl.BlockSpec((L, D_MODEL), lambda b: (row_block0 + b, 2))]
    args = [p.reshape(1, HEAD_DIM) for p in lam_params] + [subln_g.reshape(1, HEAD_W), qkv, qkv, qkv]
    if ctx is not None:
        ctx_spec = pl.BlockSpec((1, 1, PAST_LEN, D_MODEL), lambda b: (b, j, 0, 0))
        in_specs += [ctx_spec, ctx_spec]
        args += list(ctx)
    return pl.pallas_call(
        functools.partial(_attn_kernel, lam_init=lam_init, has_ctx=ctx is not None, tq=tq, nq=L // tq),
        grid=(n_seq,),
        in_specs=in_specs,
        out_specs=pl.BlockSpec((L, D_MODEL), lambda b: (b, 0)),
        out_shape=jax.ShapeDtypeStruct((n_seq * L, D_MODEL), BF16),
        compiler_params=_cparams(("arbitrary",)),
        name="diff_attention_ctx" if ctx is not None else "diff_attention",
    )(*args)


def _row_slots(pos, sel):
    big = float(BLOCK_ROWS)
    pos_a = jnp.max(pos, axis=-1, keepdims=True)
    pos_b = jnp.min(jnp.where(sel, pos, big), axis=-1, keepdims=True)
    return pos_a, pos_b


def _route_block(x, g, scale, shift, wr, br):
    tm = MOE_BLOCK
    h = _modulate(x, g, scale, shift).astype(BF16)
    logits = _dot(h, wr) + br
    lane = lax.broadcasted_iota(jnp.int32, (tm, LANES), 1)
    neg = -jnp.inf
    is_grp = jnp.logical_and(lane >= N_EXPERTS, lane < N_EXPERTS + N_GROUPS)
    lg = jnp.where(is_grp, logits, neg)
    mg = jnp.max(lg, axis=-1, keepdims=True)
    g_gate = 1.0 / jnp.sum(jnp.exp(lg - mg), axis=-1, keepdims=True)
    g_idx = jnp.min(jnp.where(lg == mg, lane - N_EXPERTS, N_GROUPS), axis=-1, keepdims=True)
    in_grp = jnp.logical_and(lane < N_EXPERTS, lane // EXPERTS_PER_GROUP == g_idx)
    le = jnp.where(in_grp, logits, neg)
    m1 = jnp.max(le, axis=-1, keepdims=True)
    i1 = jnp.min(jnp.where(le == m1, lane, LANES), axis=-1, keepdims=True)
    le2 = jnp.where(lane == i1, neg, le)
    m2 = jnp.max(le2, axis=-1, keepdims=True)
    i2 = jnp.min(jnp.where(le2 == m2, lane, LANES), axis=-1, keepdims=True)
    e2 = jnp.exp(m2 - m1)
    inv = 1.0 / (1.0 + e2)
    gate = jnp.where(lane == i1, g_gate * inv, jnp.where(lane == i2, g_gate * (e2 * inv), 0.0))
    sel = jnp.logical_or(lane == i1, lane == i2)
    onehot = jnp.where(sel, 1.0, 0.0)
    r = lax.broadcasted_iota(jnp.int32, (tm, tm), 0)
    c = lax.broadcasted_iota(jnp.int32, (tm, tm), 1)
    rank = _dot(jnp.where(c < r, 1.0, 0.0).astype(BF16), onehot.astype(BF16))
    count = jnp.sum(onehot, axis=0, keepdims=True)
    padded = jnp.ceil(count * (1.0 / CHUNK)) * CHUNK
    er = lax.broadcasted_iota(jnp.int32, (LANES, LANES), 0)
    ec = lax.broadcasted_iota(jnp.int32, (LANES, LANES), 1)
    before = jnp.where(er < ec, 1.0, 0.0).astype(BF16)
    start = _dot(jnp.broadcast_to(padded, (SUBLANES, LANES)).astype(BF16), before)[0:1, :]
    pos = jnp.where(sel, start + rank, -1.0)
    pos_a, pos_b = _row_slots(pos, sel)
    slot = lax.broadcasted_iota(jnp.int32, (tm, BLOCK_ROWS), 1).astype(F32)
    pick = jnp.where(jnp.logical_or(slot == pos_a, slot == pos_b), 1.0, 0.0).astype(BF16)
    xb = lax.dot_general(pick, h, (((0,), (0,)), ((), ())), preferred_element_type=F32).astype(BF16)
    return xb, pos, gate, count.astype(jnp.int32)


ROUTER_BLOCKS = 2


def _router_kernel(x_ref, g_ref, sh_ref, sc_ref, wr_ref, br_ref, xb_ref, pos_ref, gate_ref, cnt_ref):
    wr = wr_ref[...].astype(BF16)
    for k in range(ROUTER_BLOCKS):
        rows = slice(k * MOE_BLOCK, (k + 1) * MOE_BLOCK)
        xb, pos, gate, count = _route_block(x_ref[rows, :], g_ref[...], sc_ref[0], sh_ref[0], wr, br_ref[...])
        xb_ref[k * BLOCK_ROWS:(k + 1) * BLOCK_ROWS, :] = xb
        pos_ref[rows, :] = pos
        gate_ref[rows, :] = gate
        cnt_ref[k] = count


def _router(x, g, mods, layer, wr, br):
    tm = ROUTER_BLOCKS * MOE_BLOCK
    return pl.pallas_call(
        _router_kernel,
        grid=(N_TOK // tm,),
        in_specs=[
            pl.BlockSpec((tm, D_MODEL), lambda i: (i, 0)),
            pl.BlockSpec((1, D_MODEL), lambda i: (0, 0)),
            pl.BlockSpec((1, 1, D_MODEL), _mod_index(layer, 3, tm)),
            pl.BlockSpec((1, 1, D_MODEL), _mod_index(layer, 4, tm)),
            pl.BlockSpec((D_MODEL, LANES), lambda i: (0, 0)),
            pl.BlockSpec((1, LANES), lambda i: (0, 0)),
        ],
        out_specs=[
            pl.BlockSpec((ROUTER_BLOCKS * BLOCK_ROWS, D_MODEL), lambda i: (i, 0)),
            pl.BlockSpec((tm, LANES), lambda i: (i, 0)),
            pl.BlockSpec((tm, LANES), lambda i: (i, 0)),
            pl.BlockSpec((ROUTER_BLOCKS, 1, LANES), lambda i: (i, 0, 0)),
        ],
        out_shape=[
            jax.ShapeDtypeStruct((N_BLOCKS * BLOCK_ROWS, D_MODEL), BF16),
            jax.ShapeDtypeStruct((N_TOK, LANES), F32),
            jax.ShapeDtypeStruct((N_TOK, LANES), F32),
            jax.ShapeDtypeStruct((N_BLOCKS, 1, LANES), jnp.int32),
        ],
        compiler_params=_cparams(("arbitrary",)),
        name="moe_router",
    )(x, g.reshape(1, D_MODEL), mods, mods, wr, br)


TABLE_ROWS = -(-(N_TILES * TILE_CHUNKS) // LANES)


def _tables_kernel(cnt_ref, tend_ref, disp_ref, comb_ref):
    ne = N_EXPERTS
    cnt = jnp.concatenate([cnt_ref[...].astype(F32), jnp.zeros((LANES - N_BLOCKS, LANES), F32)], axis=0)
    m = jnp.ceil(cnt * (1.0 / CHUNK))
    r = lax.broadcasted_iota(jnp.int32, (LANES, LANES), 0)
    c = lax.broadcasted_iota(jnp.int32, (LANES, LANES), 1)
    mb = m.astype(BF16)
    bstart = _dot(mb, jnp.where(r < c, 1.0, 0.0).astype(BF16))
    cm = _dot(jnp.where(c < r, 1.0, 0.0).astype(BF16), mb)
    total = jnp.sum(m, axis=0, keepdims=True)
    tiles = jnp.ceil(total * (1.0 / TILE_CHUNKS))
    tile_end = _dot(jnp.broadcast_to(tiles, (SUBLANES, LANES)).astype(BF16),
                    jnp.where(r <= c, 1.0, 0.0).astype(BF16))[0:1, :]
    choff = (tile_end - tiles) * TILE_CHUNKS
    tend_ref[...] = tile_end.astype(jnp.int32)

    start_t = (choff + cm).T
    m_t = m.T
    bstart_t = bstart.T
    lane = lax.broadcasted_iota(jnp.int32, (ne, LANES), 1).astype(F32)
    acc = [jnp.zeros((ne, LANES), F32) for _ in range(TABLE_ROWS)]
    for b in range(N_BLOCKS):
        lo = start_t[0:ne, b:b + 1]
        hi = lo + m_t[0:ne, b:b + 1]
        bs = bstart_t[0:ne, b:b + 1]
        fwd = (b * BLOCK_CHUNKS) + bs - lo
        for j in range(TABLE_ROWS):
            g = lane + float(j * LANES)
            acc[j] = acc[j] + jnp.where(g >= lo, jnp.where(g < hi, fwd + g, 0.0), 0.0)
        back = jnp.where(lane >= bs, jnp.where(lane < hi - lo + bs, (lo - bs + lane) * CHUNK + 1.0, 0.0), 0.0)
        comb_ref[b:b + 1, :] = (jnp.sum(back, axis=0, keepdims=True) - 1.0).astype(jnp.int32)
    for j in range(TABLE_ROWS):
        disp_ref[j:j + 1, :] = (jnp.sum(acc[j], axis=0, keepdims=True) * CHUNK).astype(jnp.int32)


def _chunk_tables(counts):
    return pl.pallas_call(
        _tables_kernel,
        out_shape=[
            jax.ShapeDtypeStruct((1, LANES), jnp.int32),
            jax.ShapeDtypeStruct((TABLE_ROWS, LANES), jnp.int32),
            jax.ShapeDtypeStruct((N_BLOCKS, LANES), jnp.int32),
        ],
        name="moe_chunk_tables",
    )(counts.reshape(N_BLOCKS, LANES))


def _chunk_copies(read_row, n_chunks, src_hbm, dst, sem, wait, skip_negative=False):
    for c in range(n_chunks):
        row = read_row(c)

        def copy(row=row, c=c):
            cp = pltpu.make_async_copy(src_hbm.at[pl.ds(pl.multiple_of(row, CHUNK), CHUNK)],
                                       dst.at[pl.ds(c * CHUNK, CHUNK)], sem)
            if wait:
                cp.wait()
            else:
                cp.start()

        if skip_negative:
            pl.when(row >= 0)(copy)
        else:
            copy()


def _expert_kernel(tend_ref, src_ref, xb_hbm, w1_ref, w3_ref, w2_ref, y_hbm,
                   xbuf, ybuf, w13_scr, w2_scr, in_sem, out_sem):
    e = pl.program_id(0)
    first = jnp.where(e == 0, 0, tend_ref[0, jnp.maximum(e - 1, 0)])
    last = tend_ref[0, e]
    n_used = tend_ref[0, N_EXPERTS - 1]
    tiles_per_row = LANES // TILE_CHUNKS

    def fetch(tile, s, wait):
        row = tile // tiles_per_row
        col = (tile % tiles_per_row) * TILE_CHUNKS
        _chunk_copies(lambda c: src_ref[row, col + c], TILE_CHUNKS, xb_hbm, xbuf.at[s], in_sem.at[s], wait)

    def store(tile, s):
        rows = pl.ds(pl.multiple_of(tile * EXPERT_TILE, EXPERT_TILE), EXPERT_TILE)
        return pltpu.make_async_copy(ybuf.at[s], y_hbm.at[rows], out_sem.at[s])

    @pl.when(last > first)
    def _():
        @pl.when(first == 0)
        def _():
            fetch(0, 0, False)

        w13_scr[:, 0:D_EXPERT] = w1_ref[0, 0].astype(BF16)
        w13_scr[:, D_EXPERT:2 * D_EXPERT] = w3_ref[0, 0].astype(BF16)
        w2_scr[...] = w2_ref[0, 0].astype(BF16)

    def tile_body(t, carry):
        s = t % 2

        @pl.when(t + 1 < n_used)
        def _():
            fetch(t + 1, 1 - s, False)

        fetch(t, s, True)
        ab = _dot(xbuf[s], w13_scr[...])
        a = ab[:, 0:D_EXPERT]
        hid = (a * jax.nn.sigmoid(a)) * ab[:, D_EXPERT:2 * D_EXPERT]
        y = _dot(hid.astype(BF16), w2_scr[...]).astype(BF16)

        @pl.when(t >= 2)
        def _():
            store(t, s).wait()

        ybuf[s] = y
        store(t, s).start()
        return carry

    lax.fori_loop(first, last, tile_body, 0)

    @pl.when(e == N_EXPERTS - 1)
    def _():
        @pl.when(n_used >= 1)
        def _():
            store(0, (n_used - 1) % 2).wait()

        @pl.when(n_used >= 2)
        def _():
            store(0, n_used % 2).wait()

        ybuf[0] = jnp.zeros((EXPERT_TILE, D_MODEL), BF16)

        def start(t, carry):
            store(t, 0).start()
            return carry

        def drain(t, carry):
            store(t, 0).wait()
            return carry

        lax.fori_loop(n_used, N_TILES, start, 0)
        lax.fori_loop(n_used, N_TILES, drain, 0)


def _experts(tile_end, disp_src, xb, w1, w3, w2, layer):
    wsel = lambda e, tend, src: (layer, e, 0, 0)
    return pl.pallas_call(
        _expert_kernel,
        grid_spec=pltpu.PrefetchScalarGridSpec(
            num_scalar_prefetch=2,
            grid=(N_EXPERTS,),
            in_specs=[
                pl.BlockSpec(memory_space=pl.ANY),
                pl.BlockSpec((1, 1, D_MODEL, D_EXPERT), wsel),
                pl.BlockSpec((1, 1, D_MODEL, D_EXPERT), wsel),
                pl.BlockSpec((1, 1, D_EXPERT, D_MODEL), wsel),
            ],
            out_specs=pl.BlockSpec(memory_space=pl.ANY),
            scratch_shapes=[pltpu.VMEM((2, EXPERT_TILE, D_MODEL), BF16),
                            pltpu.VMEM((2, EXPERT_TILE, D_MODEL), BF16),
                            pltpu.VMEM((D_MODEL, 2 * D_EXPERT), BF16),
                            pltpu.VMEM((D_EXPERT, D_MODEL), BF16),
                            pltpu.SemaphoreType.DMA((2,)),
                            pltpu.SemaphoreType.DMA((2,))],
        ),
        out_shape=jax.ShapeDtypeStruct((N_TILES * EXPERT_TILE, D_MODEL), BF16),
        compiler_params=_cparams(("arbitrary",)),
        name="moe_experts",
    )(tile_end, disp_src, xb, w1, w3, w2)


def _combine_kernel(src_ref, y_hbm, x_ref, pos_ref, gate_ref, mg_ref, o_ref, ybuf, sem, *, tm):
    i = pl.program_id(0)
    slot = i % 2

    def fetch(blk, s, wait):
        _chunk_copies(lambda c: src_ref[blk, c], BLOCK_CHUNKS, y_hbm, ybuf.at[s], sem.at[s], wait,
                      skip_negative=True)

    @pl.when(i == 0)
    def _():
        ybuf[...] = jnp.zeros_like(ybuf)
        fetch(0, 0, False)

    @pl.when(i + 1 < N_BLOCKS)
    def _():
        fetch(i + 1, 1 - slot, False)

    fetch(i, slot, True)
    pos = pos_ref[...]
    sel = pos >= 0.0
    gate = gate_ref[...]
    pos_a, pos_b = _row_slots(pos, sel)
    gate_a = jnp.sum(jnp.where(pos == pos_a, gate, 0.0), axis=-1, keepdims=True)
    gate_b = jnp.sum(jnp.where(jnp.logical_and(sel, pos == pos_b), gate, 0.0), axis=-1, keepdims=True)
    row = lax.broadcasted_iota(jnp.int32, (tm, BLOCK_ROWS), 1).astype(F32)
    weights = jnp.where(row == pos_a, gate_a, jnp.where(row == pos_b, gate_b, 0.0))
    moe = _dot(weights.astype(BF16), ybuf[slot])
    o_ref[...] = x_ref[...] + mg_ref[0] * moe


def _combine(comb_src, y, x, pos, gate, mods, layer):
    tm = MOE_BLOCK

    def gate_idx(i, s):
        return ((layer * MOD_ROWS + _mod_row(i, tm)) * 6 + 5, 0, 0)

    return pl.pallas_call(
        functools.partial(_combine_kernel, tm=tm),
        grid_spec=pltpu.PrefetchScalarGridSpec(
            num_scalar_prefetch=1,
            grid=(N_BLOCKS,),
            in_specs=[
                pl.BlockSpec(memory_space=pl.ANY),
                pl.BlockSpec((tm, D_MODEL), lambda i, s: (i, 0)),
                pl.BlockSpec((tm, LANES), lambda i, s: (i, 0)),
                pl.BlockSpec((tm, LANES), lambda i, s: (i, 0)),
                pl.BlockSpec((1, 1, D_MODEL), gate_idx),
            ],
            out_specs=pl.BlockSpec((tm, D_MODEL), lambda i, s: (i, 0)),
            scratch_shapes=[pltpu.VMEM((2, BLOCK_ROWS, D_MODEL), BF16),
                            pltpu.SemaphoreType.DMA((2,))],
        ),
        out_shape=jax.ShapeDtypeStruct((N_TOK, D_MODEL), F32),
        compiler_params=_cparams(("arbitrary",)),
        name="moe_combine",
    )(comb_src, y, x, pos, gate, mods)


def _moe(x, g, mods, layer, w_group, b_group, w_expert, b_expert, w1, w3, w2):
    pad = LANES - N_EXPERTS - N_GROUPS
    wr = jnp.concatenate([w_expert, w_group, jnp.zeros((D_MODEL, pad), F32)], axis=1)
    br = jnp.concatenate([b_expert, b_group, jnp.zeros((pad,), F32)]).reshape(1, LANES)
    xb, pos, gate, counts = _router(x, g, mods, layer, wr, br)
    tile_end, disp_src, comb_src = _chunk_tables(counts)
    y = _experts(tile_end, disp_src, xb, w1, w3, w2, layer)
    return _combine(comb_src, y, x, pos, gate, mods, layer)


def _final_kernel(x_ref, g_ref, o_ref):
    x = x_ref[...]
    ms = jnp.mean(x * x, axis=-1, keepdims=True)
    o_ref[...] = (x * lax.rsqrt(ms + EPS)) * g_ref[...]


def _final_norm(x, g, *, tm=1024):
    return pl.pallas_call(
        _final_kernel,
        grid=(N_TOK // tm,),
        in_specs=[pl.BlockSpec((tm, D_MODEL), lambda i: (i, 0)),
                  pl.BlockSpec((1, D_MODEL), lambda i: (0, 0))],
        out_specs=pl.BlockSpec((tm, D_MODEL), lambda i: (i, 0)),
        out_shape=jax.ShapeDtypeStruct((N_TOK, D_MODEL), F32),
        compiler_params=_cparams(("arbitrary",)),
        name="final_norm",
    )(x, g.reshape(1, D_MODEL))


def kernel(x_prompt, x_sample, cache_k, cache_v, c, c_ctx, ada_w, ada_b, norm1_g, norm2_g, final_g, hy_w_in, hy_conv_w, hy_conv_b, hy_f_w1, hy_f_b1, hy_f_w2, hy_f_b2, hy_f_w3, hy_f_b3, hy_f_freq, hy_bias, hy_w_out, at_w_qkv, at_lam_q1, at_lam_k1, at_lam_q2, at_lam_k2, at_subln_g, at_w_o, moe_w_group, moe_b_group, moe_w_expert, moe_b_expert, moe_w1, moe_w3, moe_w2):
    x = jnp.concatenate([x_prompt.reshape(N_PROMPT, D_MODEL), x_sample.reshape(N_SAMPLE, D_MODEL)], axis=0)
    cond = jnp.concatenate([c_ctx[None, :], c, jnp.zeros((MOD_ROWS - 1 - DEC_BATCH, D_MODEL), F32)], axis=0)
    mods = _ada_table(cond, ada_w, ada_b).reshape(DEPTH * MOD_ROWS * 6, 1, D_MODEL)
    ctx_k = cache_k.reshape(DEC_BATCH, DEPTH // 2, PAST_LEN, D_MODEL)
    ctx_v = cache_v.reshape(DEC_BATCH, DEPTH // 2, PAST_LEN, D_MODEL)

    dft = {}
    for L in (SEQ, DEC_SEQ):
        fwd, inv = _dft_mats(L)
        fwd = jnp.asarray(fwd)
        f_hi, f_lo = _split_bf16(fwd)
        dft[L] = (f_hi, f_lo, jnp.asarray(inv).astype(BF16))

    new_k, new_v = [], []
    for l in range(DEPTH):
        j = l // 2
        if l % 2 == 0:
            z = _norm_matmul(x, norm1_g[l], mods, l, 0, 1, hy_w_in, j)
            ys = []
            for L, rb0, n_seq, dt in ((SEQ, 0, BATCH, 512), (DEC_SEQ, N_PROMPT // DEC_SEQ, DEC_BATCH, 256)):
                f_hi, f_lo, finv = dft[L]
                kc = _filter_spectra(L, hy_f_w1[j], hy_f_b1[j], hy_f_w2[j], hy_f_b2[j], hy_f_w3[j],
                                     hy_f_b3[j], hy_f_freq[j], f_hi, f_lo)
                ys.append(_hyena_conv(z, hy_conv_w[j], hy_conv_b[j], f_hi, finv, kc, hy_bias[j],
                                      L=L, row_block0=rb0, n_seq=n_seq, dt=dt))
            y = jnp.concatenate(ys, axis=0)
            x = _matmul_residual(y, hy_w_out, j, x, mods, l, 2)
        else:
            lam_init = 0.8 - 0.6 * math.exp(-0.3 * l)
            qkv = _norm_matmul(x, norm1_g[l], mods, l, 0, 1, at_w_qkv, j, rope=True)
            lam_params = (at_lam_q1[j], at_lam_k1[j], at_lam_q2[j], at_lam_k2[j])
            op = _attention(qkv, lam_params, at_subln_g[j], lam_init, L=SEQ, row_block0=0,
                            n_seq=BATCH, tq=SEQ)
            os_ = _attention(qkv, lam_params, at_subln_g[j], lam_init, L=DEC_SEQ,
                             row_block0=N_PROMPT // DEC_SEQ, n_seq=DEC_BATCH, tq=256,
                             ctx=(ctx_k, ctx_v), j=j)
            o = jnp.concatenate([op, os_], axis=0)
            new_k.append(qkv[:N_PROMPT, D_MODEL:2 * D_MODEL].astype(F32).reshape(BATCH, SEQ, N_HEADS, HEAD_W))
            new_v.append(qkv[:N_PROMPT, 2 * D_MODEL:].astype(F32).reshape(BATCH, SEQ, N_HEADS, HEAD_W))
            x = _matmul_residual(o, at_w_o, j, x, mods, l, 2)
        x = _moe(x, norm2_g[l], mods, l, moe_w_group[l], moe_b_group[l], moe_w_expert[l],
                 moe_b_expert[l], moe_w1, moe_w3, moe_w2)
    out = _final_norm(x, final_g)
    y_prompt = out[:N_PROMPT].reshape(BATCH, SEQ, D_MODEL)
    y_sample = out[N_PROMPT:].reshape(DEC_BATCH, DEC_SEQ, D_MODEL)
    return (y_prompt, y_sample, jnp.stack(new_k, axis=1), jnp.stack(new_v, axis=1))
```

```python
import functools
import math

import numpy as np
import jax
import jax.numpy as jnp
from jax import lax
from jax.experimental import pallas as pl
from jax.experimental.pallas import tpu as pltpu

F32 = jnp.float32
BF16 = jnp.bfloat16

D_MODEL = 1024
BATCH = 16
SEQ = 256
DEPTH = 4
DEC_BATCH = 4
DEC_SEQ = 1024
PAST_LEN = 256
GRID_W = 64
EPS = 1e-6
HY_BANDS = 16
HY_EMB = 1 + 2 * HY_BANDS
HY_FILTER_HIDDEN = 64
HY_FAST_DECAY = 0.3
HY_SLOW_DECAY = 1.5
HY_TARGET = 1e-2
N_HEADS = 8
HEAD_DIM = 64
ROT_AXIS = HEAD_DIM // 2
ROPE_BASE = 10000.0
N_GROUPS = 4
EXPERTS_PER_GROUP = 8
N_EXPERTS = N_GROUPS * EXPERTS_PER_GROUP
D_EXPERT = D_MODEL // 4

N_PROMPT = BATCH * SEQ
N_SAMPLE = DEC_BATCH * DEC_SEQ
N_TOK = N_PROMPT + N_SAMPLE
MOD_ROWS = 8
LANES = 128
SUBLANES = 8
HEAD_W = 2 * HEAD_DIM
VMEM_LIMIT = 56 * 1024 * 1024

MOE_BLOCK = 256
N_BLOCKS = N_TOK // MOE_BLOCK
CHUNK = 2 * SUBLANES
BLOCK_ROWS = 2 * MOE_BLOCK + N_EXPERTS * CHUNK
BLOCK_CHUNKS = BLOCK_ROWS // CHUNK
EXPERT_TILE = 256
TILE_CHUNKS = EXPERT_TILE // CHUNK
N_TILES = (2 * N_TOK + N_BLOCKS * N_EXPERTS * (CHUNK - 1)) // EXPERT_TILE + N_EXPERTS


def _cparams(sem):
    return pltpu.CompilerParams(dimension_semantics=sem, vmem_limit_bytes=VMEM_LIMIT)


def _mod_row(blk, tm):
    start = blk * tm
    return jnp.where(start < N_PROMPT, 0, 1 + (start - N_PROMPT) // DEC_SEQ)


def _mod_index(layer, which, tm):
    def index_map(i, *_):
        return ((layer * MOD_ROWS + _mod_row(i, tm)) * 6 + which, 0, 0)
    return index_map


@functools.lru_cache(maxsize=None)
def _dft_mats(L):
    k = np.arange(L, dtype=np.float64)[:, None]
    j = np.arange(L, dtype=np.float64)[None, :]
    ang = np.pi * ((k * j) % (2 * L)) / L
    c = np.cos(ang)
    s = np.sin(ang)
    s[0, :] = np.where(np.arange(L) % 2 == 0, 1.0, -1.0)
    fwd = np.concatenate([c, s], axis=0)
    scale = np.full((2 * L,), 1.0 / L)
    scale[0] = scale[L] = 0.5 / L
    inv = fwd.T * scale[None, :]
    return fwd.astype(np.float32), inv.astype(np.float32)


@functools.lru_cache(maxsize=None)
def _filter_feats(L):
    pos = np.arange(L, dtype=np.float64)
    t = pos / (L - 1)
    bands = np.linspace(1e-4, HY_BANDS - 1, HY_BANDS)
    ang = (2.0 * math.pi / L) * pos[:, None] * bands[None, :]
    feats = np.concatenate([t[:, None], np.cos(ang), -np.sin(ang)], axis=-1)
    feats = np.pad(feats, ((0, 0), (0, LANES - HY_EMB)))
    deltas = np.abs(np.linspace(math.log(HY_TARGET) / HY_SLOW_DECAY,
                                math.log(HY_TARGET) / HY_FAST_DECAY, D_MODEL))
    window = np.exp(-t[:, None] * deltas[None, :])
    alt = np.where(np.arange(L) % 2 == 0, 1.0, -1.0)[:, None] * np.ones((1, LANES))
    return feats.astype(np.float32), window.astype(np.float32), alt.astype(np.float32)


@functools.lru_cache(maxsize=None)
def _rope_tables():
    pos = np.arange(DEC_SEQ)
    row = (pos // GRID_W).astype(np.float64)
    col = (pos % GRID_W).astype(np.float64)
    lane = np.arange(HEAD_W)
    d = lane % HEAD_DIM
    axis = d // ROT_AXIS
    n = d % ROT_AXIS
    half = n // (ROT_AXIS // 2)
    f = n % (ROT_AXIS // 2)
    inv = ROPE_BASE ** (-f.astype(np.float64) / (ROT_AXIS // 2))
    p = np.where(axis[None, :] == 0, row[:, None], col[:, None])
    ang = p * inv[None, :]
    sign = np.where(half == 0, -1.0, 1.0)[None, :]
    return np.cos(ang).astype(np.float32), (np.sin(ang) * sign).astype(np.float32)


def _split_bf16(a):
    hi = a.astype(BF16)
    lo = (a - hi.astype(F32)).astype(BF16)
    return hi, lo


def _dot(a, b):
    return jnp.dot(a, b, preferred_element_type=F32)


def _dot3(a_hi, a_lo, b_hi, b_lo):
    return _dot(a_hi, b_hi) + (_dot(a_hi, b_lo) + _dot(a_lo, b_hi))


def _ada_kernel(c_ref, w_ref, b_ref, o_ref):
    c = c_ref[...]
    s = c * jax.nn.sigmoid(c)
    o_ref[0] = _dot(s.astype(BF16), w_ref[0].astype(BF16)) + b_ref[0]


def _ada_table(cond, ada_w, ada_b):
    tn = 1536
    n = 6 * D_MODEL
    return pl.pallas_call(
        _ada_kernel,
        grid=(DEPTH, n // tn),
        in_specs=[
            pl.BlockSpec((MOD_ROWS, D_MODEL), lambda l, j: (0, 0)),
            pl.BlockSpec((1, D_MODEL, tn), lambda l, j: (l, 0, j)),
            pl.BlockSpec((1, 1, tn), lambda l, j: (l, 0, j)),
        ],
        out_specs=pl.BlockSpec((1, MOD_ROWS, tn), lambda l, j: (l, 0, j)),
        out_shape=jax.ShapeDtypeStruct((DEPTH, MOD_ROWS, n), F32),
        compiler_params=_cparams(("arbitrary", "arbitrary")),
        name="ada_table",
    )(cond, ada_w, ada_b.reshape(DEPTH, 1, n))


def _modulate(x, g, scale, shift):
    ms = jnp.mean(x * x, axis=-1, keepdims=True)
    y = (x * lax.rsqrt(ms + EPS)) * g
    return y * (1.0 + scale) + shift


MM_COLS = 512


def _normmm_kernel(x_ref, g_ref, sh_ref, sc_ref, w_ref, *rest, rope, tm):
    if rope:
        cos_ref, sin_ref, o_ref, w_scr = rest
    else:
        o_ref, w_scr = rest
    i = pl.program_id(0)

    @pl.when(i == 0)
    def _():
        w_scr[...] = w_ref[0].astype(BF16)

    h = _modulate(x_ref[...], g_ref[...], sc_ref[0], sh_ref[0]).astype(BF16)
    n = w_scr.shape[1]
    for j in range(n // MM_COLS):
        cols = slice(j * MM_COLS, (j + 1) * MM_COLS)
        acc = _dot(h, w_scr[:, cols])
        if not (rope and j < 2 * D_MODEL // MM_COLS):
            o_ref[:, cols] = acc.astype(o_ref.dtype)
            continue

        @pl.when(i >= N_PROMPT // tm)
        def _():
            reps = MM_COLS // HEAD_W
            cos = jnp.tile(cos_ref[...], (1, reps))
            sin = jnp.tile(sin_ref[...], (1, reps))
            lane = lax.broadcasted_iota(jnp.int32, (tm, MM_COLS), 1)
            first_half = (lane % ROT_AXIS) < (ROT_AXIS // 2)
            partner = jnp.where(first_half,
                                pltpu.roll(acc, MM_COLS - ROT_AXIS // 2, 1),
                                pltpu.roll(acc, ROT_AXIS // 2, 1))
            o_ref[:, cols] = (acc * cos + partner * sin).astype(o_ref.dtype)

        @pl.when(i < N_PROMPT // tm)
        def _():
            o_ref[:, cols] = acc.astype(o_ref.dtype)


def _norm_matmul(x, g, mods, layer, which_shift, which_scale, w, widx, *, rope=False, tm=512):
    n = w.shape[2]
    rope_blocks = DEC_SEQ // tm

    def rope_idx(i):
        return (jnp.maximum(i - N_PROMPT // tm, 0) % rope_blocks, 0)

    in_specs = [
        pl.BlockSpec((tm, D_MODEL), lambda i: (i, 0)),
        pl.BlockSpec((1, D_MODEL), lambda i: (0, 0)),
        pl.BlockSpec((1, 1, D_MODEL), _mod_index(layer, which_shift, tm)),
        pl.BlockSpec((1, 1, D_MODEL), _mod_index(layer, which_scale, tm)),
        pl.BlockSpec((1, D_MODEL, n), lambda i: (widx, 0, 0), pipeline_mode=pl.Buffered(1)),
    ]
    args = [x, g.reshape(1, D_MODEL), mods, mods, w]
    if rope:
        cos, sin = _rope_tables()
        in_specs += [pl.BlockSpec((tm, HEAD_W), rope_idx), pl.BlockSpec((tm, HEAD_W), rope_idx)]
        args += [jnp.asarray(cos), jnp.asarray(sin)]
    return pl.pallas_call(
        functools.partial(_normmm_kernel, rope=rope, tm=tm),
        grid=(N_TOK // tm,),
        in_specs=in_specs,
        out_specs=pl.BlockSpec((tm, n), lambda i: (i, 0)),
        out_shape=jax.ShapeDtypeStruct((N_TOK, n), BF16),
        scratch_shapes=[pltpu.VMEM((D_MODEL, n), BF16)],
        compiler_params=_cparams(("arbitrary",)),
        name="norm_matmul_rope" if rope else "norm_matmul",
    )(*args)


def _mmres_kernel(ap_ref, as_ref, w_ref, x_ref, gate_ref, o_ref, w_scr, *, tm):
    i = pl.program_id(0)

    @pl.when(i == 0)
    def _():
        w_scr[...] = w_ref[0].astype(BF16)

    a = jnp.where(i < N_PROMPT // tm, ap_ref[...], as_ref[...])
    gate = gate_ref[0]
    for j in range(w_scr.shape[1] // MM_COLS):
        cols = slice(j * MM_COLS, (j + 1) * MM_COLS)
        o_ref[:, cols] = x_ref[:, cols] + gate[:, cols] * _dot(a, w_scr[:, cols])


def _matmul_residual(a_prompt, a_sample, w, widx, x, mods, layer, which_gate, *, tm=512):
    k = a_prompt.shape[1]
    n = w.shape[2]
    np_blocks = N_PROMPT // tm
    return pl.pallas_call(
        functools.partial(_mmres_kernel, tm=tm),
        grid=(N_TOK // tm,),
        in_specs=[
            pl.BlockSpec((tm, k), lambda i: (jnp.minimum(i, np_blocks - 1), 0)),
            pl.BlockSpec((tm, k), lambda i: (jnp.maximum(i - np_blocks, 0), 0)),
            pl.BlockSpec((1, k, n), lambda i: (widx, 0, 0), pipeline_mode=pl.Buffered(1)),
            pl.BlockSpec((tm, n), lambda i: (i, 0)),
            pl.BlockSpec((1, 1, n), _mod_index(layer, which_gate, tm)),
        ],
        out_specs=pl.BlockSpec((tm, n), lambda i: (i, 0)),
        out_shape=jax.ShapeDtypeStruct((N_TOK, n), F32),
        scratch_shapes=[pltpu.VMEM((k, n), BF16)],
        compiler_params=_cparams(("arbitrary",)),
        name="matmul_residual",
    )(a_prompt, a_sample, w, x, mods)


def _filter_kernel(feats_ref, win_ref, alt_ref, w1_ref, b1_ref, w2_ref, b2_ref, fq_ref,
                   w3f_ref, b3f_ref, w3b_ref, b3b_ref, fhi_ref, flo_ref, o_ref, *, L):
    def dense(a, w_ref, b_ref):
        a_hi, a_lo = _split_bf16(a)
        w_hi, w_lo = _split_bf16(w_ref[...])
        return _dot3(a_hi, a_lo, w_hi, w_lo) + b_ref[...]

    fq = fq_ref[...]
    h = jnp.sin(fq[0:1, :] * dense(feats_ref[...], w1_ref, b1_ref))
    h = jnp.sin(fq[1:2, :] * dense(h, w2_ref, b2_ref))
    win = win_ref[...]
    hf = dense(h, w3f_ref, b3f_ref) * win
    hb = dense(h, w3b_ref, b3b_ref) * win
    row = lax.broadcasted_iota(jnp.int32, hf.shape, 0)
    hb = jnp.where(row == 0, 0.0, hb)
    s_hi, s_lo = _split_bf16(hf + hb)
    d_hi, d_lo = _split_bf16(hf - hb)
    kr = _dot3(fhi_ref[0:L, :], flo_ref[0:L, :], s_hi, s_lo)
    ks = _dot3(fhi_ref[L:2 * L, :], flo_ref[L:2 * L, :], d_hi, d_lo)
    k_nyq = jnp.sum((hf + hb) * alt_ref[:, 0:1], axis=0, keepdims=True)
    o_ref[0, 0] = kr
    o_ref[0, 1] = jnp.where(row == 0, k_nyq, kr)
    o_ref[0, 2] = jnp.where(row == 0, 0.0, ks)


def _filter_spectra(L, w1, b1, w2, b2, w3, b3, freq, f_hi, f_lo, *, dt=256):
    feats, window, alt = _filter_feats(L)
    fh = HY_FILTER_HIDDEN
    nd = D_MODEL // dt
    w1p = jnp.pad(w1, ((0, LANES - HY_EMB), (0, 0)))
    const = lambda shape: pl.BlockSpec(shape, lambda o, c: tuple(0 for _ in shape))
    return pl.pallas_call(
        functools.partial(_filter_kernel, L=L),
        grid=(2, nd),
        in_specs=[
            const((L, LANES)),
            pl.BlockSpec((L, dt), lambda o, c: (0, c)),
            const((L, LANES)),
            const((LANES, fh)), const((1, fh)), const((fh, fh)), const((1, fh)), const((2, fh)),
            pl.BlockSpec((fh, dt), lambda o, c: (0, (2 * o) * nd + c)),
            pl.BlockSpec((1, dt), lambda o, c: (0, (2 * o) * nd + c)),
            pl.BlockSpec((fh, dt), lambda o, c: (0, (2 * o + 1) * nd + c)),
            pl.BlockSpec((1, dt), lambda o, c: (0, (2 * o + 1) * nd + c)),
            const((2 * L, L)), const((2 * L, L)),
        ],
        out_specs=pl.BlockSpec((1, 3, L, dt), lambda o, c: (o, 0, 0, c)),
        out_shape=jax.ShapeDtypeStruct((2, 3, L, D_MODEL), F32),
        compiler_params=_cparams(("arbitrary", "arbitrary")),
        name=f"hyena_filter_spectra_{L}",
    )(jnp.asarray(feats), jnp.asarray(window), jnp.asarray(alt), w1p, b1.reshape(1, fh), w2,
      b2.reshape(1, fh), freq, w3, b3.reshape(1, -1), w3, b3.reshape(1, -1), f_hi, f_lo)


def _conv_kernel(v_ref, x1_ref, x2_ref, cwv_ref, cw1_ref, cw2_ref, cbv_ref, cb1_ref, cb2_ref,
                 f_ref, finv_ref, kc_ref, bias_ref, o_ref, *, L):
    row = lax.broadcasted_iota(jnp.int32, v_ref.shape, 0)

    def short_conv(u_ref, w_ref, b_ref):
        u = u_ref[...].astype(F32)
        prev = jnp.where(row == 0, 0.0, pltpu.roll(u, 1, 0))
        nxt = jnp.where(row == L - 1, 0.0, pltpu.roll(u, L - 1, 0))
        w = w_ref[...]
        return prev * w[0:1, :] + u * w[1:2, :] + nxt * w[2:3, :] + b_ref[...]

    def long_conv(u, order):
        spec = _dot(f_ref[...], u.astype(BF16))
        a = spec[0:L, :]
        b = spec[L:2 * L, :]
        kra = kc_ref[order, 0]
        krb = kc_ref[order, 1]
        ks = kc_ref[order, 2]
        prod = jnp.concatenate([a * kra - b * ks, a * ks + b * krb], axis=0)
        return _dot(finv_ref[...], prod.astype(BF16)) + u * bias_ref[order:order + 1, :]

    v = short_conv(v_ref, cwv_ref, cbv_ref)
    y = short_conv(x1_ref, cw1_ref, cb1_ref) * long_conv(v, 0)
    y = short_conv(x2_ref, cw2_ref, cb2_ref) * long_conv(y, 1)
    o_ref[...] = y.astype(o_ref.dtype)


def _hyena_conv(z, conv_w, conv_b, f_bf, finv_bf, kc, bias, *, L, row_block0, n_seq, dt):
    nd = D_MODEL // dt
    cb = conv_b.reshape(1, 3 * D_MODEL)
    seg = lambda s: pl.BlockSpec((L, dt), lambda c, b: (row_block0 + b, s * nd + c))
    cw = lambda s: pl.BlockSpec((3, dt), lambda c, b: (0, s * nd + c))
    cbs = lambda s: pl.BlockSpec((1, dt), lambda c, b: (0, s * nd + c))
    return pl.pallas_call(
        functools.partial(_conv_kernel, L=L),
        grid=(nd, n_seq),
        in_specs=[
            seg(0), seg(1), seg(2), cw(0), cw(1), cw(2), cbs(0), cbs(1), cbs(2),
            pl.BlockSpec((2 * L, L), lambda c, b: (0, 0)),
            pl.BlockSpec((L, 2 * L), lambda c, b: (0, 0)),
            pl.BlockSpec((2, 3, L, dt), lambda c, b: (0, 0, 0, c)),
            pl.BlockSpec((2, dt), lambda c, b: (0, c)),
        ],
        out_specs=pl.BlockSpec((L, dt), lambda c, b: (b, c)),
        out_shape=jax.ShapeDtypeStruct((n_seq * L, D_MODEL), BF16),
        compiler_params=_cparams(("arbitrary", "arbitrary")),
        name=f"hyena_conv_{L}",
    )(z, z, z, conv_w, conv_w, conv_w, cb, cb, cb, f_bf, finv_bf, kc, bias)


def _attn_kernel(lq1_ref, lk1_ref, lq2_ref, lk2_ref, g_ref, q_ref, k_ref, v_ref, *rest,
                 lam_init, has_ctx, tq, nq):
    if has_ctx:
        ck_ref, cv_ref, o_ref = rest
    else:
        (o_ref,) = rest
    lam = (jnp.exp(jnp.sum(lq1_ref[...] * lk1_ref[...], axis=-1, keepdims=True))
           - jnp.exp(jnp.sum(lq2_ref[...] * lk2_ref[...], axis=-1, keepdims=True)) + lam_init)
    first_map = lax.broadcasted_iota(jnp.int32, (tq, HEAD_W), 1) < HEAD_DIM
    nt = (((1,), (1,)), ((), ()))

    def q_block(rows):
        for h in range(N_HEADS):
            cols = slice(h * HEAD_W, (h + 1) * HEAD_W)
            q = q_ref[rows, cols] * (HEAD_DIM ** -0.5)
            zero = jnp.zeros_like(q)
            q2 = jnp.concatenate([jnp.where(first_map, q, zero), jnp.where(first_map, zero, q)], axis=0)
            s = lax.dot_general(q2, k_ref[:, cols], nt, preferred_element_type=F32)
            m = jnp.max(s, axis=-1, keepdims=True)
            if has_ctx:
                sc = lax.dot_general(q2, ck_ref[0, 0, :, cols].astype(BF16), nt,
                                     preferred_element_type=F32)
                m = jnp.maximum(m, jnp.max(sc, axis=-1, keepdims=True))
                ec = jnp.exp(sc - m)
            e = jnp.exp(s - m)
            den = jnp.sum(e, axis=-1, keepdims=True)
            if has_ctx:
                den = den + jnp.sum(ec, axis=-1, keepdims=True)
            ov = _dot(e.astype(BF16), v_ref[:, cols])
            if has_ctx:
                ov = ov + _dot(ec.astype(BF16), cv_ref[0, 0, :, cols].astype(BF16))
            inv = 1.0 / den
            o = ov[0:tq] * inv[0:tq] - ov[tq:2 * tq] * (lam * inv[tq:2 * tq])
            ms = jnp.mean(o * o, axis=-1, keepdims=True)
            o = (o * lax.rsqrt(ms + EPS)) * g_ref[...] * (1.0 - lam_init)
            o_ref[rows, cols] = o.astype(o_ref.dtype)

    if nq == 1:
        q_block(slice(0, tq))
    else:
        def body(qi, carry):
            q_block(pl.ds(pl.multiple_of(qi * tq, tq), tq))
            return carry
        lax.fori_loop(0, nq, body, 0)


def _attention(qkv, lam_params, subln_g, lam_init, *, L, row_block0, n_seq, tq, ctx=None, j=0):
    small = pl.BlockSpec((1, HEAD_DIM), lambda b: (0, 0))
    in_specs = [small, small, small, small,
                pl.BlockSpec((1, HEAD_W), lambda b: (0, 0)),
                pl.BlockSpec((L, D_MODEL), lambda b: (row_block0 + b, 0)),
                pl.BlockSpec((L, D_MODEL), lambda b: (row_block0 + b, 1)),
                pl.BlockSpec((L, D_MODEL), lambda b: (row_block0 + b, 2))]
    args = [p.reshape(1, HEAD_DIM) for p in lam_params] + [subln_g.reshape(1, HEAD_W), qkv, qkv, qkv]
    if ctx is not None:
        ctx_spec = pl.BlockSpec((1, 1, PAST_LEN, D_MODEL), lambda b: (b, j, 0, 0))
        in_specs += [ctx_spec, ctx_spec]
        args += list(ctx)
    return pl.pallas_call(
        functools.partial(_attn_kernel, lam_init=lam_init, has_ctx=ctx is not None, tq=tq, nq=L // tq),
        grid=(n_seq,),
        in_specs=in_specs,
        out_specs=pl.BlockSpec((L, D_MODEL), lambda b: (b, 0)),
        out_shape=jax.ShapeDtypeStruct((n_seq * L, D_MODEL), BF16),
        compiler_params=_cparams(("arbitrary",)),
        name="diff_attention_ctx" if ctx is not None else "diff_attention",
    )(*args)


def _row_slots(pos, sel):
    big = float(BLOCK_ROWS)
    pos_a = jnp.max(pos, axis=-1, keepdims=True)
    pos_b = jnp.min(jnp.where(sel, pos, big), axis=-1, keepdims=True)
    return pos_a, pos_b


def _route_block(x, g, scale, shift, wr, br):
    tm = MOE_BLOCK
    h = _modulate(x, g, scale, shift).astype(BF16)
    logits = _dot(h, wr) + br
    lane = lax.broadcasted_iota(jnp.int32, (tm, LANES), 1)
    neg = -jnp.inf
    is_grp = jnp.logical_and(lane >= N_EXPERTS, lane < N_EXPERTS + N_GROUPS)
    lg = jnp.where(is_grp, logits, neg)
    mg = jnp.max(lg, axis=-1, keepdims=True)
    g_gate = 1.0 / jnp.sum(jnp.exp(lg - mg), axis=-1, keepdims=True)
    g_idx = jnp.min(jnp.where(lg == mg, lane - N_EXPERTS, N_GROUPS), axis=-1, keepdims=True)
    in_grp = jnp.logical_and(lane < N_EXPERTS, lane // EXPERTS_PER_GROUP == g_idx)
    le = jnp.where(in_grp, logits, neg)
    m1 = jnp.max(le, axis=-1, keepdims=True)
    i1 = jnp.min(jnp.where(le == m1, lane, LANES), axis=-1, keepdims=True)
    le2 = jnp.where(lane == i1, neg, le)
    m2 = jnp.max(le2, axis=-1, keepdims=True)
    i2 = jnp.min(jnp.where(le2 == m2, lane, LANES), axis=-1, keepdims=True)
    e2 = jnp.exp(m2 - m1)
    inv = 1.0 / (1.0 + e2)
    gate = jnp.where(lane == i1, g_gate * inv, jnp.where(lane == i2, g_gate * (e2 * inv), 0.0))
    sel = jnp.logical_or(lane == i1, lane == i2)
    onehot = jnp.where(sel, 1.0, 0.0)
    r = lax.broadcasted_iota(jnp.int32, (tm, tm), 0)
    c = lax.broadcasted_iota(jnp.int32, (tm, tm), 1)
    rank = _dot(jnp.where(c < r, 1.0, 0.0).astype(BF16), onehot.astype(BF16))
    count = jnp.sum(onehot, axis=0, keepdims=True)
    padded = jnp.ceil(count * (1.0 / CHUNK)) * CHUNK
    er = lax.broadcasted_iota(jnp.int32, (LANES, LANES), 0)
    ec = lax.broadcasted_iota(jnp.int32, (LANES, LANES), 1)
    before = jnp.where(er < ec, 1.0, 0.0).astype(BF16)
    start = _dot(jnp.broadcast_to(padded, (SUBLANES, LANES)).astype(BF16), before)[0:1, :]
    pos = jnp.where(sel, start + rank, -1.0)
    pos_a, pos_b = _row_slots(pos, sel)
    slot = lax.broadcasted_iota(jnp.int32, (tm, BLOCK_ROWS), 1).astype(F32)
    pick = jnp.where(jnp.logical_or(slot == pos_a, slot == pos_b), 1.0, 0.0).astype(BF16)
    xb = lax.dot_general(pick, h, (((0,), (0,)), ((), ())), preferred_element_type=F32).astype(BF16)
    return xb, pos, gate, count.astype(jnp.int32)


ROUTER_BLOCKS = 2


def _router_kernel(x_ref, g_ref, sh_ref, sc_ref, wr_ref, br_ref, xb_ref, pos_ref, gate_ref, cnt_ref):
    wr = wr_ref[...].astype(BF16)
    for k in range(ROUTER_BLOCKS):
        rows = slice(k * MOE_BLOCK, (k + 1) * MOE_BLOCK)
        xb, pos, gate, count = _route_block(x_ref[rows, :], g_ref[...], sc_ref[0], sh_ref[0], wr, br_ref[...])
        xb_ref[k * BLOCK_ROWS:(k + 1) * BLOCK_ROWS, :] = xb
        pos_ref[rows, :] = pos
        gate_ref[rows, :] = gate
        cnt_ref[k] = count


def _router(x, g, mods, layer, wr, br):
    tm = ROUTER_BLOCKS * MOE_BLOCK
    return pl.pallas_call(
        _router_kernel,
        grid=(N_TOK // tm,),
        in_specs=[
            pl.BlockSpec((tm, D_MODEL), lambda i: (i, 0)),
            pl.BlockSpec((1, D_MODEL), lambda i: (0, 0)),
            pl.BlockSpec((1, 1, D_MODEL), _mod_index(layer, 3, tm)),
            pl.BlockSpec((1, 1, D_MODEL), _mod_index(layer, 4, tm)),
            pl.BlockSpec((D_MODEL, LANES), lambda i: (0, 0)),
            pl.BlockSpec((1, LANES), lambda i: (0, 0)),
        ],
        out_specs=[
            pl.BlockSpec((ROUTER_BLOCKS * BLOCK_ROWS, D_MODEL), lambda i: (i, 0)),
            pl.BlockSpec((tm, LANES), lambda i: (i, 0)),
            pl.BlockSpec((tm, LANES), lambda i: (i, 0)),
            pl.BlockSpec((ROUTER_BLOCKS, 1, LANES), lambda i: (i, 0, 0)),
        ],
        out_shape=[
            jax.ShapeDtypeStruct((N_BLOCKS * BLOCK_ROWS, D_MODEL), BF16),
            jax.ShapeDtypeStruct((N_TOK, LANES), F32),
            jax.ShapeDtypeStruct((N_TOK, LANES), F32),
            jax.ShapeDtypeStruct((N_BLOCKS, 1, LANES), jnp.int32),
        ],
        compiler_params=_cparams(("arbitrary",)),
        name="moe_router",
    )(x, g.reshape(1, D_MODEL), mods, mods, wr, br)


TABLE_ROWS = -(-(N_TILES * TILE_CHUNKS) // LANES)


def _tables_kernel(cnt_ref, tend_ref, disp_ref, comb_ref):
    ne = N_EXPERTS
    cnt = jnp.concatenate([cnt_ref[...].astype(F32), jnp.zeros((LANES - N_BLOCKS, LANES), F32)], axis=0)
    m = jnp.ceil(cnt * (1.0 / CHUNK))
    r = lax.broadcasted_iota(jnp.int32, (LANES, LANES), 0)
    c = lax.broadcasted_iota(jnp.int32, (LANES, LANES), 1)
    mb = m.astype(BF16)
    bstart = _dot(mb, jnp.where(r < c, 1.0, 0.0).astype(BF16))
    cm = _dot(jnp.where(c < r, 1.0, 0.0).astype(BF16), mb)
    total = jnp.sum(m, axis=0, keepdims=True)
    tiles = jnp.ceil(total * (1.0 / TILE_CHUNKS))
    tile_end = _dot(jnp.broadcast_to(tiles, (SUBLANES, LANES)).astype(BF16),
                    jnp.where(r <= c, 1.0, 0.0).astype(BF16))[0:1, :]
    choff = (tile_end - tiles) * TILE_CHUNKS
    tend_ref[...] = tile_end.astype(jnp.int32)

    start_t = (choff + cm).T
    m_t = m.T
    bstart_t = bstart.T
    lane = lax.broadcasted_iota(jnp.int32, (ne, LANES), 1).astype(F32)
    acc = [jnp.zeros((ne, LANES), F32) for _ in range(TABLE_ROWS)]
    for b in range(N_BLOCKS):
        lo = start_t[0:ne, b:b + 1]
        hi = lo + m_t[0:ne, b:b + 1]
        bs = bstart_t[0:ne, b:b + 1]
        fwd = (b * BLOCK_CHUNKS) + bs - lo
        for j in range(TABLE_ROWS):
            g = lane + float(j * LANES)
            acc[j] = acc[j] + jnp.where(g >= lo, jnp.where(g < hi, fwd + g, 0.0), 0.0)
        back = jnp.where(lane >= bs, jnp.where(lane < hi - lo + bs, (lo - bs + lane) * CHUNK + 1.0, 0.0), 0.0)
        comb_ref[b:b + 1, :] = (jnp.sum(back, axis=0, keepdims=True) - 1.0).astype(jnp.int32)
    for j in range(TABLE_ROWS):
        disp_ref[j:j + 1, :] = (jnp.sum(acc[j], axis=0, keepdims=True) * CHUNK).astype(jnp.int32)


def _chunk_tables(counts):
    return pl.pallas_call(
        _tables_kernel,
        out_shape=[
            jax.ShapeDtypeStruct((1, LANES), jnp.int32),
            jax.ShapeDtypeStruct((TABLE_ROWS, LANES), jnp.int32),
            jax.ShapeDtypeStruct((N_BLOCKS, LANES), jnp.int32),
        ],
        name="moe_chunk_tables",
    )(counts.reshape(N_BLOCKS, LANES))


def _chunk_copies(read_row, n_chunks, src_hbm, dst, sem, wait, skip_negative=False):
    for c in range(n_chunks):
        row = read_row(c)

        def copy(row=row, c=c):
            cp = pltpu.make_async_copy(src_hbm.at[pl.ds(pl.multiple_of(row, CHUNK), CHUNK)],
                                       dst.at[pl.ds(c * CHUNK, CHUNK)], sem)
            if wait:
                cp.wait()
            else:
                cp.start()

        if skip_negative:
            pl.when(row >= 0)(copy)
        else:
            copy()


IN_DEPTH = 3


def _expert_kernel(tend_ref, src_ref, xb_hbm, w1_ref, w3_ref, w2_ref, y_hbm,
                   xbuf, ybuf, w13_scr, w2_scr, in_sem, out_sem):
    e = pl.program_id(0)
    first = jnp.where(e == 0, 0, tend_ref[0, jnp.maximum(e - 1, 0)])
    last = tend_ref[0, e]
    n_used = tend_ref[0, N_EXPERTS - 1]
    tiles_per_row = LANES // TILE_CHUNKS

    def fetch(tile, s, wait):
        row = tile // tiles_per_row
        col = (tile % tiles_per_row) * TILE_CHUNKS
        _chunk_copies(lambda c: src_ref[row, col + c], TILE_CHUNKS, xb_hbm, xbuf.at[s], in_sem.at[s], wait)

    def store(tile, s):
        rows = pl.ds(pl.multiple_of(tile * EXPERT_TILE, EXPERT_TILE), EXPERT_TILE)
        return pltpu.make_async_copy(ybuf.at[s], y_hbm.at[rows], out_sem.at[s])

    @pl.when(last > first)
    def _():
        @pl.when(first == 0)
        def _():
            fetch(0, 0, False)

            @pl.when(n_used > 1)
            def _():
                fetch(1, 1, False)

        w13_scr[:, 0:D_EXPERT] = w1_ref[0, 0].astype(BF16)
        w13_scr[:, D_EXPERT:2 * D_EXPERT] = w3_ref[0, 0].astype(BF16)
        w2_scr[...] = w2_ref[0, 0].astype(BF16)

    def tile_body(t, carry):
        s = t % 2

        @pl.when(t + (IN_DEPTH - 1) < n_used)
        def _():
            fetch(t + (IN_DEPTH - 1), (t + (IN_DEPTH - 1)) % IN_DEPTH, False)

        fetch(t, t % IN_DEPTH, True)
        ab = _dot(xbuf[t % IN_DEPTH], w13_scr[...])
        a = ab[:, 0:D_EXPERT]
        hid = (a * jax.nn.sigmoid(a)) * ab[:, D_EXPERT:2 * D_EXPERT]
        y = _dot(hid.astype(BF16), w2_scr[...]).astype(BF16)

        @pl.when(t >= 2)
        def _():
            store(t, s).wait()

        ybuf[s] = y
        store(t, s).start()
        return carry

    lax.fori_loop(first, last, tile_body, 0)

    @pl.when(e == N_EXPERTS - 1)
    def _():
        @pl.when(n_used >= 1)
        def _():
            store(0, (n_used - 1) % 2).wait()

        @pl.when(n_used >= 2)
        def _():
            store(0, n_used % 2).wait()

        ybuf[0] = jnp.zeros((EXPERT_TILE, D_MODEL), BF16)

        def start(t, carry):
            store(t, 0).start()
            return carry

        def drain(t, carry):
            store(t, 0).wait()
            return carry

        lax.fori_loop(n_used, N_TILES, start, 0)
        lax.fori_loop(n_used, N_TILES, drain, 0)


def _experts(tile_end, disp_src, xb, w1, w3, w2, layer):
    wsel = lambda e, tend, src: (layer, e, 0, 0)
    return pl.pallas_call(
        _expert_kernel,
        grid_spec=pltpu.PrefetchScalarGridSpec(
            num_scalar_prefetch=2,
            grid=(N_EXPERTS,),
            in_specs=[
                pl.BlockSpec(memory_space=pl.ANY),
                pl.BlockSpec((1, 1, D_MODEL, D_EXPERT), wsel),
                pl.BlockSpec((1, 1, D_MODEL, D_EXPERT), wsel),
                pl.BlockSpec((1, 1, D_EXPERT, D_MODEL), wsel),
            ],
            out_specs=pl.BlockSpec(memory_space=pl.ANY),
            scratch_shapes=[pltpu.VMEM((IN_DEPTH, EXPERT_TILE, D_MODEL), BF16),
                            pltpu.VMEM((2, EXPERT_TILE, D_MODEL), BF16),
                            pltpu.VMEM((D_MODEL, 2 * D_EXPERT), BF16),
                            pltpu.VMEM((D_EXPERT, D_MODEL), BF16),
                            pltpu.SemaphoreType.DMA((IN_DEPTH,)),
                            pltpu.SemaphoreType.DMA((2,))],
        ),
        out_shape=jax.ShapeDtypeStruct((N_TILES * EXPERT_TILE, D_MODEL), BF16),
        compiler_params=_cparams(("arbitrary",)),
        name="moe_experts",
    )(tile_end, disp_src, xb, w1, w3, w2)


def _combine_kernel(src_ref, y_hbm, x_ref, pos_ref, gate_ref, mg_ref, o_ref, ybuf, sem, *, tm):
    i = pl.program_id(0)
    slot = i % 2

    def fetch(blk, s, wait):
        _chunk_copies(lambda c: src_ref[blk, c], BLOCK_CHUNKS, y_hbm, ybuf.at[s], sem.at[s], wait,
                      skip_negative=True)

    @pl.when(i == 0)
    def _():
        ybuf[...] = jnp.zeros_like(ybuf)
        fetch(0, 0, False)

    @pl.when(i + 1 < N_BLOCKS)
    def _():
        fetch(i + 1, 1 - slot, False)

    fetch(i, slot, True)
    pos = pos_ref[...]
    sel = pos >= 0.0
    gate = gate_ref[...]
    pos_a, pos_b = _row_slots(pos, sel)
    gate_a = jnp.sum(jnp.where(pos == pos_a, gate, 0.0), axis=-1, keepdims=True)
    gate_b = jnp.sum(jnp.where(jnp.logical_and(sel, pos == pos_b), gate, 0.0), axis=-1, keepdims=True)
    row = lax.broadcasted_iota(jnp.int32, (tm, BLOCK_ROWS), 1).astype(F32)
    weights = jnp.where(row == pos_a, gate_a, jnp.where(row == pos_b, gate_b, 0.0))
    moe = _dot(weights.astype(BF16), ybuf[slot])
    o_ref[...] = x_ref[...] + mg_ref[0] * moe


def _combine(comb_src, y, x, pos, gate, mods, layer):
    tm = MOE_BLOCK

    def gate_idx(i, s):
        return ((layer * MOD_ROWS + _mod_row(i, tm)) * 6 + 5, 0, 0)

    return pl.pallas_call(
        functools.partial(_combine_kernel, tm=tm),
        grid_spec=pltpu.PrefetchScalarGridSpec(
            num_scalar_prefetch=1,
            grid=(N_BLOCKS,),
            in_specs=[
                pl.BlockSpec(memory_space=pl.ANY),
                pl.BlockSpec((tm, D_MODEL), lambda i, s: (i, 0)),
                pl.BlockSpec((tm, LANES), lambda i, s: (i, 0)),
                pl.BlockSpec((tm, LANES), lambda i, s: (i, 0)),
                pl.BlockSpec((1, 1, D_MODEL), gate_idx),
            ],
            out_specs=pl.BlockSpec((tm, D_MODEL), lambda i, s: (i, 0)),
            scratch_shapes=[pltpu.VMEM((2, BLOCK_ROWS, D_MODEL), BF16),
                            pltpu.SemaphoreType.DMA((2,))],
        ),
        out_shape=jax.ShapeDtypeStruct((N_TOK, D_MODEL), F32),
        compiler_params=_cparams(("arbitrary",)),
        name="moe_combine",
    )(comb_src, y, x, pos, gate, mods)


def _moe(x, g, mods, layer, w_group, b_group, w_expert, b_expert, w1, w3, w2):
    pad = LANES - N_EXPERTS - N_GROUPS
    wr = jnp.concatenate([w_expert, w_group, jnp.zeros((D_MODEL, pad), F32)], axis=1)
    br = jnp.concatenate([b_expert, b_group, jnp.zeros((pad,), F32)]).reshape(1, LANES)
    xb, pos, gate, counts = _router(x, g, mods, layer, wr, br)
    tile_end, disp_src, comb_src = _chunk_tables(counts)
    y = _experts(tile_end, disp_src, xb, w1, w3, w2, layer)
    return _combine(comb_src, y, x, pos, gate, mods, layer)


def _final_kernel(x_ref, g_ref, o_ref):
    x = x_ref[...]
    ms = jnp.mean(x * x, axis=-1, keepdims=True)
    o_ref[...] = (x * lax.rsqrt(ms + EPS)) * g_ref[...]


def _final_norm(x, g, *, tm=1024):
    return pl.pallas_call(
        _final_kernel,
        grid=(N_TOK // tm,),
        in_specs=[pl.BlockSpec((tm, D_MODEL), lambda i: (i, 0)),
                  pl.BlockSpec((1, D_MODEL), lambda i: (0, 0))],
        out_specs=pl.BlockSpec((tm, D_MODEL), lambda i: (i, 0)),
        out_shape=jax.ShapeDtypeStruct((N_TOK, D_MODEL), F32),
        compiler_params=_cparams(("arbitrary",)),
        name="final_norm",
    )(x, g.reshape(1, D_MODEL))


def kernel(x_prompt, x_sample, cache_k, cache_v, c, c_ctx, ada_w, ada_b, norm1_g, norm2_g, final_g, hy_w_in, hy_conv_w, hy_conv_b, hy_f_w1, hy_f_b1, hy_f_w2, hy_f_b2, hy_f_w3, hy_f_b3, hy_f_freq, hy_bias, hy_w_out, at_w_qkv, at_lam_q1, at_lam_k1, at_lam_q2, at_lam_k2, at_subln_g, at_w_o, moe_w_group, moe_b_group, moe_w_expert, moe_b_expert, moe_w1, moe_w3, moe_w2):
    x = jnp.concatenate([x_prompt.reshape(N_PROMPT, D_MODEL), x_sample.reshape(N_SAMPLE, D_MODEL)], axis=0)
    cond = jnp.concatenate([c_ctx[None, :], c, jnp.zeros((MOD_ROWS - 1 - DEC_BATCH, D_MODEL), F32)], axis=0)
    mods = _ada_table(cond, ada_w, ada_b).reshape(DEPTH * MOD_ROWS * 6, 1, D_MODEL)
    ctx_k = cache_k.reshape(DEC_BATCH, DEPTH // 2, PAST_LEN, D_MODEL)
    ctx_v = cache_v.reshape(DEC_BATCH, DEPTH // 2, PAST_LEN, D_MODEL)

    dft = {}
    for L in (SEQ, DEC_SEQ):
        fwd, inv = _dft_mats(L)
        fwd = jnp.asarray(fwd)
        f_hi, f_lo = _split_bf16(fwd)
        dft[L] = (f_hi, f_lo, jnp.asarray(inv).astype(BF16))

    new_k, new_v = [], []
    for l in range(DEPTH):
        j = l // 2
        if l % 2 == 0:
            z = _norm_matmul(x, norm1_g[l], mods, l, 0, 1, hy_w_in, j)
            ys = []
            for L, rb0, n_seq, dt in ((SEQ, 0, BATCH, 512), (DEC_SEQ, N_PROMPT // DEC_SEQ, DEC_BATCH, 256)):
                f_hi, f_lo, finv = dft[L]
                kc = _filter_spectra(L, hy_f_w1[j], hy_f_b1[j], hy_f_w2[j], hy_f_b2[j], hy_f_w3[j],
                                     hy_f_b3[j], hy_f_freq[j], f_hi, f_lo)
                ys.append(_hyena_conv(z, hy_conv_w[j], hy_conv_b[j], f_hi, finv, kc, hy_bias[j],
                                      L=L, row_block0=rb0, n_seq=n_seq, dt=dt))
            x = _matmul_residual(ys[0], ys[1], hy_w_out, j, x, mods, l, 2)
        else:
            lam_init = 0.8 - 0.6 * math.exp(-0.3 * l)
            qkv = _norm_matmul(x, norm1_g[l], mods, l, 0, 1, at_w_qkv, j, rope=True)
            lam_params = (at_lam_q1[j], at_lam_k1[j], at_lam_q2[j], at_lam_k2[j])
            op = _attention(qkv, lam_params, at_subln_g[j], lam_init, L=SEQ, row_block0=0,
                            n_seq=BATCH, tq=SEQ)
            os_ = _attention(qkv, lam_params, at_subln_g[j], lam_init, L=DEC_SEQ,
                             row_block0=N_PROMPT // DEC_SEQ, n_seq=DEC_BATCH, tq=256,
                             ctx=(ctx_k, ctx_v), j=j)
            new_k.append(qkv[:N_PROMPT, D_MODEL:2 * D_MODEL].astype(F32).reshape(BATCH, SEQ, N_HEADS, HEAD_W))
            new_v.append(qkv[:N_PROMPT, 2 * D_MODEL:].astype(F32).reshape(BATCH, SEQ, N_HEADS, HEAD_W))
            x = _matmul_residual(op, os_, at_w_o, j, x, mods, l, 2)
        x = _moe(x, norm2_g[l], mods, l, moe_w_group[l], moe_b_group[l], moe_w_expert[l],
                 moe_b_expert[l], moe_w1, moe_w3, moe_w2)
    out = _final_norm(x, final_g)
    y_prompt = out[:N_PROMPT].reshape(BATCH, SEQ, D_MODEL)
    y_sample = out[N_PROMPT:].reshape(DEC_BATCH, DEC_SEQ, D_MODEL)
    return (y_prompt, y_sample, jnp.stack(new_k, axis=1), jnp.stack(new_v, axis=1))
```

```python
import functools
import math

import numpy as np
import jax
import jax.numpy as jnp
from jax import lax
from jax.experimental import pallas as pl
from jax.experimental.pallas import tpu as pltpu

F32 = jnp.float32
BF16 = jnp.bfloat16

D_MODEL = 1024
BATCH = 16
SEQ = 256
DEPTH = 4
DEC_BATCH = 4
DEC_SEQ = 1024
PAST_LEN = 256
GRID_W = 64
EPS = 1e-6
HY_BANDS = 16
HY_EMB = 1 + 2 * HY_BANDS
HY_FILTER_HIDDEN = 64
HY_FAST_DECAY = 0.3
HY_SLOW_DECAY = 1.5
HY_TARGET = 1e-2
N_HEADS = 8
HEAD_DIM = 64
ROT_AXIS = HEAD_DIM // 2
ROPE_BASE = 10000.0
N_GROUPS = 4
EXPERTS_PER_GROUP = 8
N_EXPERTS = N_GROUPS * EXPERTS_PER_GROUP
D_EXPERT = D_MODEL // 4

N_PROMPT = BATCH * SEQ
N_SAMPLE = DEC_BATCH * DEC_SEQ
N_TOK = N_PROMPT + N_SAMPLE
MOD_ROWS = 8
LANES = 128
SUBLANES = 8
HEAD_W = 2 * HEAD_DIM
VMEM_LIMIT = 56 * 1024 * 1024

MOE_BLOCK = 256
N_BLOCKS = N_TOK // MOE_BLOCK
CHUNK = 2 * SUBLANES
BLOCK_ROWS = 2 * MOE_BLOCK + N_EXPERTS * CHUNK
BLOCK_CHUNKS = BLOCK_ROWS // CHUNK
EXPERT_TILE = 256
TILE_CHUNKS = EXPERT_TILE // CHUNK
N_TILES = (2 * N_TOK + N_BLOCKS * N_EXPERTS * (CHUNK - 1)) // EXPERT_TILE + N_EXPERTS


def _cparams(sem):
    return pltpu.CompilerParams(dimension_semantics=sem, vmem_limit_bytes=VMEM_LIMIT)


def _mod_row(blk, tm):
    start = blk * tm
    return jnp.where(start < N_PROMPT, 0, 1 + (start - N_PROMPT) // DEC_SEQ)


def _mod_index(layer, which, tm):
    def index_map(i, *_):
        return ((layer * MOD_ROWS + _mod_row(i, tm)) * 6 + which, 0, 0)
    return index_map


@functools.lru_cache(maxsize=None)
def _dft_mats(L):
    k = np.arange(L, dtype=np.float64)[:, None]
    j = np.arange(L, dtype=np.float64)[None, :]
    ang = np.pi * ((k * j) % (2 * L)) / L
    c = np.cos(ang)
    s = np.sin(ang)
    s[0, :] = np.where(np.arange(L) % 2 == 0, 1.0, -1.0)
    fwd = np.concatenate([c, s], axis=0)
    scale = np.full((2 * L,), 1.0 / L)
    scale[0] = scale[L] = 0.5 / L
    inv = fwd.T * scale[None, :]
    return fwd.astype(np.float32), inv.astype(np.float32)


@functools.lru_cache(maxsize=None)
def _filter_feats(L):
    pos = np.arange(L, dtype=np.float64)
    t = pos / (L - 1)
    bands = np.linspace(1e-4, HY_BANDS - 1, HY_BANDS)
    ang = (2.0 * math.pi / L) * pos[:, None] * bands[None, :]
    feats = np.concatenate([t[:, None], np.cos(ang), -np.sin(ang)], axis=-1)
    feats = np.pad(feats, ((0, 0), (0, LANES - HY_EMB)))
    deltas = np.abs(np.linspace(math.log(HY_TARGET) / HY_SLOW_DECAY,
                                math.log(HY_TARGET) / HY_FAST_DECAY, D_MODEL))
    window = np.exp(-t[:, None] * deltas[None, :])
    alt = np.where(np.arange(L) % 2 == 0, 1.0, -1.0)[:, None] * np.ones((1, LANES))
    return feats.astype(np.float32), window.astype(np.float32), alt.astype(np.float32)


@functools.lru_cache(maxsize=None)
def _rope_tables():
    pos = np.arange(DEC_SEQ)
    row = (pos // GRID_W).astype(np.float64)
    col = (pos % GRID_W).astype(np.float64)
    lane = np.arange(HEAD_W)
    d = lane % HEAD_DIM
    axis = d // ROT_AXIS
    n = d % ROT_AXIS
    half = n // (ROT_AXIS // 2)
    f = n % (ROT_AXIS // 2)
    inv = ROPE_BASE ** (-f.astype(np.float64) / (ROT_AXIS // 2))
    p = np.where(axis[None, :] == 0, row[:, None], col[:, None])
    ang = p * inv[None, :]
    sign = np.where(half == 0, -1.0, 1.0)[None, :]
    return np.cos(ang).astype(np.float32), (np.sin(ang) * sign).astype(np.float32)


def _split_bf16(a):
    hi = a.astype(BF16)
    lo = (a - hi.astype(F32)).astype(BF16)
    return hi, lo


def _dot(a, b):
    return jnp.dot(a, b, preferred_element_type=F32)


def _dot3(a_hi, a_lo, b_hi, b_lo):
    return _dot(a_hi, b_hi) + (_dot(a_hi, b_lo) + _dot(a_lo, b_hi))


def _ada_kernel(c_ref, w_ref, b_ref, o_ref):
    c = c_ref[...]
    s = c * jax.nn.sigmoid(c)
    o_ref[0] = _dot(s.astype(BF16), w_ref[0].astype(BF16)) + b_ref[0]


def _ada_table(cond, ada_w, ada_b):
    tn = 1536
    n = 6 * D_MODEL
    return pl.pallas_call(
        _ada_kernel,
        grid=(DEPTH, n // tn),
        in_specs=[
            pl.BlockSpec((MOD_ROWS, D_MODEL), lambda l, j: (0, 0)),
            pl.BlockSpec((1, D_MODEL, tn), lambda l, j: (l, 0, j)),
            pl.BlockSpec((1, 1, tn), lambda l, j: (l, 0, j)),
        ],
        out_specs=pl.BlockSpec((1, MOD_ROWS, tn), lambda l, j: (l, 0, j)),
        out_shape=jax.ShapeDtypeStruct((DEPTH, MOD_ROWS, n), F32),
        compiler_params=_cparams(("arbitrary", "arbitrary")),
        name="ada_table",
    )(cond, ada_w, ada_b.reshape(DEPTH, 1, n))


def _modulate(x, g, scale, shift):
    ms = jnp.mean(x * x, axis=-1, keepdims=True)
    y = (x * lax.rsqrt(ms + EPS)) * g
    return y * (1.0 + scale) + shift


MM_COLS = 512


def _row_group_specs(x, tm, width):
    np_blocks = N_PROMPT // tm
    if isinstance(x, tuple):
        specs = [pl.BlockSpec((tm, width), lambda i, *_: (jnp.minimum(i, np_blocks - 1), 0)),
                 pl.BlockSpec((tm, width), lambda i, *_: (jnp.maximum(i - np_blocks, 0), 0))]
        return specs, list(x)
    specs = [pl.BlockSpec((tm, width), lambda i, *_: (jnp.minimum(i, np_blocks - 1), 0)),
             pl.BlockSpec((tm, width), lambda i, *_: (jnp.maximum(i, np_blocks), 0))]
    return specs, [x, x]


def _pick_rows(i, tm, p_ref, s_ref):
    return jnp.where(i < N_PROMPT // tm, p_ref[...], s_ref[...])


ROPE_COLS = 2 * D_MODEL


def _normmm_kernel(xp_ref, xs_ref, g_ref, sh_ref, sc_ref, w_ref, *rest, rope, tm):
    if rope:
        cos_ref, sin_ref, o_ref, w_scr, wrot_scr = rest
    else:
        o_ref, w_scr = rest
    i = pl.program_id(0)

    @pl.when(i == 0)
    def _():
        w_scr[...] = w_ref[0].astype(BF16)
        if rope:
            lane = lax.broadcasted_iota(jnp.int32, (D_MODEL, MM_COLS), 1)
            first_half = (lane % ROT_AXIS) < (ROT_AXIS // 2)
            for j in range(ROPE_COLS // MM_COLS):
                cols = slice(j * MM_COLS, (j + 1) * MM_COLS)
                wc = w_ref[0, :, cols]
                wrot_scr[:, cols] = jnp.where(first_half,
                                              pltpu.roll(wc, MM_COLS - ROT_AXIS // 2, 1),
                                              pltpu.roll(wc, ROT_AXIS // 2, 1)).astype(BF16)

    x = _pick_rows(i, tm, xp_ref, xs_ref)
    h = _modulate(x, g_ref[...], sc_ref[0], sh_ref[0]).astype(BF16)
    n = w_scr.shape[1]
    for j in range(n // MM_COLS):
        cols = slice(j * MM_COLS, (j + 1) * MM_COLS)
        acc = _dot(h, w_scr[:, cols])
        if not (rope and j < ROPE_COLS // MM_COLS):
            o_ref[:, cols] = acc.astype(o_ref.dtype)
            continue

        @pl.when(i >= N_PROMPT // tm)
        def _():
            reps = MM_COLS // HEAD_W
            cos = jnp.tile(cos_ref[...], (1, reps))
            sin = jnp.tile(sin_ref[...], (1, reps))
            partner = _dot(h, wrot_scr[:, cols])
            o_ref[:, cols] = (acc * cos + partner * sin).astype(o_ref.dtype)

        @pl.when(i < N_PROMPT // tm)
        def _():
            o_ref[:, cols] = acc.astype(o_ref.dtype)


def _norm_matmul(x, g, mods, layer, which_shift, which_scale, w, widx, *, rope=False, tm=512):
    n = w.shape[2]
    rope_blocks = DEC_SEQ // tm

    def rope_idx(i):
        return (jnp.maximum(i - N_PROMPT // tm, 0) % rope_blocks, 0)

    x_specs, x_args = _row_group_specs(x, tm, D_MODEL)
    in_specs = x_specs + [
        pl.BlockSpec((1, D_MODEL), lambda i: (0, 0)),
        pl.BlockSpec((1, 1, D_MODEL), _mod_index(layer, which_shift, tm)),
        pl.BlockSpec((1, 1, D_MODEL), _mod_index(layer, which_scale, tm)),
        pl.BlockSpec((1, D_MODEL, n), lambda i: (widx, 0, 0), pipeline_mode=pl.Buffered(1)),
    ]
    args = x_args + [g.reshape(1, D_MODEL), mods, mods, w]
    scratch = [pltpu.VMEM((D_MODEL, n), BF16)]
    if rope:
        cos, sin = _rope_tables()
        in_specs += [pl.BlockSpec((tm, HEAD_W), rope_idx), pl.BlockSpec((tm, HEAD_W), rope_idx)]
        args += [jnp.asarray(cos), jnp.asarray(sin)]
        scratch += [pltpu.VMEM((D_MODEL, ROPE_COLS), BF16)]
    return pl.pallas_call(
        functools.partial(_normmm_kernel, rope=rope, tm=tm),
        grid=(N_TOK // tm,),
        in_specs=in_specs,
        out_specs=pl.BlockSpec((tm, n), lambda i: (i, 0)),
        out_shape=jax.ShapeDtypeStruct((N_TOK, n), BF16),
        scratch_shapes=scratch,
        compiler_params=_cparams(("arbitrary",)),
        name="norm_matmul_rope" if rope else "norm_matmul",
    )(*args)


def _mmres_kernel(ap_ref, as_ref, xp_ref, xs_ref, w_ref, gate_ref, o_ref, w_scr, *, tm):
    i = pl.program_id(0)

    @pl.when(i == 0)
    def _():
        w_scr[...] = w_ref[0].astype(BF16)

    a = _pick_rows(i, tm, ap_ref, as_ref)
    x = _pick_rows(i, tm, xp_ref, xs_ref)
    gate = gate_ref[0]
    for j in range(w_scr.shape[1] // MM_COLS):
        cols = slice(j * MM_COLS, (j + 1) * MM_COLS)
        o_ref[:, cols] = x[:, cols] + gate[:, cols] * _dot(a, w_scr[:, cols])


def _matmul_residual(a, w, widx, x, mods, layer, which_gate, *, tm=512):
    n = w.shape[2]
    a_specs, a_args = _row_group_specs(a, tm, D_MODEL)
    x_specs, x_args = _row_group_specs(x, tm, n)
    return pl.pallas_call(
        functools.partial(_mmres_kernel, tm=tm),
        grid=(N_TOK // tm,),
        in_specs=a_specs + x_specs + [
            pl.BlockSpec((1, D_MODEL, n), lambda i: (widx, 0, 0), pipeline_mode=pl.Buffered(1)),
            pl.BlockSpec((1, 1, n), _mod_index(layer, which_gate, tm)),
        ],
        out_specs=pl.BlockSpec((tm, n), lambda i: (i, 0)),
        out_shape=jax.ShapeDtypeStruct((N_TOK, n), F32),
        scratch_shapes=[pltpu.VMEM((D_MODEL, n), BF16)],
        compiler_params=_cparams(("arbitrary",)),
        name="matmul_residual",
    )(*a_args, *x_args, w, mods)


def _filter_kernel(feats_ref, win_ref, alt_ref, w1_ref, b1_ref, w2_ref, b2_ref, fq_ref,
                   w3f_ref, b3f_ref, w3b_ref, b3b_ref, fhi_ref, flo_ref, o_ref, *, L):
    def dense(a, w_ref, b_ref):
        a_hi, a_lo = _split_bf16(a)
        w_hi, w_lo = _split_bf16(w_ref[...])
        return _dot3(a_hi, a_lo, w_hi, w_lo) + b_ref[...]

    fq = fq_ref[...]
    h = jnp.sin(fq[0:1, :] * dense(feats_ref[...], w1_ref, b1_ref))
    h = jnp.sin(fq[1:2, :] * dense(h, w2_ref, b2_ref))
    win = win_ref[...]
    hf = dense(h, w3f_ref, b3f_ref) * win
    hb = dense(h, w3b_ref, b3b_ref) * win
    row = lax.broadcasted_iota(jnp.int32, hf.shape, 0)
    hb = jnp.where(row == 0, 0.0, hb)
    s_hi, s_lo = _split_bf16(hf + hb)
    d_hi, d_lo = _split_bf16(hf - hb)
    kr = _dot3(fhi_ref[0:L, :], flo_ref[0:L, :], s_hi, s_lo)
    ks = _dot3(fhi_ref[L:2 * L, :], flo_ref[L:2 * L, :], d_hi, d_lo)
    k_nyq = jnp.sum((hf + hb) * alt_ref[:, 0:1], axis=0, keepdims=True)
    o_ref[0, 0] = kr
    o_ref[0, 1] = jnp.where(row == 0, k_nyq, kr)
    o_ref[0, 2] = jnp.where(row == 0, 0.0, ks)


def _filter_spectra(L, w1, b1, w2, b2, w3, b3, freq, f_hi, f_lo, *, dt=256):
    feats, window, alt = _filter_feats(L)
    fh = HY_FILTER_HIDDEN
    nd = D_MODEL // dt
    w1p = jnp.pad(w1, ((0, LANES - HY_EMB), (0, 0)))
    const = lambda shape: pl.BlockSpec(shape, lambda o, c: tuple(0 for _ in shape))
    return pl.pallas_call(
        functools.partial(_filter_kernel, L=L),
        grid=(2, nd),
        in_specs=[
            const((L, LANES)),
            pl.BlockSpec((L, dt), lambda o, c: (0, c)),
            const((L, LANES)),
            const((LANES, fh)), const((1, fh)), const((fh, fh)), const((1, fh)), const((2, fh)),
            pl.BlockSpec((fh, dt), lambda o, c: (0, (2 * o) * nd + c)),
            pl.BlockSpec((1, dt), lambda o, c: (0, (2 * o) * nd + c)),
            pl.BlockSpec((fh, dt), lambda o, c: (0, (2 * o + 1) * nd + c)),
            pl.BlockSpec((1, dt), lambda o, c: (0, (2 * o + 1) * nd + c)),
            const((2 * L, L)), const((2 * L, L)),
        ],
        out_specs=pl.BlockSpec((1, 3, L, dt), lambda o, c: (o, 0, 0, c)),
        out_shape=jax.ShapeDtypeStruct((2, 3, L, D_MODEL), F32),
        compiler_params=_cparams(("arbitrary", "arbitrary")),
        name=f"hyena_filter_spectra_{L}",
    )(jnp.asarray(feats), jnp.asarray(window), jnp.asarray(alt), w1p, b1.reshape(1, fh), w2,
      b2.reshape(1, fh), freq, w3, b3.reshape(1, -1), w3, b3.reshape(1, -1), f_hi, f_lo)


def _conv_kernel(v_ref, x1_ref, x2_ref, cwv_ref, cw1_ref, cw2_ref, cbv_ref, cb1_ref, cb2_ref,
                 f_ref, finv_ref, kc_ref, bias_ref, o_ref, *, L):
    row = lax.broadcasted_iota(jnp.int32, v_ref.shape, 0)

    def short_conv(u_ref, w_ref, b_ref):
        u = u_ref[...].astype(F32)
        prev = jnp.where(row == 0, 0.0, pltpu.roll(u, 1, 0))
        nxt = jnp.where(row == L - 1, 0.0, pltpu.roll(u, L - 1, 0))
        w = w_ref[...]
        return prev * w[0:1, :] + u * w[1:2, :] + nxt * w[2:3, :] + b_ref[...]

    def long_conv(u, order):
        spec = _dot(f_ref[...], u.astype(BF16))
        a = spec[0:L, :]
        b = spec[L:2 * L, :]
        kra = kc_ref[order, 0]
        krb = kc_ref[order, 1]
        ks = kc_ref[order, 2]
        prod = jnp.concatenate([a * kra - b * ks, a * ks + b * krb], axis=0)
        return _dot(finv_ref[...], prod.astype(BF16)) + u * bias_ref[order:order + 1, :]

    v = short_conv(v_ref, cwv_ref, cbv_ref)
    y = short_conv(x1_ref, cw1_ref, cb1_ref) * long_conv(v, 0)
    y = short_conv(x2_ref, cw2_ref, cb2_ref) * long_conv(y, 1)
    o_ref[...] = y.astype(o_ref.dtype)


def _hyena_conv(z, conv_w, conv_b, f_bf, finv_bf, kc, bias, *, L, row_block0, n_seq, dt):
    nd = D_MODEL // dt
    cb = conv_b.reshape(1, 3 * D_MODEL)
    seg = lambda s: pl.BlockSpec((L, dt), lambda c, b: (row_block0 + b, s * nd + c))
    cw = lambda s: pl.BlockSpec((3, dt), lambda c, b: (0, s * nd + c))
    cbs = lambda s: pl.BlockSpec((1, dt), lambda c, b: (0, s * nd + c))
    return pl.pallas_call(
        functools.partial(_conv_kernel, L=L),
        grid=(nd, n_seq),
        in_specs=[
            seg(0), seg(1), seg(2), cw(0), cw(1), cw(2), cbs(0), cbs(1), cbs(2),
            pl.BlockSpec((2 * L, L), lambda c, b: (0, 0)),
            pl.BlockSpec((L, 2 * L), lambda c, b: (0, 0)),
            pl.BlockSpec((2, 3, L, dt), lambda c, b: (0, 0, 0, c)),
            pl.BlockSpec((2, dt), lambda c, b: (0, c)),
        ],
        out_specs=pl.BlockSpec((L, dt), lambda c, b: (b, c)),
        out_shape=jax.ShapeDtypeStruct((n_seq * L, D_MODEL), BF16),
        compiler_params=_cparams(("arbitrary", "arbitrary")),
        name=f"hyena_conv_{L}",
    )(z, z, z, conv_w, conv_w, conv_w, cb, cb, cb, f_bf, finv_bf, kc, bias)


def _attn_kernel(lq1_ref, lk1_ref, lq2_ref, lk2_ref, g_ref, q_ref, k_ref, v_ref, *rest,
                 lam_init, has_ctx, tq, nq):
    if has_ctx:
        ck_ref, cv_ref, o_ref = rest
    else:
        (o_ref,) = rest
    lam = (jnp.exp(jnp.sum(lq1_ref[...] * lk1_ref[...], axis=-1, keepdims=True))
           - jnp.exp(jnp.sum(lq2_ref[...] * lk2_ref[...], axis=-1, keepdims=True)) + lam_init)
    first_map = lax.broadcasted_iota(jnp.int32, (tq, HEAD_W), 1) < HEAD_DIM
    nt = (((1,), (1,)), ((), ()))

    def q_block(rows):
        for h in range(N_HEADS):
            cols = slice(h * HEAD_W, (h + 1) * HEAD_W)
            q = q_ref[rows, cols] * (HEAD_DIM ** -0.5)
            zero = jnp.zeros_like(q)
            q2 = jnp.concatenate([jnp.where(first_map, q, zero), jnp.where(first_map, zero, q)], axis=0)
            s = lax.dot_general(q2, k_ref[:, cols], nt, preferred_element_type=F32)
            m = jnp.max(s, axis=-1, keepdims=True)
            if has_ctx:
                sc = lax.dot_general(q2, ck_ref[0, 0, :, cols].astype(BF16), nt,
                                     preferred_element_type=F32)
                m = jnp.maximum(m, jnp.max(sc, axis=-1, keepdims=True))
                ec = jnp.exp(sc - m)
            e = jnp.exp(s - m)
            den = jnp.sum(e, axis=-1, keepdims=True)
            if has_ctx:
                den = den + jnp.sum(ec, axis=-1, keepdims=True)
            ov = _dot(e.astype(BF16), v_ref[:, cols])
            if has_ctx:
                ov = ov + _dot(ec.astype(BF16), cv_ref[0, 0, :, cols].astype(BF16))
            inv = 1.0 / den
            o = ov[0:tq] * inv[0:tq] - ov[tq:2 * tq] * (lam * inv[tq:2 * tq])
            ms = jnp.mean(o * o, axis=-1, keepdims=True)
            o = (o * lax.rsqrt(ms + EPS)) * g_ref[...] * (1.0 - lam_init)
            o_ref[rows, cols] = o.astype(o_ref.dtype)

    if nq == 1:
        q_block(slice(0, tq))
    else:
        def body(qi, carry):
            q_block(pl.ds(pl.multiple_of(qi * tq, tq), tq))
            return carry
        lax.fori_loop(0, nq, body, 0)


def _attention(qkv, lam_params, subln_g, lam_init, *, L, row_block0, n_seq, tq, ctx=None, j=0):
    small = pl.BlockSpec((1, HEAD_DIM), lambda b: (0, 0))
    in_specs = [small, small, small, small,
                pl.BlockSpec((1, HEAD_W), lambda b: (0, 0)),
                pl.BlockSpec((L, D_MODEL), lambda b: (row_block0 + b, 0)),
                pl.BlockSpec((L, D_MODEL), lambda b: (row_block0 + b, 1)),
                pl.BlockSpec((L, D_MODEL), lambda b: (row_block0 + b, 2))]
    args = [p.reshape(1, HEAD_DIM) for p in lam_params] + [subln_g.reshape(1, HEAD_W), qkv, qkv, qkv]
    if ctx is not None:
        ctx_spec = pl.BlockSpec((1, 1, PAST_LEN, D_MODEL), lambda b: (b, j, 0, 0))
        in_specs += [ctx_spec, ctx_spec]
        args += list(ctx)
    return pl.pallas_call(
        functools.partial(_attn_kernel, lam_init=lam_init, has_ctx=ctx is not None, tq=tq, nq=L // tq),
        grid=(n_seq,),
        in_specs=in_specs,
        out_specs=pl.BlockSpec((L, D_MODEL), lambda b: (b, 0)),
        out_shape=jax.ShapeDtypeStruct((n_seq * L, D_MODEL), BF16),
        compiler_params=_cparams(("arbitrary",)),
        name="diff_attention_ctx" if ctx is not None else "diff_attention",
    )(*args)


def _row_slots(pos, sel):
    big = float(BLOCK_ROWS)
    pos_a = jnp.max(pos, axis=-1, keepdims=True)
    pos_b = jnp.min(jnp.where(sel, pos, big), axis=-1, keepdims=True)
    return pos_a, pos_b


def _route_block(x, g, scale, shift, wr, br):
    tm = MOE_BLOCK
    h = _modulate(x, g, scale, shift).astype(BF16)
    logits = _dot(h, wr) + br
    lane = lax.broadcasted_iota(jnp.int32, (tm, LANES), 1)
    neg = -jnp.inf
    is_grp = jnp.logical_and(lane >= N_EXPERTS, lane < N_EXPERTS + N_GROUPS)
    lg = jnp.where(is_grp, logits, neg)
    mg = jnp.max(lg, axis=-1, keepdims=True)
    g_gate = 1.0 / jnp.sum(jnp.exp(lg - mg), axis=-1, keepdims=True)
    g_idx = jnp.min(jnp.where(lg == mg, lane - N_EXPERTS, N_GROUPS), axis=-1, keepdims=True)
    in_grp = jnp.logical_and(lane < N_EXPERTS, lane // EXPERTS_PER_GROUP == g_idx)
    le = jnp.where(in_grp, logits, neg)
    m1 = jnp.max(le, axis=-1, keepdims=True)
    i1 = jnp.min(jnp.where(le == m1, lane, LANES), axis=-1, keepdims=True)
    le2 = jnp.where(lane == i1, neg, le)
    m2 = jnp.max(le2, axis=-1, keepdims=True)
    i2 = jnp.min(jnp.where(le2 == m2, lane, LANES), axis=-1, keepdims=True)
    e2 = jnp.exp(m2 - m1)
    inv = 1.0 / (1.0 + e2)
    gate = jnp.where(lane == i1, g_gate * inv, jnp.where(lane == i2, g_gate * (e2 * inv), 0.0))
    sel = jnp.logical_or(lane == i1, lane == i2)
    onehot = jnp.where(sel, 1.0, 0.0)
    r = lax.broadcasted_iota(jnp.int32, (tm, tm), 0)
    c = lax.broadcasted_iota(jnp.int32, (tm, tm), 1)
    rank = _dot(jnp.where(c < r, 1.0, 0.0).astype(BF16), onehot.astype(BF16))
    count = jnp.sum(onehot, axis=0, keepdims=True)
    padded = jnp.ceil(count * (1.0 / CHUNK)) * CHUNK
    er = lax.broadcasted_iota(jnp.int32, (LANES, LANES), 0)
    ec = lax.broadcasted_iota(jnp.int32, (LANES, LANES), 1)
    before = jnp.where(er < ec, 1.0, 0.0).astype(BF16)
    start = _dot(jnp.broadcast_to(padded, (SUBLANES, LANES)).astype(BF16), before)[0:1, :]
    pos = jnp.where(sel, start + rank, -1.0)
    pos_a, pos_b = _row_slots(pos, sel)
    slot = lax.broadcasted_iota(jnp.int32, (tm, BLOCK_ROWS), 1).astype(F32)
    pick = jnp.where(jnp.logical_or(slot == pos_a, slot == pos_b), 1.0, 0.0).astype(BF16)
    xb = lax.dot_general(pick, h, (((0,), (0,)), ((), ())), preferred_element_type=F32).astype(BF16)
    return xb, pos, gate, count.astype(jnp.int32)


ROUTER_BLOCKS = 2


def _router_kernel(x_ref, g_ref, sh_ref, sc_ref, wr_ref, br_ref, xb_ref, pos_ref, gate_ref, cnt_ref):
    wr = wr_ref[...].astype(BF16)
    for k in range(ROUTER_BLOCKS):
        rows = slice(k * MOE_BLOCK, (k + 1) * MOE_BLOCK)
        xb, pos, gate, count = _route_block(x_ref[rows, :], g_ref[...], sc_ref[0], sh_ref[0], wr, br_ref[...])
        xb_ref[k * BLOCK_ROWS:(k + 1) * BLOCK_ROWS, :] = xb
        pos_ref[rows, :] = pos
        gate_ref[rows, :] = gate
        cnt_ref[k] = count


def _router(x, g, mods, layer, wr, br):
    tm = ROUTER_BLOCKS * MOE_BLOCK
    return pl.pallas_call(
        _router_kernel,
        grid=(N_TOK // tm,),
        in_specs=[
            pl.BlockSpec((tm, D_MODEL), lambda i: (i, 0)),
            pl.BlockSpec((1, D_MODEL), lambda i: (0, 0)),
            pl.BlockSpec((1, 1, D_MODEL), _mod_index(layer, 3, tm)),
            pl.BlockSpec((1, 1, D_MODEL), _mod_index(layer, 4, tm)),
            pl.BlockSpec((D_MODEL, LANES), lambda i: (0, 0)),
            pl.BlockSpec((1, LANES), lambda i: (0, 0)),
        ],
        out_specs=[
            pl.BlockSpec((ROUTER_BLOCKS * BLOCK_ROWS, D_MODEL), lambda i: (i, 0)),
            pl.BlockSpec((tm, LANES), lambda i: (i, 0)),
            pl.BlockSpec((tm, LANES), lambda i: (i, 0)),
            pl.BlockSpec((ROUTER_BLOCKS, 1, LANES), lambda i: (i, 0, 0)),
        ],
        out_shape=[
            jax.ShapeDtypeStruct((N_BLOCKS * BLOCK_ROWS, D_MODEL), BF16),
            jax.ShapeDtypeStruct((N_TOK, LANES), F32),
            jax.ShapeDtypeStruct((N_TOK, LANES), F32),
            jax.ShapeDtypeStruct((N_BLOCKS, 1, LANES), jnp.int32),
        ],
        compiler_params=_cparams(("arbitrary",)),
        name="moe_router",
    )(x, g.reshape(1, D_MODEL), mods, mods, wr, br)


TABLE_ROWS = -(-(N_TILES * TILE_CHUNKS) // LANES)


def _tables_kernel(cnt_ref, tend_ref, disp_ref, comb_ref):
    ne = N_EXPERTS
    cnt = jnp.concatenate([cnt_ref[...].astype(F32), jnp.zeros((LANES - N_BLOCKS, LANES), F32)], axis=0)
    m = jnp.ceil(cnt * (1.0 / CHUNK))
    r = lax.broadcasted_iota(jnp.int32, (LANES, LANES), 0)
    c = lax.broadcasted_iota(jnp.int32, (LANES, LANES), 1)
    mb = m.astype(BF16)
    bstart = _dot(mb, jnp.where(r < c, 1.0, 0.0).astype(BF16))
    cm = _dot(jnp.where(c < r, 1.0, 0.0).astype(BF16), mb)
    total = jnp.sum(m, axis=0, keepdims=True)
    tiles = jnp.ceil(total * (1.0 / TILE_CHUNKS))
    tile_end = _dot(jnp.broadcast_to(tiles, (SUBLANES, LANES)).astype(BF16),
                    jnp.where(r <= c, 1.0, 0.0).astype(BF16))[0:1, :]
    choff = (tile_end - tiles) * TILE_CHUNKS
    tend_ref[...] = tile_end.astype(jnp.int32)

    start_t = (choff + cm).T
    m_t = m.T
    bstart_t = bstart.T
    lane = lax.broadcasted_iota(jnp.int32, (ne, LANES), 1).astype(F32)
    acc = [jnp.zeros((ne, LANES), F32) for _ in range(TABLE_ROWS)]
    for b in range(N_BLOCKS):
        lo = start_t[0:ne, b:b + 1]
        hi = lo + m_t[0:ne, b:b + 1]
        bs = bstart_t[0:ne, b:b + 1]
        fwd = (b * BLOCK_CHUNKS) + bs - lo
        for j in range(TABLE_ROWS):
            g = lane + float(j * LANES)
            acc[j] = acc[j] + jnp.where(g >= lo, jnp.where(g < hi, fwd + g, 0.0), 0.0)
        back = jnp.where(lane >= bs, jnp.where(lane < hi - lo + bs, (lo - bs + lane) * CHUNK + 1.0, 0.0), 0.0)
        comb_ref[b:b + 1, :] = (jnp.sum(back, axis=0, keepdims=True) - 1.0).astype(jnp.int32)
    for j in range(TABLE_ROWS):
        disp_ref[j:j + 1, :] = (jnp.sum(acc[j], axis=0, keepdims=True) * CHUNK).astype(jnp.int32)


def _chunk_tables(counts):
    return pl.pallas_call(
        _tables_kernel,
        out_shape=[
            jax.ShapeDtypeStruct((1, LANES), jnp.int32),
            jax.ShapeDtypeStruct((TABLE_ROWS, LANES), jnp.int32),
            jax.ShapeDtypeStruct((N_BLOCKS, LANES), jnp.int32),
        ],
        name="moe_chunk_tables",
    )(counts.reshape(N_BLOCKS, LANES))


def _chunk_copies(read_row, n_chunks, src_hbm, dst, sem, wait, skip_negative=False):
    for c in range(n_chunks):
        row = read_row(c)

        def copy(row=row, c=c):
            cp = pltpu.make_async_copy(src_hbm.at[pl.ds(pl.multiple_of(row, CHUNK), CHUNK)],
                                       dst.at[pl.ds(c * CHUNK, CHUNK)], sem)
            if wait:
                cp.wait()
            else:
                cp.start()

        if skip_negative:
            pl.when(row >= 0)(copy)
        else:
            copy()


IN_DEPTH = 3


def _expert_kernel(tend_ref, src_ref, xb_hbm, w1_ref, w3_ref, w2_ref, y_hbm,
                   xbuf, ybuf, w13_scr, w2_scr, in_sem, out_sem):
    e = pl.program_id(0)
    first = jnp.where(e == 0, 0, tend_ref[0, jnp.maximum(e - 1, 0)])
    last = tend_ref[0, e]
    n_used = tend_ref[0, N_EXPERTS - 1]
    tiles_per_row = LANES // TILE_CHUNKS

    def fetch(tile, s, wait):
        row = tile // tiles_per_row
        col = (tile % tiles_per_row) * TILE_CHUNKS
        _chunk_copies(lambda c: src_ref[row, col + c], TILE_CHUNKS, xb_hbm, xbuf.at[s], in_sem.at[s], wait)

    def store(tile, s):
        rows = pl.ds(pl.multiple_of(tile * EXPERT_TILE, EXPERT_TILE), EXPERT_TILE)
        return pltpu.make_async_copy(ybuf.at[s], y_hbm.at[rows], out_sem.at[s])

    @pl.when(last > first)
    def _():
        @pl.when(first == 0)
        def _():
            fetch(0, 0, False)

            @pl.when(n_used > 1)
            def _():
                fetch(1, 1, False)

        w13_scr[:, 0:D_EXPERT] = w1_ref[0, 0].astype(BF16)
        w13_scr[:, D_EXPERT:2 * D_EXPERT] = w3_ref[0, 0].astype(BF16)
        w2_scr[...] = w2_ref[0, 0].astype(BF16)

    def tile_body(t, carry):
        s = t % 2

        @pl.when(t + (IN_DEPTH - 1) < n_used)
        def _():
            fetch(t + (IN_DEPTH - 1), (t + (IN_DEPTH - 1)) % IN_DEPTH, False)

        fetch(t, t % IN_DEPTH, True)
        ab = _dot(xbuf[t % IN_DEPTH], w13_scr[...])
        a = ab[:, 0:D_EXPERT]
        hid = (a * jax.nn.sigmoid(a)) * ab[:, D_EXPERT:2 * D_EXPERT]
        y = _dot(hid.astype(BF16), w2_scr[...]).astype(BF16)

        @pl.when(t >= 2)
        def _():
            store(t, s).wait()

        ybuf[s] = y
        store(t, s).start()
        return carry

    lax.fori_loop(first, last, tile_body, 0)

    @pl.when(e == N_EXPERTS - 1)
    def _():
        @pl.when(n_used >= 1)
        def _():
            store(0, (n_used - 1) % 2).wait()

        @pl.when(n_used >= 2)
        def _():
            store(0, n_used % 2).wait()

        ybuf[0] = jnp.zeros((EXPERT_TILE, D_MODEL), BF16)

        def start(t, carry):
            store(t, 0).start()
            return carry

        def drain(t, carry):
            store(t, 0).wait()
            return carry

        lax.fori_loop(n_used, N_TILES, start, 0)
        lax.fori_loop(n_used, N_TILES, drain, 0)


def _experts(tile_end, disp_src, xb, w1, w3, w2, layer):
    wsel = lambda e, tend, src: (layer, e, 0, 0)
    return pl.pallas_call(
        _expert_kernel,
        grid_spec=pltpu.PrefetchScalarGridSpec(
            num_scalar_prefetch=2,
            grid=(N_EXPERTS,),
            in_specs=[
                pl.BlockSpec(memory_space=pl.ANY),
                pl.BlockSpec((1, 1, D_MODEL, D_EXPERT), wsel),
                pl.BlockSpec((1, 1, D_MODEL, D_EXPERT), wsel),
                pl.BlockSpec((1, 1, D_EXPERT, D_MODEL), wsel),
            ],
            out_specs=pl.BlockSpec(memory_space=pl.ANY),
            scratch_shapes=[pltpu.VMEM((IN_DEPTH, EXPERT_TILE, D_MODEL), BF16),
                            pltpu.VMEM((2, EXPERT_TILE, D_MODEL), BF16),
                            pltpu.VMEM((D_MODEL, 2 * D_EXPERT), BF16),
                            pltpu.VMEM((D_EXPERT, D_MODEL), BF16),
                            pltpu.SemaphoreType.DMA((IN_DEPTH,)),
                            pltpu.SemaphoreType.DMA((2,))],
        ),
        out_shape=jax.ShapeDtypeStruct((N_TILES * EXPERT_TILE, D_MODEL), BF16),
        compiler_params=_cparams(("arbitrary",)),
        name="moe_experts",
    )(tile_end, disp_src, xb, w1, w3, w2)


def _combine_kernel(src_ref, y_hbm, x_ref, pos_ref, gate_ref, mg_ref, *rest, tm, final):
    if final:
        fg_ref, op_ref, os_ref, ybuf, sem = rest
    else:
        o_ref, ybuf, sem = rest
    i = pl.program_id(0)
    slot = i % 2

    def fetch(blk, s, wait):
        _chunk_copies(lambda c: src_ref[blk, c], BLOCK_CHUNKS, y_hbm, ybuf.at[s], sem.at[s], wait,
                      skip_negative=True)

    @pl.when(i == 0)
    def _():
        ybuf[...] = jnp.zeros_like(ybuf)
        fetch(0, 0, False)

    @pl.when(i + 1 < N_BLOCKS)
    def _():
        fetch(i + 1, 1 - slot, False)

    fetch(i, slot, True)
    pos = pos_ref[...]
    sel = pos >= 0.0
    gate = gate_ref[...]
    pos_a, pos_b = _row_slots(pos, sel)
    gate_a = jnp.sum(jnp.where(pos == pos_a, gate, 0.0), axis=-1, keepdims=True)
    gate_b = jnp.sum(jnp.where(jnp.logical_and(sel, pos == pos_b), gate, 0.0), axis=-1, keepdims=True)
    row = lax.broadcasted_iota(jnp.int32, (tm, BLOCK_ROWS), 1).astype(F32)
    weights = jnp.where(row == pos_a, gate_a, jnp.where(row == pos_b, gate_b, 0.0))
    moe = _dot(weights.astype(BF16), ybuf[slot])
    out = x_ref[...] + mg_ref[0] * moe
    if not final:
        o_ref[...] = out
        return

    ms = jnp.mean(out * out, axis=-1, keepdims=True)
    out = (out * lax.rsqrt(ms + EPS)) * fg_ref[...]

    @pl.when(i < N_PROMPT // tm)
    def _():
        op_ref[...] = out

    @pl.when(i >= N_PROMPT // tm)
    def _():
        os_ref[...] = out


def _combine(comb_src, y, x, pos, gate, mods, layer, final_g=None):
    tm = MOE_BLOCK
    final = final_g is not None
    np_blocks = N_PROMPT // tm

    def gate_idx(i, s):
        return ((layer * MOD_ROWS + _mod_row(i, tm)) * 6 + 5, 0, 0)

    in_specs = [
        pl.BlockSpec(memory_space=pl.ANY),
        pl.BlockSpec((tm, D_MODEL), lambda i, s: (i, 0)),
        pl.BlockSpec((tm, LANES), lambda i, s: (i, 0)),
        pl.BlockSpec((tm, LANES), lambda i, s: (i, 0)),
        pl.BlockSpec((1, 1, D_MODEL), gate_idx),
    ]
    args = [comb_src, y, x, pos, gate, mods]
    if final:
        in_specs.append(pl.BlockSpec((1, D_MODEL), lambda i, s: (0, 0)))
        args.append(final_g.reshape(1, D_MODEL))
        out_specs = [pl.BlockSpec((tm, D_MODEL), lambda i, s: (jnp.minimum(i, np_blocks - 1), 0)),
                     pl.BlockSpec((tm, D_MODEL), lambda i, s: (jnp.maximum(i - np_blocks, 0), 0))]
        out_shape = [jax.ShapeDtypeStruct((N_PROMPT, D_MODEL), F32),
                     jax.ShapeDtypeStruct((N_SAMPLE, D_MODEL), F32)]
    else:
        out_specs = pl.BlockSpec((tm, D_MODEL), lambda i, s: (i, 0))
        out_shape = jax.ShapeDtypeStruct((N_TOK, D_MODEL), F32)
    return pl.pallas_call(
        functools.partial(_combine_kernel, tm=tm, final=final),
        grid_spec=pltpu.PrefetchScalarGridSpec(
            num_scalar_prefetch=1,
            grid=(N_BLOCKS,),
            in_specs=in_specs,
            out_specs=out_specs,
            scratch_shapes=[pltpu.VMEM((2, BLOCK_ROWS, D_MODEL), BF16),
                            pltpu.SemaphoreType.DMA((2,))],
        ),
        out_shape=out_shape,
        compiler_params=_cparams(("arbitrary",)),
        name="moe_combine_final" if final else "moe_combine",
    )(*args)


def _moe(x, g, mods, layer, w_group, b_group, w_expert, b_expert, w1, w3, w2, final_g=None):
    pad = LANES - N_EXPERTS - N_GROUPS
    wr = jnp.concatenate([w_expert, w_group, jnp.zeros((D_MODEL, pad), F32)], axis=1)
    br = jnp.concatenate([b_expert, b_group, jnp.zeros((pad,), F32)]).reshape(1, LANES)
    xb, pos, gate, counts = _router(x, g, mods, layer, wr, br)
    tile_end, disp_src, comb_src = _chunk_tables(counts)
    y = _experts(tile_end, disp_src, xb, w1, w3, w2, layer)
    return _combine(comb_src, y, x, pos, gate, mods, layer, final_g)


def kernel(x_prompt, x_sample, cache_k, cache_v, c, c_ctx, ada_w, ada_b, norm1_g, norm2_g, final_g, hy_w_in, hy_conv_w, hy_conv_b, hy_f_w1, hy_f_b1, hy_f_w2, hy_f_b2, hy_f_w3, hy_f_b3, hy_f_freq, hy_bias, hy_w_out, at_w_qkv, at_lam_q1, at_lam_k1, at_lam_q2, at_lam_k2, at_subln_g, at_w_o, moe_w_group, moe_b_group, moe_w_expert, moe_b_expert, moe_w1, moe_w3, moe_w2):
    x = (x_prompt.reshape(N_PROMPT, D_MODEL), x_sample.reshape(N_SAMPLE, D_MODEL))
    cond = jnp.concatenate([c_ctx[None, :], c, jnp.zeros((MOD_ROWS - 1 - DEC_BATCH, D_MODEL), F32)], axis=0)
    mods = _ada_table(cond, ada_w, ada_b).reshape(DEPTH * MOD_ROWS * 6, 1, D_MODEL)
    ctx_k = cache_k.reshape(DEC_BATCH, DEPTH // 2, PAST_LEN, D_MODEL)
    ctx_v = cache_v.reshape(DEC_BATCH, DEPTH // 2, PAST_LEN, D_MODEL)

    dft = {}
    for L in (SEQ, DEC_SEQ):
        fwd, inv = _dft_mats(L)
        fwd = jnp.asarray(fwd)
        f_hi, f_lo = _split_bf16(fwd)
        dft[L] = (f_hi, f_lo, jnp.asarray(inv).astype(BF16))

    new_k, new_v = [], []
    for l in range(DEPTH):
        j = l // 2
        if l % 2 == 0:
            z = _norm_matmul(x, norm1_g[l], mods, l, 0, 1, hy_w_in, j)
            ys = []
            for L, rb0, n_seq, dt in ((SEQ, 0, BATCH, 512), (DEC_SEQ, N_PROMPT // DEC_SEQ, DEC_BATCH, 256)):
                f_hi, f_lo, finv = dft[L]
                kc = _filter_spectra(L, hy_f_w1[j], hy_f_b1[j], hy_f_w2[j], hy_f_b2[j], hy_f_w3[j],
                                     hy_f_b3[j], hy_f_freq[j], f_hi, f_lo)
                ys.append(_hyena_conv(z, hy_conv_w[j], hy_conv_b[j], f_hi, finv, kc, hy_bias[j],
                                      L=L, row_block0=rb0, n_seq=n_seq, dt=dt))
            x = _matmul_residual((ys[0], ys[1]), hy_w_out, j, x, mods, l, 2)
        else:
            lam_init = 0.8 - 0.6 * math.exp(-0.3 * l)
            qkv = _norm_matmul(x, norm1_g[l], mods, l, 0, 1, at_w_qkv, j, rope=True)
            lam_params = (at_lam_q1[j], at_lam_k1[j], at_lam_q2[j], at_lam_k2[j])
            op = _attention(qkv, lam_params, at_subln_g[j], lam_init, L=SEQ, row_block0=0,
                            n_seq=BATCH, tq=SEQ)
            os_ = _attention(qkv, lam_params, at_subln_g[j], lam_init, L=DEC_SEQ,
                             row_block0=N_PROMPT // DEC_SEQ, n_seq=DEC_BATCH, tq=256,
                             ctx=(ctx_k, ctx_v), j=j)
            new_k.append(qkv[:N_PROMPT, D_MODEL:2 * D_MODEL].astype(F32).reshape(BATCH, SEQ, N_HEADS, HEAD_W))
            new_v.append(qkv[:N_PROMPT, 2 * D_MODEL:].astype(F32).reshape(BATCH, SEQ, N_HEADS, HEAD_W))
            x = _matmul_residual((op, os_), at_w_o, j, x, mods, l, 2)
        x = _moe(x, norm2_g[l], mods, l, moe_w_group[l], moe_b_group[l], moe_w_expert[l],
                 moe_b_expert[l], moe_w1, moe_w3, moe_w2, final_g if l == DEPTH - 1 else None)
    y_prompt = x[0].reshape(BATCH, SEQ, D_MODEL)
    y_sample = x[1].reshape(DEC_BATCH, DEC_SEQ, D_MODEL)
    return (y_prompt, y_sample, jnp.stack(new_k, axis=1), jnp.stack(new_v, axis=1))
```

```python
import functools
import math

import numpy as np
import jax
import jax.numpy as jnp
from jax import lax
from jax.experimental import pallas as pl
from jax.experimental.pallas import tpu as pltpu

F32 = jnp.float32
BF16 = jnp.bfloat16

D_MODEL = 1024
BATCH = 16
SEQ = 256
DEPTH = 4
DEC_BATCH = 4
DEC_SEQ = 1024
PAST_LEN = 256
GRID_W = 64
EPS = 1e-6
HY_BANDS = 16
HY_EMB = 1 + 2 * HY_BANDS
HY_FILTER_HIDDEN = 64
HY_FAST_DECAY = 0.3
HY_SLOW_DECAY = 1.5
HY_TARGET = 1e-2
N_HEADS = 8
HEAD_DIM = 64
ROT_AXIS = HEAD_DIM // 2
ROPE_BASE = 10000.0
N_GROUPS = 4
EXPERTS_PER_GROUP = 8
N_EXPERTS = N_GROUPS * EXPERTS_PER_GROUP
D_EXPERT = D_MODEL // 4

N_PROMPT = BATCH * SEQ
N_SAMPLE = DEC_BATCH * DEC_SEQ
N_TOK = N_PROMPT + N_SAMPLE
MOD_ROWS = 8
LANES = 128
SUBLANES = 8
HEAD_W = 2 * HEAD_DIM
VMEM_LIMIT = 56 * 1024 * 1024

MOE_BLOCK = 256
N_BLOCKS = N_TOK // MOE_BLOCK
CHUNK = 2 * SUBLANES
BLOCK_ROWS = 2 * MOE_BLOCK + N_EXPERTS * CHUNK
BLOCK_CHUNKS = BLOCK_ROWS // CHUNK
EXPERT_TILE = 256
TILE_CHUNKS = EXPERT_TILE // CHUNK
N_TILES = (2 * N_TOK + N_BLOCKS * N_EXPERTS * (CHUNK - 1)) // EXPERT_TILE + N_EXPERTS


def _cparams(sem):
    return pltpu.CompilerParams(dimension_semantics=sem, vmem_limit_bytes=VMEM_LIMIT)


def _mod_row(blk, tm):
    start = blk * tm
    return jnp.where(start < N_PROMPT, 0, 1 + (start - N_PROMPT) // DEC_SEQ)


def _mod_index(layer, which, tm):
    def index_map(i, *_):
        return ((layer * MOD_ROWS + _mod_row(i, tm)) * 6 + which, 0, 0)
    return index_map


@functools.lru_cache(maxsize=None)
def _dft_mats(L):
    k = np.arange(L, dtype=np.float64)[:, None]
    j = np.arange(L, dtype=np.float64)[None, :]
    ang = np.pi * ((k * j) % (2 * L)) / L
    c = np.cos(ang)
    s = np.sin(ang)
    s[0, :] = np.where(np.arange(L) % 2 == 0, 1.0, -1.0)
    fwd = np.concatenate([c, s], axis=0)
    scale = np.full((2 * L,), 1.0 / L)
    scale[0] = scale[L] = 0.5 / L
    inv = fwd.T * scale[None, :]
    return fwd.astype(np.float32), inv.astype(np.float32)


@functools.lru_cache(maxsize=None)
def _filter_feats(L):
    pos = np.arange(L, dtype=np.float64)
    t = pos / (L - 1)
    bands = np.linspace(1e-4, HY_BANDS - 1, HY_BANDS)
    ang = (2.0 * math.pi / L) * pos[:, None] * bands[None, :]
    feats = np.concatenate([t[:, None], np.cos(ang), -np.sin(ang)], axis=-1)
    feats = np.pad(feats, ((0, 0), (0, LANES - HY_EMB)))
    deltas = np.abs(np.linspace(math.log(HY_TARGET) / HY_SLOW_DECAY,
                                math.log(HY_TARGET) / HY_FAST_DECAY, D_MODEL))
    window = np.exp(-t[:, None] * deltas[None, :])
    alt = np.where(np.arange(L) % 2 == 0, 1.0, -1.0)[:, None] * np.ones((1, LANES))
    return feats.astype(np.float32), window.astype(np.float32), alt.astype(np.float32)


@functools.lru_cache(maxsize=None)
def _rope_tables():
    pos = np.arange(DEC_SEQ)
    row = (pos // GRID_W).astype(np.float64)
    col = (pos % GRID_W).astype(np.float64)
    lane = np.arange(HEAD_W)
    d = lane % HEAD_DIM
    axis = d // ROT_AXIS
    n = d % ROT_AXIS
    half = n // (ROT_AXIS // 2)
    f = n % (ROT_AXIS // 2)
    inv = ROPE_BASE ** (-f.astype(np.float64) / (ROT_AXIS // 2))
    p = np.where(axis[None, :] == 0, row[:, None], col[:, None])
    ang = p * inv[None, :]
    sign = np.where(half == 0, -1.0, 1.0)[None, :]
    return np.cos(ang).astype(np.float32), (np.sin(ang) * sign).astype(np.float32)


def _split_bf16(a):
    hi = a.astype(BF16)
    lo = (a - hi.astype(F32)).astype(BF16)
    return hi, lo


def _dot(a, b):
    return jnp.dot(a, b, preferred_element_type=F32)


def _dot3(a_hi, a_lo, b_hi, b_lo):
    return _dot(a_hi, b_hi) + (_dot(a_hi, b_lo) + _dot(a_lo, b_hi))


def _ada_kernel(c_ref, w_ref, b_ref, o_ref):
    c = c_ref[...]
    s = c * jax.nn.sigmoid(c)
    o_ref[0] = _dot(s.astype(BF16), w_ref[0].astype(BF16)) + b_ref[0]


def _ada_table(cond, ada_w, ada_b):
    tn = 1536
    n = 6 * D_MODEL
    return pl.pallas_call(
        _ada_kernel,
        grid=(DEPTH, n // tn),
        in_specs=[
            pl.BlockSpec((MOD_ROWS, D_MODEL), lambda l, j: (0, 0)),
            pl.BlockSpec((1, D_MODEL, tn), lambda l, j: (l, 0, j)),
            pl.BlockSpec((1, 1, tn), lambda l, j: (l, 0, j)),
        ],
        out_specs=pl.BlockSpec((1, MOD_ROWS, tn), lambda l, j: (l, 0, j)),
        out_shape=jax.ShapeDtypeStruct((DEPTH, MOD_ROWS, n), F32),
        compiler_params=_cparams(("arbitrary", "arbitrary")),
        name="ada_table",
    )(cond, ada_w, ada_b.reshape(DEPTH, 1, n))


def _modulate(x, g, scale, shift):
    ms = jnp.mean(x * x, axis=-1, keepdims=True)
    y = (x * lax.rsqrt(ms + EPS)) * g
    return y * (1.0 + scale) + shift


MM_COLS = 512


def _row_group_specs(x, tm, width):
    np_blocks = N_PROMPT // tm
    if isinstance(x, tuple):
        specs = [pl.BlockSpec((tm, width), lambda i, *_: (jnp.minimum(i, np_blocks - 1), 0)),
                 pl.BlockSpec((tm, width), lambda i, *_: (jnp.maximum(i - np_blocks, 0), 0))]
        return specs, list(x)
    specs = [pl.BlockSpec((tm, width), lambda i, *_: (jnp.minimum(i, np_blocks - 1), 0)),
             pl.BlockSpec((tm, width), lambda i, *_: (jnp.maximum(i, np_blocks), 0))]
    return specs, [x, x]


def _pick_rows(i, tm, p_ref, s_ref):
    return jnp.where(i < N_PROMPT // tm, p_ref[...], s_ref[...])


ROPE_COLS = 2 * D_MODEL


def _normmm_kernel(xp_ref, xs_ref, g_ref, sh_ref, sc_ref, w_ref, *rest, rope, tm):
    if rope:
        cos_ref, sin_ref, o_ref, w_scr, wrot_scr = rest
    else:
        o_ref, w_scr = rest
    i = pl.program_id(0)

    @pl.when(i == 0)
    def _():
        w_scr[...] = w_ref[0].astype(BF16)
        if rope:
            lane = lax.broadcasted_iota(jnp.int32, (D_MODEL, MM_COLS), 1)
            first_half = (lane % ROT_AXIS) < (ROT_AXIS // 2)
            for j in range(ROPE_COLS // MM_COLS):
                cols = slice(j * MM_COLS, (j + 1) * MM_COLS)
                wc = w_ref[0, :, cols]
                wrot_scr[:, cols] = jnp.where(first_half,
                                              pltpu.roll(wc, MM_COLS - ROT_AXIS // 2, 1),
                                              pltpu.roll(wc, ROT_AXIS // 2, 1)).astype(BF16)

    x = _pick_rows(i, tm, xp_ref, xs_ref)
    h = _modulate(x, g_ref[...], sc_ref[0], sh_ref[0]).astype(BF16)
    n = w_scr.shape[1]
    for j in range(n // MM_COLS):
        cols = slice(j * MM_COLS, (j + 1) * MM_COLS)
        acc = _dot(h, w_scr[:, cols])
        if not (rope and j < ROPE_COLS // MM_COLS):
            o_ref[:, cols] = acc.astype(o_ref.dtype)
            continue

        @pl.when(i >= N_PROMPT // tm)
        def _():
            reps = MM_COLS // HEAD_W
            cos = jnp.tile(cos_ref[...], (1, reps))
            sin = jnp.tile(sin_ref[...], (1, reps))
            partner = _dot(h, wrot_scr[:, cols])
            o_ref[:, cols] = (acc * cos + partner * sin).astype(o_ref.dtype)

        @pl.when(i < N_PROMPT // tm)
        def _():
            o_ref[:, cols] = acc.astype(o_ref.dtype)


def _norm_matmul(x, g, mods, layer, which_shift, which_scale, w, widx, *, rope=False, tm=512):
    n = w.shape[2]
    rope_blocks = DEC_SEQ // tm

    def rope_idx(i):
        return (jnp.maximum(i - N_PROMPT // tm, 0) % rope_blocks, 0)

    x_specs, x_args = _row_group_specs(x, tm, D_MODEL)
    in_specs = x_specs + [
        pl.BlockSpec((1, D_MODEL), lambda i: (0, 0)),
        pl.BlockSpec((1, 1, D_MODEL), _mod_index(layer, which_shift, tm)),
        pl.BlockSpec((1, 1, D_MODEL), _mod_index(layer, which_scale, tm)),
        pl.BlockSpec((1, D_MODEL, n), lambda i: (widx, 0, 0), pipeline_mode=pl.Buffered(1)),
    ]
    args = x_args + [g.reshape(1, D_MODEL), mods, mods, w]
    scratch = [pltpu.VMEM((D_MODEL, n), BF16)]
    if rope:
        cos, sin = _rope_tables()
        in_specs += [pl.BlockSpec((tm, HEAD_W), rope_idx), pl.BlockSpec((tm, HEAD_W), rope_idx)]
        args += [jnp.asarray(cos), jnp.asarray(sin)]
        scratch += [pltpu.VMEM((D_MODEL, ROPE_COLS), BF16)]
    return pl.pallas_call(
        functools.partial(_normmm_kernel, rope=rope, tm=tm),
        grid=(N_TOK // tm,),
        in_specs=in_specs,
        out_specs=pl.BlockSpec((tm, n), lambda i: (i, 0)),
        out_shape=jax.ShapeDtypeStruct((N_TOK, n), BF16),
        scratch_shapes=scratch,
        compiler_params=_cparams(("arbitrary",)),
        name="norm_matmul_rope" if rope else "norm_matmul",
    )(*args)


def _mmres_kernel(ap_ref, as_ref, xp_ref, xs_ref, w_ref, gate_ref, o_ref, w_scr, *, tm):
    i = pl.program_id(0)

    @pl.when(i == 0)
    def _():
        w_scr[...] = w_ref[0].astype(BF16)

    a = _pick_rows(i, tm, ap_ref, as_ref)
    x = _pick_rows(i, tm, xp_ref, xs_ref)
    gate = gate_ref[0]
    for j in range(w_scr.shape[1] // MM_COLS):
        cols = slice(j * MM_COLS, (j + 1) * MM_COLS)
        o_ref[:, cols] = x[:, cols] + gate[:, cols] * _dot(a, w_scr[:, cols])


def _matmul_residual(a, w, widx, x, mods, layer, which_gate, *, tm=512):
    n = w.shape[2]
    a_specs, a_args = _row_group_specs(a, tm, D_MODEL)
    x_specs, x_args = _row_group_specs(x, tm, n)
    return pl.pallas_call(
        functools.partial(_mmres_kernel, tm=tm),
        grid=(N_TOK // tm,),
        in_specs=a_specs + x_specs + [
            pl.BlockSpec((1, D_MODEL, n), lambda i: (widx, 0, 0), pipeline_mode=pl.Buffered(1)),
            pl.BlockSpec((1, 1, n), _mod_index(layer, which_gate, tm)),
        ],
        out_specs=pl.BlockSpec((tm, n), lambda i: (i, 0)),
        out_shape=jax.ShapeDtypeStruct((N_TOK, n), F32),
        scratch_shapes=[pltpu.VMEM((D_MODEL, n), BF16)],
        compiler_params=_cparams(("arbitrary",)),
        name="matmul_residual",
    )(*a_args, *x_args, w, mods)


def _filter_kernel(feats_ref, win_ref, alt_ref, w1_ref, b1_ref, w2_ref, b2_ref, fq_ref,
                   w3f_ref, b3f_ref, w3b_ref, b3b_ref, f_ref, o_ref, h_scr, *, L):
    def dense(a, w_ref, b_ref):
        a_hi, a_lo = _split_bf16(a)
        w_hi, w_lo = _split_bf16(w_ref[...])
        return _dot3(a_hi, a_lo, w_hi, w_lo) + b_ref[...]

    @pl.when(jnp.logical_and(pl.program_id(0) == 0, pl.program_id(1) == 0))
    def _():
        fq = fq_ref[...]
        h1 = jnp.sin(fq[0:1, :] * dense(feats_ref[...], w1_ref, b1_ref))
        h_scr[...] = jnp.sin(fq[1:2, :] * dense(h1, w2_ref, b2_ref))

    h = h_scr[...]
    win = win_ref[...]
    hf = dense(h, w3f_ref, b3f_ref) * win
    hb = dense(h, w3b_ref, b3b_ref) * win
    row = lax.broadcasted_iota(jnp.int32, hf.shape, 0)
    hb = jnp.where(row == 0, 0.0, hb)
    kr = _dot(f_ref[0:L, :], (hf + hb).astype(BF16))
    ks = _dot(f_ref[L:2 * L, :], (hf - hb).astype(BF16))
    k_nyq = jnp.sum((hf + hb) * alt_ref[:, 0:1], axis=0, keepdims=True)
    o_ref[0, 0] = kr
    o_ref[0, 1] = jnp.where(row == 0, k_nyq, kr)
    o_ref[0, 2] = jnp.where(row == 0, 0.0, ks)


def _filter_spectra(L, w1, b1, w2, b2, w3, b3, freq, f_bf, *, dt=256):
    feats, window, alt = _filter_feats(L)
    fh = HY_FILTER_HIDDEN
    nd = D_MODEL // dt
    w1p = jnp.pad(w1, ((0, LANES - HY_EMB), (0, 0)))
    const = lambda shape: pl.BlockSpec(shape, lambda o, c: tuple(0 for _ in shape))
    return pl.pallas_call(
        functools.partial(_filter_kernel, L=L),
        grid=(2, nd),
        in_specs=[
            const((L, LANES)),
            pl.BlockSpec((L, dt), lambda o, c: (0, c)),
            const((L, LANES)),
            const((LANES, fh)), const((1, fh)), const((fh, fh)), const((1, fh)), const((2, fh)),
            pl.BlockSpec((fh, dt), lambda o, c: (0, (2 * o) * nd + c)),
            pl.BlockSpec((1, dt), lambda o, c: (0, (2 * o) * nd + c)),
            pl.BlockSpec((fh, dt), lambda o, c: (0, (2 * o + 1) * nd + c)),
            pl.BlockSpec((1, dt), lambda o, c: (0, (2 * o + 1) * nd + c)),
            const((2 * L, L)),
        ],
        out_specs=pl.BlockSpec((1, 3, L, dt), lambda o, c: (o, 0, 0, c)),
        out_shape=jax.ShapeDtypeStruct((2, 3, L, D_MODEL), F32),
        scratch_shapes=[pltpu.VMEM((L, fh), F32)],
        compiler_params=_cparams(("arbitrary", "arbitrary")),
        name=f"hyena_filter_spectra_{L}",
    )(jnp.asarray(feats), jnp.asarray(window), jnp.asarray(alt), w1p, b1.reshape(1, fh), w2,
      b2.reshape(1, fh), freq, w3, b3.reshape(1, -1), w3, b3.reshape(1, -1), f_bf)


def _conv_kernel(v_ref, x1_ref, x2_ref, cwv_ref, cw1_ref, cw2_ref, cbv_ref, cb1_ref, cb2_ref,
                 f_ref, finv_ref, kc_ref, bias_ref, o_ref, *, L):
    row = lax.broadcasted_iota(jnp.int32, v_ref.shape, 0)

    def short_conv(u_ref, w_ref, b_ref):
        u = u_ref[...].astype(F32)
        prev = jnp.where(row == 0, 0.0, pltpu.roll(u, 1, 0))
        nxt = jnp.where(row == L - 1, 0.0, pltpu.roll(u, L - 1, 0))
        w = w_ref[...]
        return prev * w[0:1, :] + u * w[1:2, :] + nxt * w[2:3, :] + b_ref[...]

    def long_conv(u, order):
        spec = _dot(f_ref[...], u.astype(BF16))
        a = spec[0:L, :]
        b = spec[L:2 * L, :]
        kra = kc_ref[order, 0]
        krb = kc_ref[order, 1]
        ks = kc_ref[order, 2]
        prod = jnp.concatenate([a * kra - b * ks, a * ks + b * krb], axis=0)
        return _dot(finv_ref[...], prod.astype(BF16)) + u * bias_ref[order:order + 1, :]

    v = short_conv(v_ref, cwv_ref, cbv_ref)
    y = short_conv(x1_ref, cw1_ref, cb1_ref) * long_conv(v, 0)
    y = short_conv(x2_ref, cw2_ref, cb2_ref) * long_conv(y, 1)
    o_ref[...] = y.astype(o_ref.dtype)


def _hyena_conv(z, conv_w, conv_b, f_bf, finv_bf, kc, bias, *, L, row_block0, n_seq, dt):
    nd = D_MODEL // dt
    cb = conv_b.reshape(1, 3 * D_MODEL)
    seg = lambda s: pl.BlockSpec((L, dt), lambda c, b: (row_block0 + b, s * nd + c))
    cw = lambda s: pl.BlockSpec((3, dt), lambda c, b: (0, s * nd + c))
    cbs = lambda s: pl.BlockSpec((1, dt), lambda c, b: (0, s * nd + c))
    return pl.pallas_call(
        functools.partial(_conv_kernel, L=L),
        grid=(nd, n_seq),
        in_specs=[
            seg(0), seg(1), seg(2), cw(0), cw(1), cw(2), cbs(0), cbs(1), cbs(2),
            pl.BlockSpec((2 * L, L), lambda c, b: (0, 0)),
            pl.BlockSpec((L, 2 * L), lambda c, b: (0, 0)),
            pl.BlockSpec((2, 3, L, dt), lambda c, b: (0, 0, 0, c)),
            pl.BlockSpec((2, dt), lambda c, b: (0, c)),
        ],
        out_specs=pl.BlockSpec((L, dt), lambda c, b: (b, c)),
        out_shape=jax.ShapeDtypeStruct((n_seq * L, D_MODEL), BF16),
        compiler_params=_cparams(("arbitrary", "arbitrary")),
        name=f"hyena_conv_{L}",
    )(z, z, z, conv_w, conv_w, conv_w, cb, cb, cb, f_bf, finv_bf, kc, bias)


def _attn_kernel(lq1_ref, lk1_ref, lq2_ref, lk2_ref, g_ref, q_ref, k_ref, v_ref, *rest,
                 lam_init, has_ctx, tq, nq):
    if has_ctx:
        ck_ref, cv_ref, o_ref = rest
    else:
        (o_ref,) = rest
    lam = (jnp.exp(jnp.sum(lq1_ref[...] * lk1_ref[...], axis=-1, keepdims=True))
           - jnp.exp(jnp.sum(lq2_ref[...] * lk2_ref[...], axis=-1, keepdims=True)) + lam_init)
    first_map = lax.broadcasted_iota(jnp.int32, (tq, HEAD_W), 1) < HEAD_DIM
    nt = (((1,), (1,)), ((), ()))

    def q_block(rows):
        for h in range(N_HEADS):
            cols = slice(h * HEAD_W, (h + 1) * HEAD_W)
            q = q_ref[rows, cols] * (HEAD_DIM ** -0.5)
            zero = jnp.zeros_like(q)
            q2 = jnp.concatenate([jnp.where(first_map, q, zero), jnp.where(first_map, zero, q)], axis=0)
            s = lax.dot_general(q2, k_ref[:, cols], nt, preferred_element_type=F32)
            m = jnp.max(s, axis=-1, keepdims=True)
            if has_ctx:
                sc = lax.dot_general(q2, ck_ref[0, 0, :, cols].astype(BF16), nt,
                                     preferred_element_type=F32)
                m = jnp.maximum(m, jnp.max(sc, axis=-1, keepdims=True))
                ec = jnp.exp(sc - m)
            e = jnp.exp(s - m)
            den = jnp.sum(e, axis=-1, keepdims=True)
            if has_ctx:
                den = den + jnp.sum(ec, axis=-1, keepdims=True)
            ov = _dot(e.astype(BF16), v_ref[:, cols])
            if has_ctx:
                ov = ov + _dot(ec.astype(BF16), cv_ref[0, 0, :, cols].astype(BF16))
            inv = 1.0 / den
            o = ov[0:tq] * inv[0:tq] - ov[tq:2 * tq] * (lam * inv[tq:2 * tq])
            ms = jnp.mean(o * o, axis=-1, keepdims=True)
            o = (o * lax.rsqrt(ms + EPS)) * g_ref[...] * (1.0 - lam_init)
            o_ref[rows, cols] = o.astype(o_ref.dtype)

    if nq == 1:
        q_block(slice(0, tq))
    else:
        def body(qi, carry):
            q_block(pl.ds(pl.multiple_of(qi * tq, tq), tq))
            return carry
        lax.fori_loop(0, nq, body, 0)


def _attention(qkv, lam_params, subln_g, lam_init, *, L, row_block0, n_seq, tq, ctx=None, j=0):
    small = pl.BlockSpec((1, HEAD_DIM), lambda b: (0, 0))
    in_specs = [small, small, small, small,
                pl.BlockSpec((1, HEAD_W), lambda b: (0, 0)),
                pl.BlockSpec((L, D_MODEL), lambda b: (row_block0 + b, 0)),
                pl.BlockSpec((L, D_MODEL), lambda b: (row_block0 + b, 1)),
                pl.BlockSpec((L, D_MODEL), lambda b: (row_block0 + b, 2))]
    args = [p.reshape(1, HEAD_DIM) for p in lam_params] + [subln_g.reshape(1, HEAD_W), qkv, qkv, qkv]
    if ctx is not None:
        ctx_spec = pl.BlockSpec((1, 1, PAST_LEN, D_MODEL), lambda b: (b, j, 0, 0))
        in_specs += [ctx_spec, ctx_spec]
        args += list(ctx)
    return pl.pallas_call(
        functools.partial(_attn_kernel, lam_init=lam_init, has_ctx=ctx is not None, tq=tq, nq=L // tq),
        grid=(n_seq,),
        in_specs=in_specs,
        out_specs=pl.BlockSpec((L, D_MODEL), lambda b: (b, 0)),
        out_shape=jax.ShapeDtypeStruct((n_seq * L, D_MODEL), BF16),
        compiler_params=_cparams(("arbitrary",)),
        name="diff_attention_ctx" if ctx is not None else "diff_attention",
    )(*args)


def _row_slots(pos, sel):
    big = float(BLOCK_ROWS)
    pos_a = jnp.max(pos, axis=-1, keepdims=True)
    pos_b = jnp.min(jnp.where(sel, pos, big), axis=-1, keepdims=True)
    return pos_a, pos_b


def _route_block(x, g, scale, shift, wr, br):
    tm = MOE_BLOCK
    h = _modulate(x, g, scale, shift).astype(BF16)
    logits = _dot(h, wr) + br
    lane = lax.broadcasted_iota(jnp.int32, (tm, LANES), 1)
    neg = -jnp.inf
    is_grp = jnp.logical_and(lane >= N_EXPERTS, lane < N_EXPERTS + N_GROUPS)
    lg = jnp.where(is_grp, logits, neg)
    mg = jnp.max(lg, axis=-1, keepdims=True)
    g_gate = 1.0 / jnp.sum(jnp.exp(lg - mg), axis=-1, keepdims=True)
    g_idx = jnp.min(jnp.where(lg == mg, lane - N_EXPERTS, N_GROUPS), axis=-1, keepdims=True)
    in_grp = jnp.logical_and(lane < N_EXPERTS, lane // EXPERTS_PER_GROUP == g_idx)
    le = jnp.where(in_grp, logits, neg)
    m1 = jnp.max(le, axis=-1, keepdims=True)
    i1 = jnp.min(jnp.where(le == m1, lane, LANES), axis=-1, keepdims=True)
    le2 = jnp.where(lane == i1, neg, le)
    m2 = jnp.max(le2, axis=-1, keepdims=True)
    i2 = jnp.min(jnp.where(le2 == m2, lane, LANES), axis=-1, keepdims=True)
    e2 = jnp.exp(m2 - m1)
    inv = 1.0 / (1.0 + e2)
    gate = jnp.where(lane == i1, g_gate * inv, jnp.where(lane == i2, g_gate * (e2 * inv), 0.0))
    sel = jnp.logical_or(lane == i1, lane == i2)
    onehot = jnp.where(sel, 1.0, 0.0)
    r = lax.broadcasted_iota(jnp.int32, (tm, tm), 0)
    c = lax.broadcasted_iota(jnp.int32, (tm, tm), 1)
    rank = _dot(jnp.where(c < r, 1.0, 0.0).astype(BF16), onehot.astype(BF16))
    count = jnp.sum(onehot, axis=0, keepdims=True)
    padded = jnp.ceil(count * (1.0 / CHUNK)) * CHUNK
    er = lax.broadcasted_iota(jnp.int32, (LANES, LANES), 0)
    ec = lax.broadcasted_iota(jnp.int32, (LANES, LANES), 1)
    before = jnp.where(er < ec, 1.0, 0.0).astype(BF16)
    start = _dot(jnp.broadcast_to(padded, (SUBLANES, LANES)).astype(BF16), before)[0:1, :]
    pos = jnp.where(sel, start + rank, -1.0)
    pos_a, pos_b = _row_slots(pos, sel)
    slot = lax.broadcasted_iota(jnp.int32, (tm, BLOCK_ROWS), 1).astype(F32)
    pick = jnp.where(jnp.logical_or(slot == pos_a, slot == pos_b), 1.0, 0.0).astype(BF16)
    xb = lax.dot_general(pick, h, (((0,), (0,)), ((), ())), preferred_element_type=F32).astype(BF16)
    return xb, pos, gate, count.astype(jnp.int32)


ROUTER_BLOCKS = 2


def _router_kernel(x_ref, g_ref, sh_ref, sc_ref, wr_ref, br_ref, xb_ref, pos_ref, gate_ref, cnt_ref):
    wr = wr_ref[...].astype(BF16)
    for k in range(ROUTER_BLOCKS):
        rows = slice(k * MOE_BLOCK, (k + 1) * MOE_BLOCK)
        xb, pos, gate, count = _route_block(x_ref[rows, :], g_ref[...], sc_ref[0], sh_ref[0], wr, br_ref[...])
        xb_ref[k * BLOCK_ROWS:(k + 1) * BLOCK_ROWS, :] = xb
        pos_ref[rows, :] = pos
        gate_ref[rows, :] = gate
        cnt_ref[k] = count


def _router(x, g, mods, layer, wr, br):
    tm = ROUTER_BLOCKS * MOE_BLOCK
    return pl.pallas_call(
        _router_kernel,
        grid=(N_TOK // tm,),
        in_specs=[
            pl.BlockSpec((tm, D_MODEL), lambda i: (i, 0)),
            pl.BlockSpec((1, D_MODEL), lambda i: (0, 0)),
            pl.BlockSpec((1, 1, D_MODEL), _mod_index(layer, 3, tm)),
            pl.BlockSpec((1, 1, D_MODEL), _mod_index(layer, 4, tm)),
            pl.BlockSpec((D_MODEL, LANES), lambda i: (0, 0)),
            pl.BlockSpec((1, LANES), lambda i: (0, 0)),
        ],
        out_specs=[
            pl.BlockSpec((ROUTER_BLOCKS * BLOCK_ROWS, D_MODEL), lambda i: (i, 0)),
            pl.BlockSpec((tm, LANES), lambda i: (i, 0)),
            pl.BlockSpec((tm, LANES), lambda i: (i, 0)),
            pl.BlockSpec((ROUTER_BLOCKS, 1, LANES), lambda i: (i, 0, 0)),
        ],
        out_shape=[
            jax.ShapeDtypeStruct((N_BLOCKS * BLOCK_ROWS, D_MODEL), BF16),
            jax.ShapeDtypeStruct((N_TOK, LANES), F32),
            jax.ShapeDtypeStruct((N_TOK, LANES), F32),
            jax.ShapeDtypeStruct((N_BLOCKS, 1, LANES), jnp.int32),
        ],
        compiler_params=_cparams(("arbitrary",)),
        name="moe_router",
    )(x, g.reshape(1, D_MODEL), mods, mods, wr, br)


TABLE_ROWS = -(-(N_TILES * TILE_CHUNKS) // LANES)


def _tables_kernel(cnt_ref, tend_ref, disp_ref, comb_ref):
    ne = N_EXPERTS
    cnt = jnp.concatenate([cnt_ref[...].astype(F32), jnp.zeros((LANES - N_BLOCKS, LANES), F32)], axis=0)
    m = jnp.ceil(cnt * (1.0 / CHUNK))
    r = lax.broadcasted_iota(jnp.int32, (LANES, LANES), 0)
    c = lax.broadcasted_iota(jnp.int32, (LANES, LANES), 1)
    mb = m.astype(BF16)
    bstart = _dot(mb, jnp.where(r < c, 1.0, 0.0).astype(BF16))
    cm = _dot(jnp.where(c < r, 1.0, 0.0).astype(BF16), mb)
    total = jnp.sum(m, axis=0, keepdims=True)
    tiles = jnp.ceil(total * (1.0 / TILE_CHUNKS))
    tile_end = _dot(jnp.broadcast_to(tiles, (SUBLANES, LANES)).astype(BF16),
                    jnp.where(r <= c, 1.0, 0.0).astype(BF16))[0:1, :]
    choff = (tile_end - tiles) * TILE_CHUNKS
    tend_ref[...] = tile_end.astype(jnp.int32)

    start_t = (choff + cm).T
    m_t = m.T
    bstart_t = bstart.T
    lane = lax.broadcasted_iota(jnp.int32, (ne, LANES), 1).astype(F32)
    acc = [jnp.zeros((ne, LANES), F32) for _ in range(TABLE_ROWS)]
    for b in range(N_BLOCKS):
        lo = start_t[0:ne, b:b + 1]
        hi = lo + m_t[0:ne, b:b + 1]
        bs = bstart_t[0:ne, b:b + 1]
        fwd = (b * BLOCK_CHUNKS) + bs - lo
        for j in range(TABLE_ROWS):
            g = lane + float(j * LANES)
            acc[j] = acc[j] + jnp.where(g >= lo, jnp.where(g < hi, fwd + g, 0.0), 0.0)
        back = jnp.where(lane >= bs, jnp.where(lane < hi - lo + bs, (lo - bs + lane) * CHUNK + 1.0, 0.0), 0.0)
        comb_ref[b:b + 1, :] = (jnp.sum(back, axis=0, keepdims=True) - 1.0).astype(jnp.int32)
    for j in range(TABLE_ROWS):
        disp_ref[j:j + 1, :] = (jnp.sum(acc[j], axis=0, keepdims=True) * CHUNK).astype(jnp.int32)


def _chunk_tables(counts):
    return pl.pallas_call(
        _tables_kernel,
        out_shape=[
            jax.ShapeDtypeStruct((1, LANES), jnp.int32),
            jax.ShapeDtypeStruct((TABLE_ROWS, LANES), jnp.int32),
            jax.ShapeDtypeStruct((N_BLOCKS, LANES), jnp.int32),
        ],
        name="moe_chunk_tables",
    )(counts.reshape(N_BLOCKS, LANES))


def _chunk_copies(read_row, n_chunks, src_hbm, dst, sem, wait, skip_negative=False):
    for c in range(n_chunks):
        row = read_row(c)

        def copy(row=row, c=c):
            cp = pltpu.make_async_copy(src_hbm.at[pl.ds(pl.multiple_of(row, CHUNK), CHUNK)],
                                       dst.at[pl.ds(c * CHUNK, CHUNK)], sem)
            if wait:
                cp.wait()
            else:
                cp.start()

        if skip_negative:
            pl.when(row >= 0)(copy)
        else:
            copy()


IN_DEPTH = 3


def _expert_kernel(tend_ref, src_ref, xb_hbm, w1_ref, w3_ref, w2_ref, y_hbm,
                   xbuf, ybuf, w13_scr, w2_scr, in_sem, out_sem):
    e = pl.program_id(0)
    first = jnp.where(e == 0, 0, tend_ref[0, jnp.maximum(e - 1, 0)])
    last = tend_ref[0, e]
    n_used = tend_ref[0, N_EXPERTS - 1]
    tiles_per_row = LANES // TILE_CHUNKS

    def fetch(tile, s, wait):
        row = tile // tiles_per_row
        col = (tile % tiles_per_row) * TILE_CHUNKS
        _chunk_copies(lambda c: src_ref[row, col + c], TILE_CHUNKS, xb_hbm, xbuf.at[s], in_sem.at[s], wait)

    def store(tile, s):
        rows = pl.ds(pl.multiple_of(tile * EXPERT_TILE, EXPERT_TILE), EXPERT_TILE)
        return pltpu.make_async_copy(ybuf.at[s], y_hbm.at[rows], out_sem.at[s])

    lead = IN_DEPTH - 1

    @pl.when(last > first)
    def _():
        @pl.when(first == 0)
        def _():
            for k in range(lead):
                fetch(jnp.minimum(k, n_used - 1), k, False)

        w13_scr[:, 0:D_EXPERT] = w1_ref[0, 0].astype(BF16)
        w13_scr[:, D_EXPERT:2 * D_EXPERT] = w3_ref[0, 0].astype(BF16)
        w2_scr[...] = w2_ref[0, 0].astype(BF16)

    def tile_body(t, carry):
        s = t % 2
        fetch(t, t % IN_DEPTH, True)
        ab = _dot(xbuf[t % IN_DEPTH], w13_scr[...])
        fetch(jnp.minimum(t + lead, n_used - 1), (t + lead) % IN_DEPTH, False)
        a = ab[:, 0:D_EXPERT]
        hid = (a * jax.nn.sigmoid(a)) * ab[:, D_EXPERT:2 * D_EXPERT]
        y = _dot(hid.astype(BF16), w2_scr[...]).astype(BF16)

        @pl.when(t >= 2)
        def _():
            store(t, s).wait()

        ybuf[s] = y
        store(t, s).start()
        return carry

    lax.fori_loop(first, last, tile_body, 0)

    @pl.when(e == N_EXPERTS - 1)
    def _():
        for k in range(lead):
            fetch(n_used - 1, (n_used + k) % IN_DEPTH, True)

        @pl.when(n_used >= 1)
        def _():
            store(0, (n_used - 1) % 2).wait()

        @pl.when(n_used >= 2)
        def _():
            store(0, n_used % 2).wait()

        ybuf[0] = jnp.zeros((EXPERT_TILE, D_MODEL), BF16)

        def start(t, carry):
            store(t, 0).start()
            return carry

        def drain(t, carry):
            store(t, 0).wait()
            return carry

        lax.fori_loop(n_used, N_TILES, start, 0)
        lax.fori_loop(n_used, N_TILES, drain, 0)


def _experts(tile_end, disp_src, xb, w1, w3, w2, layer):
    wsel = lambda e, tend, src: (layer, e, 0, 0)
    return pl.pallas_call(
        _expert_kernel,
        grid_spec=pltpu.PrefetchScalarGridSpec(
            num_scalar_prefetch=2,
            grid=(N_EXPERTS,),
            in_specs=[
                pl.BlockSpec(memory_space=pl.ANY),
                pl.BlockSpec((1, 1, D_MODEL, D_EXPERT), wsel),
                pl.BlockSpec((1, 1, D_MODEL, D_EXPERT), wsel),
                pl.BlockSpec((1, 1, D_EXPERT, D_MODEL), wsel),
            ],
            out_specs=pl.BlockSpec(memory_space=pl.ANY),
            scratch_shapes=[pltpu.VMEM((IN_DEPTH, EXPERT_TILE, D_MODEL), BF16),
                            pltpu.VMEM((2, EXPERT_TILE, D_MODEL), BF16),
                            pltpu.VMEM((D_MODEL, 2 * D_EXPERT), BF16),
                            pltpu.VMEM((D_EXPERT, D_MODEL), BF16),
                            pltpu.SemaphoreType.DMA((IN_DEPTH,)),
                            pltpu.SemaphoreType.DMA((2,))],
        ),
        out_shape=jax.ShapeDtypeStruct((N_TILES * EXPERT_TILE, D_MODEL), BF16),
        compiler_params=_cparams(("arbitrary",)),
        name="moe_experts",
    )(tile_end, disp_src, xb, w1, w3, w2)


def _combine_kernel(src_ref, y_hbm, x_ref, pos_ref, gate_ref, mg_ref, *rest, tm, final):
    if final:
        fg_ref, op_ref, os_ref, ybuf, sem = rest
    else:
        o_ref, ybuf, sem = rest
    i = pl.program_id(0)
    slot = i % 2

    def fetch(blk, s, wait):
        _chunk_copies(lambda c: src_ref[blk, c], BLOCK_CHUNKS, y_hbm, ybuf.at[s], sem.at[s], wait,
                      skip_negative=True)

    @pl.when(i == 0)
    def _():
        ybuf[...] = jnp.zeros_like(ybuf)
        fetch(0, 0, False)

    @pl.when(i + 1 < N_BLOCKS)
    def _():
        fetch(i + 1, 1 - slot, False)

    fetch(i, slot, True)
    pos = pos_ref[...]
    sel = pos >= 0.0
    gate = gate_ref[...]
    pos_a, pos_b = _row_slots(pos, sel)
    gate_a = jnp.sum(jnp.where(pos == pos_a, gate, 0.0), axis=-1, keepdims=True)
    gate_b = jnp.sum(jnp.where(jnp.logical_and(sel, pos == pos_b), gate, 0.0), axis=-1, keepdims=True)
    row = lax.broadcasted_iota(jnp.int32, (tm, BLOCK_ROWS), 1).astype(F32)
    weights = jnp.where(row == pos_a, gate_a, jnp.where(row == pos_b, gate_b, 0.0))
    moe = _dot(weights.astype(BF16), ybuf[slot])
    out = x_ref[...] + mg_ref[0] * moe
    if not final:
        o_ref[...] = out
        return

    ms = jnp.mean(out * out, axis=-1, keepdims=True)
    out = (out * lax.rsqrt(ms + EPS)) * fg_ref[...]

    @pl.when(i < N_PROMPT // tm)
    def _():
        op_ref[...] = out

    @pl.when(i >= N_PROMPT // tm)
    def _():
        os_ref[...] = out


def _combine(comb_src, y, x, pos, gate, mods, layer, final_g=None):
    tm = MOE_BLOCK
    final = final_g is not None
    np_blocks = N_PROMPT // tm

    def gate_idx(i, s):
        return ((layer * MOD_ROWS + _mod_row(i, tm)) * 6 + 5, 0, 0)

    in_specs = [
        pl.BlockSpec(memory_space=pl.ANY),
        pl.BlockSpec((tm, D_MODEL), lambda i, s: (i, 0)),
        pl.BlockSpec((tm, LANES), lambda i, s: (i, 0)),
        pl.BlockSpec((tm, LANES), lambda i, s: (i, 0)),
        pl.BlockSpec((1, 1, D_MODEL), gate_idx),
    ]
    args = [comb_src, y, x, pos, gate, mods]
    if final:
        in_specs.append(pl.BlockSpec((1, D_MODEL), lambda i, s: (0, 0)))
        args.append(final_g.reshape(1, D_MODEL))
        out_specs = [pl.BlockSpec((tm, D_MODEL), lambda i, s: (jnp.minimum(i, np_blocks - 1), 0)),
                     pl.BlockSpec((tm, D_MODEL), lambda i, s: (jnp.maximum(i - np_blocks, 0), 0))]
        out_shape = [jax.ShapeDtypeStruct((N_PROMPT, D_MODEL), F32),
                     jax.ShapeDtypeStruct((N_SAMPLE, D_MODEL), F32)]
    else:
        out_specs = pl.BlockSpec((tm, D_MODEL), lambda i, s: (i, 0))
        out_shape = jax.ShapeDtypeStruct((N_TOK, D_MODEL), F32)
    return pl.pallas_call(
        functools.partial(_combine_kernel, tm=tm, final=final),
        grid_spec=pltpu.PrefetchScalarGridSpec(
            num_scalar_prefetch=1,
            grid=(N_BLOCKS,),
            in_specs=in_specs,
            out_specs=out_specs,
            scratch_shapes=[pltpu.VMEM((2, BLOCK_ROWS, D_MODEL), BF16),
                            pltpu.SemaphoreType.DMA((2,))],
        ),
        out_shape=out_shape,
        compiler_params=_cparams(("arbitrary",)),
        name="moe_combine_final" if final else "moe_combine",
    )(*args)


def _moe(x, g, mods, layer, w_group, b_group, w_expert, b_expert, w1, w3, w2, final_g=None):
    pad = LANES - N_EXPERTS - N_GROUPS
    wr = jnp.concatenate([w_expert, w_group, jnp.zeros((D_MODEL, pad), F32)], axis=1)
    br = jnp.concatenate([b_expert, b_group, jnp.zeros((pad,), F32)]).reshape(1, LANES)
    xb, pos, gate, counts = _router(x, g, mods, layer, wr, br)
    tile_end, disp_src, comb_src = _chunk_tables(counts)
    y = _experts(tile_end, disp_src, xb, w1, w3, w2, layer)
    return _combine(comb_src, y, x, pos, gate, mods, layer, final_g)


def kernel(x_prompt, x_sample, cache_k, cache_v, c, c_ctx, ada_w, ada_b, norm1_g, norm2_g, final_g, hy_w_in, hy_conv_w, hy_conv_b, hy_f_w1, hy_f_b1, hy_f_w2, hy_f_b2, hy_f_w3, hy_f_b3, hy_f_freq, hy_bias, hy_w_out, at_w_qkv, at_lam_q1, at_lam_k1, at_lam_q2, at_lam_k2, at_subln_g, at_w_o, moe_w_group, moe_b_group, moe_w_expert, moe_b_expert, moe_w1, moe_w3, moe_w2):
    x = (x_prompt.reshape(N_PROMPT, D_MODEL), x_sample.reshape(N_SAMPLE, D_MODEL))
    cond = jnp.concatenate([c_ctx[None, :], c, jnp.zeros((MOD_ROWS - 1 - DEC_BATCH, D_MODEL), F32)], axis=0)
    mods = _ada_table(cond, ada_w, ada_b).reshape(DEPTH * MOD_ROWS * 6, 1, D_MODEL)
    ctx_k = cache_k.reshape(DEC_BATCH, DEPTH // 2, PAST_LEN, D_MODEL)
    ctx_v = cache_v.reshape(DEC_BATCH, DEPTH // 2, PAST_LEN, D_MODEL)

    dft = {}
    for L in (SEQ, DEC_SEQ):
        fwd, inv = _dft_mats(L)
        dft[L] = (jnp.asarray(fwd).astype(BF16), jnp.asarray(inv).astype(BF16))

    new_k, new_v = [], []
    for l in range(DEPTH):
        j = l // 2
        if l % 2 == 0:
            z = _norm_matmul(x, norm1_g[l], mods, l, 0, 1, hy_w_in, j)
            ys = []
            for L, rb0, n_seq, dt in ((SEQ, 0, BATCH, 512), (DEC_SEQ, N_PROMPT // DEC_SEQ, DEC_BATCH, 256)):
                f_bf, finv = dft[L]
                kc = _filter_spectra(L, hy_f_w1[j], hy_f_b1[j], hy_f_w2[j], hy_f_b2[j], hy_f_w3[j],
                                     hy_f_b3[j], hy_f_freq[j], f_bf)
                ys.append(_hyena_conv(z, hy_conv_w[j], hy_conv_b[j], f_bf, finv, kc, hy_bias[j],
                                      L=L, row_block0=rb0, n_seq=n_seq, dt=dt))
            x = _matmul_residual((ys[0], ys[1]), hy_w_out, j, x, mods, l, 2)
        else:
            lam_init = 0.8 - 0.6 * math.exp(-0.3 * l)
            qkv = _norm_matmul(x, norm1_g[l], mods, l, 0, 1, at_w_qkv, j, rope=True)
            lam_params = (at_lam_q1[j], at_lam_k1[j], at_lam_q2[j], at_lam_k2[j])
            op = _attention(qkv, lam_params, at_subln_g[j], lam_init, L=SEQ, row_block0=0,
                            n_seq=BATCH, tq=SEQ)
            os_ = _attention(qkv, lam_params, at_subln_g[j], lam_init, L=DEC_SEQ,
                             row_block0=N_PROMPT // DEC_SEQ, n_seq=DEC_BATCH, tq=256,
                             ctx=(ctx_k, ctx_v), j=j)
            new_k.append(qkv[:N_PROMPT, D_MODEL:2 * D_MODEL].astype(F32).reshape(BATCH, SEQ, N_HEADS, HEAD_W))
            new_v.append(qkv[:N_PROMPT, 2 * D_MODEL:].astype(F32).reshape(BATCH, SEQ, N_HEADS, HEAD_W))
            x = _matmul_residual((op, os_), at_w_o, j, x, mods, l, 2)
        x = _moe(x, norm2_g[l], mods, l, moe_w_group[l], moe_b_group[l], moe_w_expert[l],
                 moe_b_expert[l], moe_w1, moe_w3, moe_w2, final_g if l == DEPTH - 1 else None)
    y_prompt = x[0].reshape(BATCH, SEQ, D_MODEL)
    y_sample = x[1].reshape(DEC_BATCH, DEC_SEQ, D_MODEL)
    return (y_prompt, y_sample, jnp.stack(new_k, axis=1), jnp.stack(new_v, axis=1))
```

```python
import functools
import math

import numpy as np
import jax
import jax.numpy as jnp
from jax import lax
from jax.experimental import pallas as pl
from jax.experimental.pallas import tpu as pltpu

F32 = jnp.float32
BF16 = jnp.bfloat16

D_MODEL = 1024
BATCH = 16
SEQ = 256
DEPTH = 4
DEC_BATCH = 4
DEC_SEQ = 1024
PAST_LEN = 256
GRID_W = 64
EPS = 1e-6
HY_BANDS = 16
HY_EMB = 1 + 2 * HY_BANDS
HY_FILTER_HIDDEN = 64
HY_FAST_DECAY = 0.3
HY_SLOW_DECAY = 1.5
HY_TARGET = 1e-2
N_HEADS = 8
HEAD_DIM = 64
ROT_AXIS = HEAD_DIM // 2
ROPE_BASE = 10000.0
N_GROUPS = 4
EXPERTS_PER_GROUP = 8
N_EXPERTS = N_GROUPS * EXPERTS_PER_GROUP
D_EXPERT = D_MODEL // 4

N_PROMPT = BATCH * SEQ
N_SAMPLE = DEC_BATCH * DEC_SEQ
N_TOK = N_PROMPT + N_SAMPLE
MOD_ROWS = 8
LANES = 128
SUBLANES = 8
HEAD_W = 2 * HEAD_DIM
VMEM_LIMIT = 56 * 1024 * 1024

MOE_BLOCK = 256
N_BLOCKS = N_TOK // MOE_BLOCK
CHUNK = 2 * SUBLANES
BLOCK_ROWS = 2 * MOE_BLOCK + N_EXPERTS * CHUNK
BLOCK_CHUNKS = BLOCK_ROWS // CHUNK
EXPERT_TILE = 256
TILE_CHUNKS = EXPERT_TILE // CHUNK
N_TILES = (2 * N_TOK + N_BLOCKS * N_EXPERTS * (CHUNK - 1)) // EXPERT_TILE + N_EXPERTS


def _cparams(sem):
    return pltpu.CompilerParams(dimension_semantics=sem, vmem_limit_bytes=VMEM_LIMIT)


def _mod_row(blk, tm):
    start = blk * tm
    return jnp.where(start < N_PROMPT, 0, 1 + (start - N_PROMPT) // DEC_SEQ)


def _mod_index(layer, which, tm):
    def index_map(i, *_):
        return ((layer * MOD_ROWS + _mod_row(i, tm)) * 6 + which, 0, 0)
    return index_map


@functools.lru_cache(maxsize=None)
def _dft_mats(L):
    k = np.arange(L, dtype=np.float64)[:, None]
    j = np.arange(L, dtype=np.float64)[None, :]
    ang = np.pi * ((k * j) % (2 * L)) / L
    c = np.cos(ang)
    s = np.sin(ang)
    s[0, :] = np.where(np.arange(L) % 2 == 0, 1.0, -1.0)
    fwd = np.concatenate([c, s], axis=0)
    scale = np.full((2 * L,), 1.0 / L)
    scale[0] = scale[L] = 0.5 / L
    inv = fwd.T * scale[None, :]
    return fwd.astype(np.float32), inv.astype(np.float32)


@functools.lru_cache(maxsize=None)
def _filter_feats(L):
    pos = np.arange(L, dtype=np.float64)
    t = pos / (L - 1)
    bands = np.linspace(1e-4, HY_BANDS - 1, HY_BANDS)
    ang = (2.0 * math.pi / L) * pos[:, None] * bands[None, :]
    feats = np.concatenate([t[:, None], np.cos(ang), -np.sin(ang)], axis=-1)
    feats = np.pad(feats, ((0, 0), (0, LANES - HY_EMB)))
    deltas = np.abs(np.linspace(math.log(HY_TARGET) / HY_SLOW_DECAY,
                                math.log(HY_TARGET) / HY_FAST_DECAY, D_MODEL))
    window = np.exp(-t[:, None] * deltas[None, :])
    alt = np.where(np.arange(L) % 2 == 0, 1.0, -1.0)[:, None] * np.ones((1, LANES))
    return feats.astype(np.float32), window.astype(np.float32), alt.astype(np.float32)


@functools.lru_cache(maxsize=None)
def _rope_tables():
    pos = np.arange(DEC_SEQ)
    row = (pos // GRID_W).astype(np.float64)
    col = (pos % GRID_W).astype(np.float64)
    lane = np.arange(HEAD_W)
    d = lane % HEAD_DIM
    axis = d // ROT_AXIS
    n = d % ROT_AXIS
    half = n // (ROT_AXIS // 2)
    f = n % (ROT_AXIS // 2)
    inv = ROPE_BASE ** (-f.astype(np.float64) / (ROT_AXIS // 2))
    p = np.where(axis[None, :] == 0, row[:, None], col[:, None])
    ang = p * inv[None, :]
    sign = np.where(half == 0, -1.0, 1.0)[None, :]
    return np.cos(ang).astype(np.float32), (np.sin(ang) * sign).astype(np.float32)


def _split_bf16(a):
    hi = a.astype(BF16)
    lo = (a - hi.astype(F32)).astype(BF16)
    return hi, lo


def _dot(a, b):
    return jnp.dot(a, b, preferred_element_type=F32)


def _dot3(a_hi, a_lo, b_hi, b_lo):
    return _dot(a_hi, b_hi) + (_dot(a_hi, b_lo) + _dot(a_lo, b_hi))


def _ada_kernel(c_ref, w_ref, b_ref, o_ref):
    c = c_ref[...]
    s = c * jax.nn.sigmoid(c)
    o_ref[0] = _dot(s.astype(BF16), w_ref[0].astype(BF16)) + b_ref[0]


def _ada_table(cond, ada_w, ada_b):
    tn = 1536
    n = 6 * D_MODEL
    return pl.pallas_call(
        _ada_kernel,
        grid=(DEPTH, n // tn),
        in_specs=[
            pl.BlockSpec((MOD_ROWS, D_MODEL), lambda l, j: (0, 0)),
            pl.BlockSpec((1, D_MODEL, tn), lambda l, j: (l, 0, j)),
            pl.BlockSpec((1, 1, tn), lambda l, j: (l, 0, j)),
        ],
        out_specs=pl.BlockSpec((1, MOD_ROWS, tn), lambda l, j: (l, 0, j)),
        out_shape=jax.ShapeDtypeStruct((DEPTH, MOD_ROWS, n), F32),
        compiler_params=_cparams(("arbitrary", "arbitrary")),
        name="ada_table",
    )(cond, ada_w, ada_b.reshape(DEPTH, 1, n))


def _modulate(x, g, scale, shift):
    ms = jnp.mean(x * x, axis=-1, keepdims=True)
    y = (x * lax.rsqrt(ms + EPS)) * g
    return y * (1.0 + scale) + shift


MM_COLS = 512


def _row_group_specs(x, tm, width):
    np_blocks = N_PROMPT // tm
    if isinstance(x, tuple):
        specs = [pl.BlockSpec((tm, width), lambda i, *_: (jnp.minimum(i, np_blocks - 1), 0)),
                 pl.BlockSpec((tm, width), lambda i, *_: (jnp.maximum(i - np_blocks, 0), 0))]
        return specs, list(x)
    specs = [pl.BlockSpec((tm, width), lambda i, *_: (jnp.minimum(i, np_blocks - 1), 0)),
             pl.BlockSpec((tm, width), lambda i, *_: (jnp.maximum(i, np_blocks), 0))]
    return specs, [x, x]


def _pick_rows(i, tm, p_ref, s_ref):
    return jnp.where(i < N_PROMPT // tm, p_ref[...], s_ref[...])


ROPE_COLS = 2 * D_MODEL


def _normmm_kernel(xp_ref, xs_ref, g_ref, sh_ref, sc_ref, w_ref, *rest, rope, tm):
    if rope:
        cos_ref, sin_ref, o_ref, w_scr, wrot_scr = rest
    else:
        o_ref, w_scr = rest
    i = pl.program_id(0)

    @pl.when(i == 0)
    def _():
        w_scr[...] = w_ref[0].astype(BF16)
        if rope:
            lane = lax.broadcasted_iota(jnp.int32, (D_MODEL, MM_COLS), 1)
            first_half = (lane % ROT_AXIS) < (ROT_AXIS // 2)
            for j in range(ROPE_COLS // MM_COLS):
                cols = slice(j * MM_COLS, (j + 1) * MM_COLS)
                wc = w_ref[0, :, cols]
                wrot_scr[:, cols] = jnp.where(first_half,
                                              pltpu.roll(wc, MM_COLS - ROT_AXIS // 2, 1),
                                              pltpu.roll(wc, ROT_AXIS // 2, 1)).astype(BF16)

    x = _pick_rows(i, tm, xp_ref, xs_ref)
    h = _modulate(x, g_ref[...], sc_ref[0], sh_ref[0]).astype(BF16)
    n = w_scr.shape[1]
    for j in range(n // MM_COLS):
        cols = slice(j * MM_COLS, (j + 1) * MM_COLS)
        acc = _dot(h, w_scr[:, cols])
        if not (rope and j < ROPE_COLS // MM_COLS):
            o_ref[:, cols] = acc.astype(o_ref.dtype)
            continue

        @pl.when(i >= N_PROMPT // tm)
        def _():
            partner = _dot(h, wrot_scr[:, cols])
            cos = cos_ref[...]
            sin = sin_ref[...]
            for hh in range(MM_COLS // HEAD_W):
                sl = slice(hh * HEAD_W, (hh + 1) * HEAD_W)
                out = acc[:, sl] * cos + partner[:, sl] * sin
                o_ref[:, j * MM_COLS + hh * HEAD_W:j * MM_COLS + (hh + 1) * HEAD_W] = out.astype(o_ref.dtype)

        @pl.when(i < N_PROMPT // tm)
        def _():
            o_ref[:, cols] = acc.astype(o_ref.dtype)


def _norm_matmul(x, g, mods, layer, which_shift, which_scale, w, widx, *, rope=False, tm=512):
    n = w.shape[2]
    rope_blocks = DEC_SEQ // tm

    def rope_idx(i):
        return (jnp.maximum(i - N_PROMPT // tm, 0) % rope_blocks, 0)

    x_specs, x_args = _row_group_specs(x, tm, D_MODEL)
    in_specs = x_specs + [
        pl.BlockSpec((1, D_MODEL), lambda i: (0, 0)),
        pl.BlockSpec((1, 1, D_MODEL), _mod_index(layer, which_shift, tm)),
        pl.BlockSpec((1, 1, D_MODEL), _mod_index(layer, which_scale, tm)),
        pl.BlockSpec((1, D_MODEL, n), lambda i: (widx, 0, 0), pipeline_mode=pl.Buffered(1)),
    ]
    args = x_args + [g.reshape(1, D_MODEL), mods, mods, w]
    scratch = [pltpu.VMEM((D_MODEL, n), BF16)]
    if rope:
        cos, sin = _rope_tables()
        in_specs += [pl.BlockSpec((tm, HEAD_W), rope_idx), pl.BlockSpec((tm, HEAD_W), rope_idx)]
        args += [jnp.asarray(cos), jnp.asarray(sin)]
        scratch += [pltpu.VMEM((D_MODEL, ROPE_COLS), BF16)]
    return pl.pallas_call(
        functools.partial(_normmm_kernel, rope=rope, tm=tm),
        grid=(N_TOK // tm,),
        in_specs=in_specs,
        out_specs=pl.BlockSpec((tm, n), lambda i: (i, 0)),
        out_shape=jax.ShapeDtypeStruct((N_TOK, n), BF16),
        scratch_shapes=scratch,
        compiler_params=_cparams(("arbitrary",)),
        name="norm_matmul_rope" if rope else "norm_matmul",
    )(*args)


def _mmres_kernel(ap_ref, as_ref, xp_ref, xs_ref, w_ref, gate_ref, o_ref, w_scr, *, tm):
    i = pl.program_id(0)

    @pl.when(i == 0)
    def _():
        w_scr[...] = w_ref[0].astype(BF16)

    a = _pick_rows(i, tm, ap_ref, as_ref)
    x = _pick_rows(i, tm, xp_ref, xs_ref)
    gate = gate_ref[0]
    for j in range(w_scr.shape[1] // MM_COLS):
        cols = slice(j * MM_COLS, (j + 1) * MM_COLS)
        o_ref[:, cols] = x[:, cols] + gate[:, cols] * _dot(a, w_scr[:, cols])


def _matmul_residual(a, w, widx, x, mods, layer, which_gate, *, tm=512):
    n = w.shape[2]
    a_specs, a_args = _row_group_specs(a, tm, D_MODEL)
    x_specs, x_args = _row_group_specs(x, tm, n)
    return pl.pallas_call(
        functools.partial(_mmres_kernel, tm=tm),
        grid=(N_TOK // tm,),
        in_specs=a_specs + x_specs + [
            pl.BlockSpec((1, D_MODEL, n), lambda i: (widx, 0, 0), pipeline_mode=pl.Buffered(1)),
            pl.BlockSpec((1, 1, n), _mod_index(layer, which_gate, tm)),
        ],
        out_specs=pl.BlockSpec((tm, n), lambda i: (i, 0)),
        out_shape=jax.ShapeDtypeStruct((N_TOK, n), F32),
        scratch_shapes=[pltpu.VMEM((D_MODEL, n), BF16)],
        compiler_params=_cparams(("arbitrary",)),
        name="matmul_residual",
    )(*a_args, *x_args, w, mods)


def _filter_kernel(feats_ref, win_ref, alt_ref, w1_ref, b1_ref, w2_ref, b2_ref, fq_ref,
                   w3f_ref, b3f_ref, w3b_ref, b3b_ref, f_ref, o_ref, h_scr, *, L):
    def dense(a, w_ref, b_ref):
        a_hi, a_lo = _split_bf16(a)
        w_hi, w_lo = _split_bf16(w_ref[...])
        return _dot3(a_hi, a_lo, w_hi, w_lo) + b_ref[...]

    @pl.when(jnp.logical_and(pl.program_id(0) == 0, pl.program_id(1) == 0))
    def _():
        fq = fq_ref[...]
        h1 = jnp.sin(fq[0:1, :] * dense(feats_ref[...], w1_ref, b1_ref))
        h_scr[...] = jnp.sin(fq[1:2, :] * dense(h1, w2_ref, b2_ref))

    h = h_scr[...]
    win = win_ref[...]
    hf = dense(h, w3f_ref, b3f_ref) * win
    hb = dense(h, w3b_ref, b3b_ref) * win
    row = lax.broadcasted_iota(jnp.int32, hf.shape, 0)
    hb = jnp.where(row == 0, 0.0, hb)
    kr = _dot(f_ref[0:L, :], (hf + hb).astype(BF16))
    ks = _dot(f_ref[L:2 * L, :], (hf - hb).astype(BF16))
    k_nyq = jnp.sum((hf + hb) * alt_ref[:, 0:1], axis=0, keepdims=True)
    o_ref[0, 0] = kr
    o_ref[0, 1] = jnp.where(row == 0, k_nyq, kr)
    o_ref[0, 2] = jnp.where(row == 0, 0.0, ks)


def _filter_spectra(L, w1, b1, w2, b2, w3, b3, freq, f_bf, *, dt=256):
    feats, window, alt = _filter_feats(L)
    fh = HY_FILTER_HIDDEN
    nd = D_MODEL // dt
    w1p = jnp.pad(w1, ((0, LANES - HY_EMB), (0, 0)))
    const = lambda shape: pl.BlockSpec(shape, lambda o, c: tuple(0 for _ in shape))
    return pl.pallas_call(
        functools.partial(_filter_kernel, L=L),
        grid=(2, nd),
        in_specs=[
            const((L, LANES)),
            pl.BlockSpec((L, dt), lambda o, c: (0, c)),
            const((L, LANES)),
            const((LANES, fh)), const((1, fh)), const((fh, fh)), const((1, fh)), const((2, fh)),
            pl.BlockSpec((fh, dt), lambda o, c: (0, (2 * o) * nd + c)),
            pl.BlockSpec((1, dt), lambda o, c: (0, (2 * o) * nd + c)),
            pl.BlockSpec((fh, dt), lambda o, c: (0, (2 * o + 1) * nd + c)),
            pl.BlockSpec((1, dt), lambda o, c: (0, (2 * o + 1) * nd + c)),
            const((2 * L, L)),
        ],
        out_specs=pl.BlockSpec((1, 3, L, dt), lambda o, c: (o, 0, 0, c)),
        out_shape=jax.ShapeDtypeStruct((2, 3, L, D_MODEL), F32),
        scratch_shapes=[pltpu.VMEM((L, fh), F32)],
        compiler_params=_cparams(("arbitrary", "arbitrary")),
        name=f"hyena_filter_spectra_{L}",
    )(jnp.asarray(feats), jnp.asarray(window), jnp.asarray(alt), w1p, b1.reshape(1, fh), w2,
      b2.reshape(1, fh), freq, w3, b3.reshape(1, -1), w3, b3.reshape(1, -1), f_bf)


def _conv_kernel(v_ref, x1_ref, x2_ref, cwv_ref, cw1_ref, cw2_ref, cbv_ref, cb1_ref, cb2_ref,
                 f_ref, finv_ref, kc_ref, bias_ref, o_ref, *, L):
    row = lax.broadcasted_iota(jnp.int32, v_ref.shape, 0)

    def short_conv(u_ref, w_ref, b_ref):
        u = u_ref[...].astype(F32)
        prev = jnp.where(row == 0, 0.0, pltpu.roll(u, 1, 0))
        nxt = jnp.where(row == L - 1, 0.0, pltpu.roll(u, L - 1, 0))
        w = w_ref[...]
        return prev * w[0:1, :] + u * w[1:2, :] + nxt * w[2:3, :] + b_ref[...]

    def long_conv(u, order):
        spec = _dot(f_ref[...], u.astype(BF16))
        a = spec[0:L, :]
        b = spec[L:2 * L, :]
        kra = kc_ref[order, 0]
        krb = kc_ref[order, 1]
        ks = kc_ref[order, 2]
        prod = jnp.concatenate([a * kra - b * ks, a * ks + b * krb], axis=0)
        return _dot(finv_ref[...], prod.astype(BF16)) + u * bias_ref[order:order + 1, :]

    v = short_conv(v_ref, cwv_ref, cbv_ref)
    y = short_conv(x1_ref, cw1_ref, cb1_ref) * long_conv(v, 0)
    y = short_conv(x2_ref, cw2_ref, cb2_ref) * long_conv(y, 1)
    o_ref[...] = y.astype(o_ref.dtype)


def _hyena_conv(z, conv_w, conv_b, f_bf, finv_bf, kc, bias, *, L, row_block0, n_seq, dt):
    nd = D_MODEL // dt
    cb = conv_b.reshape(1, 3 * D_MODEL)
    seg = lambda s: pl.BlockSpec((L, dt), lambda c, b: (row_block0 + b, s * nd + c))
    cw = lambda s: pl.BlockSpec((3, dt), lambda c, b: (0, s * nd + c))
    cbs = lambda s: pl.BlockSpec((1, dt), lambda c, b: (0, s * nd + c))
    return pl.pallas_call(
        functools.partial(_conv_kernel, L=L),
        grid=(nd, n_seq),
        in_specs=[
            seg(0), seg(1), seg(2), cw(0), cw(1), cw(2), cbs(0), cbs(1), cbs(2),
            pl.BlockSpec((2 * L, L), lambda c, b: (0, 0)),
            pl.BlockSpec((L, 2 * L), lambda c, b: (0, 0)),
            pl.BlockSpec((2, 3, L, dt), lambda c, b: (0, 0, 0, c)),
            pl.BlockSpec((2, dt), lambda c, b: (0, c)),
        ],
        out_specs=pl.BlockSpec((L, dt), lambda c, b: (b, c)),
        out_shape=jax.ShapeDtypeStruct((n_seq * L, D_MODEL), BF16),
        compiler_params=_cparams(("arbitrary", "arbitrary")),
        name=f"hyena_conv_{L}",
    )(z, z, z, conv_w, conv_w, conv_w, cb, cb, cb, f_bf, finv_bf, kc, bias)


def _attn_kernel(lq1_ref, lk1_ref, lq2_ref, lk2_ref, g_ref, q_ref, k_ref, v_ref, *rest,
                 lam_init, has_ctx, tq, nq):
    if has_ctx:
        ck_ref, cv_ref, o_ref = rest
    else:
        (o_ref,) = rest
    lam = (jnp.exp(jnp.sum(lq1_ref[...] * lk1_ref[...], axis=-1, keepdims=True))
           - jnp.exp(jnp.sum(lq2_ref[...] * lk2_ref[...], axis=-1, keepdims=True)) + lam_init)
    first_map = lax.broadcasted_iota(jnp.int32, (tq, HEAD_W), 1) < HEAD_DIM
    nt = (((1,), (1,)), ((), ()))

    def q_block(rows):
        for h in range(N_HEADS):
            cols = slice(h * HEAD_W, (h + 1) * HEAD_W)
            q = q_ref[rows, cols] * (HEAD_DIM ** -0.5)
            zero = jnp.zeros_like(q)
            q2 = jnp.concatenate([jnp.where(first_map, q, zero), jnp.where(first_map, zero, q)], axis=0)
            s = lax.dot_general(q2, k_ref[:, cols], nt, preferred_element_type=F32)
            m = jnp.max(s, axis=-1, keepdims=True)
            if has_ctx:
                sc = lax.dot_general(q2, ck_ref[0, 0, :, cols].astype(BF16), nt,
                                     preferred_element_type=F32)
                m = jnp.maximum(m, jnp.max(sc, axis=-1, keepdims=True))
                ec = jnp.exp(sc - m)
            e = jnp.exp(s - m)
            den = jnp.sum(e, axis=-1, keepdims=True)
            if has_ctx:
                den = den + jnp.sum(ec, axis=-1, keepdims=True)
            ov = _dot(e.astype(BF16), v_ref[:, cols])
            if has_ctx:
                ov = ov + _dot(ec.astype(BF16), cv_ref[0, 0, :, cols].astype(BF16))
            inv = 1.0 / den
            o = ov[0:tq] * inv[0:tq] - ov[tq:2 * tq] * (lam * inv[tq:2 * tq])
            ms = jnp.mean(o * o, axis=-1, keepdims=True)
            o = (o * lax.rsqrt(ms + EPS)) * g_ref[...] * (1.0 - lam_init)
            o_ref[rows, cols] = o.astype(o_ref.dtype)

    if nq == 1:
        q_block(slice(0, tq))
    else:
        def body(qi, carry):
            q_block(pl.ds(pl.multiple_of(qi * tq, tq), tq))
            return carry
        lax.fori_loop(0, nq, body, 0)


def _attention(qkv, lam_params, subln_g, lam_init, *, L, row_block0, n_seq, tq, ctx=None, j=0):
    small = pl.BlockSpec((1, HEAD_DIM), lambda b: (0, 0))
    in_specs = [small, small, small, small,
                pl.BlockSpec((1, HEAD_W), lambda b: (0, 0)),
                pl.BlockSpec((L, D_MODEL), lambda b: (row_block0 + b, 0)),
                pl.BlockSpec((L, D_MODEL), lambda b: (row_block0 + b, 1)),
                pl.BlockSpec((L, D_MODEL), lambda b: (row_block0 + b, 2))]
    args = [p.reshape(1, HEAD_DIM) for p in lam_params] + [subln_g.reshape(1, HEAD_W), qkv, qkv, qkv]
    if ctx is not None:
        ctx_spec = pl.BlockSpec((1, 1, PAST_LEN, D_MODEL), lambda b: (b, j, 0, 0))
        in_specs += [ctx_spec, ctx_spec]
        args += list(ctx)
    return pl.pallas_call(
        functools.partial(_attn_kernel, lam_init=lam_init, has_ctx=ctx is not None, tq=tq, nq=L // tq),
        grid=(n_seq,),
        in_specs=in_specs,
        out_specs=pl.BlockSpec((L, D_MODEL), lambda b: (b, 0)),
        out_shape=jax.ShapeDtypeStruct((n_seq * L, D_MODEL), BF16),
        compiler_params=_cparams(("arbitrary",)),
        name="diff_attention_ctx" if ctx is not None else "diff_attention",
    )(*args)


def _row_slots(pos, sel):
    big = float(BLOCK_ROWS)
    pos_a = jnp.max(pos, axis=-1, keepdims=True)
    pos_b = jnp.min(jnp.where(sel, pos, big), axis=-1, keepdims=True)
    return pos_a, pos_b


def _route_rows(h, wr, br):
    tm = h.shape[0]
    logits = _dot(h, wr) + br
    lane = lax.broadcasted_iota(jnp.int32, (tm, LANES), 1)
    neg = -jnp.inf
    is_grp = jnp.logical_and(lane >= N_EXPERTS, lane < N_EXPERTS + N_GROUPS)
    lg = jnp.where(is_grp, logits, neg)
    mg = jnp.max(lg, axis=-1, keepdims=True)
    g_gate = 1.0 / jnp.sum(jnp.exp(lg - mg), axis=-1, keepdims=True)
    g_idx = jnp.min(jnp.where(lg == mg, lane - N_EXPERTS, N_GROUPS), axis=-1, keepdims=True)
    in_grp = jnp.logical_and(lane < N_EXPERTS, lane // EXPERTS_PER_GROUP == g_idx)
    le = jnp.where(in_grp, logits, neg)
    m1 = jnp.max(le, axis=-1, keepdims=True)
    i1 = jnp.min(jnp.where(le == m1, lane, LANES), axis=-1, keepdims=True)
    le2 = jnp.where(lane == i1, neg, le)
    m2 = jnp.max(le2, axis=-1, keepdims=True)
    i2 = jnp.min(jnp.where(le2 == m2, lane, LANES), axis=-1, keepdims=True)
    e2 = jnp.exp(m2 - m1)
    inv = 1.0 / (1.0 + e2)
    gate = jnp.where(lane == i1, g_gate * inv, jnp.where(lane == i2, g_gate * (e2 * inv), 0.0))
    return gate, jnp.where(jnp.logical_or(lane == i1, lane == i2), 1.0, 0.0)


def _sort_block(h, onehot):
    tm = MOE_BLOCK
    sel = onehot > 0.5
    r = lax.broadcasted_iota(jnp.int32, (tm, tm), 0)
    c = lax.broadcasted_iota(jnp.int32, (tm, tm), 1)
    rank = _dot(jnp.where(c < r, 1.0, 0.0).astype(BF16), onehot.astype(BF16))
    count = jnp.sum(onehot, axis=0, keepdims=True)
    padded = jnp.ceil(count * (1.0 / CHUNK)) * CHUNK
    er = lax.broadcasted_iota(jnp.int32, (LANES, LANES), 0)
    ec = lax.broadcasted_iota(jnp.int32, (LANES, LANES), 1)
    before = jnp.where(er < ec, 1.0, 0.0).astype(BF16)
    start = _dot(jnp.broadcast_to(padded, (SUBLANES, LANES)).astype(BF16), before)[0:1, :]
    pos = jnp.where(sel, start + rank, -1.0)
    pos_a, pos_b = _row_slots(pos, sel)
    slot = lax.broadcasted_iota(jnp.int32, (tm, BLOCK_ROWS), 1).astype(F32)
    pick = jnp.where(jnp.logical_or(slot == pos_a, slot == pos_b), 1.0, 0.0).astype(BF16)
    xb = lax.dot_general(pick, h, (((0,), (0,)), ((), ())), preferred_element_type=F32).astype(BF16)
    return xb, pos, count.astype(jnp.int32)


ROUTER_BLOCKS = 4


def _router_kernel(x_ref, g_ref, sh_ref, sc_ref, wr_ref, br_ref, xb_ref, pos_ref, gate_ref, cnt_ref):
    h = _modulate(x_ref[...], g_ref[...], sc_ref[0], sh_ref[0]).astype(BF16)
    gate, sel = _route_rows(h, wr_ref[...].astype(BF16), br_ref[...])
    gate_ref[...] = gate
    for k in range(ROUTER_BLOCKS):
        rows = slice(k * MOE_BLOCK, (k + 1) * MOE_BLOCK)
        xb, pos, count = _sort_block(h[rows, :], sel[rows, :])
        xb_ref[k * BLOCK_ROWS:(k + 1) * BLOCK_ROWS, :] = xb
        pos_ref[rows, :] = pos
        cnt_ref[k] = count


def _router(x, g, mods, layer, wr, br):
    tm = ROUTER_BLOCKS * MOE_BLOCK
    return pl.pallas_call(
        _router_kernel,
        grid=(N_TOK // tm,),
        in_specs=[
            pl.BlockSpec((tm, D_MODEL), lambda i: (i, 0)),
            pl.BlockSpec((1, D_MODEL), lambda i: (0, 0)),
            pl.BlockSpec((1, 1, D_MODEL), _mod_index(layer, 3, tm)),
            pl.BlockSpec((1, 1, D_MODEL), _mod_index(layer, 4, tm)),
            pl.BlockSpec((D_MODEL, LANES), lambda i: (0, 0)),
            pl.BlockSpec((1, LANES), lambda i: (0, 0)),
        ],
        out_specs=[
            pl.BlockSpec((ROUTER_BLOCKS * BLOCK_ROWS, D_MODEL), lambda i: (i, 0)),
            pl.BlockSpec((tm, LANES), lambda i: (i, 0)),
            pl.BlockSpec((tm, LANES), lambda i: (i, 0)),
            pl.BlockSpec((ROUTER_BLOCKS, 1, LANES), lambda i: (i, 0, 0)),
        ],
        out_shape=[
            jax.ShapeDtypeStruct((N_BLOCKS * BLOCK_ROWS, D_MODEL), BF16),
            jax.ShapeDtypeStruct((N_TOK, LANES), F32),
            jax.ShapeDtypeStruct((N_TOK, LANES), F32),
            jax.ShapeDtypeStruct((N_BLOCKS, 1, LANES), jnp.int32),
        ],
        compiler_params=_cparams(("arbitrary",)),
        name="moe_router",
    )(x, g.reshape(1, D_MODEL), mods, mods, wr, br)


TABLE_ROWS = -(-(N_TILES * TILE_CHUNKS) // LANES)


def _tables_kernel(cnt_ref, tend_ref, disp_ref, comb_ref):
    ne = N_EXPERTS
    cnt = jnp.concatenate([cnt_ref[...].astype(F32), jnp.zeros((LANES - N_BLOCKS, LANES), F32)], axis=0)
    m = jnp.ceil(cnt * (1.0 / CHUNK))
    r = lax.broadcasted_iota(jnp.int32, (LANES, LANES), 0)
    c = lax.broadcasted_iota(jnp.int32, (LANES, LANES), 1)
    mb = m.astype(BF16)
    bstart = _dot(mb, jnp.where(r < c, 1.0, 0.0).astype(BF16))
    cm = _dot(jnp.where(c < r, 1.0, 0.0).astype(BF16), mb)
    total = jnp.sum(m, axis=0, keepdims=True)
    tiles = jnp.ceil(total * (1.0 / TILE_CHUNKS))
    tile_end = _dot(jnp.broadcast_to(tiles, (SUBLANES, LANES)).astype(BF16),
                    jnp.where(r <= c, 1.0, 0.0).astype(BF16))[0:1, :]
    choff = (tile_end - tiles) * TILE_CHUNKS
    tend_ref[...] = tile_end.astype(jnp.int32)

    start_t = (choff + cm).T
    m_t = m.T
    bstart_t = bstart.T
    lane = lax.broadcasted_iota(jnp.int32, (ne, LANES), 1).astype(F32)
    acc = [jnp.zeros((ne, LANES), F32) for _ in range(TABLE_ROWS)]
    for b in range(N_BLOCKS):
        lo = start_t[0:ne, b:b + 1]
        hi = lo + m_t[0:ne, b:b + 1]
        bs = bstart_t[0:ne, b:b + 1]
        fwd = (b * BLOCK_CHUNKS) + bs - lo
        for j in range(TABLE_ROWS):
            g = lane + float(j * LANES)
            acc[j] = acc[j] + jnp.where(g >= lo, jnp.where(g < hi, fwd + g, 0.0), 0.0)
        back = jnp.where(lane >= bs, jnp.where(lane < hi - lo + bs, (lo - bs + lane) * CHUNK + 1.0, 0.0), 0.0)
        comb_ref[b:b + 1, :] = (jnp.sum(back, axis=0, keepdims=True) - 1.0).astype(jnp.int32)
    for j in range(TABLE_ROWS):
        disp_ref[j:j + 1, :] = (jnp.sum(acc[j], axis=0, keepdims=True) * CHUNK).astype(jnp.int32)


def _chunk_tables(counts):
    return pl.pallas_call(
        _tables_kernel,
        out_shape=[
            jax.ShapeDtypeStruct((1, LANES), jnp.int32),
            jax.ShapeDtypeStruct((TABLE_ROWS, LANES), jnp.int32),
            jax.ShapeDtypeStruct((N_BLOCKS, LANES), jnp.int32),
        ],
        name="moe_chunk_tables",
    )(counts.reshape(N_BLOCKS, LANES))


def _chunk_copies(read_row, n_chunks, src_hbm, dst, sem, wait, skip_negative=False):
    for c in range(n_chunks):
        row = read_row(c)

        def copy(row=row, c=c):
            cp = pltpu.make_async_copy(src_hbm.at[pl.ds(pl.multiple_of(row, CHUNK), CHUNK)],
                                       dst.at[pl.ds(c * CHUNK, CHUNK)], sem)
            if wait:
                cp.wait()
            else:
                cp.start()

        if skip_negative:
            pl.when(row >= 0)(copy)
        else:
            copy()


IN_DEPTH = 3


def _expert_kernel(tend_ref, src_ref, xb_hbm, w1_ref, w3_ref, w2_ref, y_hbm,
                   xbuf, ybuf, w13_scr, w2_scr, in_sem, out_sem):
    e = pl.program_id(0)
    first = jnp.where(e == 0, 0, tend_ref[0, jnp.maximum(e - 1, 0)])
    last = tend_ref[0, e]
    n_used = tend_ref[0, N_EXPERTS - 1]
    tiles_per_row = LANES // TILE_CHUNKS

    def fetch(tile, s, wait):
        row = tile // tiles_per_row
        col = (tile % tiles_per_row) * TILE_CHUNKS
        _chunk_copies(lambda c: src_ref[row, col + c], TILE_CHUNKS, xb_hbm, xbuf.at[s], in_sem.at[s], wait)

    def store(tile, s):
        rows = pl.ds(pl.multiple_of(tile * EXPERT_TILE, EXPERT_TILE), EXPERT_TILE)
        return pltpu.make_async_copy(ybuf.at[s], y_hbm.at[rows], out_sem.at[s])

    lead = IN_DEPTH - 1

    @pl.when(last > first)
    def _():
        @pl.when(first == 0)
        def _():
            for k in range(lead):
                fetch(jnp.minimum(k, n_used - 1), k, False)

        w13_scr[:, 0:D_EXPERT] = w1_ref[0, 0].astype(BF16)
        w13_scr[:, D_EXPERT:2 * D_EXPERT] = w3_ref[0, 0].astype(BF16)
        w2_scr[...] = w2_ref[0, 0].astype(BF16)

    def tile_body(t, carry):
        s = t % 2
        fetch(t, t % IN_DEPTH, True)
        ab = _dot(xbuf[t % IN_DEPTH], w13_scr[...])
        fetch(jnp.minimum(t + lead, n_used - 1), (t + lead) % IN_DEPTH, False)
        a = ab[:, 0:D_EXPERT]
        hid = (a * jax.nn.sigmoid(a)) * ab[:, D_EXPERT:2 * D_EXPERT]
        y = _dot(hid.astype(BF16), w2_scr[...]).astype(BF16)

        @pl.when(t >= 2)
        def _():
            store(t, s).wait()

        ybuf[s] = y
        store(t, s).start()
        return carry

    lax.fori_loop(first, last, tile_body, 0)

    @pl.when(e == N_EXPERTS - 1)
    def _():
        for k in range(lead):
            fetch(n_used - 1, (n_used + k) % IN_DEPTH, True)

        @pl.when(n_used >= 1)
        def _():
            store(0, (n_used - 1) % 2).wait()

        @pl.when(n_used >= 2)
        def _():
            store(0, n_used % 2).wait()

        ybuf[0] = jnp.zeros((EXPERT_TILE, D_MODEL), BF16)

        def start(t, carry):
            store(t, 0).start()
            return carry

        def drain(t, carry):
            store(t, 0).wait()
            return carry

        lax.fori_loop(n_used, N_TILES, start, 0)
        lax.fori_loop(n_used, N_TILES, drain, 0)


def _experts(tile_end, disp_src, xb, w1, w3, w2, layer):
    wsel = lambda e, tend, src: (layer, e, 0, 0)
    return pl.pallas_call(
        _expert_kernel,
        grid_spec=pltpu.PrefetchScalarGridSpec(
            num_scalar_prefetch=2,
            grid=(N_EXPERTS,),
            in_specs=[
                pl.BlockSpec(memory_space=pl.ANY),
                pl.BlockSpec((1, 1, D_MODEL, D_EXPERT), wsel),
                pl.BlockSpec((1, 1, D_MODEL, D_EXPERT), wsel),
                pl.BlockSpec((1, 1, D_EXPERT, D_MODEL), wsel),
            ],
            out_specs=pl.BlockSpec(memory_space=pl.ANY),
            scratch_shapes=[pltpu.VMEM((IN_DEPTH, EXPERT_TILE, D_MODEL), BF16),
                            pltpu.VMEM((2, EXPERT_TILE, D_MODEL), BF16),
                            pltpu.VMEM((D_MODEL, 2 * D_EXPERT), BF16),
                            pltpu.VMEM((D_EXPERT, D_MODEL), BF16),
                            pltpu.SemaphoreType.DMA((IN_DEPTH,)),
                            pltpu.SemaphoreType.DMA((2,))],
        ),
        out_shape=jax.ShapeDtypeStruct((N_TILES * EXPERT_TILE, D_MODEL), BF16),
        compiler_params=_cparams(("arbitrary",)),
        name="moe_experts",
    )(tile_end, disp_src, xb, w1, w3, w2)


def _combine_kernel(src_ref, y_hbm, x_ref, pos_ref, gate_ref, mg_ref, *rest, tm, final):
    if final:
        fg_ref, op_ref, os_ref, ybuf, sem = rest
    else:
        o_ref, ybuf, sem = rest
    i = pl.program_id(0)
    slot = i % 2

    def fetch(blk, s, wait):
        _chunk_copies(lambda c: src_ref[blk, c], BLOCK_CHUNKS, y_hbm, ybuf.at[s], sem.at[s], wait,
                      skip_negative=True)

    @pl.when(i == 0)
    def _():
        ybuf[...] = jnp.zeros_like(ybuf)
        fetch(0, 0, False)

    @pl.when(i + 1 < N_BLOCKS)
    def _():
        fetch(i + 1, 1 - slot, False)

    fetch(i, slot, True)
    pos = pos_ref[...]
    sel = pos >= 0.0
    gate = gate_ref[...]
    pos_a, pos_b = _row_slots(pos, sel)
    gate_a = jnp.sum(jnp.where(pos == pos_a, gate, 0.0), axis=-1, keepdims=True)
    gate_b = jnp.sum(jnp.where(jnp.logical_and(sel, pos == pos_b), gate, 0.0), axis=-1, keepdims=True)
    row = lax.broadcasted_iota(jnp.int32, (tm, BLOCK_ROWS), 1).astype(F32)
    weights = jnp.where(row == pos_a, gate_a, jnp.where(row == pos_b, gate_b, 0.0))
    moe = _dot(weights.astype(BF16), ybuf[slot])
    out = x_ref[...] + mg_ref[0] * moe
    if not final:
        o_ref[...] = out
        return

    ms = jnp.mean(out * out, axis=-1, keepdims=True)
    out = (out * lax.rsqrt(ms + EPS)) * fg_ref[...]

    @pl.when(i < N_PROMPT // tm)
    def _():
        op_ref[...] = out

    @pl.when(i >= N_PROMPT // tm)
    def _():
        os_ref[...] = out


def _combine(comb_src, y, x, pos, gate, mods, layer, final_g=None):
    tm = MOE_BLOCK
    final = final_g is not None
    np_blocks = N_PROMPT // tm

    def gate_idx(i, s):
        return ((layer * MOD_ROWS + _mod_row(i, tm)) * 6 + 5, 0, 0)

    in_specs = [
        pl.BlockSpec(memory_space=pl.ANY),
        pl.BlockSpec((tm, D_MODEL), lambda i, s: (i, 0)),
        pl.BlockSpec((tm, LANES), lambda i, s: (i, 0)),
        pl.BlockSpec((tm, LANES), lambda i, s: (i, 0)),
        pl.BlockSpec((1, 1, D_MODEL), gate_idx),
    ]
    args = [comb_src, y, x, pos, gate, mods]
    if final:
        in_specs.append(pl.BlockSpec((1, D_MODEL), lambda i, s: (0, 0)))
        args.append(final_g.reshape(1, D_MODEL))
        out_specs = [pl.BlockSpec((tm, D_MODEL), lambda i, s: (jnp.minimum(i, np_blocks - 1), 0)),
                     pl.BlockSpec((tm, D_MODEL), lambda i, s: (jnp.maximum(i - np_blocks, 0), 0))]
        out_shape = [jax.ShapeDtypeStruct((N_PROMPT, D_MODEL), F32),
                     jax.ShapeDtypeStruct((N_SAMPLE, D_MODEL), F32)]
    else:
        out_specs = pl.BlockSpec((tm, D_MODEL), lambda i, s: (i, 0))
        out_shape = jax.ShapeDtypeStruct((N_TOK, D_MODEL), F32)
    return pl.pallas_call(
        functools.partial(_combine_kernel, tm=tm, final=final),
        grid_spec=pltpu.PrefetchScalarGridSpec(
            num_scalar_prefetch=1,
            grid=(N_BLOCKS,),
            in_specs=in_specs,
            out_specs=out_specs,
            scratch_shapes=[pltpu.VMEM((2, BLOCK_ROWS, D_MODEL), BF16),
                            pltpu.SemaphoreType.DMA((2,))],
        ),
        out_shape=out_shape,
        compiler_params=_cparams(("arbitrary",)),
        name="moe_combine_final" if final else "moe_combine",
    )(*args)


def _moe(x, g, mods, layer, w_group, b_group, w_expert, b_expert, w1, w3, w2, final_g=None):
    pad = LANES - N_EXPERTS - N_GROUPS
    wr = jnp.concatenate([w_expert, w_group, jnp.zeros((D_MODEL, pad), F32)], axis=1)
    br = jnp.concatenate([b_expert, b_group, jnp.zeros((pad,), F32)]).reshape(1, LANES)
    xb, pos, gate, counts = _router(x, g, mods, layer, wr, br)
    tile_end, disp_src, comb_src = _chunk_tables(counts)
    y = _experts(tile_end, disp_src, xb, w1, w3, w2, layer)
    return _combine(comb_src, y, x, pos, gate, mods, layer, final_g)


def kernel(x_prompt, x_sample, cache_k, cache_v, c, c_ctx, ada_w, ada_b, norm1_g, norm2_g, final_g, hy_w_in, hy_conv_w, hy_conv_b, hy_f_w1, hy_f_b1, hy_f_w2, hy_f_b2, hy_f_w3, hy_f_b3, hy_f_freq, hy_bias, hy_w_out, at_w_qkv, at_lam_q1, at_lam_k1, at_lam_q2, at_lam_k2, at_subln_g, at_w_o, moe_w_group, moe_b_group, moe_w_expert, moe_b_expert, moe_w1, moe_w3, moe_w2):
    x = (x_prompt.reshape(N_PROMPT, D_MODEL), x_sample.reshape(N_SAMPLE, D_MODEL))
    cond = jnp.concatenate([c_ctx[None, :], c, jnp.zeros((MOD_ROWS - 1 - DEC_BATCH, D_MODEL), F32)], axis=0)
    mods = _ada_table(cond, ada_w, ada_b).reshape(DEPTH * MOD_ROWS * 6, 1, D_MODEL)
    ctx_k = cache_k.reshape(DEC_BATCH, DEPTH // 2, PAST_LEN, D_MODEL)
    ctx_v = cache_v.reshape(DEC_BATCH, DEPTH // 2, PAST_LEN, D_MODEL)

    dft = {}
    for L in (SEQ, DEC_SEQ):
        fwd, inv = _dft_mats(L)
        dft[L] = (jnp.asarray(fwd).astype(BF16), jnp.asarray(inv).astype(BF16))

    new_k, new_v = [], []
    for l in range(DEPTH):
        j = l // 2
        if l % 2 == 0:
            z = _norm_matmul(x, norm1_g[l], mods, l, 0, 1, hy_w_in, j)
            ys = []
            for L, rb0, n_seq, dt in ((SEQ, 0, BATCH, 512), (DEC_SEQ, N_PROMPT // DEC_SEQ, DEC_BATCH, 256)):
                f_bf, finv = dft[L]
                kc = _filter_spectra(L, hy_f_w1[j], hy_f_b1[j], hy_f_w2[j], hy_f_b2[j], hy_f_w3[j],
                                     hy_f_b3[j], hy_f_freq[j], f_bf)
                ys.append(_hyena_conv(z, hy_conv_w[j], hy_conv_b[j], f_bf, finv, kc, hy_bias[j],
                                      L=L, row_block0=rb0, n_seq=n_seq, dt=dt))
            x = _matmul_residual((ys[0], ys[1]), hy_w_out, j, x, mods, l, 2)
        else:
            lam_init = 0.8 - 0.6 * math.exp(-0.3 * l)
            qkv = _norm_matmul(x, norm1_g[l], mods, l, 0, 1, at_w_qkv, j, rope=True)
            lam_params = (at_lam_q1[j], at_lam_k1[j], at_lam_q2[j], at_lam_k2[j])
            op = _attention(qkv, lam_params, at_subln_g[j], lam_init, L=SEQ, row_block0=0,
                            n_seq=BATCH, tq=SEQ)
            os_ = _attention(qkv, lam_params, at_subln_g[j], lam_init, L=DEC_SEQ,
                             row_block0=N_PROMPT // DEC_SEQ, n_seq=DEC_BATCH, tq=256,
                             ctx=(ctx_k, ctx_v), j=j)
            new_k.append(qkv[:N_PROMPT, D_MODEL:2 * D_MODEL].astype(F32).reshape(BATCH, SEQ, N_HEADS, HEAD_W))
            new_v.append(qkv[:N_PROMPT, 2 * D_MODEL:].astype(F32).reshape(BATCH, SEQ, N_HEADS, HEAD_W))
            x = _matmul_residual((op, os_), at_w_o, j, x, mods, l, 2)
        x = _moe(x, norm2_g[l], mods, l, moe_w_group[l], moe_b_group[l], moe_w_expert[l],
                 moe_b_expert[l], moe_w1, moe_w3, moe_w2, final_g if l == DEPTH - 1 else None)
    y_prompt = x[0].reshape(BATCH, SEQ, D_MODEL)
    y_sample = x[1].reshape(DEC_BATCH, DEC_SEQ, D_MODEL)
    return (y_prompt, y_sample, jnp.stack(new_k, axis=1), jnp.stack(new_v, axis=1))
```

```python
import functools
import math

import numpy as np
import jax
import jax.numpy as jnp
from jax import lax
from jax.experimental import pallas as pl
from jax.experimental.pallas import tpu as pltpu

F32 = jnp.float32
BF16 = jnp.bfloat16

D_MODEL = 1024
BATCH = 16
SEQ = 256
DEPTH = 4
DEC_BATCH = 4
DEC_SEQ = 1024
PAST_LEN = 256
GRID_W = 64
EPS = 1e-6
HY_BANDS = 16
HY_EMB = 1 + 2 * HY_BANDS
HY_FILTER_HIDDEN = 64
HY_FAST_DECAY = 0.3
HY_SLOW_DECAY = 1.5
HY_TARGET = 1e-2
N_HEADS = 8
HEAD_DIM = 64
ROT_AXIS = HEAD_DIM // 2
ROPE_BASE = 10000.0
N_GROUPS = 4
EXPERTS_PER_GROUP = 8
N_EXPERTS = N_GROUPS * EXPERTS_PER_GROUP
D_EXPERT = D_MODEL // 4

N_PROMPT = BATCH * SEQ
N_SAMPLE = DEC_BATCH * DEC_SEQ
N_TOK = N_PROMPT + N_SAMPLE
MOD_ROWS = 8
LANES = 128
SUBLANES = 8
HEAD_W = 2 * HEAD_DIM
VMEM_LIMIT = 56 * 1024 * 1024

MOE_BLOCK = 256
N_BLOCKS = N_TOK // MOE_BLOCK
CHUNK = 2 * SUBLANES
BLOCK_ROWS = 2 * MOE_BLOCK + N_EXPERTS * CHUNK
BLOCK_CHUNKS = BLOCK_ROWS // CHUNK
EXPERT_TILE = 256
TILE_CHUNKS = EXPERT_TILE // CHUNK
N_TILES = (2 * N_TOK + N_BLOCKS * N_EXPERTS * (CHUNK - 1)) // EXPERT_TILE + N_EXPERTS


def _cparams(sem):
    return pltpu.CompilerParams(dimension_semantics=sem, vmem_limit_bytes=VMEM_LIMIT)


def _mod_row(blk, tm):
    start = blk * tm
    return jnp.where(start < N_PROMPT, 0, 1 + (start - N_PROMPT) // DEC_SEQ)


def _mod_index(layer, which, tm):
    def index_map(i, *_):
        return ((layer * MOD_ROWS + _mod_row(i, tm)) * 6 + which, 0, 0)
    return index_map


@functools.lru_cache(maxsize=None)
def _dft_mats(L):
    k = np.arange(L, dtype=np.float64)[:, None]
    j = np.arange(L, dtype=np.float64)[None, :]
    ang = np.pi * ((k * j) % (2 * L)) / L
    c = np.cos(ang)
    s = np.sin(ang)
    s[0, :] = np.where(np.arange(L) % 2 == 0, 1.0, -1.0)
    fwd = np.concatenate([c, s], axis=0)
    scale = np.full((2 * L,), 1.0 / L)
    scale[0] = scale[L] = 0.5 / L
    inv = fwd.T * scale[None, :]
    return fwd.astype(np.float32), inv.astype(np.float32)


@functools.lru_cache(maxsize=None)
def _filter_feats(L):
    pos = np.arange(L, dtype=np.float64)
    t = pos / (L - 1)
    bands = np.linspace(1e-4, HY_BANDS - 1, HY_BANDS)
    ang = (2.0 * math.pi / L) * pos[:, None] * bands[None, :]
    feats = np.concatenate([t[:, None], np.cos(ang), -np.sin(ang)], axis=-1)
    feats = np.pad(feats, ((0, 0), (0, LANES - HY_EMB)))
    deltas = np.abs(np.linspace(math.log(HY_TARGET) / HY_SLOW_DECAY,
                                math.log(HY_TARGET) / HY_FAST_DECAY, D_MODEL))
    window = np.exp(-t[:, None] * deltas[None, :])
    alt = np.where(np.arange(L) % 2 == 0, 1.0, -1.0)[:, None] * np.ones((1, LANES))
    return feats.astype(np.float32), window.astype(np.float32), alt.astype(np.float32)


@functools.lru_cache(maxsize=None)
def _rope_tables():
    pos = np.arange(DEC_SEQ)
    row = (pos // GRID_W).astype(np.float64)
    col = (pos % GRID_W).astype(np.float64)
    lane = np.arange(HEAD_W)
    d = lane % HEAD_DIM
    axis = d // ROT_AXIS
    n = d % ROT_AXIS
    half = n // (ROT_AXIS // 2)
    f = n % (ROT_AXIS // 2)
    inv = ROPE_BASE ** (-f.astype(np.float64) / (ROT_AXIS // 2))
    p = np.where(axis[None, :] == 0, row[:, None], col[:, None])
    ang = p * inv[None, :]
    sign = np.where(half == 0, -1.0, 1.0)[None, :]
    return np.cos(ang).astype(np.float32), (np.sin(ang) * sign).astype(np.float32)


def _split_bf16(a):
    hi = a.astype(BF16)
    lo = (a - hi.astype(F32)).astype(BF16)
    return hi, lo


def _dot(a, b):
    return jnp.dot(a, b, preferred_element_type=F32)


def _dot3(a_hi, a_lo, b_hi, b_lo):
    return _dot(a_hi, b_hi) + (_dot(a_hi, b_lo) + _dot(a_lo, b_hi))


def _ada_kernel(c_ref, w_ref, b_ref, o_ref):
    c = c_ref[...]
    s = c * jax.nn.sigmoid(c)
    o_ref[0] = _dot(s.astype(BF16), w_ref[0].astype(BF16)) + b_ref[0]


def _ada_table(cond, ada_w, ada_b):
    tn = 1536
    n = 6 * D_MODEL
    return pl.pallas_call(
        _ada_kernel,
        grid=(DEPTH, n // tn),
        in_specs=[
            pl.BlockSpec((MOD_ROWS, D_MODEL), lambda l, j: (0, 0)),
            pl.BlockSpec((1, D_MODEL, tn), lambda l, j: (l, 0, j)),
            pl.BlockSpec((1, 1, tn), lambda l, j: (l, 0, j)),
        ],
        out_specs=pl.BlockSpec((1, MOD_ROWS, tn), lambda l, j: (l, 0, j)),
        out_shape=jax.ShapeDtypeStruct((DEPTH, MOD_ROWS, n), F32),
        compiler_params=_cparams(("arbitrary", "arbitrary")),
        name="ada_table",
    )(cond, ada_w, ada_b.reshape(DEPTH, 1, n))


def _modulate(x, g, scale, shift):
    ms = jnp.mean(x * x, axis=-1, keepdims=True)
    y = (x * lax.rsqrt(ms + EPS)) * g
    return y * (1.0 + scale) + shift


MM_COLS = 512


def _row_group_specs(x, tm, width):
    np_blocks = N_PROMPT // tm
    if isinstance(x, tuple):
        specs = [pl.BlockSpec((tm, width), lambda i, *_: (jnp.minimum(i, np_blocks - 1), 0)),
                 pl.BlockSpec((tm, width), lambda i, *_: (jnp.maximum(i - np_blocks, 0), 0))]
        return specs, list(x)
    specs = [pl.BlockSpec((tm, width), lambda i, *_: (jnp.minimum(i, np_blocks - 1), 0)),
             pl.BlockSpec((tm, width), lambda i, *_: (jnp.maximum(i, np_blocks), 0))]
    return specs, [x, x]


def _pick_rows(i, tm, p_ref, s_ref):
    return jnp.where(i < N_PROMPT // tm, p_ref[...], s_ref[...])


ROPE_COLS = 2 * D_MODEL


def _normmm_kernel(xp_ref, xs_ref, g_ref, sh_ref, sc_ref, w_ref, *rest, rope, tm):
    if rope:
        cos_ref, sin_ref, o_ref, w_scr, wrot_scr = rest
    else:
        o_ref, w_scr = rest
    i = pl.program_id(0)

    @pl.when(i == 0)
    def _():
        w_scr[...] = w_ref[0].astype(BF16)
        if rope:
            lane = lax.broadcasted_iota(jnp.int32, (D_MODEL, MM_COLS), 1)
            first_half = (lane % ROT_AXIS) < (ROT_AXIS // 2)
            for j in range(ROPE_COLS // MM_COLS):
                cols = slice(j * MM_COLS, (j + 1) * MM_COLS)
                wc = w_ref[0, :, cols]
                wrot_scr[:, cols] = jnp.where(first_half,
                                              pltpu.roll(wc, MM_COLS - ROT_AXIS // 2, 1),
                                              pltpu.roll(wc, ROT_AXIS // 2, 1)).astype(BF16)

    x = _pick_rows(i, tm, xp_ref, xs_ref)
    h = _modulate(x, g_ref[...], sc_ref[0], sh_ref[0]).astype(BF16)
    n = w_scr.shape[1]
    for j in range(n // MM_COLS):
        cols = slice(j * MM_COLS, (j + 1) * MM_COLS)
        acc = _dot(h, w_scr[:, cols])
        if not (rope and j < ROPE_COLS // MM_COLS):
            o_ref[:, cols] = acc.astype(o_ref.dtype)
            continue

        partner = _dot(h, wrot_scr[:, cols])
        cos = cos_ref[...]
        sin = sin_ref[...]
        for hh in range(MM_COLS // HEAD_W):
            sl = slice(hh * HEAD_W, (hh + 1) * HEAD_W)
            out = acc[:, sl] * cos + partner[:, sl] * sin
            o_ref[:, j * MM_COLS + hh * HEAD_W:j * MM_COLS + (hh + 1) * HEAD_W] = out.astype(o_ref.dtype)


def _norm_matmul(x, g, mods, layer, which_shift, which_scale, w, widx, *, rope=False, tm=512):
    n = w.shape[2]
    rope_blocks = DEC_SEQ // tm

    np_blocks = N_PROMPT // tm

    def rope_idx(i):
        return (jnp.where(i < np_blocks, 0, 1 + (i - np_blocks) % rope_blocks), 0)

    x_specs, x_args = _row_group_specs(x, tm, D_MODEL)
    in_specs = x_specs + [
        pl.BlockSpec((1, D_MODEL), lambda i: (0, 0)),
        pl.BlockSpec((1, 1, D_MODEL), _mod_index(layer, which_shift, tm)),
        pl.BlockSpec((1, 1, D_MODEL), _mod_index(layer, which_scale, tm)),
        pl.BlockSpec((1, D_MODEL, n), lambda i: (widx, 0, 0), pipeline_mode=pl.Buffered(1)),
    ]
    args = x_args + [g.reshape(1, D_MODEL), mods, mods, w]
    scratch = [pltpu.VMEM((D_MODEL, n), BF16)]
    if rope:
        cos, sin = _rope_tables()
        cos = np.concatenate([np.ones((tm, HEAD_W), np.float32), cos], axis=0)
        sin = np.concatenate([np.zeros((tm, HEAD_W), np.float32), sin], axis=0)
        in_specs += [pl.BlockSpec((tm, HEAD_W), rope_idx), pl.BlockSpec((tm, HEAD_W), rope_idx)]
        args += [jnp.asarray(cos), jnp.asarray(sin)]
        scratch += [pltpu.VMEM((D_MODEL, ROPE_COLS), BF16)]
    return pl.pallas_call(
        functools.partial(_normmm_kernel, rope=rope, tm=tm),
        grid=(N_TOK // tm,),
        in_specs=in_specs,
        out_specs=pl.BlockSpec((tm, n), lambda i: (i, 0)),
        out_shape=jax.ShapeDtypeStruct((N_TOK, n), BF16),
        scratch_shapes=scratch,
        compiler_params=_cparams(("arbitrary",)),
        name="norm_matmul_rope" if rope else "norm_matmul",
    )(*args)


def _mmres_kernel(ap_ref, as_ref, xp_ref, xs_ref, w_ref, gate_ref, o_ref, w_scr, *, tm):
    i = pl.program_id(0)

    @pl.when(i == 0)
    def _():
        w_scr[...] = w_ref[0].astype(BF16)

    a = _pick_rows(i, tm, ap_ref, as_ref)
    x = _pick_rows(i, tm, xp_ref, xs_ref)
    gate = gate_ref[0]
    for j in range(w_scr.shape[1] // MM_COLS):
        cols = slice(j * MM_COLS, (j + 1) * MM_COLS)
        o_ref[:, cols] = x[:, cols] + gate[:, cols] * _dot(a, w_scr[:, cols])


def _matmul_residual(a, w, widx, x, mods, layer, which_gate, *, tm=512):
    n = w.shape[2]
    a_specs, a_args = _row_group_specs(a, tm, D_MODEL)
    x_specs, x_args = _row_group_specs(x, tm, n)
    return pl.pallas_call(
        functools.partial(_mmres_kernel, tm=tm),
        grid=(N_TOK // tm,),
        in_specs=a_specs + x_specs + [
            pl.BlockSpec((1, D_MODEL, n), lambda i: (widx, 0, 0), pipeline_mode=pl.Buffered(1)),
            pl.BlockSpec((1, 1, n), _mod_index(layer, which_gate, tm)),
        ],
        out_specs=pl.BlockSpec((tm, n), lambda i: (i, 0)),
        out_shape=jax.ShapeDtypeStruct((N_TOK, n), F32),
        scratch_shapes=[pltpu.VMEM((D_MODEL, n), BF16)],
        compiler_params=_cparams(("arbitrary",)),
        name="matmul_residual",
    )(*a_args, *x_args, w, mods)


def _filter_kernel(feats_ref, win_ref, alt_ref, w1_ref, b1_ref, w2_ref, b2_ref, fq_ref,
                   w3f_ref, b3f_ref, w3b_ref, b3b_ref, f_ref, o_ref, h_scr, *, L):
    def dense(a, w_ref, b_ref):
        a_hi, a_lo = _split_bf16(a)
        w_hi, w_lo = _split_bf16(w_ref[...])
        return _dot3(a_hi, a_lo, w_hi, w_lo) + b_ref[...]

    @pl.when(jnp.logical_and(pl.program_id(0) == 0, pl.program_id(1) == 0))
    def _():
        fq = fq_ref[...]
        h1 = jnp.sin(fq[0:1, :] * dense(feats_ref[...], w1_ref, b1_ref))
        h_scr[...] = jnp.sin(fq[1:2, :] * dense(h1, w2_ref, b2_ref))

    h = h_scr[...]
    win = win_ref[...]
    hf = dense(h, w3f_ref, b3f_ref) * win
    hb = dense(h, w3b_ref, b3b_ref) * win
    row = lax.broadcasted_iota(jnp.int32, hf.shape, 0)
    hb = jnp.where(row == 0, 0.0, hb)
    kr = _dot(f_ref[0:L, :], (hf + hb).astype(BF16))
    ks = _dot(f_ref[L:2 * L, :], (hf - hb).astype(BF16))
    k_nyq = jnp.sum((hf + hb) * alt_ref[:, 0:1], axis=0, keepdims=True)
    o_ref[0, 0] = kr
    o_ref[0, 1] = jnp.where(row == 0, k_nyq, kr)
    o_ref[0, 2] = jnp.where(row == 0, 0.0, ks)


def _filter_spectra(L, w1, b1, w2, b2, w3, b3, freq, f_bf, *, dt=256):
    feats, window, alt = _filter_feats(L)
    fh = HY_FILTER_HIDDEN
    nd = D_MODEL // dt
    w1p = jnp.pad(w1, ((0, LANES - HY_EMB), (0, 0)))
    const = lambda shape: pl.BlockSpec(shape, lambda o, c: tuple(0 for _ in shape))
    return pl.pallas_call(
        functools.partial(_filter_kernel, L=L),
        grid=(2, nd),
        in_specs=[
            const((L, LANES)),
            pl.BlockSpec((L, dt), lambda o, c: (0, c)),
            const((L, LANES)),
            const((LANES, fh)), const((1, fh)), const((fh, fh)), const((1, fh)), const((2, fh)),
            pl.BlockSpec((fh, dt), lambda o, c: (0, (2 * o) * nd + c)),
            pl.BlockSpec((1, dt), lambda o, c: (0, (2 * o) * nd + c)),
            pl.BlockSpec((fh, dt), lambda o, c: (0, (2 * o + 1) * nd + c)),
            pl.BlockSpec((1, dt), lambda o, c: (0, (2 * o + 1) * nd + c)),
            const((2 * L, L)),
        ],
        out_specs=pl.BlockSpec((1, 3, L, dt), lambda o, c: (o, 0, 0, c)),
        out_shape=jax.ShapeDtypeStruct((2, 3, L, D_MODEL), F32),
        scratch_shapes=[pltpu.VMEM((L, fh), F32)],
        compiler_params=_cparams(("arbitrary", "arbitrary")),
        name=f"hyena_filter_spectra_{L}",
    )(jnp.asarray(feats), jnp.asarray(window), jnp.asarray(alt), w1p, b1.reshape(1, fh), w2,
      b2.reshape(1, fh), freq, w3, b3.reshape(1, -1), w3, b3.reshape(1, -1), f_bf)


def _conv_kernel(v_ref, x1_ref, x2_ref, cwv_ref, cw1_ref, cw2_ref, cbv_ref, cb1_ref, cb2_ref,
                 f_ref, finv_ref, kc_ref, bias_ref, o_ref, *, L):
    row = lax.broadcasted_iota(jnp.int32, v_ref.shape, 0)

    def short_conv(u_ref, w_ref, b_ref):
        u = u_ref[...].astype(F32)
        prev = jnp.where(row == 0, 0.0, pltpu.roll(u, 1, 0))
        nxt = jnp.where(row == L - 1, 0.0, pltpu.roll(u, L - 1, 0))
        w = w_ref[...]
        return prev * w[0:1, :] + u * w[1:2, :] + nxt * w[2:3, :] + b_ref[...]

    def long_conv(u, order):
        spec = _dot(f_ref[...], u.astype(BF16))
        a = spec[0:L, :]
        b = spec[L:2 * L, :]
        kra = kc_ref[order, 0]
        krb = kc_ref[order, 1]
        ks = kc_ref[order, 2]
        prod = jnp.concatenate([a * kra - b * ks, a * ks + b * krb], axis=0)
        return _dot(finv_ref[...], prod.astype(BF16)) + u * bias_ref[order:order + 1, :]

    v = short_conv(v_ref, cwv_ref, cbv_ref)
    y = short_conv(x1_ref, cw1_ref, cb1_ref) * long_conv(v, 0)
    y = short_conv(x2_ref, cw2_ref, cb2_ref) * long_conv(y, 1)
    o_ref[...] = y.astype(o_ref.dtype)


def _hyena_conv(z, conv_w, conv_b, f_bf, finv_bf, kc, bias, *, L, row_block0, n_seq, dt):
    nd = D_MODEL // dt
    cb = conv_b.reshape(1, 3 * D_MODEL)
    seg = lambda s: pl.BlockSpec((L, dt), lambda c, b: (row_block0 + b, s * nd + c))
    cw = lambda s: pl.BlockSpec((3, dt), lambda c, b: (0, s * nd + c))
    cbs = lambda s: pl.BlockSpec((1, dt), lambda c, b: (0, s * nd + c))
    return pl.pallas_call(
        functools.partial(_conv_kernel, L=L),
        grid=(nd, n_seq),
        in_specs=[
            seg(0), seg(1), seg(2), cw(0), cw(1), cw(2), cbs(0), cbs(1), cbs(2),
            pl.BlockSpec((2 * L, L), lambda c, b: (0, 0)),
            pl.BlockSpec((L, 2 * L), lambda c, b: (0, 0)),
            pl.BlockSpec((2, 3, L, dt), lambda c, b: (0, 0, 0, c)),
            pl.BlockSpec((2, dt), lambda c, b: (0, c)),
        ],
        out_specs=pl.BlockSpec((L, dt), lambda c, b: (b, c)),
        out_shape=jax.ShapeDtypeStruct((n_seq * L, D_MODEL), BF16),
        compiler_params=_cparams(("arbitrary", "arbitrary")),
        name=f"hyena_conv_{L}",
    )(z, z, z, conv_w, conv_w, conv_w, cb, cb, cb, f_bf, finv_bf, kc, bias)


def _attn_kernel(lq1_ref, lk1_ref, lq2_ref, lk2_ref, g_ref, q_ref, k_ref, v_ref, *rest,
                 lam_init, has_ctx, tq, nq):
    if has_ctx:
        ck_ref, cv_ref, o_ref = rest
    else:
        (o_ref,) = rest
    lam = (jnp.exp(jnp.sum(lq1_ref[...] * lk1_ref[...], axis=-1, keepdims=True))
           - jnp.exp(jnp.sum(lq2_ref[...] * lk2_ref[...], axis=-1, keepdims=True)) + lam_init)
    first_map = lax.broadcasted_iota(jnp.int32, (tq, HEAD_W), 1) < HEAD_DIM
    nt = (((1,), (1,)), ((), ()))

    def q_block(rows):
        for h in range(N_HEADS):
            cols = slice(h * HEAD_W, (h + 1) * HEAD_W)
            q = q_ref[rows, cols] * (HEAD_DIM ** -0.5)
            zero = jnp.zeros_like(q)
            q2 = jnp.concatenate([jnp.where(first_map, q, zero), jnp.where(first_map, zero, q)], axis=0)
            s = lax.dot_general(q2, k_ref[:, cols], nt, preferred_element_type=F32)
            m = jnp.max(s, axis=-1, keepdims=True)
            if has_ctx:
                sc = lax.dot_general(q2, ck_ref[0, 0, :, cols].astype(BF16), nt,
                                     preferred_element_type=F32)
                m = jnp.maximum(m, jnp.max(sc, axis=-1, keepdims=True))
                ec = jnp.exp(sc - m)
            e = jnp.exp(s - m)
            den = jnp.sum(e, axis=-1, keepdims=True)
            if has_ctx:
                den = den + jnp.sum(ec, axis=-1, keepdims=True)
            ov = _dot(e.astype(BF16), v_ref[:, cols])
            if has_ctx:
                ov = ov + _dot(ec.astype(BF16), cv_ref[0, 0, :, cols].astype(BF16))
            inv = 1.0 / den
            o = ov[0:tq] * inv[0:tq] - ov[tq:2 * tq] * (lam * inv[tq:2 * tq])
            ms = jnp.mean(o * o, axis=-1, keepdims=True)
            o = (o * lax.rsqrt(ms + EPS)) * g_ref[...] * (1.0 - lam_init)
            o_ref[rows, cols] = o.astype(o_ref.dtype)

    if nq == 1:
        q_block(slice(0, tq))
    else:
        def body(qi, carry):
            q_block(pl.ds(pl.multiple_of(qi * tq, tq), tq))
            return carry
        lax.fori_loop(0, nq, body, 0)


def _attention(qkv, lam_params, subln_g, lam_init, *, L, row_block0, n_seq, tq, ctx=None, j=0):
    small = pl.BlockSpec((1, HEAD_DIM), lambda b: (0, 0))
    in_specs = [small, small, small, small,
                pl.BlockSpec((1, HEAD_W), lambda b: (0, 0)),
                pl.BlockSpec((L, D_MODEL), lambda b: (row_block0 + b, 0)),
                pl.BlockSpec((L, D_MODEL), lambda b: (row_block0 + b, 1)),
                pl.BlockSpec((L, D_MODEL), lambda b: (row_block0 + b, 2))]
    args = [p.reshape(1, HEAD_DIM) for p in lam_params] + [subln_g.reshape(1, HEAD_W), qkv, qkv, qkv]
    if ctx is not None:
        ctx_spec = pl.BlockSpec((1, 1, PAST_LEN, D_MODEL), lambda b: (b, j, 0, 0))
        in_specs += [ctx_spec, ctx_spec]
        args += list(ctx)
    return pl.pallas_call(
        functools.partial(_attn_kernel, lam_init=lam_init, has_ctx=ctx is not None, tq=tq, nq=L // tq),
        grid=(n_seq,),
        in_specs=in_specs,
        out_specs=pl.BlockSpec((L, D_MODEL), lambda b: (b, 0)),
        out_shape=jax.ShapeDtypeStruct((n_seq * L, D_MODEL), BF16),
        compiler_params=_cparams(("arbitrary",)),
        name="diff_attention_ctx" if ctx is not None else "diff_attention",
    )(*args)


def _row_slots(pos, sel):
    big = float(BLOCK_ROWS)
    pos_a = jnp.max(pos, axis=-1, keepdims=True)
    pos_b = jnp.min(jnp.where(sel, pos, big), axis=-1, keepdims=True)
    return pos_a, pos_b


def _route_rows(h, wr, br):
    tm = h.shape[0]
    logits = _dot(h, wr) + br
    lane = lax.broadcasted_iota(jnp.int32, (tm, LANES), 1)
    neg = -jnp.inf
    is_grp = jnp.logical_and(lane >= N_EXPERTS, lane < N_EXPERTS + N_GROUPS)
    lg = jnp.where(is_grp, logits, neg)
    mg = jnp.max(lg, axis=-1, keepdims=True)
    g_gate = 1.0 / jnp.sum(jnp.exp(lg - mg), axis=-1, keepdims=True)
    g_idx = jnp.min(jnp.where(lg == mg, lane - N_EXPERTS, N_GROUPS), axis=-1, keepdims=True)
    in_grp = jnp.logical_and(lane < N_EXPERTS, lane // EXPERTS_PER_GROUP == g_idx)
    le = jnp.where(in_grp, logits, neg)
    m1 = jnp.max(le, axis=-1, keepdims=True)
    i1 = jnp.min(jnp.where(le == m1, lane, LANES), axis=-1, keepdims=True)
    le2 = jnp.where(lane == i1, neg, le)
    m2 = jnp.max(le2, axis=-1, keepdims=True)
    i2 = jnp.min(jnp.where(le2 == m2, lane, LANES), axis=-1, keepdims=True)
    e2 = jnp.exp(m2 - m1)
    inv = 1.0 / (1.0 + e2)
    gate = jnp.where(lane == i1, g_gate * inv, jnp.where(lane == i2, g_gate * (e2 * inv), 0.0))
    return gate, jnp.where(jnp.logical_or(lane == i1, lane == i2), 1.0, 0.0)


def _sort_block(h, onehot):
    tm = MOE_BLOCK
    sel = onehot > 0.5
    r = lax.broadcasted_iota(jnp.int32, (tm, tm), 0)
    c = lax.broadcasted_iota(jnp.int32, (tm, tm), 1)
    rank = _dot(jnp.where(c < r, 1.0, 0.0).astype(BF16), onehot.astype(BF16))
    count = jnp.sum(onehot, axis=0, keepdims=True)
    padded = jnp.ceil(count * (1.0 / CHUNK)) * CHUNK
    er = lax.broadcasted_iota(jnp.int32, (LANES, LANES), 0)
    ec = lax.broadcasted_iota(jnp.int32, (LANES, LANES), 1)
    before = jnp.where(er < ec, 1.0, 0.0).astype(BF16)
    start = _dot(jnp.broadcast_to(padded, (SUBLANES, LANES)).astype(BF16), before)[0:1, :]
    pos = jnp.where(sel, start + rank, -1.0)
    pos_a, pos_b = _row_slots(pos, sel)
    slot = lax.broadcasted_iota(jnp.int32, (tm, BLOCK_ROWS), 1).astype(F32)
    pick = jnp.where(jnp.logical_or(slot == pos_a, slot == pos_b), 1.0, 0.0).astype(BF16)
    xb = lax.dot_general(pick, h, (((0,), (0,)), ((), ())), preferred_element_type=F32).astype(BF16)
    return xb, pos, count.astype(jnp.int32)


ROUTER_BLOCKS = 4


def _router_kernel(x_ref, g_ref, sh_ref, sc_ref, wr_ref, br_ref, xb_ref, pos_ref, gate_ref, cnt_ref):
    h = _modulate(x_ref[...], g_ref[...], sc_ref[0], sh_ref[0]).astype(BF16)
    gate, sel = _route_rows(h, wr_ref[...].astype(BF16), br_ref[...])
    gate_ref[...] = gate
    for k in range(ROUTER_BLOCKS):
        rows = slice(k * MOE_BLOCK, (k + 1) * MOE_BLOCK)
        xb, pos, count = _sort_block(h[rows, :], sel[rows, :])
        xb_ref[k * BLOCK_ROWS:(k + 1) * BLOCK_ROWS, :] = xb
        pos_ref[rows, :] = pos
        cnt_ref[k] = count


def _router(x, g, mods, layer, wr, br):
    tm = ROUTER_BLOCKS * MOE_BLOCK
    return pl.pallas_call(
        _router_kernel,
        grid=(N_TOK // tm,),
        in_specs=[
            pl.BlockSpec((tm, D_MODEL), lambda i: (i, 0)),
            pl.BlockSpec((1, D_MODEL), lambda i: (0, 0)),
            pl.BlockSpec((1, 1, D_MODEL), _mod_index(layer, 3, tm)),
            pl.BlockSpec((1, 1, D_MODEL), _mod_index(layer, 4, tm)),
            pl.BlockSpec((D_MODEL, LANES), lambda i: (0, 0)),
            pl.BlockSpec((1, LANES), lambda i: (0, 0)),
        ],
        out_specs=[
            pl.BlockSpec((ROUTER_BLOCKS * BLOCK_ROWS, D_MODEL), lambda i: (i, 0)),
            pl.BlockSpec((tm, LANES), lambda i: (i, 0)),
            pl.BlockSpec((tm, LANES), lambda i: (i, 0)),
            pl.BlockSpec((ROUTER_BLOCKS, 1, LANES), lambda i: (i, 0, 0)),
        ],
        out_shape=[
            jax.ShapeDtypeStruct((N_BLOCKS * BLOCK_ROWS, D_MODEL), BF16),
            jax.ShapeDtypeStruct((N_TOK, LANES), F32),
            jax.ShapeDtypeStruct((N_TOK, LANES), F32),
            jax.ShapeDtypeStruct((N_BLOCKS, 1, LANES), jnp.int32),
        ],
        compiler_params=_cparams(("arbitrary",)),
        name="moe_router",
    )(x, g.reshape(1, D_MODEL), mods, mods, wr, br)


TABLE_ROWS = -(-(N_TILES * TILE_CHUNKS) // LANES)


def _tables_kernel(cnt_ref, tend_ref, disp_ref, comb_ref):
    ne = N_EXPERTS
    cnt = jnp.concatenate([cnt_ref[...].astype(F32), jnp.zeros((LANES - N_BLOCKS, LANES), F32)], axis=0)
    m = jnp.ceil(cnt * (1.0 / CHUNK))
    r = lax.broadcasted_iota(jnp.int32, (LANES, LANES), 0)
    c = lax.broadcasted_iota(jnp.int32, (LANES, LANES), 1)
    mb = m.astype(BF16)
    bstart = _dot(mb, jnp.where(r < c, 1.0, 0.0).astype(BF16))
    cm = _dot(jnp.where(c < r, 1.0, 0.0).astype(BF16), mb)
    total = jnp.sum(m, axis=0, keepdims=True)
    tiles = jnp.ceil(total * (1.0 / TILE_CHUNKS))
    tile_end = _dot(jnp.broadcast_to(tiles, (SUBLANES, LANES)).astype(BF16),
                    jnp.where(r <= c, 1.0, 0.0).astype(BF16))[0:1, :]
    choff = (tile_end - tiles) * TILE_CHUNKS
    tend_ref[...] = tile_end.astype(jnp.int32)

    start_t = (choff + cm).T
    m_t = m.T
    bstart_t = bstart.T
    lane = lax.broadcasted_iota(jnp.int32, (ne, LANES), 1).astype(F32)
    acc = [jnp.zeros((ne, LANES), F32) for _ in range(TABLE_ROWS)]
    for b in range(N_BLOCKS):
        lo = start_t[0:ne, b:b + 1]
        hi = lo + m_t[0:ne, b:b + 1]
        bs = bstart_t[0:ne, b:b + 1]
        fwd = (b * BLOCK_CHUNKS) + bs - lo
        for j in range(TABLE_ROWS):
            g = lane + float(j * LANES)
            acc[j] = acc[j] + jnp.where(g >= lo, jnp.where(g < hi, fwd + g, 0.0), 0.0)
        back = jnp.where(lane >= bs, jnp.where(lane < hi - lo + bs, (lo - bs + lane) * CHUNK + 1.0, 0.0), 0.0)
        comb_ref[b:b + 1, :] = (jnp.sum(back, axis=0, keepdims=True) - 1.0).astype(jnp.int32)
    for j in range(TABLE_ROWS):
        disp_ref[j:j + 1, :] = (jnp.sum(acc[j], axis=0, keepdims=True) * CHUNK).astype(jnp.int32)


def _chunk_tables(counts):
    return pl.pallas_call(
        _tables_kernel,
        out_shape=[
            jax.ShapeDtypeStruct((1, LANES), jnp.int32),
            jax.ShapeDtypeStruct((TABLE_ROWS, LANES), jnp.int32),
            jax.ShapeDtypeStruct((N_BLOCKS, LANES), jnp.int32),
        ],
        name="moe_chunk_tables",
    )(counts.reshape(N_BLOCKS, LANES))


def _chunk_copies(read_row, n_chunks, src_hbm, dst, sem, wait, skip_negative=False):
    for c in range(n_chunks):
        row = read_row(c)

        def copy(row=row, c=c):
            cp = pltpu.make_async_copy(src_hbm.at[pl.ds(pl.multiple_of(row, CHUNK), CHUNK)],
                                       dst.at[pl.ds(c * CHUNK, CHUNK)], sem)
            if wait:
                cp.wait()
            else:
                cp.start()

        if skip_negative:
            pl.when(row >= 0)(copy)
        else:
            copy()


IN_DEPTH = 3


def _expert_kernel(tend_ref, src_ref, xb_hbm, w1_ref, w3_ref, w2_ref, y_hbm,
                   xbuf, ybuf, zbuf, w13_scr, w2_scr, in_sem, out_sem, zero_sem):
    e = pl.program_id(0)
    first = jnp.where(e == 0, 0, tend_ref[0, jnp.maximum(e - 1, 0)])
    last = tend_ref[0, e]
    n_used = tend_ref[0, N_EXPERTS - 1]
    tiles_per_row = LANES // TILE_CHUNKS

    def fetch(tile, s, wait):
        row = tile // tiles_per_row
        col = (tile % tiles_per_row) * TILE_CHUNKS
        _chunk_copies(lambda c: src_ref[row, col + c], TILE_CHUNKS, xb_hbm, xbuf.at[s], in_sem.at[s], wait)

    def store(tile, s):
        rows = pl.ds(pl.multiple_of(tile * EXPERT_TILE, EXPERT_TILE), EXPERT_TILE)
        return pltpu.make_async_copy(ybuf.at[s], y_hbm.at[rows], out_sem.at[s])

    def zero_fill(tile):
        rows = pl.ds(pl.multiple_of(tile * EXPERT_TILE, EXPERT_TILE), EXPERT_TILE)
        return pltpu.make_async_copy(zbuf, y_hbm.at[rows], zero_sem.at[0])

    @pl.when(e == 0)
    def _():
        zbuf[...] = jnp.zeros_like(zbuf)

        def start(t, carry):
            zero_fill(t).start()
            return carry

        lax.fori_loop(n_used, N_TILES, start, 0)

    lead = IN_DEPTH - 1

    @pl.when(last > first)
    def _():
        @pl.when(first == 0)
        def _():
            for k in range(lead):
                fetch(jnp.minimum(k, n_used - 1), k, False)

        w13_scr[:, 0:D_EXPERT] = w1_ref[0, 0].astype(BF16)
        w13_scr[:, D_EXPERT:2 * D_EXPERT] = w3_ref[0, 0].astype(BF16)
        w2_scr[...] = w2_ref[0, 0].astype(BF16)

    def tile_body(t, carry):
        s = t % 2
        fetch(t, t % IN_DEPTH, True)
        ab = _dot(xbuf[t % IN_DEPTH], w13_scr[...])
        fetch(jnp.minimum(t + lead, n_used - 1), (t + lead) % IN_DEPTH, False)
        a = ab[:, 0:D_EXPERT]
        hid = (a * jax.nn.sigmoid(a)) * ab[:, D_EXPERT:2 * D_EXPERT]
        y = _dot(hid.astype(BF16), w2_scr[...]).astype(BF16)

        @pl.when(t >= 2)
        def _():
            store(t, s).wait()

        ybuf[s] = y
        store(t, s).start()
        return carry

    lax.fori_loop(first, last, tile_body, 0)

    @pl.when(e == N_EXPERTS - 1)
    def _():
        for k in range(lead):
            fetch(n_used - 1, (n_used + k) % IN_DEPTH, True)

        @pl.when(n_used >= 1)
        def _():
            store(0, (n_used - 1) % 2).wait()

        @pl.when(n_used >= 2)
        def _():
            store(0, n_used % 2).wait()

        def drain(t, carry):
            zero_fill(t).wait()
            return carry

        lax.fori_loop(n_used, N_TILES, drain, 0)


def _experts(tile_end, disp_src, xb, w1, w3, w2, layer):
    wsel = lambda e, tend, src: (layer, e, 0, 0)
    return pl.pallas_call(
        _expert_kernel,
        grid_spec=pltpu.PrefetchScalarGridSpec(
            num_scalar_prefetch=2,
            grid=(N_EXPERTS,),
            in_specs=[
                pl.BlockSpec(memory_space=pl.ANY),
                pl.BlockSpec((1, 1, D_MODEL, D_EXPERT), wsel),
                pl.BlockSpec((1, 1, D_MODEL, D_EXPERT), wsel),
                pl.BlockSpec((1, 1, D_EXPERT, D_MODEL), wsel),
            ],
            out_specs=pl.BlockSpec(memory_space=pl.ANY),
            scratch_shapes=[pltpu.VMEM((IN_DEPTH, EXPERT_TILE, D_MODEL), BF16),
                            pltpu.VMEM((2, EXPERT_TILE, D_MODEL), BF16),
                            pltpu.VMEM((EXPERT_TILE, D_MODEL), BF16),
                            pltpu.VMEM((D_MODEL, 2 * D_EXPERT), BF16),
                            pltpu.VMEM((D_EXPERT, D_MODEL), BF16),
                            pltpu.SemaphoreType.DMA((IN_DEPTH,)),
                            pltpu.SemaphoreType.DMA((2,)),
                            pltpu.SemaphoreType.DMA((1,))],
        ),
        out_shape=jax.ShapeDtypeStruct((N_TILES * EXPERT_TILE, D_MODEL), BF16),
        compiler_params=_cparams(("arbitrary",)),
        name="moe_experts",
    )(tile_end, disp_src, xb, w1, w3, w2)


def _combine_kernel(src_ref, y_hbm, x_ref, pos_ref, gate_ref, mg_ref, *rest, tm, final):
    if final:
        fg_ref, op_ref, os_ref, ybuf, sem = rest
    else:
        o_ref, ybuf, sem = rest
    i = pl.program_id(0)
    slot = i % 2

    def fetch(blk, s, wait):
        _chunk_copies(lambda c: src_ref[blk, c], BLOCK_CHUNKS, y_hbm, ybuf.at[s], sem.at[s], wait,
                      skip_negative=True)

    @pl.when(i == 0)
    def _():
        ybuf[...] = jnp.zeros_like(ybuf)
        fetch(0, 0, False)

    @pl.when(i + 1 < N_BLOCKS)
    def _():
        fetch(i + 1, 1 - slot, False)

    fetch(i, slot, True)
    pos = pos_ref[...]
    sel = pos >= 0.0
    gate = gate_ref[...]
    pos_a, pos_b = _row_slots(pos, sel)
    gate_a = jnp.sum(jnp.where(pos == pos_a, gate, 0.0), axis=-1, keepdims=True)
    gate_b = jnp.sum(jnp.where(jnp.logical_and(sel, pos == pos_b), gate, 0.0), axis=-1, keepdims=True)
    row = lax.broadcasted_iota(jnp.int32, (tm, BLOCK_ROWS), 1).astype(F32)
    weights = jnp.where(row == pos_a, gate_a, jnp.where(row == pos_b, gate_b, 0.0))
    moe = _dot(weights.astype(BF16), ybuf[slot])
    out = x_ref[...] + mg_ref[0] * moe
    if not final:
        o_ref[...] = out
        return

    ms = jnp.mean(out * out, axis=-1, keepdims=True)
    out = (out * lax.rsqrt(ms + EPS)) * fg_ref[...]

    @pl.when(i < N_PROMPT // tm)
    def _():
        op_ref[...] = out

    @pl.when(i >= N_PROMPT // tm)
    def _():
        os_ref[...] = out


def _combine(comb_src, y, x, pos, gate, mods, layer, final_g=None):
    tm = MOE_BLOCK
    final = final_g is not None
    np_blocks = N_PROMPT // tm

    def gate_idx(i, s):
        return ((layer * MOD_ROWS + _mod_row(i, tm)) * 6 + 5, 0, 0)

    in_specs = [
        pl.BlockSpec(memory_space=pl.ANY),
        pl.BlockSpec((tm, D_MODEL), lambda i, s: (i, 0)),
        pl.BlockSpec((tm, LANES), lambda i, s: (i, 0)),
        pl.BlockSpec((tm, LANES), lambda i, s: (i, 0)),
        pl.BlockSpec((1, 1, D_MODEL), gate_idx),
    ]
    args = [comb_src, y, x, pos, gate, mods]
    if final:
        in_specs.append(pl.BlockSpec((1, D_MODEL), lambda i, s: (0, 0)))
        args.append(final_g.reshape(1, D_MODEL))
        out_specs = [pl.BlockSpec((tm, D_MODEL), lambda i, s: (jnp.minimum(i, np_blocks - 1), 0)),
                     pl.BlockSpec((tm, D_MODEL), lambda i, s: (jnp.maximum(i - np_blocks, 0), 0))]
        out_shape = [jax.ShapeDtypeStruct((N_PROMPT, D_MODEL), F32),
                     jax.ShapeDtypeStruct((N_SAMPLE, D_MODEL), F32)]
    else:
        out_specs = pl.BlockSpec((tm, D_MODEL), lambda i, s: (i, 0))
        out_shape = jax.ShapeDtypeStruct((N_TOK, D_MODEL), F32)
    return pl.pallas_call(
        functools.partial(_combine_kernel, tm=tm, final=final),
        grid_spec=pltpu.PrefetchScalarGridSpec(
            num_scalar_prefetch=1,
            grid=(N_BLOCKS,),
            in_specs=in_specs,
            out_specs=out_specs,
            scratch_shapes=[pltpu.VMEM((2, BLOCK_ROWS, D_MODEL), BF16),
                            pltpu.SemaphoreType.DMA((2,))],
        ),
        out_shape=out_shape,
        compiler_params=_cparams(("arbitrary",)),
        name="moe_combine_final" if final else "moe_combine",
    )(*args)


def _moe(x, g, mods, layer, w_group, b_group, w_expert, b_expert, w1, w3, w2, final_g=None):
    pad = LANES - N_EXPERTS - N_GROUPS
    wr = jnp.concatenate([w_expert, w_group, jnp.zeros((D_MODEL, pad), F32)], axis=1)
    br = jnp.concatenate([b_expert, b_group, jnp.zeros((pad,), F32)]).reshape(1, LANES)
    xb, pos, gate, counts = _router(x, g, mods, layer, wr, br)
    tile_end, disp_src, comb_src = _chunk_tables(counts)
    y = _experts(tile_end, disp_src, xb, w1, w3, w2, layer)
    return _combine(comb_src, y, x, pos, gate, mods, layer, final_g)


def kernel(x_prompt, x_sample, cache_k, cache_v, c, c_ctx, ada_w, ada_b, norm1_g, norm2_g, final_g, hy_w_in, hy_conv_w, hy_conv_b, hy_f_w1, hy_f_b1, hy_f_w2, hy_f_b2, hy_f_w3, hy_f_b3, hy_f_freq, hy_bias, hy_w_out, at_w_qkv, at_lam_q1, at_lam_k1, at_lam_q2, at_lam_k2, at_subln_g, at_w_o, moe_w_group, moe_b_group, moe_w_expert, moe_b_expert, moe_w1, moe_w3, moe_w2):
    x = (x_prompt.reshape(N_PROMPT, D_MODEL), x_sample.reshape(N_SAMPLE, D_MODEL))
    cond = jnp.concatenate([c_ctx[None, :], c, jnp.zeros((MOD_ROWS - 1 - DEC_BATCH, D_MODEL), F32)], axis=0)
    mods = _ada_table(cond, ada_w, ada_b).reshape(DEPTH * MOD_ROWS * 6, 1, D_MODEL)
    ctx_k = cache_k.reshape(DEC_BATCH, DEPTH // 2, PAST_LEN, D_MODEL)
    ctx_v = cache_v.reshape(DEC_BATCH, DEPTH // 2, PAST_LEN, D_MODEL)

    dft = {}
    for L in (SEQ, DEC_SEQ):
        fwd, inv = _dft_mats(L)
        dft[L] = (jnp.asarray(fwd).astype(BF16), jnp.asarray(inv).astype(BF16))

    new_k, new_v = [], []
    for l in range(DEPTH):
        j = l // 2
        if l % 2 == 0:
            z = _norm_matmul(x, norm1_g[l], mods, l, 0, 1, hy_w_in, j)
            ys = []
            for L, rb0, n_seq, dt in ((SEQ, 0, BATCH, 1024), (DEC_SEQ, N_PROMPT // DEC_SEQ, DEC_BATCH, 256)):
                f_bf, finv = dft[L]
                kc = _filter_spectra(L, hy_f_w1[j], hy_f_b1[j], hy_f_w2[j], hy_f_b2[j], hy_f_w3[j],
                                     hy_f_b3[j], hy_f_freq[j], f_bf)
                ys.append(_hyena_conv(z, hy_conv_w[j], hy_conv_b[j], f_bf, finv, kc, hy_bias[j],
                                      L=L, row_block0=rb0, n_seq=n_seq, dt=dt))
            x = _matmul_residual((ys[0], ys[1]), hy_w_out, j, x, mods, l, 2)
        else:
            lam_init = 0.8 - 0.6 * math.exp(-0.3 * l)
            qkv = _norm_matmul(x, norm1_g[l], mods, l, 0, 1, at_w_qkv, j, rope=True)
            lam_params = (at_lam_q1[j], at_lam_k1[j], at_lam_q2[j], at_lam_k2[j])
            op = _attention(qkv, lam_params, at_subln_g[j], lam_init, L=SEQ, row_block0=0,
                            n_seq=BATCH, tq=SEQ)
            os_ = _attention(qkv, lam_params, at_subln_g[j], lam_init, L=DEC_SEQ,
                             row_block0=N_PROMPT // DEC_SEQ, n_seq=DEC_BATCH, tq=256,
                             ctx=(ctx_k, ctx_v), j=j)
            new_k.append(qkv[:N_PROMPT, D_MODEL:2 * D_MODEL].astype(F32).reshape(BATCH, SEQ, N_HEADS, HEAD_W))
            new_v.append(qkv[:N_PROMPT, 2 * D_MODEL:].astype(F32).reshape(BATCH, SEQ, N_HEADS, HEAD_W))
            x = _matmul_residual((op, os_), at_w_o, j, x, mods, l, 2)
        x = _moe(x, norm2_g[l], mods, l, moe_w_group[l], moe_b_group[l], moe_w_expert[l],
                 moe_b_expert[l], moe_w1, moe_w3, moe_w2, final_g if l == DEPTH - 1 else None)
    y_prompt = x[0].reshape(BATCH, SEQ, D_MODEL)
    y_sample = x[1].reshape(DEC_BATCH, DEC_SEQ, D_MODEL)
    return (y_prompt, y_sample, jnp.stack(new_k, axis=1), jnp.stack(new_v, axis=1))
```

```python
import functools
import math

import numpy as np
import jax
import jax.numpy as jnp
from jax import lax
from jax.experimental import pallas as pl
from jax.experimental.pallas import tpu as pltpu

F32 = jnp.float32
BF16 = jnp.bfloat16

D_MODEL = 1024
BATCH = 16
SEQ = 256
DEPTH = 4
DEC_BATCH = 4
DEC_SEQ = 1024
PAST_LEN = 256
GRID_W = 64
EPS = 1e-6
HY_BANDS = 16
HY_EMB = 1 + 2 * HY_BANDS
HY_FILTER_HIDDEN = 64
HY_FAST_DECAY = 0.3
HY_SLOW_DECAY = 1.5
HY_TARGET = 1e-2
N_HEADS = 8
HEAD_DIM = 64
ROT_AXIS = HEAD_DIM // 2
ROPE_BASE = 10000.0
N_GROUPS = 4
EXPERTS_PER_GROUP = 8
N_EXPERTS = N_GROUPS * EXPERTS_PER_GROUP
D_EXPERT = D_MODEL // 4

N_PROMPT = BATCH * SEQ
N_SAMPLE = DEC_BATCH * DEC_SEQ
N_TOK = N_PROMPT + N_SAMPLE
MOD_ROWS = 8
LANES = 128
SUBLANES = 8
HEAD_W = 2 * HEAD_DIM
VMEM_LIMIT = 56 * 1024 * 1024

MOE_BLOCK = 256
N_BLOCKS = N_TOK // MOE_BLOCK
CHUNK = 2 * SUBLANES
BLOCK_ROWS = 2 * MOE_BLOCK + N_EXPERTS * CHUNK
BLOCK_CHUNKS = BLOCK_ROWS // CHUNK
EXPERT_TILE = 256
TILE_CHUNKS = EXPERT_TILE // CHUNK
N_TILES = (2 * N_TOK + N_BLOCKS * N_EXPERTS * (CHUNK - 1)) // EXPERT_TILE + N_EXPERTS


def _cparams(sem):
    return pltpu.CompilerParams(dimension_semantics=sem, vmem_limit_bytes=VMEM_LIMIT)


def _mod_row(blk, tm):
    start = blk * tm
    return jnp.where(start < N_PROMPT, 0, 1 + (start - N_PROMPT) // DEC_SEQ)


def _mod_index(layer, which, tm):
    def index_map(i, *_):
        return ((layer * MOD_ROWS + _mod_row(i, tm)) * 6 + which, 0, 0)
    return index_map


@functools.lru_cache(maxsize=None)
def _dft_mats(L):
    k = np.arange(L, dtype=np.float64)[:, None]
    j = np.arange(L, dtype=np.float64)[None, :]
    ang = np.pi * ((k * j) % (2 * L)) / L
    c = np.cos(ang)
    s = np.sin(ang)
    s[0, :] = np.where(np.arange(L) % 2 == 0, 1.0, -1.0)
    fwd = np.concatenate([c, s], axis=0)
    scale = np.full((2 * L,), 1.0 / L)
    scale[0] = scale[L] = 0.5 / L
    inv = fwd.T * scale[None, :]
    return fwd.astype(np.float32), inv.astype(np.float32)


@functools.lru_cache(maxsize=None)
def _filter_feats(L):
    pos = np.arange(L, dtype=np.float64)
    t = pos / (L - 1)
    bands = np.linspace(1e-4, HY_BANDS - 1, HY_BANDS)
    ang = (2.0 * math.pi / L) * pos[:, None] * bands[None, :]
    feats = np.concatenate([t[:, None], np.cos(ang), -np.sin(ang)], axis=-1)
    feats = np.pad(feats, ((0, 0), (0, LANES - HY_EMB)))
    deltas = np.abs(np.linspace(math.log(HY_TARGET) / HY_SLOW_DECAY,
                                math.log(HY_TARGET) / HY_FAST_DECAY, D_MODEL))
    window = np.exp(-t[:, None] * deltas[None, :])
    alt = np.where(np.arange(L) % 2 == 0, 1.0, -1.0)[:, None] * np.ones((1, LANES))
    return feats.astype(np.float32), window.astype(np.float32), alt.astype(np.float32)


@functools.lru_cache(maxsize=None)
def _rope_tables():
    pos = np.arange(DEC_SEQ)
    row = (pos // GRID_W).astype(np.float64)
    col = (pos % GRID_W).astype(np.float64)
    lane = np.arange(HEAD_W)
    d = lane % HEAD_DIM
    axis = d // ROT_AXIS
    n = d % ROT_AXIS
    half = n // (ROT_AXIS // 2)
    f = n % (ROT_AXIS // 2)
    inv = ROPE_BASE ** (-f.astype(np.float64) / (ROT_AXIS // 2))
    p = np.where(axis[None, :] == 0, row[:, None], col[:, None])
    ang = p * inv[None, :]
    sign = np.where(half == 0, -1.0, 1.0)[None, :]
    return np.cos(ang).astype(np.float32), (np.sin(ang) * sign).astype(np.float32)


def _split_bf16(a):
    hi = a.astype(BF16)
    lo = (a - hi.astype(F32)).astype(BF16)
    return hi, lo


def _dot(a, b):
    return jnp.dot(a, b, preferred_element_type=F32)


def _dot3(a_hi, a_lo, b_hi, b_lo):
    return _dot(a_hi, b_hi) + (_dot(a_hi, b_lo) + _dot(a_lo, b_hi))


def _ada_kernel(c_ref, w_ref, b_ref, o_ref):
    c = c_ref[...]
    s = c * jax.nn.sigmoid(c)
    o_ref[0] = _dot(s.astype(BF16), w_ref[0].astype(BF16)) + b_ref[0]


def _ada_table(cond, ada_w, ada_b):
    tn = 1536
    n = 6 * D_MODEL
    return pl.pallas_call(
        _ada_kernel,
        grid=(DEPTH, n // tn),
        in_specs=[
            pl.BlockSpec((MOD_ROWS, D_MODEL), lambda l, j: (0, 0)),
            pl.BlockSpec((1, D_MODEL, tn), lambda l, j: (l, 0, j)),
            pl.BlockSpec((1, 1, tn), lambda l, j: (l, 0, j)),
        ],
        out_specs=pl.BlockSpec((1, MOD_ROWS, tn), lambda l, j: (l, 0, j)),
        out_shape=jax.ShapeDtypeStruct((DEPTH, MOD_ROWS, n), F32),
        compiler_params=_cparams(("arbitrary", "arbitrary")),
        name="ada_table",
    )(cond, ada_w, ada_b.reshape(DEPTH, 1, n))


def _modulate(x, g, scale, shift):
    ms = jnp.mean(x * x, axis=-1, keepdims=True)
    y = (x * lax.rsqrt(ms + EPS)) * g
    return y * (1.0 + scale) + shift


MM_COLS = 512


def _row_group_specs(x, tm, width):
    np_blocks = N_PROMPT // tm
    if isinstance(x, tuple):
        specs = [pl.BlockSpec((tm, width), lambda i, *_: (jnp.minimum(i, np_blocks - 1), 0)),
                 pl.BlockSpec((tm, width), lambda i, *_: (jnp.maximum(i - np_blocks, 0), 0))]
        return specs, list(x)
    specs = [pl.BlockSpec((tm, width), lambda i, *_: (jnp.minimum(i, np_blocks - 1), 0)),
             pl.BlockSpec((tm, width), lambda i, *_: (jnp.maximum(i, np_blocks), 0))]
    return specs, [x, x]


def _pick_rows(i, tm, p_ref, s_ref):
    return jnp.where(i < N_PROMPT // tm, p_ref[...], s_ref[...])


def _normmm_kernel(xp_ref, xs_ref, g_ref, sh_ref, sc_ref, w_ref, o_ref, w_scr, *, tm):
    i = pl.program_id(0)

    @pl.when(i == 0)
    def _():
        w_scr[...] = w_ref[0].astype(BF16)

    x = _pick_rows(i, tm, xp_ref, xs_ref)
    h = _modulate(x, g_ref[...], sc_ref[0], sh_ref[0]).astype(BF16)
    for j in range(w_scr.shape[1] // MM_COLS):
        cols = slice(j * MM_COLS, (j + 1) * MM_COLS)
        o_ref[:, cols] = _dot(h, w_scr[:, cols]).astype(o_ref.dtype)


def _norm_matmul(x, g, mods, layer, which_shift, which_scale, w, widx, *, tm=512):
    n = w.shape[2]
    x_specs, x_args = _row_group_specs(x, tm, D_MODEL)
    return pl.pallas_call(
        functools.partial(_normmm_kernel, tm=tm),
        grid=(N_TOK // tm,),
        in_specs=x_specs + [
            pl.BlockSpec((1, D_MODEL), lambda i: (0, 0)),
            pl.BlockSpec((1, 1, D_MODEL), _mod_index(layer, which_shift, tm)),
            pl.BlockSpec((1, 1, D_MODEL), _mod_index(layer, which_scale, tm)),
            pl.BlockSpec((1, D_MODEL, n), lambda i: (widx, 0, 0), pipeline_mode=pl.Buffered(1)),
        ],
        out_specs=pl.BlockSpec((tm, n), lambda i: (i, 0)),
        out_shape=jax.ShapeDtypeStruct((N_TOK, n), BF16),
        scratch_shapes=[pltpu.VMEM((D_MODEL, n), BF16)],
        compiler_params=_cparams(("arbitrary",)),
        name="norm_matmul",
    )(*x_args, g.reshape(1, D_MODEL), mods, mods, w)


def _mmres_kernel(ap_ref, as_ref, xp_ref, xs_ref, w_ref, gate_ref, o_ref, w_scr, *, tm):
    i = pl.program_id(0)

    @pl.when(i == 0)
    def _():
        w_scr[...] = w_ref[0].astype(BF16)

    a = _pick_rows(i, tm, ap_ref, as_ref)
    x = _pick_rows(i, tm, xp_ref, xs_ref)
    gate = gate_ref[0]
    for j in range(w_scr.shape[1] // MM_COLS):
        cols = slice(j * MM_COLS, (j + 1) * MM_COLS)
        o_ref[:, cols] = x[:, cols] + gate[:, cols] * _dot(a, w_scr[:, cols])


def _matmul_residual(a, w, widx, x, mods, layer, which_gate, *, tm=512):
    n = w.shape[2]
    a_specs, a_args = _row_group_specs(a, tm, D_MODEL)
    x_specs, x_args = _row_group_specs(x, tm, n)
    return pl.pallas_call(
        functools.partial(_mmres_kernel, tm=tm),
        grid=(N_TOK // tm,),
        in_specs=a_specs + x_specs + [
            pl.BlockSpec((1, D_MODEL, n), lambda i: (widx, 0, 0), pipeline_mode=pl.Buffered(1)),
            pl.BlockSpec((1, 1, n), _mod_index(layer, which_gate, tm)),
        ],
        out_specs=pl.BlockSpec((tm, n), lambda i: (i, 0)),
        out_shape=jax.ShapeDtypeStruct((N_TOK, n), F32),
        scratch_shapes=[pltpu.VMEM((D_MODEL, n), BF16)],
        compiler_params=_cparams(("arbitrary",)),
        name="matmul_residual",
    )(*a_args, *x_args, w, mods)


def _filter_kernel(feats_ref, win_ref, alt_ref, w1_ref, b1_ref, w2_ref, b2_ref, fq_ref,
                   w3f_ref, b3f_ref, w3b_ref, b3b_ref, f_ref, o_ref, h_scr, *, L):
    def dense(a, w_ref, b_ref):
        a_hi, a_lo = _split_bf16(a)
        w_hi, w_lo = _split_bf16(w_ref[...])
        return _dot3(a_hi, a_lo, w_hi, w_lo) + b_ref[...]

    @pl.when(jnp.logical_and(pl.program_id(0) == 0, pl.program_id(1) == 0))
    def _():
        fq = fq_ref[...]
        h1 = jnp.sin(fq[0:1, :] * dense(feats_ref[...], w1_ref, b1_ref))
        h_scr[...] = jnp.sin(fq[1:2, :] * dense(h1, w2_ref, b2_ref))

    h = h_scr[...]
    win = win_ref[...]
    hf = dense(h, w3f_ref, b3f_ref) * win
    hb = dense(h, w3b_ref, b3b_ref) * win
    row = lax.broadcasted_iota(jnp.int32, hf.shape, 0)
    hb = jnp.where(row == 0, 0.0, hb)
    kr = _dot(f_ref[0:L, :], (hf + hb).astype(BF16))
    ks = _dot(f_ref[L:2 * L, :], (hf - hb).astype(BF16))
    k_nyq = jnp.sum((hf + hb) * alt_ref[:, 0:1], axis=0, keepdims=True)
    o_ref[0, 0] = kr
    o_ref[0, 1] = jnp.where(row == 0, k_nyq, kr)
    o_ref[0, 2] = jnp.where(row == 0, 0.0, ks)


def _filter_spectra(L, w1, b1, w2, b2, w3, b3, freq, f_bf, *, dt=256):
    feats, window, alt = _filter_feats(L)
    fh = HY_FILTER_HIDDEN
    nd = D_MODEL // dt
    w1p = jnp.pad(w1, ((0, LANES - HY_EMB), (0, 0)))
    const = lambda shape: pl.BlockSpec(shape, lambda o, c: tuple(0 for _ in shape))
    return pl.pallas_call(
        functools.partial(_filter_kernel, L=L),
        grid=(2, nd),
        in_specs=[
            const((L, LANES)),
            pl.BlockSpec((L, dt), lambda o, c: (0, c)),
            const((L, LANES)),
            const((LANES, fh)), const((1, fh)), const((fh, fh)), const((1, fh)), const((2, fh)),
            pl.BlockSpec((fh, dt), lambda o, c: (0, (2 * o) * nd + c)),
            pl.BlockSpec((1, dt), lambda o, c: (0, (2 * o) * nd + c)),
            pl.BlockSpec((fh, dt), lambda o, c: (0, (2 * o + 1) * nd + c)),
            pl.BlockSpec((1, dt), lambda o, c: (0, (2 * o + 1) * nd + c)),
            const((2 * L, L)),
        ],
        out_specs=pl.BlockSpec((1, 3, L, dt), lambda o, c: (o, 0, 0, c)),
        out_shape=jax.ShapeDtypeStruct((2, 3, L, D_MODEL), F32),
        scratch_shapes=[pltpu.VMEM((L, fh), F32)],
        compiler_params=_cparams(("arbitrary", "arbitrary")),
        name=f"hyena_filter_spectra_{L}",
    )(jnp.asarray(feats), jnp.asarray(window), jnp.asarray(alt), w1p, b1.reshape(1, fh), w2,
      b2.reshape(1, fh), freq, w3, b3.reshape(1, -1), w3, b3.reshape(1, -1), f_bf)


def _conv_kernel(v_ref, x1_ref, x2_ref, cwv_ref, cw1_ref, cw2_ref, cbv_ref, cb1_ref, cb2_ref,
                 f_ref, finv_ref, kc_ref, bias_ref, o_ref, *, L):
    row = lax.broadcasted_iota(jnp.int32, v_ref.shape, 0)

    def short_conv(u_ref, w_ref, b_ref):
        u = u_ref[...].astype(F32)
        prev = jnp.where(row == 0, 0.0, pltpu.roll(u, 1, 0))
        nxt = jnp.where(row == L - 1, 0.0, pltpu.roll(u, L - 1, 0))
        w = w_ref[...]
        return prev * w[0:1, :] + u * w[1:2, :] + nxt * w[2:3, :] + b_ref[...]

    def long_conv(u, order):
        spec = _dot(f_ref[...], u.astype(BF16))
        a = spec[0:L, :]
        b = spec[L:2 * L, :]
        kra = kc_ref[order, 0]
        krb = kc_ref[order, 1]
        ks = kc_ref[order, 2]
        prod = jnp.concatenate([a * kra - b * ks, a * ks + b * krb], axis=0)
        return _dot(finv_ref[...], prod.astype(BF16)) + u * bias_ref[order:order + 1, :]

    v = short_conv(v_ref, cwv_ref, cbv_ref)
    y = short_conv(x1_ref, cw1_ref, cb1_ref) * long_conv(v, 0)
    y = short_conv(x2_ref, cw2_ref, cb2_ref) * long_conv(y, 1)
    o_ref[...] = y.astype(o_ref.dtype)


def _hyena_conv(z, conv_w, conv_b, f_bf, finv_bf, kc, bias, *, L, row_block0, n_seq, dt):
    nd = D_MODEL // dt
    cb = conv_b.reshape(1, 3 * D_MODEL)
    seg = lambda s: pl.BlockSpec((L, dt), lambda c, b: (row_block0 + b, s * nd + c))
    cw = lambda s: pl.BlockSpec((3, dt), lambda c, b: (0, s * nd + c))
    cbs = lambda s: pl.BlockSpec((1, dt), lambda c, b: (0, s * nd + c))
    return pl.pallas_call(
        functools.partial(_conv_kernel, L=L),
        grid=(nd, n_seq),
        in_specs=[
            seg(0), seg(1), seg(2), cw(0), cw(1), cw(2), cbs(0), cbs(1), cbs(2),
            pl.BlockSpec((2 * L, L), lambda c, b: (0, 0), pipeline_mode=pl.Buffered(1)),
            pl.BlockSpec((L, 2 * L), lambda c, b: (0, 0), pipeline_mode=pl.Buffered(1)),
            pl.BlockSpec((2, 3, L, dt), lambda c, b: (0, 0, 0, c), pipeline_mode=pl.Buffered(1)),
            pl.BlockSpec((2, dt), lambda c, b: (0, c)),
        ],
        out_specs=pl.BlockSpec((L, dt), lambda c, b: (b, c)),
        out_shape=jax.ShapeDtypeStruct((n_seq * L, D_MODEL), BF16),
        compiler_params=_cparams(("arbitrary", "arbitrary")),
        name=f"hyena_conv_{L}",
    )(z, z, z, conv_w, conv_w, conv_w, cb, cb, cb, f_bf, finv_bf, kc, bias)


def _attn_kernel(lq1_ref, lk1_ref, lq2_ref, lk2_ref, g_ref, q_ref, k_ref, v_ref, *rest,
                 lam_init, has_ctx, tq, nq):
    if has_ctx:
        ck_ref, cv_ref, cos_ref, sin_ref, o_ref, k_scr = rest
    else:
        (o_ref,) = rest
    lam = (jnp.exp(jnp.sum(lq1_ref[...] * lk1_ref[...], axis=-1, keepdims=True))
           - jnp.exp(jnp.sum(lq2_ref[...] * lk2_ref[...], axis=-1, keepdims=True)) + lam_init)
    lane = lax.broadcasted_iota(jnp.int32, (tq, HEAD_W), 1)
    first_map = lane < HEAD_DIM
    first_half = (lane % ROT_AXIS) < (ROT_AXIS // 2)
    nt = (((1,), (1,)), ((), ()))

    def rotary(x, rows):
        x = x.astype(F32)
        partner = jnp.where(first_half, pltpu.roll(x, HEAD_W - ROT_AXIS // 2, 1),
                            pltpu.roll(x, ROT_AXIS // 2, 1))
        return (x * cos_ref[rows, :] + partner * sin_ref[rows, :]).astype(BF16)

    if has_ctx:
        for r in range(k_ref.shape[0] // tq):
            rows = slice(r * tq, (r + 1) * tq)
            for h in range(N_HEADS):
                cols = slice(h * HEAD_W, (h + 1) * HEAD_W)
                k_scr[rows, cols] = rotary(k_ref[rows, cols], rows)
        keys = k_scr
    else:
        keys = k_ref

    def q_block(rows):
        for h in range(N_HEADS):
            cols = slice(h * HEAD_W, (h + 1) * HEAD_W)
            q = q_ref[rows, cols]
            if has_ctx:
                q = rotary(q, rows)
            q = q * (HEAD_DIM ** -0.5)
            zero = jnp.zeros_like(q)
            q2 = jnp.concatenate([jnp.where(first_map, q, zero), jnp.where(first_map, zero, q)], axis=0)
            s = lax.dot_general(q2, keys[:, cols], nt, preferred_element_type=F32)
            m = jnp.max(s, axis=-1, keepdims=True)
            if has_ctx:
                sc = lax.dot_general(q2, ck_ref[0, 0, :, cols].astype(BF16), nt,
                                     preferred_element_type=F32)
                m = jnp.maximum(m, jnp.max(sc, axis=-1, keepdims=True))
                ec = jnp.exp(sc - m)
            e = jnp.exp(s - m)
            den = jnp.sum(e, axis=-1, keepdims=True)
            if has_ctx:
                den = den + jnp.sum(ec, axis=-1, keepdims=True)
            ov = _dot(e.astype(BF16), v_ref[:, cols])
            if has_ctx:
                ov = ov + _dot(ec.astype(BF16), cv_ref[0, 0, :, cols].astype(BF16))
            inv = 1.0 / den
            o = ov[0:tq] * inv[0:tq] - ov[tq:2 * tq] * (lam * inv[tq:2 * tq])
            ms = jnp.mean(o * o, axis=-1, keepdims=True)
            o = (o * lax.rsqrt(ms + EPS)) * g_ref[...] * (1.0 - lam_init)
            o_ref[rows, cols] = o.astype(o_ref.dtype)

    if nq == 1:
        q_block(slice(0, tq))
    else:
        def body(qi, carry):
            q_block(pl.ds(pl.multiple_of(qi * tq, tq), tq))
            return carry
        lax.fori_loop(0, nq, body, 0)


def _attention(qkv, lam_params, subln_g, lam_init, *, L, row_block0, n_seq, tq, ctx=None, j=0):
    small = pl.BlockSpec((1, HEAD_DIM), lambda b: (0, 0))
    in_specs = [small, small, small, small,
                pl.BlockSpec((1, HEAD_W), lambda b: (0, 0)),
                pl.BlockSpec((L, D_MODEL), lambda b: (row_block0 + b, 0)),
                pl.BlockSpec((L, D_MODEL), lambda b: (row_block0 + b, 1)),
                pl.BlockSpec((L, D_MODEL), lambda b: (row_block0 + b, 2))]
    args = [p.reshape(1, HEAD_DIM) for p in lam_params] + [subln_g.reshape(1, HEAD_W), qkv, qkv, qkv]
    scratch = []
    if ctx is not None:
        ctx_spec = pl.BlockSpec((1, 1, PAST_LEN, D_MODEL), lambda b: (b, j, 0, 0))
        rope_spec = pl.BlockSpec((L, HEAD_W), lambda b: (0, 0))
        cos, sin = _rope_tables()
        in_specs += [ctx_spec, ctx_spec, rope_spec, rope_spec]
        args += list(ctx) + [jnp.asarray(cos), jnp.asarray(sin)]
        scratch = [pltpu.VMEM((L, D_MODEL), BF16)]
    return pl.pallas_call(
        functools.partial(_attn_kernel, lam_init=lam_init, has_ctx=ctx is not None, tq=tq, nq=L // tq),
        grid=(n_seq,),
        in_specs=in_specs,
        out_specs=pl.BlockSpec((L, D_MODEL), lambda b: (b, 0)),
        out_shape=jax.ShapeDtypeStruct((n_seq * L, D_MODEL), BF16),
        scratch_shapes=scratch,
        compiler_params=_cparams(("arbitrary",)),
        name="diff_attention_ctx" if ctx is not None else "diff_attention",
    )(*args)


def _row_slots(pos, sel):
    big = float(BLOCK_ROWS)
    pos_a = jnp.max(pos, axis=-1, keepdims=True)
    pos_b = jnp.min(jnp.where(sel, pos, big), axis=-1, keepdims=True)
    return pos_a, pos_b


def _route_rows(h, wr, br):
    tm = h.shape[0]
    logits = _dot(h, wr) + br
    lane = lax.broadcasted_iota(jnp.int32, (tm, LANES), 1)
    neg = -jnp.inf
    is_grp = jnp.logical_and(lane >= N_EXPERTS, lane < N_EXPERTS + N_GROUPS)
    lg = jnp.where(is_grp, logits, neg)
    mg = jnp.max(lg, axis=-1, keepdims=True)
    g_gate = 1.0 / jnp.sum(jnp.exp(lg - mg), axis=-1, keepdims=True)
    g_idx = jnp.min(jnp.where(lg == mg, lane - N_EXPERTS, N_GROUPS), axis=-1, keepdims=True)
    in_grp = jnp.logical_and(lane < N_EXPERTS, lane // EXPERTS_PER_GROUP == g_idx)
    le = jnp.where(in_grp, logits, neg)
    m1 = jnp.max(le, axis=-1, keepdims=True)
    i1 = jnp.min(jnp.where(le == m1, lane, LANES), axis=-1, keepdims=True)
    le2 = jnp.where(lane == i1, neg, le)
    m2 = jnp.max(le2, axis=-1, keepdims=True)
    i2 = jnp.min(jnp.where(le2 == m2, lane, LANES), axis=-1, keepdims=True)
    e2 = jnp.exp(m2 - m1)
    inv = 1.0 / (1.0 + e2)
    gate = jnp.where(lane == i1, g_gate * inv, jnp.where(lane == i2, g_gate * (e2 * inv), 0.0))
    return gate, jnp.where(jnp.logical_or(lane == i1, lane == i2), 1.0, 0.0)


def _sort_block(h, onehot):
    tm = MOE_BLOCK
    sel = onehot > 0.5
    r = lax.broadcasted_iota(jnp.int32, (tm, tm), 0)
    c = lax.broadcasted_iota(jnp.int32, (tm, tm), 1)
    rank = _dot(jnp.where(c < r, 1.0, 0.0).astype(BF16), onehot.astype(BF16))
    count = jnp.sum(onehot, axis=0, keepdims=True)
    padded = jnp.ceil(count * (1.0 / CHUNK)) * CHUNK
    er = lax.broadcasted_iota(jnp.int32, (LANES, LANES), 0)
    ec = lax.broadcasted_iota(jnp.int32, (LANES, LANES), 1)
    before = jnp.where(er < ec, 1.0, 0.0).astype(BF16)
    start = _dot(jnp.broadcast_to(padded, (SUBLANES, LANES)).astype(BF16), before)[0:1, :]
    pos = jnp.where(sel, start + rank, -1.0)
    pos_a, pos_b = _row_slots(pos, sel)
    slot = lax.broadcasted_iota(jnp.int32, (tm, BLOCK_ROWS), 1).astype(F32)
    pick = jnp.where(jnp.logical_or(slot == pos_a, slot == pos_b), 1.0, 0.0).astype(BF16)
    xb = lax.dot_general(pick, h, (((0,), (0,)), ((), ())), preferred_element_type=F32).astype(BF16)
    return xb, pos, count.astype(jnp.int32)


ROUTER_BLOCKS = 4


def _router_kernel(x_ref, g_ref, sh_ref, sc_ref, wr_ref, br_ref, xb_ref, pos_ref, gate_ref, cnt_ref):
    h = _modulate(x_ref[...], g_ref[...], sc_ref[0], sh_ref[0]).astype(BF16)
    gate, sel = _route_rows(h, wr_ref[...].astype(BF16), br_ref[...])
    gate_ref[...] = gate
    for k in range(ROUTER_BLOCKS):
        rows = slice(k * MOE_BLOCK, (k + 1) * MOE_BLOCK)
        xb, pos, count = _sort_block(h[rows, :], sel[rows, :])
        xb_ref[k * BLOCK_ROWS:(k + 1) * BLOCK_ROWS, :] = xb
        pos_ref[rows, :] = pos
        cnt_ref[k] = count


def _router(x, g, mods, layer, wr, br):
    tm = ROUTER_BLOCKS * MOE_BLOCK
    return pl.pallas_call(
        _router_kernel,
        grid=(N_TOK // tm,),
        in_specs=[
            pl.BlockSpec((tm, D_MODEL), lambda i: (i, 0)),
            pl.BlockSpec((1, D_MODEL), lambda i: (0, 0)),
            pl.BlockSpec((1, 1, D_MODEL), _mod_index(layer, 3, tm)),
            pl.BlockSpec((1, 1, D_MODEL), _mod_index(layer, 4, tm)),
            pl.BlockSpec((D_MODEL, LANES), lambda i: (0, 0)),
            pl.BlockSpec((1, LANES), lambda i: (0, 0)),
        ],
        out_specs=[
            pl.BlockSpec((ROUTER_BLOCKS * BLOCK_ROWS, D_MODEL), lambda i: (i, 0)),
            pl.BlockSpec((tm, LANES), lambda i: (i, 0)),
            pl.BlockSpec((tm, LANES), lambda i: (i, 0)),
            pl.BlockSpec((ROUTER_BLOCKS, 1, LANES), lambda i: (i, 0, 0)),
        ],
        out_shape=[
            jax.ShapeDtypeStruct((N_BLOCKS * BLOCK_ROWS, D_MODEL), BF16),
            jax.ShapeDtypeStruct((N_TOK, LANES), F32),
            jax.ShapeDtypeStruct((N_TOK, LANES), F32),
            jax.ShapeDtypeStruct((N_BLOCKS, 1, LANES), jnp.int32),
        ],
        compiler_params=_cparams(("arbitrary",)),
        name="moe_router",
    )(x, g.reshape(1, D_MODEL), mods, mods, wr, br)


TABLE_ROWS = -(-(N_TILES * TILE_CHUNKS) // LANES)


def _tables_kernel(cnt_ref, tend_ref, disp_ref, comb_ref):
    ne = N_EXPERTS
    cnt = jnp.concatenate([cnt_ref[...].astype(F32), jnp.zeros((LANES - N_BLOCKS, LANES), F32)], axis=0)
    m = jnp.ceil(cnt * (1.0 / CHUNK))
    r = lax.broadcasted_iota(jnp.int32, (LANES, LANES), 0)
    c = lax.broadcasted_iota(jnp.int32, (LANES, LANES), 1)
    mb = m.astype(BF16)
    bstart = _dot(mb, jnp.where(r < c, 1.0, 0.0).astype(BF16))
    cm = _dot(jnp.where(c < r, 1.0, 0.0).astype(BF16), mb)
    total = jnp.sum(m, axis=0, keepdims=True)
    tiles = jnp.ceil(total * (1.0 / TILE_CHUNKS))
    tile_end = _dot(jnp.broadcast_to(tiles, (SUBLANES, LANES)).astype(BF16),
                    jnp.where(r <= c, 1.0, 0.0).astype(BF16))[0:1, :]
    choff = (tile_end - tiles) * TILE_CHUNKS
    tend_ref[...] = tile_end.astype(jnp.int32)

    start_t = (choff + cm).T
    m_t = m.T
    bstart_t = bstart.T
    lane = lax.broadcasted_iota(jnp.int32, (ne, LANES), 1).astype(F32)
    acc = [jnp.zeros((ne, LANES), F32) for _ in range(TABLE_ROWS)]
    for b in range(N_BLOCKS):
        lo = start_t[0:ne, b:b + 1]
        hi = lo + m_t[0:ne, b:b + 1]
        bs = bstart_t[0:ne, b:b + 1]
        fwd = (b * BLOCK_CHUNKS) + bs - lo
        for j in range(TABLE_ROWS):
            g = lane + float(j * LANES)
            acc[j] = acc[j] + jnp.where(g >= lo, jnp.where(g < hi, fwd + g, 0.0), 0.0)
        back = jnp.where(lane >= bs, jnp.where(lane < hi - lo + bs, (lo - bs + lane) * CHUNK + 1.0, 0.0), 0.0)
        comb_ref[b:b + 1, :] = (jnp.sum(back, axis=0, keepdims=True) - 1.0).astype(jnp.int32)
    for j in range(TABLE_ROWS):
        disp_ref[j:j + 1, :] = (jnp.sum(acc[j], axis=0, keepdims=True) * CHUNK).astype(jnp.int32)


def _chunk_tables(counts):
    return pl.pallas_call(
        _tables_kernel,
        out_shape=[
            jax.ShapeDtypeStruct((1, LANES), jnp.int32),
            jax.ShapeDtypeStruct((TABLE_ROWS, LANES), jnp.int32),
            jax.ShapeDtypeStruct((N_BLOCKS, LANES), jnp.int32),
        ],
        name="moe_chunk_tables",
    )(counts.reshape(N_BLOCKS, LANES))


def _chunk_copies(read_row, n_chunks, src_hbm, dst, sem, wait, skip_negative=False):
    for c in range(n_chunks):
        row = read_row(c)

        def copy(row=row, c=c):
            cp = pltpu.make_async_copy(src_hbm.at[pl.ds(pl.multiple_of(row, CHUNK), CHUNK)],
                                       dst.at[pl.ds(c * CHUNK, CHUNK)], sem)
            if wait:
                cp.wait()
            else:
                cp.start()

        if skip_negative:
            pl.when(row >= 0)(copy)
        else:
            copy()


IN_DEPTH = 3


def _expert_kernel(tend_ref, src_ref, xb_hbm, w1_ref, w3_ref, w2_ref, y_hbm,
                   xbuf, ybuf, zbuf, w13_scr, w2_scr, in_sem, out_sem, zero_sem):
    e = pl.program_id(0)
    first = jnp.where(e == 0, 0, tend_ref[0, jnp.maximum(e - 1, 0)])
    last = tend_ref[0, e]
    n_used = tend_ref[0, N_EXPERTS - 1]
    tiles_per_row = LANES // TILE_CHUNKS

    def fetch(tile, s, wait):
        row = tile // tiles_per_row
        col = (tile % tiles_per_row) * TILE_CHUNKS
        _chunk_copies(lambda c: src_ref[row, col + c], TILE_CHUNKS, xb_hbm, xbuf.at[s], in_sem.at[s], wait)

    def store(tile, s):
        rows = pl.ds(pl.multiple_of(tile * EXPERT_TILE, EXPERT_TILE), EXPERT_TILE)
        return pltpu.make_async_copy(ybuf.at[s], y_hbm.at[rows], out_sem.at[s])

    def zero_fill(tile):
        rows = pl.ds(pl.multiple_of(tile * EXPERT_TILE, EXPERT_TILE), EXPERT_TILE)
        return pltpu.make_async_copy(zbuf, y_hbm.at[rows], zero_sem.at[0])

    @pl.when(e == 0)
    def _():
        zbuf[...] = jnp.zeros_like(zbuf)

        def start(t, carry):
            zero_fill(t).start()
            return carry

        lax.fori_loop(n_used, N_TILES, start, 0)

    lead = IN_DEPTH - 1

    @pl.when(last > first)
    def _():
        @pl.when(first == 0)
        def _():
            for k in range(lead):
                fetch(jnp.minimum(k, n_used - 1), k, False)

        w13_scr[:, 0:D_EXPERT] = w1_ref[0, 0].astype(BF16)
        w13_scr[:, D_EXPERT:2 * D_EXPERT] = w3_ref[0, 0].astype(BF16)
        w2_scr[...] = w2_ref[0, 0].astype(BF16)

    def tile_body(t, carry):
        s = t % 2
        fetch(t, t % IN_DEPTH, True)
        ab = _dot(xbuf[t % IN_DEPTH], w13_scr[...])
        fetch(jnp.minimum(t + lead, n_used - 1), (t + lead) % IN_DEPTH, False)
        a = ab[:, 0:D_EXPERT]
        hid = (a * jax.nn.sigmoid(a)) * ab[:, D_EXPERT:2 * D_EXPERT]
        y = _dot(hid.astype(BF16), w2_scr[...]).astype(BF16)

        @pl.when(t >= 2)
        def _():
            store(t, s).wait()

        ybuf[s] = y
        store(t, s).start()
        return carry

    lax.fori_loop(first, last, tile_body, 0)

    @pl.when(e == N_EXPERTS - 1)
    def _():
        for k in range(lead):
            fetch(n_used - 1, (n_used + k) % IN_DEPTH, True)

        @pl.when(n_used >= 1)
        def _():
            store(0, (n_used - 1) % 2).wait()

        @pl.when(n_used >= 2)
        def _():
            store(0, n_used % 2).wait()

        def drain(t, carry):
            zero_fill(t).wait()
            return carry

        lax.fori_loop(n_used, N_TILES, drain, 0)


def _experts(tile_end, disp_src, xb, w1, w3, w2, layer):
    wsel = lambda e, tend, src: (layer, e, 0, 0)
    return pl.pallas_call(
        _expert_kernel,
        grid_spec=pltpu.PrefetchScalarGridSpec(
            num_scalar_prefetch=2,
            grid=(N_EXPERTS,),
            in_specs=[
                pl.BlockSpec(memory_space=pl.ANY),
                pl.BlockSpec((1, 1, D_MODEL, D_EXPERT), wsel),
                pl.BlockSpec((1, 1, D_MODEL, D_EXPERT), wsel),
                pl.BlockSpec((1, 1, D_EXPERT, D_MODEL), wsel),
            ],
            out_specs=pl.BlockSpec(memory_space=pl.ANY),
            scratch_shapes=[pltpu.VMEM((IN_DEPTH, EXPERT_TILE, D_MODEL), BF16),
                            pltpu.VMEM((2, EXPERT_TILE, D_MODEL), BF16),
                            pltpu.VMEM((EXPERT_TILE, D_MODEL), BF16),
                            pltpu.VMEM((D_MODEL, 2 * D_EXPERT), BF16),
                            pltpu.VMEM((D_EXPERT, D_MODEL), BF16),
                            pltpu.SemaphoreType.DMA((IN_DEPTH,)),
                            pltpu.SemaphoreType.DMA((2,)),
                            pltpu.SemaphoreType.DMA((1,))],
        ),
        out_shape=jax.ShapeDtypeStruct((N_TILES * EXPERT_TILE, D_MODEL), BF16),
        compiler_params=_cparams(("arbitrary",)),
        name="moe_experts",
    )(tile_end, disp_src, xb, w1, w3, w2)


def _combine_kernel(src_ref, y_hbm, x_ref, pos_ref, gate_ref, mg_ref, *rest, tm, final):
    if final:
        fg_ref, op_ref, os_ref, ybuf, sem = rest
    else:
        o_ref, ybuf, sem = rest
    i = pl.program_id(0)
    slot = i % 2

    def fetch(blk, s, wait):
        _chunk_copies(lambda c: src_ref[blk, c], BLOCK_CHUNKS, y_hbm, ybuf.at[s], sem.at[s], wait,
                      skip_negative=True)

    @pl.when(i == 0)
    def _():
        ybuf[...] = jnp.zeros_like(ybuf)
        fetch(0, 0, False)

    @pl.when(i + 1 < N_BLOCKS)
    def _():
        fetch(i + 1, 1 - slot, False)

    fetch(i, slot, True)
    pos = pos_ref[...]
    sel = pos >= 0.0
    gate = gate_ref[...]
    pos_a, pos_b = _row_slots(pos, sel)
    gate_a = jnp.sum(jnp.where(pos == pos_a, gate, 0.0), axis=-1, keepdims=True)
    gate_b = jnp.sum(jnp.where(jnp.logical_and(sel, pos == pos_b), gate, 0.0), axis=-1, keepdims=True)
    row = lax.broadcasted_iota(jnp.int32, (tm, BLOCK_ROWS), 1).astype(F32)
    weights = jnp.where(row == pos_a, gate_a, jnp.where(row == pos_b, gate_b, 0.0))
    moe = _dot(weights.astype(BF16), ybuf[slot])
    out = x_ref[...] + mg_ref[0] * moe
    if not final:
        o_ref[...] = out
        return

    ms = jnp.mean(out * out, axis=-1, keepdims=True)
    out = (out * lax.rsqrt(ms + EPS)) * fg_ref[...]

    @pl.when(i < N_PROMPT // tm)
    def _():
        op_ref[...] = out

    @pl.when(i >= N_PROMPT // tm)
    def _():
        os_ref[...] = out


def _combine(comb_src, y, x, pos, gate, mods, layer, final_g=None):
    tm = MOE_BLOCK
    final = final_g is not None
    np_blocks = N_PROMPT // tm

    def gate_idx(i, s):
        return ((layer * MOD_ROWS + _mod_row(i, tm)) * 6 + 5, 0, 0)

    in_specs = [
        pl.BlockSpec(memory_space=pl.ANY),
        pl.BlockSpec((tm, D_MODEL), lambda i, s: (i, 0)),
        pl.BlockSpec((tm, LANES), lambda i, s: (i, 0)),
        pl.BlockSpec((tm, LANES), lambda i, s: (i, 0)),
        pl.BlockSpec((1, 1, D_MODEL), gate_idx),
    ]
    args = [comb_src, y, x, pos, gate, mods]
    if final:
        in_specs.append(pl.BlockSpec((1, D_MODEL), lambda i, s: (0, 0)))
        args.append(final_g.reshape(1, D_MODEL))
        out_specs = [pl.BlockSpec((tm, D_MODEL), lambda i, s: (jnp.minimum(i, np_blocks - 1), 0)),
                     pl.BlockSpec((tm, D_MODEL), lambda i, s: (jnp.maximum(i - np_blocks, 0), 0))]
        out_shape = [jax.ShapeDtypeStruct((N_PROMPT, D_MODEL), F32),
                     jax.ShapeDtypeStruct((N_SAMPLE, D_MODEL), F32)]
    else:
        out_specs = pl.BlockSpec((tm, D_MODEL), lambda i, s: (i, 0))
        out_shape = jax.ShapeDtypeStruct((N_TOK, D_MODEL), F32)
    return pl.pallas_call(
        functools.partial(_combine_kernel, tm=tm, final=final),
        grid_spec=pltpu.PrefetchScalarGridSpec(
            num_scalar_prefetch=1,
            grid=(N_BLOCKS,),
            in_specs=in_specs,
            out_specs=out_specs,
            scratch_shapes=[pltpu.VMEM((2, BLOCK_ROWS, D_MODEL), BF16),
                            pltpu.SemaphoreType.DMA((2,))],
        ),
        out_shape=out_shape,
        compiler_params=_cparams(("arbitrary",)),
        name="moe_combine_final" if final else "moe_combine",
    )(*args)


def _moe(x, g, mods, layer, w_group, b_group, w_expert, b_expert, w1, w3, w2, final_g=None):
    pad = LANES - N_EXPERTS - N_GROUPS
    wr = jnp.concatenate([w_expert, w_group, jnp.zeros((D_MODEL, pad), F32)], axis=1)
    br = jnp.concatenate([b_expert, b_group, jnp.zeros((pad,), F32)]).reshape(1, LANES)
    xb, pos, gate, counts = _router(x, g, mods, layer, wr, br)
    tile_end, disp_src, comb_src = _chunk_tables(counts)
    y = _experts(tile_end, disp_src, xb, w1, w3, w2, layer)
    return _combine(comb_src, y, x, pos, gate, mods, layer, final_g)


def kernel(x_prompt, x_sample, cache_k, cache_v, c, c_ctx, ada_w, ada_b, norm1_g, norm2_g, final_g, hy_w_in, hy_conv_w, hy_conv_b, hy_f_w1, hy_f_b1, hy_f_w2, hy_f_b2, hy_f_w3, hy_f_b3, hy_f_freq, hy_bias, hy_w_out, at_w_qkv, at_lam_q1, at_lam_k1, at_lam_q2, at_lam_k2, at_subln_g, at_w_o, moe_w_group, moe_b_group, moe_w_expert, moe_b_expert, moe_w1, moe_w3, moe_w2):
    x = (x_prompt.reshape(N_PROMPT, D_MODEL), x_sample.reshape(N_SAMPLE, D_MODEL))
    cond = jnp.concatenate([c_ctx[None, :], c, jnp.zeros((MOD_ROWS - 1 - DEC_BATCH, D_MODEL), F32)], axis=0)
    mods = _ada_table(cond, ada_w, ada_b).reshape(DEPTH * MOD_ROWS * 6, 1, D_MODEL)
    ctx_k = cache_k.reshape(DEC_BATCH, DEPTH // 2, PAST_LEN, D_MODEL)
    ctx_v = cache_v.reshape(DEC_BATCH, DEPTH // 2, PAST_LEN, D_MODEL)

    dft = {}
    for L in (SEQ, DEC_SEQ):
        fwd, inv = _dft_mats(L)
        dft[L] = (jnp.asarray(fwd).astype(BF16), jnp.asarray(inv).astype(BF16))

    new_k, new_v = [], []
    for l in range(DEPTH):
        j = l // 2
        if l % 2 == 0:
            z = _norm_matmul(x, norm1_g[l], mods, l, 0, 1, hy_w_in, j)
            ys = []
            for L, rb0, n_seq, dt in ((SEQ, 0, BATCH, 1024), (DEC_SEQ, N_PROMPT // DEC_SEQ, DEC_BATCH, 512)):
                f_bf, finv = dft[L]
                kc = _filter_spectra(L, hy_f_w1[j], hy_f_b1[j], hy_f_w2[j], hy_f_b2[j], hy_f_w3[j],
                                     hy_f_b3[j], hy_f_freq[j], f_bf)
                ys.append(_hyena_conv(z, hy_conv_w[j], hy_conv_b[j], f_bf, finv, kc, hy_bias[j],
                                      L=L, row_block0=rb0, n_seq=n_seq, dt=dt))
            x = _matmul_residual((ys[0], ys[1]), hy_w_out, j, x, mods, l, 2)
        else:
            lam_init = 0.8 - 0.6 * math.exp(-0.3 * l)
            qkv = _norm_matmul(x, norm1_g[l], mods, l, 0, 1, at_w_qkv, j)
            lam_params = (at_lam_q1[j], at_lam_k1[j], at_lam_q2[j], at_lam_k2[j])
            op = _attention(qkv, lam_params, at_subln_g[j], lam_init, L=SEQ, row_block0=0,
                            n_seq=BATCH, tq=SEQ)
            os_ = _attention(qkv, lam_params, at_subln_g[j], lam_init, L=DEC_SEQ,
                             row_block0=N_PROMPT // DEC_SEQ, n_seq=DEC_BATCH, tq=256,
                             ctx=(ctx_k, ctx_v), j=j)
            new_k.append(qkv[:N_PROMPT, D_MODEL:2 * D_MODEL].astype(F32).reshape(BATCH, SEQ, N_HEADS, HEAD_W))
            new_v.append(qkv[:N_PROMPT, 2 * D_MODEL:].astype(F32).reshape(BATCH, SEQ, N_HEADS, HEAD_W))
            x = _matmul_residual((op, os_), at_w_o, j, x, mods, l, 2)
        x = _moe(x, norm2_g[l], mods, l, moe_w_group[l], moe_b_group[l], moe_w_expert[l],
                 moe_b_expert[l], moe_w1, moe_w3, moe_w2, final_g if l == DEPTH - 1 else None)
    y_prompt = x[0].reshape(BATCH, SEQ, D_MODEL)
    y_sample = x[1].reshape(DEC_BATCH, DEC_SEQ, D_MODEL)
    return (y_prompt, y_sample, jnp.stack(new_k, axis=1), jnp.stack(new_v, axis=1))
```

```python
import functools
import math

import numpy as np
import jax
import jax.numpy as jnp
from jax import lax
from jax.experimental import pallas as pl
from jax.experimental.pallas import tpu as pltpu

F32 = jnp.float32
BF16 = jnp.bfloat16

D_MODEL = 1024
BATCH = 16
SEQ = 256
DEPTH = 4
DEC_BATCH = 4
DEC_SEQ = 1024
PAST_LEN = 256
GRID_W = 64
EPS = 1e-6
HY_BANDS = 16
HY_EMB = 1 + 2 * HY_BANDS
HY_FILTER_HIDDEN = 64
HY_FAST_DECAY = 0.3
HY_SLOW_DECAY = 1.5
HY_TARGET = 1e-2
N_HEADS = 8
HEAD_DIM = 64
ROT_AXIS = HEAD_DIM // 2
ROPE_BASE = 10000.0
N_GROUPS = 4
EXPERTS_PER_GROUP = 8
N_EXPERTS = N_GROUPS * EXPERTS_PER_GROUP
D_EXPERT = D_MODEL // 4

N_PROMPT = BATCH * SEQ
N_SAMPLE = DEC_BATCH * DEC_SEQ
N_TOK = N_PROMPT + N_SAMPLE
MOD_ROWS = 8
LANES = 128
SUBLANES = 8
HEAD_W = 2 * HEAD_DIM
VMEM_LIMIT = 56 * 1024 * 1024

MOE_BLOCK = 256
N_BLOCKS = N_TOK // MOE_BLOCK
CHUNK = 2 * SUBLANES
BLOCK_ROWS = 2 * MOE_BLOCK + N_EXPERTS * CHUNK
BLOCK_CHUNKS = BLOCK_ROWS // CHUNK
EXPERT_TILE = 256
TILE_CHUNKS = EXPERT_TILE // CHUNK
N_TILES = (2 * N_TOK + N_BLOCKS * N_EXPERTS * (CHUNK - 1)) // EXPERT_TILE + N_EXPERTS


def _cparams(sem):
    return pltpu.CompilerParams(dimension_semantics=sem, vmem_limit_bytes=VMEM_LIMIT)


def _mod_row(blk, tm):
    start = blk * tm
    return jnp.where(start < N_PROMPT, 0, 1 + (start - N_PROMPT) // DEC_SEQ)


def _mod_index(layer, which, tm):
    def index_map(i, *_):
        return ((layer * MOD_ROWS + _mod_row(i, tm)) * 6 + which, 0, 0)
    return index_map


@functools.lru_cache(maxsize=None)
def _dft_mats(L):
    k = np.arange(L, dtype=np.float64)[:, None]
    j = np.arange(L, dtype=np.float64)[None, :]
    ang = np.pi * ((k * j) % (2 * L)) / L
    c = np.cos(ang)
    s = np.sin(ang)
    s[0, :] = np.where(np.arange(L) % 2 == 0, 1.0, -1.0)
    fwd = np.concatenate([c, s], axis=0)
    scale = np.full((2 * L,), 1.0 / L)
    scale[0] = scale[L] = 0.5 / L
    inv = fwd.T * scale[None, :]
    return fwd.astype(np.float32), inv.astype(np.float32)


@functools.lru_cache(maxsize=None)
def _filter_feats(L):
    pos = np.arange(L, dtype=np.float64)
    t = pos / (L - 1)
    bands = np.linspace(1e-4, HY_BANDS - 1, HY_BANDS)
    ang = (2.0 * math.pi / L) * pos[:, None] * bands[None, :]
    feats = np.concatenate([t[:, None], np.cos(ang), -np.sin(ang)], axis=-1)
    feats = np.pad(feats, ((0, 0), (0, LANES - HY_EMB)))
    deltas = np.abs(np.linspace(math.log(HY_TARGET) / HY_SLOW_DECAY,
                                math.log(HY_TARGET) / HY_FAST_DECAY, D_MODEL))
    window = np.exp(-t[:, None] * deltas[None, :])
    alt = np.where(np.arange(L) % 2 == 0, 1.0, -1.0)[:, None] * np.ones((1, LANES))
    return feats.astype(np.float32), window.astype(np.float32), alt.astype(np.float32)


@functools.lru_cache(maxsize=None)
def _rope_tables():
    pos = np.arange(DEC_SEQ)
    row = (pos // GRID_W).astype(np.float64)
    col = (pos % GRID_W).astype(np.float64)
    lane = np.arange(HEAD_W)
    d = lane % HEAD_DIM
    axis = d // ROT_AXIS
    n = d % ROT_AXIS
    half = n // (ROT_AXIS // 2)
    f = n % (ROT_AXIS // 2)
    inv = ROPE_BASE ** (-f.astype(np.float64) / (ROT_AXIS // 2))
    p = np.where(axis[None, :] == 0, row[:, None], col[:, None])
    ang = p * inv[None, :]
    sign = np.where(half == 0, -1.0, 1.0)[None, :]
    return np.cos(ang).astype(np.float32), (np.sin(ang) * sign).astype(np.float32)


def _split_bf16(a):
    hi = a.astype(BF16)
    lo = (a - hi.astype(F32)).astype(BF16)
    return hi, lo


def _dot(a, b):
    return jnp.dot(a, b, preferred_element_type=F32)


def _dot3(a_hi, a_lo, b_hi, b_lo):
    return _dot(a_hi, b_hi) + (_dot(a_hi, b_lo) + _dot(a_lo, b_hi))


def _ada_kernel(c_ref, w_ref, b_ref, o_ref):
    c = c_ref[...]
    s = c * jax.nn.sigmoid(c)
    o_ref[0] = _dot(s.astype(BF16), w_ref[0].astype(BF16)) + b_ref[0]


def _ada_table(cond, ada_w, ada_b):
    tn = 1536
    n = 6 * D_MODEL
    return pl.pallas_call(
        _ada_kernel,
        grid=(DEPTH, n // tn),
        in_specs=[
            pl.BlockSpec((MOD_ROWS, D_MODEL), lambda l, j: (0, 0)),
            pl.BlockSpec((1, D_MODEL, tn), lambda l, j: (l, 0, j)),
            pl.BlockSpec((1, 1, tn), lambda l, j: (l, 0, j)),
        ],
        out_specs=pl.BlockSpec((1, MOD_ROWS, tn), lambda l, j: (l, 0, j)),
        out_shape=jax.ShapeDtypeStruct((DEPTH, MOD_ROWS, n), F32),
        compiler_params=_cparams(("arbitrary", "arbitrary")),
        name="ada_table",
    )(cond, ada_w, ada_b.reshape(DEPTH, 1, n))


def _modulate(x, g, scale, shift):
    ms = jnp.mean(x * x, axis=-1, keepdims=True)
    y = (x * lax.rsqrt(ms + EPS)) * g
    return y * (1.0 + scale) + shift


MM_COLS = 512


def _row_group_specs(x, tm, width):
    np_blocks = N_PROMPT // tm
    if isinstance(x, tuple):
        specs = [pl.BlockSpec((tm, width), lambda i, *_: (jnp.minimum(i, np_blocks - 1), 0)),
                 pl.BlockSpec((tm, width), lambda i, *_: (jnp.maximum(i - np_blocks, 0), 0))]
        return specs, list(x)
    specs = [pl.BlockSpec((tm, width), lambda i, *_: (jnp.minimum(i, np_blocks - 1), 0)),
             pl.BlockSpec((tm, width), lambda i, *_: (jnp.maximum(i, np_blocks), 0))]
    return specs, [x, x]


def _pick_rows(i, tm, p_ref, s_ref):
    return jnp.where(i < N_PROMPT // tm, p_ref[...], s_ref[...])


def _normmm_kernel(xp_ref, xs_ref, g_ref, sh_ref, sc_ref, w_ref, o_ref, w_scr, *, tm):
    i = pl.program_id(0)

    @pl.when(i == 0)
    def _():
        w_scr[...] = w_ref[0].astype(BF16)

    x = _pick_rows(i, tm, xp_ref, xs_ref)
    h = _modulate(x, g_ref[...], sc_ref[0], sh_ref[0]).astype(BF16)
    for j in range(w_scr.shape[1] // MM_COLS):
        cols = slice(j * MM_COLS, (j + 1) * MM_COLS)
        o_ref[:, cols] = _dot(h, w_scr[:, cols]).astype(o_ref.dtype)


def _norm_matmul(x, g, mods, layer, which_shift, which_scale, w, widx, *, tm=512):
    n = w.shape[2]
    x_specs, x_args = _row_group_specs(x, tm, D_MODEL)
    return pl.pallas_call(
        functools.partial(_normmm_kernel, tm=tm),
        grid=(N_TOK // tm,),
        in_specs=x_specs + [
            pl.BlockSpec((1, D_MODEL), lambda i: (0, 0)),
            pl.BlockSpec((1, 1, D_MODEL), _mod_index(layer, which_shift, tm)),
            pl.BlockSpec((1, 1, D_MODEL), _mod_index(layer, which_scale, tm)),
            pl.BlockSpec((1, D_MODEL, n), lambda i: (widx, 0, 0), pipeline_mode=pl.Buffered(1)),
        ],
        out_specs=pl.BlockSpec((tm, n), lambda i: (i, 0)),
        out_shape=jax.ShapeDtypeStruct((N_TOK, n), BF16),
        scratch_shapes=[pltpu.VMEM((D_MODEL, n), BF16)],
        compiler_params=_cparams(("arbitrary",)),
        name="norm_matmul",
    )(*x_args, g.reshape(1, D_MODEL), mods, mods, w)


def _mmres_kernel(ap_ref, as_ref, xp_ref, xs_ref, w_ref, gate_ref, o_ref, w_scr, *, tm):
    i = pl.program_id(0)

    @pl.when(i == 0)
    def _():
        w_scr[...] = w_ref[0].astype(BF16)

    a = _pick_rows(i, tm, ap_ref, as_ref)
    x = _pick_rows(i, tm, xp_ref, xs_ref)
    gate = gate_ref[0]
    for j in range(w_scr.shape[1] // MM_COLS):
        cols = slice(j * MM_COLS, (j + 1) * MM_COLS)
        o_ref[:, cols] = x[:, cols] + gate[:, cols] * _dot(a, w_scr[:, cols])


def _matmul_residual(a, w, widx, x, mods, layer, which_gate, *, tm=512):
    n = w.shape[2]
    a_specs, a_args = _row_group_specs(a, tm, D_MODEL)
    x_specs, x_args = _row_group_specs(x, tm, n)
    return pl.pallas_call(
        functools.partial(_mmres_kernel, tm=tm),
        grid=(N_TOK // tm,),
        in_specs=a_specs + x_specs + [
            pl.BlockSpec((1, D_MODEL, n), lambda i: (widx, 0, 0), pipeline_mode=pl.Buffered(1)),
            pl.BlockSpec((1, 1, n), _mod_index(layer, which_gate, tm)),
        ],
        out_specs=pl.BlockSpec((tm, n), lambda i: (i, 0)),
        out_shape=jax.ShapeDtypeStruct((N_TOK, n), F32),
        scratch_shapes=[pltpu.VMEM((D_MODEL, n), BF16)],
        compiler_params=_cparams(("arbitrary",)),
        name="matmul_residual",
    )(*a_args, *x_args, w, mods)


def _filter_kernel(feats_ref, win_ref, alt_ref, w1_ref, b1_ref, w2_ref, b2_ref, fq_ref,
                   w3f_ref, b3f_ref, w3b_ref, b3b_ref, f_ref, o_ref, h_scr, *, L):
    def dense(a, w_ref, b_ref):
        a_hi, a_lo = _split_bf16(a)
        w_hi, w_lo = _split_bf16(w_ref[...])
        return _dot3(a_hi, a_lo, w_hi, w_lo) + b_ref[...]

    @pl.when(jnp.logical_and(pl.program_id(0) == 0, pl.program_id(1) == 0))
    def _():
        fq = fq_ref[...]
        h1 = jnp.sin(fq[0:1, :] * dense(feats_ref[...], w1_ref, b1_ref))
        h_scr[...] = jnp.sin(fq[1:2, :] * dense(h1, w2_ref, b2_ref))

    h = h_scr[...]
    win = win_ref[...]
    hf = dense(h, w3f_ref, b3f_ref) * win
    hb = dense(h, w3b_ref, b3b_ref) * win
    row = lax.broadcasted_iota(jnp.int32, hf.shape, 0)
    hb = jnp.where(row == 0, 0.0, hb)
    kr = _dot(f_ref[0:L, :], (hf + hb).astype(BF16))
    ks = _dot(f_ref[L:2 * L, :], (hf - hb).astype(BF16))
    k_nyq = jnp.sum((hf + hb) * alt_ref[:, 0:1], axis=0, keepdims=True)
    o_ref[0, 0] = kr
    o_ref[0, 1] = jnp.where(row == 0, k_nyq, kr)
    o_ref[0, 2] = jnp.where(row == 0, 0.0, ks)


def _filter_spectra(L, w1, b1, w2, b2, w3, b3, freq, f_bf, *, dt=512):
    feats, window, alt = _filter_feats(L)
    fh = HY_FILTER_HIDDEN
    nd = D_MODEL // dt
    w1p = jnp.pad(w1, ((0, LANES - HY_EMB), (0, 0)))
    const = lambda shape: pl.BlockSpec(shape, lambda o, c: tuple(0 for _ in shape))
    return pl.pallas_call(
        functools.partial(_filter_kernel, L=L),
        grid=(2, nd),
        in_specs=[
            const((L, LANES)),
            pl.BlockSpec((L, dt), lambda o, c: (0, c)),
            const((L, LANES)),
            const((LANES, fh)), const((1, fh)), const((fh, fh)), const((1, fh)), const((2, fh)),
            pl.BlockSpec((fh, dt), lambda o, c: (0, (2 * o) * nd + c)),
            pl.BlockSpec((1, dt), lambda o, c: (0, (2 * o) * nd + c)),
            pl.BlockSpec((fh, dt), lambda o, c: (0, (2 * o + 1) * nd + c)),
            pl.BlockSpec((1, dt), lambda o, c: (0, (2 * o + 1) * nd + c)),
            const((2 * L, L)),
        ],
        out_specs=pl.BlockSpec((1, 3, L, dt), lambda o, c: (o, 0, 0, c)),
        out_shape=jax.ShapeDtypeStruct((2, 3, L, D_MODEL), F32),
        scratch_shapes=[pltpu.VMEM((L, fh), F32)],
        compiler_params=_cparams(("arbitrary", "arbitrary")),
        name=f"hyena_filter_spectra_{L}",
    )(jnp.asarray(feats), jnp.asarray(window), jnp.asarray(alt), w1p, b1.reshape(1, fh), w2,
      b2.reshape(1, fh), freq, w3, b3.reshape(1, -1), w3, b3.reshape(1, -1), f_bf)


def _conv_kernel(v_ref, x1_ref, x2_ref, cwv_ref, cw1_ref, cw2_ref, cbv_ref, cb1_ref, cb2_ref,
                 f_ref, finv_ref, kc_ref, bias_ref, o_ref, *, L):
    row = lax.broadcasted_iota(jnp.int32, v_ref.shape, 0)

    def short_conv(u_ref, w_ref, b_ref):
        u = u_ref[...].astype(F32)
        prev = jnp.where(row == 0, 0.0, pltpu.roll(u, 1, 0))
        nxt = jnp.where(row == L - 1, 0.0, pltpu.roll(u, L - 1, 0))
        w = w_ref[...]
        return prev * w[0:1, :] + u * w[1:2, :] + nxt * w[2:3, :] + b_ref[...]

    def long_conv(u, order):
        spec = _dot(f_ref[...], u.astype(BF16))
        a = spec[0:L, :]
        b = spec[L:2 * L, :]
        kra = kc_ref[order, 0]
        krb = kc_ref[order, 1]
        ks = kc_ref[order, 2]
        prod = jnp.concatenate([a * kra - b * ks, a * ks + b * krb], axis=0)
        return _dot(finv_ref[...], prod.astype(BF16)) + u * bias_ref[order:order + 1, :]

    v = short_conv(v_ref, cwv_ref, cbv_ref)
    y = short_conv(x1_ref, cw1_ref, cb1_ref) * long_conv(v, 0)
    y = short_conv(x2_ref, cw2_ref, cb2_ref) * long_conv(y, 1)
    o_ref[...] = y.astype(o_ref.dtype)


def _hyena_conv(z, conv_w, conv_b, f_bf, finv_bf, kc, bias, *, L, row_block0, n_seq, dt):
    nd = D_MODEL // dt
    cb = conv_b.reshape(1, 3 * D_MODEL)
    seg = lambda s: pl.BlockSpec((L, dt), lambda c, b: (row_block0 + b, s * nd + c))
    cw = lambda s: pl.BlockSpec((3, dt), lambda c, b: (0, s * nd + c))
    cbs = lambda s: pl.BlockSpec((1, dt), lambda c, b: (0, s * nd + c))
    return pl.pallas_call(
        functools.partial(_conv_kernel, L=L),
        grid=(nd, n_seq),
        in_specs=[
            seg(0), seg(1), seg(2), cw(0), cw(1), cw(2), cbs(0), cbs(1), cbs(2),
            pl.BlockSpec((2 * L, L), lambda c, b: (0, 0), pipeline_mode=pl.Buffered(1)),
            pl.BlockSpec((L, 2 * L), lambda c, b: (0, 0), pipeline_mode=pl.Buffered(1)),
            pl.BlockSpec((2, 3, L, dt), lambda c, b: (0, 0, 0, c), pipeline_mode=pl.Buffered(1)),
            pl.BlockSpec((2, dt), lambda c, b: (0, c)),
        ],
        out_specs=pl.BlockSpec((L, dt), lambda c, b: (b, c)),
        out_shape=jax.ShapeDtypeStruct((n_seq * L, D_MODEL), BF16),
        compiler_params=_cparams(("arbitrary", "arbitrary")),
        name=f"hyena_conv_{L}",
    )(z, z, z, conv_w, conv_w, conv_w, cb, cb, cb, f_bf, finv_bf, kc, bias)


def _attn_kernel(lq1_ref, lk1_ref, lq2_ref, lk2_ref, g_ref, q_ref, k_ref, v_ref, *rest,
                 lam_init, has_ctx, tq, nq):
    if has_ctx:
        ck_ref, cv_ref, cos_ref, sin_ref, o_ref, k_scr = rest
    else:
        (o_ref,) = rest
    lam = (jnp.exp(jnp.sum(lq1_ref[...] * lk1_ref[...], axis=-1, keepdims=True))
           - jnp.exp(jnp.sum(lq2_ref[...] * lk2_ref[...], axis=-1, keepdims=True)) + lam_init)
    lane = lax.broadcasted_iota(jnp.int32, (tq, HEAD_W), 1)
    first_map = lane < HEAD_DIM
    first_half = (lane % ROT_AXIS) < (ROT_AXIS // 2)
    nt = (((1,), (1,)), ((), ()))

    def rotary(x, rows):
        x = x.astype(F32)
        partner = jnp.where(first_half, pltpu.roll(x, HEAD_W - ROT_AXIS // 2, 1),
                            pltpu.roll(x, ROT_AXIS // 2, 1))
        return (x * cos_ref[rows, :] + partner * sin_ref[rows, :]).astype(BF16)

    if has_ctx:
        for r in range(k_ref.shape[0] // tq):
            rows = slice(r * tq, (r + 1) * tq)
            for h in range(N_HEADS):
                cols = slice(h * HEAD_W, (h + 1) * HEAD_W)
                k_scr[rows, cols] = rotary(k_ref[rows, cols], rows)
        keys = k_scr
    else:
        keys = k_ref

    def q_block(rows):
        for h in range(N_HEADS):
            cols = slice(h * HEAD_W, (h + 1) * HEAD_W)
            q = q_ref[rows, cols]
            if has_ctx:
                q = rotary(q, rows)
            q = q * (HEAD_DIM ** -0.5)
            zero = jnp.zeros_like(q)
            q2 = jnp.concatenate([jnp.where(first_map, q, zero), jnp.where(first_map, zero, q)], axis=0)
            s = lax.dot_general(q2, keys[:, cols], nt, preferred_element_type=F32)
            m = jnp.max(s, axis=-1, keepdims=True)
            if has_ctx:
                sc = lax.dot_general(q2, ck_ref[0, 0, :, cols].astype(BF16), nt,
                                     preferred_element_type=F32)
                m = jnp.maximum(m, jnp.max(sc, axis=-1, keepdims=True))
                ec = jnp.exp(sc - m)
            e = jnp.exp(s - m)
            den = jnp.sum(e, axis=-1, keepdims=True)
            if has_ctx:
                den = den + jnp.sum(ec, axis=-1, keepdims=True)
            ov = _dot(e.astype(BF16), v_ref[:, cols])
            if has_ctx:
                ov = ov + _dot(ec.astype(BF16), cv_ref[0, 0, :, cols].astype(BF16))
            inv = 1.0 / den
            o = ov[0:tq] * inv[0:tq] - ov[tq:2 * tq] * (lam * inv[tq:2 * tq])
            ms = jnp.mean(o * o, axis=-1, keepdims=True)
            o = (o * lax.rsqrt(ms + EPS)) * g_ref[...] * (1.0 - lam_init)
            o_ref[rows, cols] = o.astype(o_ref.dtype)

    if nq == 1:
        q_block(slice(0, tq))
    else:
        def body(qi, carry):
            q_block(pl.ds(pl.multiple_of(qi * tq, tq), tq))
            return carry
        lax.fori_loop(0, nq, body, 0)


def _attention(qkv, lam_params, subln_g, lam_init, *, L, row_block0, n_seq, tq, ctx=None, j=0):
    small = pl.BlockSpec((1, HEAD_DIM), lambda b: (0, 0))
    in_specs = [small, small, small, small,
                pl.BlockSpec((1, HEAD_W), lambda b: (0, 0)),
                pl.BlockSpec((L, D_MODEL), lambda b: (row_block0 + b, 0)),
                pl.BlockSpec((L, D_MODEL), lambda b: (row_block0 + b, 1)),
                pl.BlockSpec((L, D_MODEL), lambda b: (row_block0 + b, 2))]
    args = [p.reshape(1, HEAD_DIM) for p in lam_params] + [subln_g.reshape(1, HEAD_W), qkv, qkv, qkv]
    scratch = []
    if ctx is not None:
        ctx_spec = pl.BlockSpec((1, 1, PAST_LEN, D_MODEL), lambda b: (b, j, 0, 0))
        rope_spec = pl.BlockSpec((L, HEAD_W), lambda b: (0, 0))
        cos, sin = _rope_tables()
        in_specs += [ctx_spec, ctx_spec, rope_spec, rope_spec]
        args += list(ctx) + [jnp.asarray(cos), jnp.asarray(sin)]
        scratch = [pltpu.VMEM((L, D_MODEL), BF16)]
    return pl.pallas_call(
        functools.partial(_attn_kernel, lam_init=lam_init, has_ctx=ctx is not None, tq=tq, nq=L // tq),
        grid=(n_seq,),
        in_specs=in_specs,
        out_specs=pl.BlockSpec((L, D_MODEL), lambda b: (b, 0)),
        out_shape=jax.ShapeDtypeStruct((n_seq * L, D_MODEL), BF16),
        scratch_shapes=scratch,
        compiler_params=_cparams(("arbitrary",)),
        name="diff_attention_ctx" if ctx is not None else "diff_attention",
    )(*args)


def _row_slots(pos, sel):
    big = float(BLOCK_ROWS)
    pos_a = jnp.max(pos, axis=-1, keepdims=True)
    pos_b = jnp.min(jnp.where(sel, pos, big), axis=-1, keepdims=True)
    return pos_a, pos_b


def _route_rows(h, wr, br):
    tm = h.shape[0]
    logits = _dot(h, wr) + br
    lane = lax.broadcasted_iota(jnp.int32, (tm, LANES), 1)
    neg = -jnp.inf
    is_grp = jnp.logical_and(lane >= N_EXPERTS, lane < N_EXPERTS + N_GROUPS)
    lg = jnp.where(is_grp, logits, neg)
    mg = jnp.max(lg, axis=-1, keepdims=True)
    g_gate = 1.0 / jnp.sum(jnp.exp(lg - mg), axis=-1, keepdims=True)
    g_idx = jnp.min(jnp.where(lg == mg, lane - N_EXPERTS, N_GROUPS), axis=-1, keepdims=True)
    in_grp = jnp.logical_and(lane < N_EXPERTS, lane // EXPERTS_PER_GROUP == g_idx)
    le = jnp.where(in_grp, logits, neg)
    m1 = jnp.max(le, axis=-1, keepdims=True)
    i1 = jnp.min(jnp.where(le == m1, lane, LANES), axis=-1, keepdims=True)
    le2 = jnp.where(lane == i1, neg, le)
    m2 = jnp.max(le2, axis=-1, keepdims=True)
    i2 = jnp.min(jnp.where(le2 == m2, lane, LANES), axis=-1, keepdims=True)
    e2 = jnp.exp(m2 - m1)
    inv = 1.0 / (1.0 + e2)
    gate = jnp.where(lane == i1, g_gate * inv, jnp.where(lane == i2, g_gate * (e2 * inv), 0.0))
    return gate, jnp.where(jnp.logical_or(lane == i1, lane == i2), 1.0, 0.0)


def _sort_block(h, onehot):
    tm = MOE_BLOCK
    sel = onehot > 0.5
    r = lax.broadcasted_iota(jnp.int32, (tm, tm), 0)
    c = lax.broadcasted_iota(jnp.int32, (tm, tm), 1)
    rank = _dot(jnp.where(c < r, 1.0, 0.0).astype(BF16), onehot.astype(BF16))
    count = jnp.sum(onehot, axis=0, keepdims=True)
    padded = jnp.ceil(count * (1.0 / CHUNK)) * CHUNK
    er = lax.broadcasted_iota(jnp.int32, (LANES, LANES), 0)
    ec = lax.broadcasted_iota(jnp.int32, (LANES, LANES), 1)
    before = jnp.where(er < ec, 1.0, 0.0).astype(BF16)
    start = _dot(jnp.broadcast_to(padded, (SUBLANES, LANES)).astype(BF16), before)[0:1, :]
    pos = jnp.where(sel, start + rank, -1.0)
    pos_a, pos_b = _row_slots(pos, sel)
    slot = lax.broadcasted_iota(jnp.int32, (tm, BLOCK_ROWS), 1).astype(F32)
    pick = jnp.where(jnp.logical_or(slot == pos_a, slot == pos_b), 1.0, 0.0).astype(BF16)
    xb = lax.dot_general(pick, h, (((0,), (0,)), ((), ())), preferred_element_type=F32).astype(BF16)
    return xb, pos, count.astype(jnp.int32)


ROUTER_BLOCKS = 4


def _router_kernel(x_ref, g_ref, sh_ref, sc_ref, wr_ref, br_ref, xb_ref, pos_ref, gate_ref, cnt_ref):
    h = _modulate(x_ref[...], g_ref[...], sc_ref[0], sh_ref[0]).astype(BF16)
    gate, sel = _route_rows(h, wr_ref[...].astype(BF16), br_ref[...])
    gate_ref[...] = gate
    for k in range(ROUTER_BLOCKS):
        rows = slice(k * MOE_BLOCK, (k + 1) * MOE_BLOCK)
        xb, pos, count = _sort_block(h[rows, :], sel[rows, :])
        xb_ref[k * BLOCK_ROWS:(k + 1) * BLOCK_ROWS, :] = xb
        pos_ref[rows, :] = pos
        cnt_ref[k] = count


def _router(x, g, mods, layer, wr, br):
    tm = ROUTER_BLOCKS * MOE_BLOCK
    return pl.pallas_call(
        _router_kernel,
        grid=(N_TOK // tm,),
        in_specs=[
            pl.BlockSpec((tm, D_MODEL), lambda i: (i, 0)),
            pl.BlockSpec((1, D_MODEL), lambda i: (0, 0)),
            pl.BlockSpec((1, 1, D_MODEL), _mod_index(layer, 3, tm)),
            pl.BlockSpec((1, 1, D_MODEL), _mod_index(layer, 4, tm)),
            pl.BlockSpec((D_MODEL, LANES), lambda i: (0, 0)),
            pl.BlockSpec((1, LANES), lambda i: (0, 0)),
        ],
        out_specs=[
            pl.BlockSpec((ROUTER_BLOCKS * BLOCK_ROWS, D_MODEL), lambda i: (i, 0)),
            pl.BlockSpec((tm, LANES), lambda i: (i, 0)),
            pl.BlockSpec((tm, LANES), lambda i: (i, 0)),
            pl.BlockSpec((ROUTER_BLOCKS, 1, LANES), lambda i: (i, 0, 0)),
        ],
        out_shape=[
            jax.ShapeDtypeStruct((N_BLOCKS * BLOCK_ROWS, D_MODEL), BF16),
            jax.ShapeDtypeStruct((N_TOK, LANES), F32),
            jax.ShapeDtypeStruct((N_TOK, LANES), F32),
            jax.ShapeDtypeStruct((N_BLOCKS, 1, LANES), jnp.int32),
        ],
        compiler_params=_cparams(("arbitrary",)),
        name="moe_router",
    )(x, g.reshape(1, D_MODEL), mods, mods, wr, br)


TABLE_ROWS = -(-(N_TILES * TILE_CHUNKS) // LANES)


def _tables_kernel(cnt_ref, tend_ref, disp_ref, comb_ref):
    ne = N_EXPERTS
    cnt = jnp.concatenate([cnt_ref[...].astype(F32), jnp.zeros((LANES - N_BLOCKS, LANES), F32)], axis=0)
    m = jnp.ceil(cnt * (1.0 / CHUNK))
    r = lax.broadcasted_iota(jnp.int32, (LANES, LANES), 0)
    c = lax.broadcasted_iota(jnp.int32, (LANES, LANES), 1)
    mb = m.astype(BF16)
    bstart = _dot(mb, jnp.where(r < c, 1.0, 0.0).astype(BF16))
    cm = _dot(jnp.where(c < r, 1.0, 0.0).astype(BF16), mb)
    total = jnp.sum(m, axis=0, keepdims=True)
    tiles = jnp.ceil(total * (1.0 / TILE_CHUNKS))
    tile_end = _dot(jnp.broadcast_to(tiles, (SUBLANES, LANES)).astype(BF16),
                    jnp.where(r <= c, 1.0, 0.0).astype(BF16))[0:1, :]
    choff = (tile_end - tiles) * TILE_CHUNKS
    tend_ref[...] = tile_end.astype(jnp.int32)

    start_t = (choff + cm).T
    m_t = m.T
    bstart_t = bstart.T
    lane = lax.broadcasted_iota(jnp.int32, (ne, LANES), 1).astype(F32)
    acc = [jnp.zeros((ne, LANES), F32) for _ in range(TABLE_ROWS)]
    for b in range(N_BLOCKS):
        lo = start_t[0:ne, b:b + 1]
        hi = lo + m_t[0:ne, b:b + 1]
        bs = bstart_t[0:ne, b:b + 1]
        fwd = (b * BLOCK_CHUNKS) + bs - lo
        for j in range(TABLE_ROWS):
            g = lane + float(j * LANES)
            acc[j] = acc[j] + jnp.where(g >= lo, jnp.where(g < hi, fwd + g, 0.0), 0.0)
        back = jnp.where(lane >= bs, jnp.where(lane < hi - lo + bs, (lo - bs + lane) * CHUNK + 1.0, 0.0), 0.0)
        comb_ref[b:b + 1, :] = (jnp.sum(back, axis=0, keepdims=True) - 1.0).astype(jnp.int32)
    for j in range(TABLE_ROWS):
        disp_ref[j:j + 1, :] = (jnp.sum(acc[j], axis=0, keepdims=True) * CHUNK).astype(jnp.int32)


def _chunk_tables(counts):
    return pl.pallas_call(
        _tables_kernel,
        out_shape=[
            jax.ShapeDtypeStruct((1, LANES), jnp.int32),
            jax.ShapeDtypeStruct((TABLE_ROWS, LANES), jnp.int32),
            jax.ShapeDtypeStruct((N_BLOCKS, LANES), jnp.int32),
        ],
        name="moe_chunk_tables",
    )(counts.reshape(N_BLOCKS, LANES))


def _chunk_copies(read_row, n_chunks, src_hbm, dst, sem, wait, skip_negative=False, priority=0):
    for c in range(n_chunks):
        row = read_row(c)

        def copy(row=row, c=c):
            cp = pltpu.make_async_copy(src_hbm.at[pl.ds(pl.multiple_of(row, CHUNK), CHUNK)],
                                       dst.at[pl.ds(c * CHUNK, CHUNK)], sem)
            if wait:
                cp.wait()
            else:
                cp.start(priority=priority)

        if skip_negative:
            pl.when(row >= 0)(copy)
        else:
            copy()


IN_DEPTH = 3


def _expert_kernel(tend_ref, src_ref, xb_hbm, w1_ref, w3_ref, w2_ref, y_hbm,
                   xbuf, ybuf, zbuf, w13_scr, w2_scr, in_sem, out_sem, zero_sem):
    e = pl.program_id(0)
    first = jnp.where(e == 0, 0, tend_ref[0, jnp.maximum(e - 1, 0)])
    last = tend_ref[0, e]
    n_used = tend_ref[0, N_EXPERTS - 1]
    tiles_per_row = LANES // TILE_CHUNKS

    def fetch(tile, s, wait):
        row = tile // tiles_per_row
        col = (tile % tiles_per_row) * TILE_CHUNKS
        _chunk_copies(lambda c: src_ref[row, col + c], TILE_CHUNKS, xb_hbm, xbuf.at[s], in_sem.at[s], wait,
                      priority=1)

    def store(tile, s):
        rows = pl.ds(pl.multiple_of(tile * EXPERT_TILE, EXPERT_TILE), EXPERT_TILE)
        return pltpu.make_async_copy(ybuf.at[s], y_hbm.at[rows], out_sem.at[s])

    def zero_fill(tile):
        rows = pl.ds(pl.multiple_of(tile * EXPERT_TILE, EXPERT_TILE), EXPERT_TILE)
        return pltpu.make_async_copy(zbuf, y_hbm.at[rows], zero_sem.at[0])

    @pl.when(e == 0)
    def _():
        zbuf[...] = jnp.zeros_like(zbuf)

        def start(t, carry):
            zero_fill(t).start()
            return carry

        lax.fori_loop(n_used, N_TILES, start, 0)

    lead = IN_DEPTH - 1

    @pl.when(last > first)
    def _():
        @pl.when(first == 0)
        def _():
            for k in range(lead):
                fetch(jnp.minimum(k, n_used - 1), k, False)

        w13_scr[:, 0:D_EXPERT] = w1_ref[0, 0].astype(BF16)
        w13_scr[:, D_EXPERT:2 * D_EXPERT] = w3_ref[0, 0].astype(BF16)
        w2_scr[...] = w2_ref[0, 0].astype(BF16)

    def tile_body(t, carry):
        s = t % 2
        fetch(t, t % IN_DEPTH, True)
        ab = _dot(xbuf[t % IN_DEPTH], w13_scr[...])
        fetch(jnp.minimum(t + lead, n_used - 1), (t + lead) % IN_DEPTH, False)
        a = ab[:, 0:D_EXPERT]
        hid = (a * jax.nn.sigmoid(a)) * ab[:, D_EXPERT:2 * D_EXPERT]
        y = _dot(hid.astype(BF16), w2_scr[...]).astype(BF16)

        @pl.when(t >= 2)
        def _():
            store(t, s).wait()

        ybuf[s] = y
        store(t, s).start()
        return carry

    lax.fori_loop(first, last, tile_body, 0)

    @pl.when(e == N_EXPERTS - 1)
    def _():
        for k in range(lead):
            fetch(n_used - 1, (n_used + k) % IN_DEPTH, True)

        @pl.when(n_used >= 1)
        def _():
            store(0, (n_used - 1) % 2).wait()

        @pl.when(n_used >= 2)
        def _():
            store(0, n_used % 2).wait()

        def drain(t, carry):
            zero_fill(t).wait()
            return carry

        lax.fori_loop(n_used, N_TILES, drain, 0)


def _experts(tile_end, disp_src, xb, w1, w3, w2, layer):
    wsel = lambda e, tend, src: (layer, e, 0, 0)
    return pl.pallas_call(
        _expert_kernel,
        grid_spec=pltpu.PrefetchScalarGridSpec(
            num_scalar_prefetch=2,
            grid=(N_EXPERTS,),
            in_specs=[
                pl.BlockSpec(memory_space=pl.ANY),
                pl.BlockSpec((1, 1, D_MODEL, D_EXPERT), wsel),
                pl.BlockSpec((1, 1, D_MODEL, D_EXPERT), wsel),
                pl.BlockSpec((1, 1, D_EXPERT, D_MODEL), wsel),
            ],
            out_specs=pl.BlockSpec(memory_space=pl.ANY),
            scratch_shapes=[pltpu.VMEM((IN_DEPTH, EXPERT_TILE, D_MODEL), BF16),
                            pltpu.VMEM((2, EXPERT_TILE, D_MODEL), BF16),
                            pltpu.VMEM((EXPERT_TILE, D_MODEL), BF16),
                            pltpu.VMEM((D_MODEL, 2 * D_EXPERT), BF16),
                            pltpu.VMEM((D_EXPERT, D_MODEL), BF16),
                            pltpu.SemaphoreType.DMA((IN_DEPTH,)),
                            pltpu.SemaphoreType.DMA((2,)),
                            pltpu.SemaphoreType.DMA((1,))],
        ),
        out_shape=jax.ShapeDtypeStruct((N_TILES * EXPERT_TILE, D_MODEL), BF16),
        compiler_params=_cparams(("arbitrary",)),
        name="moe_experts",
    )(tile_end, disp_src, xb, w1, w3, w2)


def _combine_kernel(src_ref, y_hbm, x_ref, pos_ref, gate_ref, mg_ref, *rest, tm, final):
    if final:
        fg_ref, op_ref, os_ref, ybuf, sem = rest
    else:
        o_ref, ybuf, sem = rest
    i = pl.program_id(0)
    slot = i % 2

    def fetch(blk, s, wait):
        _chunk_copies(lambda c: src_ref[blk, c], BLOCK_CHUNKS, y_hbm, ybuf.at[s], sem.at[s], wait,
                      skip_negative=True)

    @pl.when(i == 0)
    def _():
        ybuf[...] = jnp.zeros_like(ybuf)
        fetch(0, 0, False)

    @pl.when(i + 1 < N_BLOCKS)
    def _():
        fetch(i + 1, 1 - slot, False)

    fetch(i, slot, True)
    pos = pos_ref[...]
    sel = pos >= 0.0
    gate = gate_ref[...]
    pos_a, pos_b = _row_slots(pos, sel)
    gate_a = jnp.sum(jnp.where(pos == pos_a, gate, 0.0), axis=-1, keepdims=True)
    gate_b = jnp.sum(jnp.where(jnp.logical_and(sel, pos == pos_b), gate, 0.0), axis=-1, keepdims=True)
    row = lax.broadcasted_iota(jnp.int32, (tm, BLOCK_ROWS), 1).astype(F32)
    weights = jnp.where(row == pos_a, gate_a, jnp.where(row == pos_b, gate_b, 0.0))
    moe = _dot(weights.astype(BF16), ybuf[slot])
    out = x_ref[...] + mg_ref[0] * moe
    if not final:
        o_ref[...] = out
        return

    ms = jnp.mean(out * out, axis=-1, keepdims=True)
    out = (out * lax.rsqrt(ms + EPS)) * fg_ref[...]

    @pl.when(i < N_PROMPT // tm)
    def _():
        op_ref[...] = out

    @pl.when(i >= N_PROMPT // tm)
    def _():
        os_ref[...] = out


def _combine(comb_src, y, x, pos, gate, mods, layer, final_g=None):
    tm = MOE_BLOCK
    final = final_g is not None
    np_blocks = N_PROMPT // tm

    def gate_idx(i, s):
        return ((layer * MOD_ROWS + _mod_row(i, tm)) * 6 + 5, 0, 0)

    in_specs = [
        pl.BlockSpec(memory_space=pl.ANY),
        pl.BlockSpec((tm, D_MODEL), lambda i, s: (i, 0)),
        pl.BlockSpec((tm, LANES), lambda i, s: (i, 0)),
        pl.BlockSpec((tm, LANES), lambda i, s: (i, 0)),
        pl.BlockSpec((1, 1, D_MODEL), gate_idx),
    ]
    args = [comb_src, y, x, pos, gate, mods]
    if final:
        in_specs.append(pl.BlockSpec((1, D_MODEL), lambda i, s: (0, 0)))
        args.append(final_g.reshape(1, D_MODEL))
        out_specs = [pl.BlockSpec((tm, D_MODEL), lambda i, s: (jnp.minimum(i, np_blocks - 1), 0)),
                     pl.BlockSpec((tm, D_MODEL), lambda i, s: (jnp.maximum(i - np_blocks, 0), 0))]
        out_shape = [jax.ShapeDtypeStruct((N_PROMPT, D_MODEL), F32),
                     jax.ShapeDtypeStruct((N_SAMPLE, D_MODEL), F32)]
    else:
        out_specs = pl.BlockSpec((tm, D_MODEL), lambda i, s: (i, 0))
        out_shape = jax.ShapeDtypeStruct((N_TOK, D_MODEL), F32)
    return pl.pallas_call(
        functools.partial(_combine_kernel, tm=tm, final=final),
        grid_spec=pltpu.PrefetchScalarGridSpec(
            num_scalar_prefetch=1,
            grid=(N_BLOCKS,),
            in_specs=in_specs,
            out_specs=out_specs,
            scratch_shapes=[pltpu.VMEM((2, BLOCK_ROWS, D_MODEL), BF16),
                            pltpu.SemaphoreType.DMA((2,))],
        ),
        out_shape=out_shape,
        compiler_params=_cparams(("arbitrary",)),
        name="moe_combine_final" if final else "moe_combine",
    )(*args)


def _moe(x, g, mods, layer, w_group, b_group, w_expert, b_expert, w1, w3, w2, final_g=None):
    pad = LANES - N_EXPERTS - N_GROUPS
    wr = jnp.concatenate([w_expert, w_group, jnp.zeros((D_MODEL, pad), F32)], axis=1)
    br = jnp.concatenate([b_expert, b_group, jnp.zeros((pad,), F32)]).reshape(1, LANES)
    xb, pos, gate, counts = _router(x, g, mods, layer, wr, br)
    tile_end, disp_src, comb_src = _chunk_tables(counts)
    y = _experts(tile_end, disp_src, xb, w1, w3, w2, layer)
    return _combine(comb_src, y, x, pos, gate, mods, layer, final_g)


def kernel(x_prompt, x_sample, cache_k, cache_v, c, c_ctx, ada_w, ada_b, norm1_g, norm2_g, final_g, hy_w_in, hy_conv_w, hy_conv_b, hy_f_w1, hy_f_b1, hy_f_w2, hy_f_b2, hy_f_w3, hy_f_b3, hy_f_freq, hy_bias, hy_w_out, at_w_qkv, at_lam_q1, at_lam_k1, at_lam_q2, at_lam_k2, at_subln_g, at_w_o, moe_w_group, moe_b_group, moe_w_expert, moe_b_expert, moe_w1, moe_w3, moe_w2):
    x = (x_prompt.reshape(N_PROMPT, D_MODEL), x_sample.reshape(N_SAMPLE, D_MODEL))
    cond = jnp.concatenate([c_ctx[None, :], c, jnp.zeros((MOD_ROWS - 1 - DEC_BATCH, D_MODEL), F32)], axis=0)
    mods = _ada_table(cond, ada_w, ada_b).reshape(DEPTH * MOD_ROWS * 6, 1, D_MODEL)
    ctx_k = cache_k.reshape(DEC_BATCH, DEPTH // 2, PAST_LEN, D_MODEL)
    ctx_v = cache_v.reshape(DEC_BATCH, DEPTH // 2, PAST_LEN, D_MODEL)

    dft = {}
    for L in (SEQ, DEC_SEQ):
        fwd, inv = _dft_mats(L)
        dft[L] = (jnp.asarray(fwd).astype(BF16), jnp.asarray(inv).astype(BF16))

    new_k, new_v = [], []
    for l in range(DEPTH):
        j = l // 2
        if l % 2 == 0:
            z = _norm_matmul(x, norm1_g[l], mods, l, 0, 1, hy_w_in, j)
            ys = []
            for L, rb0, n_seq, dt in ((SEQ, 0, BATCH, 1024), (DEC_SEQ, N_PROMPT // DEC_SEQ, DEC_BATCH, 512)):
                f_bf, finv = dft[L]
                kc = _filter_spectra(L, hy_f_w1[j], hy_f_b1[j], hy_f_w2[j], hy_f_b2[j], hy_f_w3[j],
                                     hy_f_b3[j], hy_f_freq[j], f_bf)
                ys.append(_hyena_conv(z, hy_conv_w[j], hy_conv_b[j], f_bf, finv, kc, hy_bias[j],
                                      L=L, row_block0=rb0, n_seq=n_seq, dt=dt))
            x = _matmul_residual((ys[0], ys[1]), hy_w_out, j, x, mods, l, 2)
        else:
            lam_init = 0.8 - 0.6 * math.exp(-0.3 * l)
            qkv = _norm_matmul(x, norm1_g[l], mods, l, 0, 1, at_w_qkv, j)
            lam_params = (at_lam_q1[j], at_lam_k1[j], at_lam_q2[j], at_lam_k2[j])
            op = _attention(qkv, lam_params, at_subln_g[j], lam_init, L=SEQ, row_block0=0,
                            n_seq=BATCH, tq=SEQ)
            os_ = _attention(qkv, lam_params, at_subln_g[j], lam_init, L=DEC_SEQ,
                             row_block0=N_PROMPT // DEC_SEQ, n_seq=DEC_BATCH, tq=256,
                             ctx=(ctx_k, ctx_v), j=j)
            new_k.append(qkv[:N_PROMPT, D_MODEL:2 * D_MODEL].astype(F32).reshape(BATCH, SEQ, N_HEADS, HEAD_W))
            new_v.append(qkv[:N_PROMPT, 2 * D_MODEL:].astype(F32).reshape(BATCH, SEQ, N_HEADS, HEAD_W))
            x = _matmul_residual((op, os_), at_w_o, j, x, mods, l, 2)
        x = _moe(x, norm2_g[l], mods, l, moe_w_group[l], moe_b_group[l], moe_w_expert[l],
                 moe_b_expert[l], moe_w1, moe_w3, moe_w2, final_g if l == DEPTH - 1 else None)
    y_prompt = x[0].reshape(BATCH, SEQ, D_MODEL)
    y_sample = x[1].reshape(DEC_BATCH, DEC_SEQ, D_MODEL)
    return (y_prompt, y_sample, jnp.stack(new_k, axis=1), jnp.stack(new_v, axis=1))
```

```python
import functools
import math

import numpy as np
import jax
import jax.numpy as jnp
from jax import lax
from jax.experimental import pallas as pl
from jax.experimental.pallas import tpu as pltpu

F32 = jnp.float32
BF16 = jnp.bfloat16

D_MODEL = 1024
BATCH = 16
SEQ = 256
DEPTH = 4
DEC_BATCH = 4
DEC_SEQ = 1024
PAST_LEN = 256
GRID_W = 64
EPS = 1e-6
HY_BANDS = 16
HY_EMB = 1 + 2 * HY_BANDS
HY_FILTER_HIDDEN = 64
HY_FAST_DECAY = 0.3
HY_SLOW_DECAY = 1.5
HY_TARGET = 1e-2
N_HEADS = 8
HEAD_DIM = 64
ROT_AXIS = HEAD_DIM // 2
ROPE_BASE = 10000.0
N_GROUPS = 4
EXPERTS_PER_GROUP = 8
N_EXPERTS = N_GROUPS * EXPERTS_PER_GROUP
D_EXPERT = D_MODEL // 4

N_PROMPT = BATCH * SEQ
N_SAMPLE = DEC_BATCH * DEC_SEQ
N_TOK = N_PROMPT + N_SAMPLE
MOD_ROWS = 8
LANES = 128
SUBLANES = 8
HEAD_W = 2 * HEAD_DIM
VMEM_LIMIT = 56 * 1024 * 1024

MOE_BLOCK = 256
N_BLOCKS = N_TOK // MOE_BLOCK
CHUNK = 2 * SUBLANES
BLOCK_ROWS = 2 * MOE_BLOCK + N_EXPERTS * CHUNK
BLOCK_CHUNKS = BLOCK_ROWS // CHUNK
EXPERT_TILE = 256
TILE_CHUNKS = EXPERT_TILE // CHUNK
N_TILES = (2 * N_TOK + N_BLOCKS * N_EXPERTS * (CHUNK - 1)) // EXPERT_TILE + N_EXPERTS


def _cparams(sem):
    return pltpu.CompilerParams(dimension_semantics=sem, vmem_limit_bytes=VMEM_LIMIT)


def _mod_row(blk, tm):
    start = blk * tm
    return jnp.where(start < N_PROMPT, 0, 1 + (start - N_PROMPT) // DEC_SEQ)


def _mod_index(layer, which, tm):
    def index_map(i, *_):
        return ((layer * MOD_ROWS + _mod_row(i, tm)) * 6 + which, 0, 0)
    return index_map


@functools.lru_cache(maxsize=None)
def _dft_mats(L):
    k = np.arange(L, dtype=np.float64)[:, None]
    j = np.arange(L, dtype=np.float64)[None, :]
    ang = np.pi * ((k * j) % (2 * L)) / L
    c = np.cos(ang)
    s = np.sin(ang)
    s[0, :] = np.where(np.arange(L) % 2 == 0, 1.0, -1.0)
    fwd = np.concatenate([c, s], axis=0)
    scale = np.full((2 * L,), 1.0 / L)
    scale[0] = scale[L] = 0.5 / L
    inv = fwd.T * scale[None, :]
    return fwd.astype(np.float32), inv.astype(np.float32)


@functools.lru_cache(maxsize=None)
def _filter_feats(L):
    pos = np.arange(L, dtype=np.float64)
    t = pos / (L - 1)
    bands = np.linspace(1e-4, HY_BANDS - 1, HY_BANDS)
    ang = (2.0 * math.pi / L) * pos[:, None] * bands[None, :]
    feats = np.concatenate([t[:, None], np.cos(ang), -np.sin(ang)], axis=-1)
    feats = np.pad(feats, ((0, 0), (0, LANES - HY_EMB)))
    deltas = np.abs(np.linspace(math.log(HY_TARGET) / HY_SLOW_DECAY,
                                math.log(HY_TARGET) / HY_FAST_DECAY, D_MODEL))
    window = np.exp(-t[:, None] * deltas[None, :])
    alt = np.where(np.arange(L) % 2 == 0, 1.0, -1.0)[:, None] * np.ones((1, LANES))
    return feats.astype(np.float32), window.astype(np.float32), alt.astype(np.float32)


@functools.lru_cache(maxsize=None)
def _rope_tables():
    pos = np.arange(DEC_SEQ)
    row = (pos // GRID_W).astype(np.float64)
    col = (pos % GRID_W).astype(np.float64)
    lane = np.arange(HEAD_W)
    d = lane % HEAD_DIM
    axis = d // ROT_AXIS
    n = d % ROT_AXIS
    half = n // (ROT_AXIS // 2)
    f = n % (ROT_AXIS // 2)
    inv = ROPE_BASE ** (-f.astype(np.float64) / (ROT_AXIS // 2))
    p = np.where(axis[None, :] == 0, row[:, None], col[:, None])
    ang = p * inv[None, :]
    sign = np.where(half == 0, -1.0, 1.0)[None, :]
    return np.cos(ang).astype(np.float32), (np.sin(ang) * sign).astype(np.float32)


def _split_bf16(a):
    hi = a.astype(BF16)
    lo = (a - hi.astype(F32)).astype(BF16)
    return hi, lo


def _dot(a, b):
    return jnp.dot(a, b, preferred_element_type=F32)


def _dot3(a_hi, a_lo, b_hi, b_lo):
    return _dot(a_hi, b_hi) + (_dot(a_hi, b_lo) + _dot(a_lo, b_hi))


def _ada_kernel(c_ref, w_ref, b_ref, o_ref):
    c = c_ref[...]
    s = c * jax.nn.sigmoid(c)
    o_ref[0] = _dot(s.astype(BF16), w_ref[0].astype(BF16)) + b_ref[0]


def _ada_table(cond, ada_w, ada_b):
    tn = 1536
    n = 6 * D_MODEL
    return pl.pallas_call(
        _ada_kernel,
        grid=(DEPTH, n // tn),
        in_specs=[
            pl.BlockSpec((MOD_ROWS, D_MODEL), lambda l, j: (0, 0)),
            pl.BlockSpec((1, D_MODEL, tn), lambda l, j: (l, 0, j)),
            pl.BlockSpec((1, 1, tn), lambda l, j: (l, 0, j)),
        ],
        out_specs=pl.BlockSpec((1, MOD_ROWS, tn), lambda l, j: (l, 0, j)),
        out_shape=jax.ShapeDtypeStruct((DEPTH, MOD_ROWS, n), F32),
        compiler_params=_cparams(("arbitrary", "arbitrary")),
        name="ada_table",
    )(cond, ada_w, ada_b.reshape(DEPTH, 1, n))


def _modulate(x, g, scale, shift):
    ms = jnp.mean(x * x, axis=-1, keepdims=True)
    y = (x * lax.rsqrt(ms + EPS)) * g
    return y * (1.0 + scale) + shift


MM_COLS = 512


def _row_group_specs(x, tm, width):
    np_blocks = N_PROMPT // tm
    if isinstance(x, tuple):
        specs = [pl.BlockSpec((tm, width), lambda i, *_: (jnp.minimum(i, np_blocks - 1), 0)),
                 pl.BlockSpec((tm, width), lambda i, *_: (jnp.maximum(i - np_blocks, 0), 0))]
        return specs, list(x)
    specs = [pl.BlockSpec((tm, width), lambda i, *_: (jnp.minimum(i, np_blocks - 1), 0)),
             pl.BlockSpec((tm, width), lambda i, *_: (jnp.maximum(i, np_blocks), 0))]
    return specs, [x, x]


def _pick_rows(i, tm, p_ref, s_ref):
    return jnp.where(i < N_PROMPT // tm, p_ref[...], s_ref[...])


def _normmm_kernel(xp_ref, xs_ref, g_ref, sh_ref, sc_ref, w_ref, o_ref, w_scr, *, tm):
    i = pl.program_id(0)

    @pl.when(i == 0)
    def _():
        w_scr[...] = w_ref[0].astype(BF16)

    x = _pick_rows(i, tm, xp_ref, xs_ref)
    h = _modulate(x, g_ref[...], sc_ref[0], sh_ref[0]).astype(BF16)
    for j in range(w_scr.shape[1] // MM_COLS):
        cols = slice(j * MM_COLS, (j + 1) * MM_COLS)
        o_ref[:, cols] = _dot(h, w_scr[:, cols]).astype(o_ref.dtype)


def _norm_matmul(x, g, mods, layer, which_shift, which_scale, w, widx, *, tm=512):
    n = w.shape[2]
    x_specs, x_args = _row_group_specs(x, tm, D_MODEL)
    return pl.pallas_call(
        functools.partial(_normmm_kernel, tm=tm),
        grid=(N_TOK // tm,),
        in_specs=x_specs + [
            pl.BlockSpec((1, D_MODEL), lambda i: (0, 0)),
            pl.BlockSpec((1, 1, D_MODEL), _mod_index(layer, which_shift, tm)),
            pl.BlockSpec((1, 1, D_MODEL), _mod_index(layer, which_scale, tm)),
            pl.BlockSpec((1, D_MODEL, n), lambda i: (widx, 0, 0), pipeline_mode=pl.Buffered(1)),
        ],
        out_specs=pl.BlockSpec((tm, n), lambda i: (i, 0)),
        out_shape=jax.ShapeDtypeStruct((N_TOK, n), BF16),
        scratch_shapes=[pltpu.VMEM((D_MODEL, n), BF16)],
        compiler_params=_cparams(("arbitrary",)),
        name="norm_matmul",
    )(*x_args, g.reshape(1, D_MODEL), mods, mods, w)


def _mmres_kernel(ap_ref, as_ref, xp_ref, xs_ref, w_ref, gate_ref, o_ref, w_scr, *, tm):
    i = pl.program_id(0)

    @pl.when(i == 0)
    def _():
        w_scr[...] = w_ref[0].astype(BF16)

    a = _pick_rows(i, tm, ap_ref, as_ref)
    x = _pick_rows(i, tm, xp_ref, xs_ref)
    gate = gate_ref[0]
    for j in range(w_scr.shape[1] // MM_COLS):
        cols = slice(j * MM_COLS, (j + 1) * MM_COLS)
        o_ref[:, cols] = x[:, cols] + gate[:, cols] * _dot(a, w_scr[:, cols])


def _matmul_residual(a, w, widx, x, mods, layer, which_gate, *, tm=512):
    n = w.shape[2]
    a_specs, a_args = _row_group_specs(a, tm, D_MODEL)
    x_specs, x_args = _row_group_specs(x, tm, n)
    return pl.pallas_call(
        functools.partial(_mmres_kernel, tm=tm),
        grid=(N_TOK // tm,),
        in_specs=a_specs + x_specs + [
            pl.BlockSpec((1, D_MODEL, n), lambda i: (widx, 0, 0), pipeline_mode=pl.Buffered(1)),
            pl.BlockSpec((1, 1, n), _mod_index(layer, which_gate, tm)),
        ],
        out_specs=pl.BlockSpec((tm, n), lambda i: (i, 0)),
        out_shape=jax.ShapeDtypeStruct((N_TOK, n), F32),
        scratch_shapes=[pltpu.VMEM((D_MODEL, n), BF16)],
        compiler_params=_cparams(("arbitrary",)),
        name="matmul_residual",
    )(*a_args, *x_args, w, mods)


def _filter_kernel(feats_ref, win_ref, alt_ref, w1_ref, b1_ref, w2_ref, b2_ref, fq_ref,
                   w3f_ref, b3f_ref, w3b_ref, b3b_ref, f_ref, o_ref, h_scr, *, L):
    def dense(a, w_ref, b_ref):
        a_hi, a_lo = _split_bf16(a)
        w_hi, w_lo = _split_bf16(w_ref[...])
        return _dot3(a_hi, a_lo, w_hi, w_lo) + b_ref[...]

    @pl.when(jnp.logical_and(pl.program_id(0) == 0, pl.program_id(1) == 0))
    def _():
        fq = fq_ref[...]
        h1 = jnp.sin(fq[0:1, :] * dense(feats_ref[...], w1_ref, b1_ref))
        h_scr[...] = jnp.sin(fq[1:2, :] * dense(h1, w2_ref, b2_ref))

    h = h_scr[...]
    win = win_ref[...]
    hf = dense(h, w3f_ref, b3f_ref) * win
    hb = dense(h, w3b_ref, b3b_ref) * win
    row = lax.broadcasted_iota(jnp.int32, hf.shape, 0)
    hb = jnp.where(row == 0, 0.0, hb)
    kr = _dot(f_ref[0:L, :], (hf + hb).astype(BF16))
    ks = _dot(f_ref[L:2 * L, :], (hf - hb).astype(BF16))
    k_nyq = jnp.sum((hf + hb) * alt_ref[:, 0:1], axis=0, keepdims=True)
    o_ref[0, 0] = kr
    o_ref[0, 1] = jnp.where(row == 0, k_nyq, kr)
    o_ref[0, 2] = jnp.where(row == 0, 0.0, ks)


def _filter_spectra(L, w1, b1, w2, b2, w3, b3, freq, f_bf, *, dt=512):
    feats, window, alt = _filter_feats(L)
    fh = HY_FILTER_HIDDEN
    nd = D_MODEL // dt
    w1p = jnp.pad(w1, ((0, LANES - HY_EMB), (0, 0)))
    const = lambda shape: pl.BlockSpec(shape, lambda o, c: tuple(0 for _ in shape))
    return pl.pallas_call(
        functools.partial(_filter_kernel, L=L),
        grid=(2, nd),
        in_specs=[
            const((L, LANES)),
            pl.BlockSpec((L, dt), lambda o, c: (0, c)),
            const((L, LANES)),
            const((LANES, fh)), const((1, fh)), const((fh, fh)), const((1, fh)), const((2, fh)),
            pl.BlockSpec((fh, dt), lambda o, c: (0, (2 * o) * nd + c)),
            pl.BlockSpec((1, dt), lambda o, c: (0, (2 * o) * nd + c)),
            pl.BlockSpec((fh, dt), lambda o, c: (0, (2 * o + 1) * nd + c)),
            pl.BlockSpec((1, dt), lambda o, c: (0, (2 * o + 1) * nd + c)),
            const((2 * L, L)),
        ],
        out_specs=pl.BlockSpec((1, 3, L, dt), lambda o, c: (o, 0, 0, c)),
        out_shape=jax.ShapeDtypeStruct((2, 3, L, D_MODEL), F32),
        scratch_shapes=[pltpu.VMEM((L, fh), F32)],
        compiler_params=_cparams(("arbitrary", "arbitrary")),
        name=f"hyena_filter_spectra_{L}",
    )(jnp.asarray(feats), jnp.asarray(window), jnp.asarray(alt), w1p, b1.reshape(1, fh), w2,
      b2.reshape(1, fh), freq, w3, b3.reshape(1, -1), w3, b3.reshape(1, -1), f_bf)


def _conv_kernel(v_ref, x1_ref, x2_ref, cwv_ref, cw1_ref, cw2_ref, cbv_ref, cb1_ref, cb2_ref,
                 f_ref, finv_ref, kc_ref, bias_ref, o_ref, *, L):
    row = lax.broadcasted_iota(jnp.int32, v_ref.shape, 0)

    def short_conv(u_ref, w_ref, b_ref):
        u = u_ref[...].astype(F32)
        prev = jnp.where(row == 0, 0.0, pltpu.roll(u, 1, 0))
        nxt = jnp.where(row == L - 1, 0.0, pltpu.roll(u, L - 1, 0))
        w = w_ref[...]
        return prev * w[0:1, :] + u * w[1:2, :] + nxt * w[2:3, :] + b_ref[...]

    def long_conv(u, order):
        spec = _dot(f_ref[...], u.astype(BF16))
        a = spec[0:L, :]
        b = spec[L:2 * L, :]
        kra = kc_ref[order, 0]
        krb = kc_ref[order, 1]
        ks = kc_ref[order, 2]
        prod = jnp.concatenate([a * kra - b * ks, a * ks + b * krb], axis=0)
        return _dot(finv_ref[...], prod.astype(BF16)) + u * bias_ref[order:order + 1, :]

    v = short_conv(v_ref, cwv_ref, cbv_ref)
    y = short_conv(x1_ref, cw1_ref, cb1_ref) * long_conv(v, 0)
    y = short_conv(x2_ref, cw2_ref, cb2_ref) * long_conv(y, 1)
    o_ref[...] = y.astype(o_ref.dtype)


def _hyena_conv(z, conv_w, conv_b, f_bf, finv_bf, kc, bias, *, L, row_block0, n_seq, dt):
    nd = D_MODEL // dt
    cb = conv_b.reshape(1, 3 * D_MODEL)
    seg = lambda s: pl.BlockSpec((L, dt), lambda c, b: (row_block0 + b, s * nd + c))
    cw = lambda s: pl.BlockSpec((3, dt), lambda c, b: (0, s * nd + c))
    cbs = lambda s: pl.BlockSpec((1, dt), lambda c, b: (0, s * nd + c))
    return pl.pallas_call(
        functools.partial(_conv_kernel, L=L),
        grid=(nd, n_seq),
        in_specs=[
            seg(0), seg(1), seg(2), cw(0), cw(1), cw(2), cbs(0), cbs(1), cbs(2),
            pl.BlockSpec((2 * L, L), lambda c, b: (0, 0), pipeline_mode=pl.Buffered(1)),
            pl.BlockSpec((L, 2 * L), lambda c, b: (0, 0), pipeline_mode=pl.Buffered(1)),
            pl.BlockSpec((2, 3, L, dt), lambda c, b: (0, 0, 0, c), pipeline_mode=pl.Buffered(1)),
            pl.BlockSpec((2, dt), lambda c, b: (0, c)),
        ],
        out_specs=pl.BlockSpec((L, dt), lambda c, b: (b, c)),
        out_shape=jax.ShapeDtypeStruct((n_seq * L, D_MODEL), BF16),
        compiler_params=_cparams(("arbitrary", "arbitrary")),
        name=f"hyena_conv_{L}",
    )(z, z, z, conv_w, conv_w, conv_w, cb, cb, cb, f_bf, finv_bf, kc, bias)


def _attn_kernel(lq1_ref, lk1_ref, lq2_ref, lk2_ref, g_ref, q_ref, k_ref, v_ref, *rest,
                 lam_init, has_ctx, tq, nq):
    if has_ctx:
        ck_ref, cv_ref, cos_ref, sin_ref, o_ref, k_scr = rest
    else:
        (o_ref,) = rest
    lam = (jnp.exp(jnp.sum(lq1_ref[...] * lk1_ref[...], axis=-1, keepdims=True))
           - jnp.exp(jnp.sum(lq2_ref[...] * lk2_ref[...], axis=-1, keepdims=True)) + lam_init)
    lane = lax.broadcasted_iota(jnp.int32, (tq, HEAD_W), 1)
    first_map = lane < HEAD_DIM
    first_half = (lane % ROT_AXIS) < (ROT_AXIS // 2)
    nt = (((1,), (1,)), ((), ()))

    def rotary(x, rows):
        x = x.astype(F32)
        partner = jnp.where(first_half, pltpu.roll(x, HEAD_W - ROT_AXIS // 2, 1),
                            pltpu.roll(x, ROT_AXIS // 2, 1))
        return (x * cos_ref[rows, :] + partner * sin_ref[rows, :]).astype(BF16)

    if has_ctx:
        for r in range(k_ref.shape[0] // tq):
            rows = slice(r * tq, (r + 1) * tq)
            for h in range(N_HEADS):
                cols = slice(h * HEAD_W, (h + 1) * HEAD_W)
                k_scr[rows, cols] = rotary(k_ref[rows, cols], rows)
        keys = k_scr
    else:
        keys = k_ref

    def q_block(rows):
        for h in range(N_HEADS):
            cols = slice(h * HEAD_W, (h + 1) * HEAD_W)
            q = q_ref[rows, cols]
            if has_ctx:
                q = rotary(q, rows)
            q = q * (HEAD_DIM ** -0.5)
            zero = jnp.zeros_like(q)
            q2 = jnp.concatenate([jnp.where(first_map, q, zero), jnp.where(first_map, zero, q)], axis=0)
            s = lax.dot_general(q2, keys[:, cols], nt, preferred_element_type=F32)
            m = jnp.max(s, axis=-1, keepdims=True)
            if has_ctx:
                sc = lax.dot_general(q2, ck_ref[0, 0, :, cols].astype(BF16), nt,
                                     preferred_element_type=F32)
                m = jnp.maximum(m, jnp.max(sc, axis=-1, keepdims=True))
                ec = jnp.exp(sc - m)
            e = jnp.exp(s - m)
            den = jnp.sum(e, axis=-1, keepdims=True)
            if has_ctx:
                den = den + jnp.sum(ec, axis=-1, keepdims=True)
            ov = _dot(e.astype(BF16), v_ref[:, cols])
            if has_ctx:
                ov = ov + _dot(ec.astype(BF16), cv_ref[0, 0, :, cols].astype(BF16))
            inv = 1.0 / den
            o = ov[0:tq] * inv[0:tq] - ov[tq:2 * tq] * (lam * inv[tq:2 * tq])
            ms = jnp.mean(o * o, axis=-1, keepdims=True)
            o = (o * lax.rsqrt(ms + EPS)) * g_ref[...] * (1.0 - lam_init)
            o_ref[rows, cols] = o.astype(o_ref.dtype)

    if nq == 1:
        q_block(slice(0, tq))
    else:
        def body(qi, carry):
            q_block(pl.ds(pl.multiple_of(qi * tq, tq), tq))
            return carry
        lax.fori_loop(0, nq, body, 0)


def _attention(qkv, lam_params, subln_g, lam_init, *, L, row_block0, n_seq, tq, ctx=None, j=0):
    small = pl.BlockSpec((1, HEAD_DIM), lambda b: (0, 0))
    in_specs = [small, small, small, small,
                pl.BlockSpec((1, HEAD_W), lambda b: (0, 0)),
                pl.BlockSpec((L, D_MODEL), lambda b: (row_block0 + b, 0)),
                pl.BlockSpec((L, D_MODEL), lambda b: (row_block0 + b, 1)),
                pl.BlockSpec((L, D_MODEL), lambda b: (row_block0 + b, 2))]
    args = [p.reshape(1, HEAD_DIM) for p in lam_params] + [subln_g.reshape(1, HEAD_W), qkv, qkv, qkv]
    scratch = []
    if ctx is not None:
        ctx_spec = pl.BlockSpec((1, 1, PAST_LEN, D_MODEL), lambda b: (b, j, 0, 0))
        rope_spec = pl.BlockSpec((L, HEAD_W), lambda b: (0, 0))
        cos, sin = _rope_tables()
        in_specs += [ctx_spec, ctx_spec, rope_spec, rope_spec]
        args += list(ctx) + [jnp.asarray(cos), jnp.asarray(sin)]
        scratch = [pltpu.VMEM((L, D_MODEL), BF16)]
    return pl.pallas_call(
        functools.partial(_attn_kernel, lam_init=lam_init, has_ctx=ctx is not None, tq=tq, nq=L // tq),
        grid=(n_seq,),
        in_specs=in_specs,
        out_specs=pl.BlockSpec((L, D_MODEL), lambda b: (b, 0)),
        out_shape=jax.ShapeDtypeStruct((n_seq * L, D_MODEL), BF16),
        scratch_shapes=scratch,
        compiler_params=_cparams(("arbitrary",)),
        name="diff_attention_ctx" if ctx is not None else "diff_attention",
    )(*args)


def _row_slots(pos, sel):
    big = float(BLOCK_ROWS)
    pos_a = jnp.max(pos, axis=-1, keepdims=True)
    pos_b = jnp.min(jnp.where(sel, pos, big), axis=-1, keepdims=True)
    return pos_a, pos_b


def _route_rows(h, wr, br):
    tm = h.shape[0]
    logits = _dot(h, wr) + br
    lane = lax.broadcasted_iota(jnp.int32, (tm, LANES), 1)
    neg = -jnp.inf
    is_grp = jnp.logical_and(lane >= N_EXPERTS, lane < N_EXPERTS + N_GROUPS)
    lg = jnp.where(is_grp, logits, neg)
    mg = jnp.max(lg, axis=-1, keepdims=True)
    g_gate = 1.0 / jnp.sum(jnp.exp(lg - mg), axis=-1, keepdims=True)
    g_idx = jnp.min(jnp.where(lg == mg, lane - N_EXPERTS, N_GROUPS), axis=-1, keepdims=True)
    in_grp = jnp.logical_and(lane < N_EXPERTS, lane // EXPERTS_PER_GROUP == g_idx)
    le = jnp.where(in_grp, logits, neg)
    m1 = jnp.max(le, axis=-1, keepdims=True)
    i1 = jnp.min(jnp.where(le == m1, lane, LANES), axis=-1, keepdims=True)
    le2 = jnp.where(lane == i1, neg, le)
    m2 = jnp.max(le2, axis=-1, keepdims=True)
    i2 = jnp.min(jnp.where(le2 == m2, lane, LANES), axis=-1, keepdims=True)
    e2 = jnp.exp(m2 - m1)
    inv = 1.0 / (1.0 + e2)
    gate = jnp.where(lane == i1, g_gate * inv, jnp.where(lane == i2, g_gate * (e2 * inv), 0.0))
    return gate, jnp.where(jnp.logical_or(lane == i1, lane == i2), 1.0, 0.0)


def _sort_block(h, onehot):
    tm = MOE_BLOCK
    sel = onehot > 0.5
    r = lax.broadcasted_iota(jnp.int32, (tm, tm), 0)
    c = lax.broadcasted_iota(jnp.int32, (tm, tm), 1)
    rank = _dot(jnp.where(c < r, 1.0, 0.0).astype(BF16), onehot.astype(BF16))
    count = jnp.sum(onehot, axis=0, keepdims=True)
    padded = jnp.ceil(count * (1.0 / CHUNK)) * CHUNK
    er = lax.broadcasted_iota(jnp.int32, (LANES, LANES), 0)
    ec = lax.broadcasted_iota(jnp.int32, (LANES, LANES), 1)
    before = jnp.where(er < ec, 1.0, 0.0).astype(BF16)
    start = _dot(jnp.broadcast_to(padded, (SUBLANES, LANES)).astype(BF16), before)[0:1, :]
    pos = jnp.where(sel, start + rank, -1.0)
    pos_a, pos_b = _row_slots(pos, sel)
    slot = lax.broadcasted_iota(jnp.int32, (tm, BLOCK_ROWS), 1).astype(F32)
    pick = jnp.where(jnp.logical_or(slot == pos_a, slot == pos_b), 1.0, 0.0).astype(BF16)
    xb = lax.dot_general(pick, h, (((0,), (0,)), ((), ())), preferred_element_type=F32).astype(BF16)
    return xb, pos, count.astype(jnp.int32)


ROUTER_BLOCKS = 4


def _router_kernel(x_ref, g_ref, sh_ref, sc_ref, wr_ref, br_ref, xb_ref, pos_ref, gate_ref, cnt_ref):
    h = _modulate(x_ref[...], g_ref[...], sc_ref[0], sh_ref[0]).astype(BF16)
    gate, sel = _route_rows(h, wr_ref[...].astype(BF16), br_ref[...])
    gate_ref[...] = gate
    for k in range(ROUTER_BLOCKS):
        rows = slice(k * MOE_BLOCK, (k + 1) * MOE_BLOCK)
        xb, pos, count = _sort_block(h[rows, :], sel[rows, :])
        xb_ref[k * BLOCK_ROWS:(k + 1) * BLOCK_ROWS, :] = xb
        pos_ref[rows, :] = pos
        cnt_ref[k] = count


def _router(x, g, mods, layer, wr, br):
    tm = ROUTER_BLOCKS * MOE_BLOCK
    return pl.pallas_call(
        _router_kernel,
        grid=(N_TOK // tm,),
        in_specs=[
            pl.BlockSpec((tm, D_MODEL), lambda i: (i, 0)),
            pl.BlockSpec((1, D_MODEL), lambda i: (0, 0)),
            pl.BlockSpec((1, 1, D_MODEL), _mod_index(layer, 3, tm)),
            pl.BlockSpec((1, 1, D_MODEL), _mod_index(layer, 4, tm)),
            pl.BlockSpec((D_MODEL, LANES), lambda i: (0, 0)),
            pl.BlockSpec((1, LANES), lambda i: (0, 0)),
        ],
        out_specs=[
            pl.BlockSpec((ROUTER_BLOCKS * BLOCK_ROWS, D_MODEL), lambda i: (i, 0)),
            pl.BlockSpec((tm, LANES), lambda i: (i, 0)),
            pl.BlockSpec((tm, LANES), lambda i: (i, 0)),
            pl.BlockSpec((ROUTER_BLOCKS, 1, LANES), lambda i: (i, 0, 0)),
        ],
        out_shape=[
            jax.ShapeDtypeStruct((N_BLOCKS * BLOCK_ROWS, D_MODEL), BF16),
            jax.ShapeDtypeStruct((N_TOK, LANES), F32),
            jax.ShapeDtypeStruct((N_TOK, LANES), F32),
            jax.ShapeDtypeStruct((N_BLOCKS, 1, LANES), jnp.int32),
        ],
        compiler_params=_cparams(("arbitrary",)),
        name="moe_router",
    )(x, g.reshape(1, D_MODEL), mods, mods, wr, br)


TABLE_ROWS = -(-(N_TILES * TILE_CHUNKS) // LANES)


def _tables_kernel(cnt_ref, tend_ref, disp_ref, comb_ref):
    ne = N_EXPERTS
    cnt = jnp.concatenate([cnt_ref[...].astype(F32), jnp.zeros((LANES - N_BLOCKS, LANES), F32)], axis=0)
    m = jnp.ceil(cnt * (1.0 / CHUNK))
    r = lax.broadcasted_iota(jnp.int32, (LANES, LANES), 0)
    c = lax.broadcasted_iota(jnp.int32, (LANES, LANES), 1)
    mb = m.astype(BF16)
    bstart = _dot(mb, jnp.where(r < c, 1.0, 0.0).astype(BF16))
    cm = _dot(jnp.where(c < r, 1.0, 0.0).astype(BF16), mb)
    total = jnp.sum(m, axis=0, keepdims=True)
    tiles = jnp.ceil(total * (1.0 / TILE_CHUNKS))
    tile_end = _dot(jnp.broadcast_to(tiles, (SUBLANES, LANES)).astype(BF16),
                    jnp.where(r <= c, 1.0, 0.0).astype(BF16))[0:1, :]
    choff = (tile_end - tiles) * TILE_CHUNKS
    tend_ref[...] = tile_end.astype(jnp.int32)

    start_t = (choff + cm).T
    m_t = m.T
    bstart_t = bstart.T
    lane = lax.broadcasted_iota(jnp.int32, (ne, LANES), 1).astype(F32)
    acc = [jnp.zeros((ne, LANES), F32) for _ in range(TABLE_ROWS)]
    for b in range(N_BLOCKS):
        lo = start_t[0:ne, b:b + 1]
        hi = lo + m_t[0:ne, b:b + 1]
        bs = bstart_t[0:ne, b:b + 1]
        fwd = (b * BLOCK_CHUNKS) + bs - lo
        for j in range(TABLE_ROWS):
            g = lane + float(j * LANES)
            acc[j] = acc[j] + jnp.where(g >= lo, jnp.where(g < hi, fwd + g, 0.0), 0.0)
        back = jnp.where(lane >= bs, jnp.where(lane < hi - lo + bs, (lo - bs + lane) * CHUNK + 1.0, 0.0), 0.0)
        comb_ref[b:b + 1, :] = (jnp.sum(back, axis=0, keepdims=True) - 1.0).astype(jnp.int32)
    for j in range(TABLE_ROWS):
        disp_ref[j:j + 1, :] = (jnp.sum(acc[j], axis=0, keepdims=True) * CHUNK).astype(jnp.int32)


def _chunk_tables(counts):
    return pl.pallas_call(
        _tables_kernel,
        out_shape=[
            jax.ShapeDtypeStruct((1, LANES), jnp.int32),
            jax.ShapeDtypeStruct((TABLE_ROWS, LANES), jnp.int32),
            jax.ShapeDtypeStruct((N_BLOCKS, LANES), jnp.int32),
        ],
        name="moe_chunk_tables",
    )(counts.reshape(N_BLOCKS, LANES))


def _chunk_copies(read_row, n_chunks, src_hbm, dst, sem, wait, skip_negative=False, priority=0):
    for c in range(n_chunks):
        row = read_row(c)

        def copy(row=row, c=c):
            cp = pltpu.make_async_copy(src_hbm.at[pl.ds(pl.multiple_of(row, CHUNK), CHUNK)],
                                       dst.at[pl.ds(c * CHUNK, CHUNK)], sem)
            if wait:
                cp.wait()
            else:
                cp.start(priority=priority)

        if skip_negative:
            pl.when(row >= 0)(copy)
        else:
            copy()


IN_DEPTH = 3


def _expert_kernel(tend_ref, src_ref, xb_hbm, w1_ref, w3_ref, w2_ref, y_hbm,
                   xbuf, ybuf, zbuf, w13_scr, w2_scr, in_sem, out_sem, zero_sem):
    e = pl.program_id(0)
    first = jnp.where(e == 0, 0, tend_ref[0, jnp.maximum(e - 1, 0)])
    last = tend_ref[0, e]
    n_used = tend_ref[0, N_EXPERTS - 1]
    tiles_per_row = LANES // TILE_CHUNKS

    def fetch(tile, s, wait):
        row = tile // tiles_per_row
        col = (tile % tiles_per_row) * TILE_CHUNKS
        _chunk_copies(lambda c: src_ref[row, col + c], TILE_CHUNKS, xb_hbm, xbuf.at[s], in_sem.at[s], wait,
                      priority=1)

    def store(tile, s):
        rows = pl.ds(pl.multiple_of(tile * EXPERT_TILE, EXPERT_TILE), EXPERT_TILE)
        return pltpu.make_async_copy(ybuf.at[s], y_hbm.at[rows], out_sem.at[s])

    def zero_fill(tile):
        rows = pl.ds(pl.multiple_of(tile * EXPERT_TILE, EXPERT_TILE), EXPERT_TILE)
        return pltpu.make_async_copy(zbuf, y_hbm.at[rows], zero_sem.at[0])

    @pl.when(e == 0)
    def _():
        zbuf[...] = jnp.zeros_like(zbuf)

        def start(t, carry):
            zero_fill(t).start()
            return carry

        lax.fori_loop(n_used, N_TILES, start, 0)

    lead = IN_DEPTH - 1

    @pl.when(last > first)
    def _():
        @pl.when(first == 0)
        def _():
            for k in range(lead):
                fetch(jnp.minimum(k, n_used - 1), k, False)

        w13_scr[:, 0:D_EXPERT] = w1_ref[0, 0].astype(BF16)
        w13_scr[:, D_EXPERT:2 * D_EXPERT] = w3_ref[0, 0].astype(BF16)
        w2_scr[...] = w2_ref[0, 0].astype(BF16)

    def tile_body(t, carry):
        s = t % 2
        fetch(t, t % IN_DEPTH, True)
        ab = _dot(xbuf[t % IN_DEPTH], w13_scr[...])
        fetch(jnp.minimum(t + lead, n_used - 1), (t + lead) % IN_DEPTH, False)
        a = ab[:, 0:D_EXPERT]
        hid = (a * jax.nn.sigmoid(a)) * ab[:, D_EXPERT:2 * D_EXPERT]
        y = _dot(hid.astype(BF16), w2_scr[...]).astype(BF16)

        @pl.when(t >= 2)
        def _():
            store(t, s).wait()

        ybuf[s] = y
        store(t, s).start()
        return carry

    lax.fori_loop(first, last, tile_body, 0)

    @pl.when(e == N_EXPERTS - 1)
    def _():
        for k in range(lead):
            fetch(n_used - 1, (n_used + k) % IN_DEPTH, True)

        @pl.when(n_used >= 1)
        def _():
            store(0, (n_used - 1) % 2).wait()

        @pl.when(n_used >= 2)
        def _():
            store(0, n_used % 2).wait()

        def drain(t, carry):
            zero_fill(t).wait()
            return carry

        lax.fori_loop(n_used, N_TILES, drain, 0)


def _experts(tile_end, disp_src, xb, w1, w3, w2, layer):
    wsel = lambda e, tend, src: (layer, e, 0, 0)
    return pl.pallas_call(
        _expert_kernel,
        grid_spec=pltpu.PrefetchScalarGridSpec(
            num_scalar_prefetch=2,
            grid=(N_EXPERTS,),
            in_specs=[
                pl.BlockSpec(memory_space=pl.ANY),
                pl.BlockSpec((1, 1, D_MODEL, D_EXPERT), wsel),
                pl.BlockSpec((1, 1, D_MODEL, D_EXPERT), wsel),
                pl.BlockSpec((1, 1, D_EXPERT, D_MODEL), wsel),
            ],
            out_specs=pl.BlockSpec(memory_space=pl.ANY),
            scratch_shapes=[pltpu.VMEM((IN_DEPTH, EXPERT_TILE, D_MODEL), BF16),
                            pltpu.VMEM((2, EXPERT_TILE, D_MODEL), BF16),
                            pltpu.VMEM((EXPERT_TILE, D_MODEL), BF16),
                            pltpu.VMEM((D_MODEL, 2 * D_EXPERT), BF16),
                            pltpu.VMEM((D_EXPERT, D_MODEL), BF16),
                            pltpu.SemaphoreType.DMA((IN_DEPTH,)),
                            pltpu.SemaphoreType.DMA((2,)),
                            pltpu.SemaphoreType.DMA((1,))],
        ),
        out_shape=jax.ShapeDtypeStruct((N_TILES * EXPERT_TILE, D_MODEL), BF16),
        compiler_params=_cparams(("arbitrary",)),
        name="moe_experts",
    )(tile_end, disp_src, xb, w1, w3, w2)


def _combine_kernel(src_ref, y_hbm, x_ref, pos_ref, gate_ref, mg_ref, *rest, tm, final):
    if final:
        fg_ref, op_ref, os_ref, ybuf, sem = rest
    else:
        o_ref, ybuf, sem = rest
    i = pl.program_id(0)
    slot = i % 2

    def fetch(blk, s, wait):
        _chunk_copies(lambda c: src_ref[blk, c], BLOCK_CHUNKS, y_hbm, ybuf.at[s], sem.at[s], wait,
                      skip_negative=True)

    @pl.when(i == 0)
    def _():
        ybuf[...] = jnp.zeros_like(ybuf)
        fetch(0, 0, False)

    @pl.when(i + 1 < N_BLOCKS)
    def _():
        fetch(i + 1, 1 - slot, False)

    fetch(i, slot, True)
    pos = pos_ref[...]
    sel = pos >= 0.0
    gate = gate_ref[...]
    pos_a, pos_b = _row_slots(pos, sel)
    gate_a = jnp.sum(jnp.where(pos == pos_a, gate, 0.0), axis=-1, keepdims=True)
    gate_b = jnp.sum(jnp.where(jnp.logical_and(sel, pos == pos_b), gate, 0.0), axis=-1, keepdims=True)
    row = lax.broadcasted_iota(jnp.int32, (tm, BLOCK_ROWS), 1).astype(F32)
    weights = jnp.where(row == pos_a, gate_a, jnp.where(row == pos_b, gate_b, 0.0))
    moe = _dot(weights.astype(BF16), ybuf[slot])
    out = x_ref[...] + mg_ref[0] * moe
    if not final:
        o_ref[...] = out
        return

    ms = jnp.mean(out * out, axis=-1, keepdims=True)
    out = (out * lax.rsqrt(ms + EPS)) * fg_ref[...]

    @pl.when(i < N_PROMPT // tm)
    def _():
        op_ref[...] = out

    @pl.when(i >= N_PROMPT // tm)
    def _():
        os_ref[...] = out


def _combine(comb_src, y, x, pos, gate, mods, layer, final_g=None):
    tm = MOE_BLOCK
    final = final_g is not None
    np_blocks = N_PROMPT // tm

    def gate_idx(i, s):
        return ((layer * MOD_ROWS + _mod_row(i, tm)) * 6 + 5, 0, 0)

    in_specs = [
        pl.BlockSpec(memory_space=pl.ANY),
        pl.BlockSpec((tm, D_MODEL), lambda i, s: (i, 0)),
        pl.BlockSpec((tm, LANES), lambda i, s: (i, 0)),
        pl.BlockSpec((tm, LANES), lambda i, s: (i, 0)),
        pl.BlockSpec((1, 1, D_MODEL), gate_idx),
    ]
    args = [comb_src, y, x, pos, gate, mods]
    if final:
        in_specs.append(pl.BlockSpec((1, D_MODEL), lambda i, s: (0, 0)))
        args.append(final_g.reshape(1, D_MODEL))
        out_specs = [pl.BlockSpec((tm, D_MODEL), lambda i, s: (jnp.minimum(i, np_blocks - 1), 0)),
                     pl.BlockSpec((tm, D_MODEL), lambda i, s: (jnp.maximum(i - np_blocks, 0), 0))]
        out_shape = [jax.ShapeDtypeStruct((N_PROMPT, D_MODEL), F32),
                     jax.ShapeDtypeStruct((N_SAMPLE, D_MODEL), F32)]
    else:
        out_specs = pl.BlockSpec((tm, D_MODEL), lambda i, s: (i, 0))
        out_shape = jax.ShapeDtypeStruct((N_TOK, D_MODEL), F32)
    return pl.pallas_call(
        functools.partial(_combine_kernel, tm=tm, final=final),
        grid_spec=pltpu.PrefetchScalarGridSpec(
            num_scalar_prefetch=1,
            grid=(N_BLOCKS,),
            in_specs=in_specs,
            out_specs=out_specs,
            scratch_shapes=[pltpu.VMEM((2, BLOCK_ROWS, D_MODEL), BF16),
                            pltpu.SemaphoreType.DMA((2,))],
        ),
        out_shape=out_shape,
        compiler_params=_cparams(("arbitrary",)),
        name="moe_combine_final" if final else "moe_combine",
    )(*args)


def _moe(x, g, mods, layer, w_group, b_group, w_expert, b_expert, w1, w3, w2, final_g=None):
    pad = LANES - N_EXPERTS - N_GROUPS
    wr = jnp.concatenate([w_expert, w_group, jnp.zeros((D_MODEL, pad), F32)], axis=1)
    br = jnp.concatenate([b_expert, b_group, jnp.zeros((pad,), F32)]).reshape(1, LANES)
    xb, pos, gate, counts = _router(x, g, mods, layer, wr, br)
    tile_end, disp_src, comb_src = _chunk_tables(counts)
    y = _experts(tile_end, disp_src, xb, w1, w3, w2, layer)
    return _combine(comb_src, y, x, pos, gate, mods, layer, final_g)


def _cache_kernel(*refs):
    n = (len(refs) - 2) // 2
    k_refs, v_refs, ok_ref, ov_ref = refs[:n], refs[n:2 * n], refs[2 * n], refs[2 * n + 1]
    j = pl.program_id(1)
    for src_refs, o_ref in ((k_refs, ok_ref), (v_refs, ov_ref)):
        x = src_refs[0][...]
        for a in range(1, n):
            x = jnp.where(j == a, src_refs[a][...], x)
        x = x.astype(F32)
        for h in range(N_HEADS):
            o_ref[0, 0, :, h, :] = x[:, h * HEAD_W:(h + 1) * HEAD_W]


def _new_caches(qkvs):
    n = len(qkvs)
    k_spec = pl.BlockSpec((SEQ, D_MODEL), lambda b, j: (b, 1))
    v_spec = pl.BlockSpec((SEQ, D_MODEL), lambda b, j: (b, 2))
    out_spec = pl.BlockSpec((1, 1, SEQ, N_HEADS, HEAD_W), lambda b, j: (b, j, 0, 0, 0))
    shape = jax.ShapeDtypeStruct((BATCH, n, SEQ, N_HEADS, HEAD_W), F32)
    return pl.pallas_call(
        _cache_kernel,
        grid=(BATCH, n),
        in_specs=[k_spec] * n + [v_spec] * n,
        out_specs=[out_spec, out_spec],
        out_shape=[shape, shape],
        compiler_params=_cparams(("arbitrary", "arbitrary")),
        name="new_caches",
    )(*qkvs, *qkvs)


def kernel(x_prompt, x_sample, cache_k, cache_v, c, c_ctx, ada_w, ada_b, norm1_g, norm2_g, final_g, hy_w_in, hy_conv_w, hy_conv_b, hy_f_w1, hy_f_b1, hy_f_w2, hy_f_b2, hy_f_w3, hy_f_b3, hy_f_freq, hy_bias, hy_w_out, at_w_qkv, at_lam_q1, at_lam_k1, at_lam_q2, at_lam_k2, at_subln_g, at_w_o, moe_w_group, moe_b_group, moe_w_expert, moe_b_expert, moe_w1, moe_w3, moe_w2):
    x = (x_prompt.reshape(N_PROMPT, D_MODEL), x_sample.reshape(N_SAMPLE, D_MODEL))
    cond = jnp.concatenate([c_ctx[None, :], c, jnp.zeros((MOD_ROWS - 1 - DEC_BATCH, D_MODEL), F32)], axis=0)
    mods = _ada_table(cond, ada_w, ada_b).reshape(DEPTH * MOD_ROWS * 6, 1, D_MODEL)
    ctx_k = cache_k.reshape(DEC_BATCH, DEPTH // 2, PAST_LEN, D_MODEL)
    ctx_v = cache_v.reshape(DEC_BATCH, DEPTH // 2, PAST_LEN, D_MODEL)

    dft = {}
    for L in (SEQ, DEC_SEQ):
        fwd, inv = _dft_mats(L)
        dft[L] = (jnp.asarray(fwd).astype(BF16), jnp.asarray(inv).astype(BF16))

    qkvs = []
    for l in range(DEPTH):
        j = l // 2
        if l % 2 == 0:
            z = _norm_matmul(x, norm1_g[l], mods, l, 0, 1, hy_w_in, j)
            ys = []
            for L, rb0, n_seq, dt in ((SEQ, 0, BATCH, 1024), (DEC_SEQ, N_PROMPT // DEC_SEQ, DEC_BATCH, 512)):
                f_bf, finv = dft[L]
                kc = _filter_spectra(L, hy_f_w1[j], hy_f_b1[j], hy_f_w2[j], hy_f_b2[j], hy_f_w3[j],
                                     hy_f_b3[j], hy_f_freq[j], f_bf)
                ys.append(_hyena_conv(z, hy_conv_w[j], hy_conv_b[j], f_bf, finv, kc, hy_bias[j],
                                      L=L, row_block0=rb0, n_seq=n_seq, dt=dt))
            x = _matmul_residual((ys[0], ys[1]), hy_w_out, j, x, mods, l, 2)
        else:
            lam_init = 0.8 - 0.6 * math.exp(-0.3 * l)
            qkv = _norm_matmul(x, norm1_g[l], mods, l, 0, 1, at_w_qkv, j)
            lam_params = (at_lam_q1[j], at_lam_k1[j], at_lam_q2[j], at_lam_k2[j])
            op = _attention(qkv, lam_params, at_subln_g[j], lam_init, L=SEQ, row_block0=0,
                            n_seq=BATCH, tq=SEQ)
            os_ = _attention(qkv, lam_params, at_subln_g[j], lam_init, L=DEC_SEQ,
                             row_block0=N_PROMPT // DEC_SEQ, n_seq=DEC_BATCH, tq=256,
                             ctx=(ctx_k, ctx_v), j=j)
            qkvs.append(qkv)
            x = _matmul_residual((op, os_), at_w_o, j, x, mods, l, 2)
        x = _moe(x, norm2_g[l], mods, l, moe_w_group[l], moe_b_group[l], moe_w_expert[l],
                 moe_b_expert[l], moe_w1, moe_w3, moe_w2, final_g if l == DEPTH - 1 else None)
    y_prompt = x[0].reshape(BATCH, SEQ, D_MODEL)
    y_sample = x[1].reshape(DEC_BATCH, DEC_SEQ, D_MODEL)
    new_cache_k, new_cache_v = _new_caches(qkvs)
    return (y_prompt, y_sample, new_cache_k, new_cache_v)
```

```python
import functools
import math

import numpy as np
import jax
import jax.numpy as jnp
from jax import lax
from jax.experimental import pallas as pl
from jax.experimental.pallas import tpu as pltpu

F32 = jnp.float32
BF16 = jnp.bfloat16

D_MODEL = 1024
BATCH = 16
SEQ = 256
DEPTH = 4
DEC_BATCH = 4
DEC_SEQ = 1024
PAST_LEN = 256
GRID_W = 64
EPS = 1e-6
HY_BANDS = 16
HY_EMB = 1 + 2 * HY_BANDS
HY_FILTER_HIDDEN = 64
HY_FAST_DECAY = 0.3
HY_SLOW_DECAY = 1.5
HY_TARGET = 1e-2
N_HEADS = 8
HEAD_DIM = 64
ROT_AXIS = HEAD_DIM // 2
ROPE_BASE = 10000.0
N_GROUPS = 4
EXPERTS_PER_GROUP = 8
N_EXPERTS = N_GROUPS * EXPERTS_PER_GROUP
D_EXPERT = D_MODEL // 4

N_PROMPT = BATCH * SEQ
N_SAMPLE = DEC_BATCH * DEC_SEQ
N_TOK = N_PROMPT + N_SAMPLE
MOD_ROWS = 8
LANES = 128
SUBLANES = 8
HEAD_W = 2 * HEAD_DIM
VMEM_LIMIT = 56 * 1024 * 1024

MOE_BLOCK = 256
N_BLOCKS = N_TOK // MOE_BLOCK
CHUNK = 2 * SUBLANES
BLOCK_ROWS = 2 * MOE_BLOCK + N_EXPERTS * CHUNK
BLOCK_CHUNKS = BLOCK_ROWS // CHUNK
EXPERT_TILE = 256
TILE_CHUNKS = EXPERT_TILE // CHUNK
N_TILES = (2 * N_TOK + N_BLOCKS * N_EXPERTS * (CHUNK - 1)) // EXPERT_TILE + N_EXPERTS


def _cparams(sem):
    return pltpu.CompilerParams(dimension_semantics=sem, vmem_limit_bytes=VMEM_LIMIT)


def _mod_row(blk, tm):
    start = blk * tm
    return jnp.where(start < N_PROMPT, 0, 1 + (start - N_PROMPT) // DEC_SEQ)


def _mod_index(layer, which, tm):
    def index_map(i, *_):
        return ((layer * MOD_ROWS + _mod_row(i, tm)) * 6 + which, 0, 0)
    return index_map


@functools.lru_cache(maxsize=None)
def _dft_mats(L):
    k = np.arange(L, dtype=np.float64)[:, None]
    j = np.arange(L, dtype=np.float64)[None, :]
    ang = np.pi * ((k * j) % (2 * L)) / L
    c = np.cos(ang)
    s = np.sin(ang)
    s[0, :] = np.where(np.arange(L) % 2 == 0, 1.0, -1.0)
    fwd = np.concatenate([c, s], axis=0)
    scale = np.full((2 * L,), 1.0 / L)
    scale[0] = scale[L] = 0.5 / L
    inv = fwd.T * scale[None, :]
    return fwd.astype(np.float32), inv.astype(np.float32)


@functools.lru_cache(maxsize=None)
def _filter_feats(L):
    pos = np.arange(L, dtype=np.float64)
    t = pos / (L - 1)
    bands = np.linspace(1e-4, HY_BANDS - 1, HY_BANDS)
    ang = (2.0 * math.pi / L) * pos[:, None] * bands[None, :]
    feats = np.concatenate([t[:, None], np.cos(ang), -np.sin(ang)], axis=-1)
    feats = np.pad(feats, ((0, 0), (0, LANES - HY_EMB)))
    deltas = np.abs(np.linspace(math.log(HY_TARGET) / HY_SLOW_DECAY,
                                math.log(HY_TARGET) / HY_FAST_DECAY, D_MODEL))
    window = np.exp(-t[:, None] * deltas[None, :])
    alt = np.where(np.arange(L) % 2 == 0, 1.0, -1.0)[:, None] * np.ones((1, LANES))
    return feats.astype(np.float32), window.astype(np.float32), alt.astype(np.float32)


@functools.lru_cache(maxsize=None)
def _rope_tables():
    pos = np.arange(DEC_SEQ)
    row = (pos // GRID_W).astype(np.float64)
    col = (pos % GRID_W).astype(np.float64)
    lane = np.arange(HEAD_W)
    d = lane % HEAD_DIM
    axis = d // ROT_AXIS
    n = d % ROT_AXIS
    half = n // (ROT_AXIS // 2)
    f = n % (ROT_AXIS // 2)
    inv = ROPE_BASE ** (-f.astype(np.float64) / (ROT_AXIS // 2))
    p = np.where(axis[None, :] == 0, row[:, None], col[:, None])
    ang = p * inv[None, :]
    sign = np.where(half == 0, -1.0, 1.0)[None, :]
    return np.cos(ang).astype(np.float32), (np.sin(ang) * sign).astype(np.float32)


def _split_bf16(a):
    hi = a.astype(BF16)
    lo = (a - hi.astype(F32)).astype(BF16)
    return hi, lo


def _dot(a, b):
    return jnp.dot(a, b, preferred_element_type=F32)


def _dot3(a_hi, a_lo, b_hi, b_lo):
    return _dot(a_hi, b_hi) + (_dot(a_hi, b_lo) + _dot(a_lo, b_hi))


def _ada_kernel(c_ref, w_ref, b_ref, o_ref):
    c = c_ref[...]
    s = c * jax.nn.sigmoid(c)
    o_ref[0] = _dot(s.astype(BF16), w_ref[0].astype(BF16)) + b_ref[0]


def _ada_table(cond, ada_w, ada_b):
    tn = 1536
    n = 6 * D_MODEL
    return pl.pallas_call(
        _ada_kernel,
        grid=(DEPTH, n // tn),
        in_specs=[
            pl.BlockSpec((MOD_ROWS, D_MODEL), lambda l, j: (0, 0)),
            pl.BlockSpec((1, D_MODEL, tn), lambda l, j: (l, 0, j)),
            pl.BlockSpec((1, 1, tn), lambda l, j: (l, 0, j)),
        ],
        out_specs=pl.BlockSpec((1, MOD_ROWS, tn), lambda l, j: (l, 0, j)),
        out_shape=jax.ShapeDtypeStruct((DEPTH, MOD_ROWS, n), F32),
        compiler_params=_cparams(("arbitrary", "arbitrary")),
        name="ada_table",
    )(cond, ada_w, ada_b.reshape(DEPTH, 1, n))


def _modulate(x, g, scale, shift):
    ms = jnp.mean(x * x, axis=-1, keepdims=True)
    y = (x * lax.rsqrt(ms + EPS)) * g
    return y * (1.0 + scale) + shift


MM_COLS = 512


def _row_group_specs(x, tm, width):
    np_blocks = N_PROMPT // tm
    if isinstance(x, tuple):
        specs = [pl.BlockSpec((tm, width), lambda i, *_: (jnp.minimum(i, np_blocks - 1), 0)),
                 pl.BlockSpec((tm, width), lambda i, *_: (jnp.maximum(i - np_blocks, 0), 0))]
        return specs, list(x)
    specs = [pl.BlockSpec((tm, width), lambda i, *_: (jnp.minimum(i, np_blocks - 1), 0)),
             pl.BlockSpec((tm, width), lambda i, *_: (jnp.maximum(i, np_blocks), 0))]
    return specs, [x, x]


def _pick_rows(i, tm, p_ref, s_ref):
    return jnp.where(i < N_PROMPT // tm, p_ref[...], s_ref[...])


def _normmm_kernel(xp_ref, xs_ref, g_ref, sh_ref, sc_ref, w_ref, o_ref, w_scr, *, tm):
    i = pl.program_id(0)

    @pl.when(i == 0)
    def _():
        w_scr[...] = w_ref[0].astype(BF16)

    x = _pick_rows(i, tm, xp_ref, xs_ref)
    h = _modulate(x, g_ref[...], sc_ref[0], sh_ref[0]).astype(BF16)
    for j in range(w_scr.shape[1] // MM_COLS):
        cols = slice(j * MM_COLS, (j + 1) * MM_COLS)
        o_ref[:, cols] = _dot(h, w_scr[:, cols]).astype(o_ref.dtype)


def _norm_matmul(x, g, mods, layer, which_shift, which_scale, w, widx, *, tm=512):
    n = w.shape[2]
    x_specs, x_args = _row_group_specs(x, tm, D_MODEL)
    return pl.pallas_call(
        functools.partial(_normmm_kernel, tm=tm),
        grid=(N_TOK // tm,),
        in_specs=x_specs + [
            pl.BlockSpec((1, D_MODEL), lambda i: (0, 0)),
            pl.BlockSpec((1, 1, D_MODEL), _mod_index(layer, which_shift, tm)),
            pl.BlockSpec((1, 1, D_MODEL), _mod_index(layer, which_scale, tm)),
            pl.BlockSpec((1, D_MODEL, n), lambda i: (widx, 0, 0), pipeline_mode=pl.Buffered(1)),
        ],
        out_specs=pl.BlockSpec((tm, n), lambda i: (i, 0)),
        out_shape=jax.ShapeDtypeStruct((N_TOK, n), BF16),
        scratch_shapes=[pltpu.VMEM((D_MODEL, n), BF16)],
        compiler_params=_cparams(("arbitrary",)),
        name="norm_matmul",
    )(*x_args, g.reshape(1, D_MODEL), mods, mods, w)


def _mmres_kernel(ap_ref, as_ref, xp_ref, xs_ref, w_ref, gate_ref, o_ref, w_scr, *, tm):
    i = pl.program_id(0)

    @pl.when(i == 0)
    def _():
        w_scr[...] = w_ref[0].astype(BF16)

    a = _pick_rows(i, tm, ap_ref, as_ref)
    x = _pick_rows(i, tm, xp_ref, xs_ref)
    gate = gate_ref[0]
    for j in range(w_scr.shape[1] // MM_COLS):
        cols = slice(j * MM_COLS, (j + 1) * MM_COLS)
        o_ref[:, cols] = x[:, cols] + gate[:, cols] * _dot(a, w_scr[:, cols])


def _matmul_residual(a, w, widx, x, mods, layer, which_gate, *, tm=512):
    n = w.shape[2]
    a_specs, a_args = _row_group_specs(a, tm, D_MODEL)
    x_specs, x_args = _row_group_specs(x, tm, n)
    return pl.pallas_call(
        functools.partial(_mmres_kernel, tm=tm),
        grid=(N_TOK // tm,),
        in_specs=a_specs + x_specs + [
            pl.BlockSpec((1, D_MODEL, n), lambda i: (widx, 0, 0), pipeline_mode=pl.Buffered(1)),
            pl.BlockSpec((1, 1, n), _mod_index(layer, which_gate, tm)),
        ],
        out_specs=pl.BlockSpec((tm, n), lambda i: (i, 0)),
        out_shape=jax.ShapeDtypeStruct((N_TOK, n), F32),
        scratch_shapes=[pltpu.VMEM((D_MODEL, n), BF16)],
        compiler_params=_cparams(("arbitrary",)),
        name="matmul_residual",
    )(*a_args, *x_args, w, mods)


def _filter_kernel(feats_ref, win_ref, alt_ref, w1_ref, b1_ref, w2_ref, b2_ref, fq_ref,
                   w3f_ref, b3f_ref, w3b_ref, b3b_ref, f_ref, o_ref, h_scr, *, L):
    def dense(a, w_ref, b_ref):
        a_hi, a_lo = _split_bf16(a)
        w_hi, w_lo = _split_bf16(w_ref[...])
        return _dot3(a_hi, a_lo, w_hi, w_lo) + b_ref[...]

    @pl.when(jnp.logical_and(pl.program_id(0) == 0, pl.program_id(1) == 0))
    def _():
        fq = fq_ref[...]
        h1 = jnp.sin(fq[0:1, :] * dense(feats_ref[...], w1_ref, b1_ref))
        h_scr[...] = jnp.sin(fq[1:2, :] * dense(h1, w2_ref, b2_ref))

    h = h_scr[...]
    win = win_ref[...]
    hf = dense(h, w3f_ref, b3f_ref) * win
    hb = dense(h, w3b_ref, b3b_ref) * win
    row = lax.broadcasted_iota(jnp.int32, hf.shape, 0)
    hb = jnp.where(row == 0, 0.0, hb)
    kr = _dot(f_ref[0:L, :], (hf + hb).astype(BF16))
    ks = _dot(f_ref[L:2 * L, :], (hf - hb).astype(BF16))
    k_nyq = jnp.sum((hf + hb) * alt_ref[:, 0:1], axis=0, keepdims=True)
    o_ref[0, 0] = kr
    o_ref[0, 1] = jnp.where(row == 0, k_nyq, kr)
    o_ref[0, 2] = jnp.where(row == 0, 0.0, ks)


def _filter_spectra(L, w1, b1, w2, b2, w3, b3, freq, f_bf, *, dt=512):
    feats, window, alt = _filter_feats(L)
    fh = HY_FILTER_HIDDEN
    nd = D_MODEL // dt
    w1p = jnp.pad(w1, ((0, LANES - HY_EMB), (0, 0)))
    const = lambda shape: pl.BlockSpec(shape, lambda o, c: tuple(0 for _ in shape))
    return pl.pallas_call(
        functools.partial(_filter_kernel, L=L),
        grid=(2, nd),
        in_specs=[
            const((L, LANES)),
            pl.BlockSpec((L, dt), lambda o, c: (0, c)),
            const((L, LANES)),
            const((LANES, fh)), const((1, fh)), const((fh, fh)), const((1, fh)), const((2, fh)),
            pl.BlockSpec((fh, dt), lambda o, c: (0, (2 * o) * nd + c)),
            pl.BlockSpec((1, dt), lambda o, c: (0, (2 * o) * nd + c)),
            pl.BlockSpec((fh, dt), lambda o, c: (0, (2 * o + 1) * nd + c)),
            pl.BlockSpec((1, dt), lambda o, c: (0, (2 * o + 1) * nd + c)),
            const((2 * L, L)),
        ],
        out_specs=pl.BlockSpec((1, 3, L, dt), lambda o, c: (o, 0, 0, c)),
        out_shape=jax.ShapeDtypeStruct((2, 3, L, D_MODEL), F32),
        scratch_shapes=[pltpu.VMEM((L, fh), F32)],
        compiler_params=_cparams(("arbitrary", "arbitrary")),
        name=f"hyena_filter_spectra_{L}",
    )(jnp.asarray(feats), jnp.asarray(window), jnp.asarray(alt), w1p, b1.reshape(1, fh), w2,
      b2.reshape(1, fh), freq, w3, b3.reshape(1, -1), w3, b3.reshape(1, -1), f_bf)


def _conv_kernel(v_ref, x1_ref, x2_ref, cwv_ref, cw1_ref, cw2_ref, cbv_ref, cb1_ref, cb2_ref,
                 f_ref, finv_ref, kc_ref, bias_ref, o_ref, *, L):
    row = lax.broadcasted_iota(jnp.int32, v_ref.shape, 0)

    def short_conv(u_ref, w_ref, b_ref):
        u = u_ref[...].astype(F32)
        prev = jnp.where(row == 0, 0.0, pltpu.roll(u, 1, 0))
        nxt = jnp.where(row == L - 1, 0.0, pltpu.roll(u, L - 1, 0))
        w = w_ref[...]
        return prev * w[0:1, :] + u * w[1:2, :] + nxt * w[2:3, :] + b_ref[...]

    def long_conv(u, order):
        spec = _dot(f_ref[...], u.astype(BF16))
        a = spec[0:L, :]
        b = spec[L:2 * L, :]
        kra = kc_ref[order, 0]
        krb = kc_ref[order, 1]
        ks = kc_ref[order, 2]
        prod = jnp.concatenate([a * kra - b * ks, a * ks + b * krb], axis=0)
        return _dot(finv_ref[...], prod.astype(BF16)) + u * bias_ref[order:order + 1, :]

    v = short_conv(v_ref, cwv_ref, cbv_ref)
    y = short_conv(x1_ref, cw1_ref, cb1_ref) * long_conv(v, 0)
    y = short_conv(x2_ref, cw2_ref, cb2_ref) * long_conv(y, 1)
    o_ref[...] = y.astype(o_ref.dtype)


def _hyena_conv(z, conv_w, conv_b, f_bf, finv_bf, kc, bias, *, L, row_block0, n_seq, dt):
    nd = D_MODEL // dt
    cb = conv_b.reshape(1, 3 * D_MODEL)
    seg = lambda s: pl.BlockSpec((L, dt), lambda c, b: (row_block0 + b, s * nd + c))
    cw = lambda s: pl.BlockSpec((3, dt), lambda c, b: (0, s * nd + c))
    cbs = lambda s: pl.BlockSpec((1, dt), lambda c, b: (0, s * nd + c))
    return pl.pallas_call(
        functools.partial(_conv_kernel, L=L),
        grid=(nd, n_seq),
        in_specs=[
            seg(0), seg(1), seg(2), cw(0), cw(1), cw(2), cbs(0), cbs(1), cbs(2),
            pl.BlockSpec((2 * L, L), lambda c, b: (0, 0), pipeline_mode=pl.Buffered(1)),
            pl.BlockSpec((L, 2 * L), lambda c, b: (0, 0), pipeline_mode=pl.Buffered(1)),
            pl.BlockSpec((2, 3, L, dt), lambda c, b: (0, 0, 0, c), pipeline_mode=pl.Buffered(1)),
            pl.BlockSpec((2, dt), lambda c, b: (0, c)),
        ],
        out_specs=pl.BlockSpec((L, dt), lambda c, b: (b, c)),
        out_shape=jax.ShapeDtypeStruct((n_seq * L, D_MODEL), BF16),
        compiler_params=_cparams(("arbitrary", "arbitrary")),
        name=f"hyena_conv_{L}",
    )(z, z, z, conv_w, conv_w, conv_w, cb, cb, cb, f_bf, finv_bf, kc, bias)


def _attn_kernel(lq1_ref, lk1_ref, lq2_ref, lk2_ref, g_ref, q_ref, k_ref, v_ref, *rest,
                 lam_init, has_ctx, tq, nq):
    if has_ctx:
        ck_ref, cv_ref, cos_ref, sin_ref, o_ref, k_scr = rest
    else:
        (o_ref,) = rest
    lam = (jnp.exp(jnp.sum(lq1_ref[...] * lk1_ref[...], axis=-1, keepdims=True))
           - jnp.exp(jnp.sum(lq2_ref[...] * lk2_ref[...], axis=-1, keepdims=True)) + lam_init)
    lane = lax.broadcasted_iota(jnp.int32, (tq, HEAD_W), 1)
    first_map = lane < HEAD_DIM
    first_half = (lane % ROT_AXIS) < (ROT_AXIS // 2)
    nt = (((1,), (1,)), ((), ()))

    def rotary(x, rows):
        x = x.astype(F32)
        partner = jnp.where(first_half, pltpu.roll(x, HEAD_W - ROT_AXIS // 2, 1),
                            pltpu.roll(x, ROT_AXIS // 2, 1))
        return (x * cos_ref[rows, :] + partner * sin_ref[rows, :]).astype(BF16)

    if has_ctx:
        for r in range(k_ref.shape[0] // tq):
            rows = slice(r * tq, (r + 1) * tq)
            for h in range(N_HEADS):
                cols = slice(h * HEAD_W, (h + 1) * HEAD_W)
                k_scr[rows, cols] = rotary(k_ref[rows, cols], rows)
        keys = k_scr
    else:
        keys = k_ref

    def q_block(rows):
        for h in range(N_HEADS):
            cols = slice(h * HEAD_W, (h + 1) * HEAD_W)
            q = q_ref[rows, cols]
            if has_ctx:
                q = rotary(q, rows)
            q = q * (HEAD_DIM ** -0.5)
            zero = jnp.zeros_like(q)
            q2 = jnp.concatenate([jnp.where(first_map, q, zero), jnp.where(first_map, zero, q)], axis=0)
            s = lax.dot_general(q2, keys[:, cols], nt, preferred_element_type=F32)
            m = jnp.max(s, axis=-1, keepdims=True)
            if has_ctx:
                sc = lax.dot_general(q2, ck_ref[0, 0, :, cols].astype(BF16), nt,
                                     preferred_element_type=F32)
                m = jnp.maximum(m, jnp.max(sc, axis=-1, keepdims=True))
                ec = jnp.exp(sc - m)
            e = jnp.exp(s - m)
            den = jnp.sum(e, axis=-1, keepdims=True)
            if has_ctx:
                den = den + jnp.sum(ec, axis=-1, keepdims=True)
            ov = _dot(e.astype(BF16), v_ref[:, cols])
            if has_ctx:
                ov = ov + _dot(ec.astype(BF16), cv_ref[0, 0, :, cols].astype(BF16))
            inv = 1.0 / den
            o = ov[0:tq] * inv[0:tq] - ov[tq:2 * tq] * (lam * inv[tq:2 * tq])
            ms = jnp.mean(o * o, axis=-1, keepdims=True)
            o = (o * lax.rsqrt(ms + EPS)) * g_ref[...] * (1.0 - lam_init)
            o_ref[rows, cols] = o.astype(o_ref.dtype)

    if nq == 1:
        q_block(slice(0, tq))
    else:
        def body(qi, carry):
            q_block(pl.ds(pl.multiple_of(qi * tq, tq), tq))
            return carry
        lax.fori_loop(0, nq, body, 0)


def _attention(qkv, lam_params, subln_g, lam_init, *, L, row_block0, n_seq, tq, ctx=None, j=0):
    small = pl.BlockSpec((1, HEAD_DIM), lambda b: (0, 0))
    in_specs = [small, small, small, small,
                pl.BlockSpec((1, HEAD_W), lambda b: (0, 0)),
                pl.BlockSpec((L, D_MODEL), lambda b: (row_block0 + b, 0)),
                pl.BlockSpec((L, D_MODEL), lambda b: (row_block0 + b, 1)),
                pl.BlockSpec((L, D_MODEL), lambda b: (row_block0 + b, 2))]
    args = [p.reshape(1, HEAD_DIM) for p in lam_params] + [subln_g.reshape(1, HEAD_W), qkv, qkv, qkv]
    scratch = []
    if ctx is not None:
        ctx_spec = pl.BlockSpec((1, 1, PAST_LEN, D_MODEL), lambda b: (b, j, 0, 0))
        rope_spec = pl.BlockSpec((L, HEAD_W), lambda b: (0, 0))
        cos, sin = _rope_tables()
        in_specs += [ctx_spec, ctx_spec, rope_spec, rope_spec]
        args += list(ctx) + [jnp.asarray(cos), jnp.asarray(sin)]
        scratch = [pltpu.VMEM((L, D_MODEL), BF16)]
    return pl.pallas_call(
        functools.partial(_attn_kernel, lam_init=lam_init, has_ctx=ctx is not None, tq=tq, nq=L // tq),
        grid=(n_seq,),
        in_specs=in_specs,
        out_specs=pl.BlockSpec((L, D_MODEL), lambda b: (b, 0)),
        out_shape=jax.ShapeDtypeStruct((n_seq * L, D_MODEL), BF16),
        scratch_shapes=scratch,
        compiler_params=_cparams(("arbitrary",)),
        name="diff_attention_ctx" if ctx is not None else "diff_attention",
    )(*args)


def _row_slots(pos, sel):
    big = float(BLOCK_ROWS)
    pos_a = jnp.max(pos, axis=-1, keepdims=True)
    pos_b = jnp.min(jnp.where(sel, pos, big), axis=-1, keepdims=True)
    return pos_a, pos_b


def _route_rows(h, wr, br):
    tm = h.shape[0]
    logits = _dot(h, wr) + br
    lane = lax.broadcasted_iota(jnp.int32, (tm, LANES), 1)
    neg = -jnp.inf
    is_grp = jnp.logical_and(lane >= N_EXPERTS, lane < N_EXPERTS + N_GROUPS)
    lg = jnp.where(is_grp, logits, neg)
    mg = jnp.max(lg, axis=-1, keepdims=True)
    g_gate = 1.0 / jnp.sum(jnp.exp(lg - mg), axis=-1, keepdims=True)
    g_idx = jnp.min(jnp.where(lg == mg, lane - N_EXPERTS, N_GROUPS), axis=-1, keepdims=True)
    in_grp = jnp.logical_and(lane < N_EXPERTS, lane // EXPERTS_PER_GROUP == g_idx)
    le = jnp.where(in_grp, logits, neg)
    m1 = jnp.max(le, axis=-1, keepdims=True)
    i1 = jnp.min(jnp.where(le == m1, lane, LANES), axis=-1, keepdims=True)
    le2 = jnp.where(lane == i1, neg, le)
    m2 = jnp.max(le2, axis=-1, keepdims=True)
    i2 = jnp.min(jnp.where(le2 == m2, lane, LANES), axis=-1, keepdims=True)
    e2 = jnp.exp(m2 - m1)
    inv = 1.0 / (1.0 + e2)
    gate = jnp.where(lane == i1, g_gate * inv, jnp.where(lane == i2, g_gate * (e2 * inv), 0.0))
    return gate, jnp.where(jnp.logical_or(lane == i1, lane == i2), 1.0, 0.0)


def _sort_block(h, onehot):
    tm = MOE_BLOCK
    sel = onehot > 0.5
    r = lax.broadcasted_iota(jnp.int32, (tm, tm), 0)
    c = lax.broadcasted_iota(jnp.int32, (tm, tm), 1)
    rank = _dot(jnp.where(c < r, 1.0, 0.0).astype(BF16), onehot.astype(BF16))
    count = jnp.sum(onehot, axis=0, keepdims=True)
    padded = jnp.ceil(count * (1.0 / CHUNK)) * CHUNK
    er = lax.broadcasted_iota(jnp.int32, (LANES, LANES), 0)
    ec = lax.broadcasted_iota(jnp.int32, (LANES, LANES), 1)
    before = jnp.where(er < ec, 1.0, 0.0).astype(BF16)
    start = _dot(jnp.broadcast_to(padded, (SUBLANES, LANES)).astype(BF16), before)[0:1, :]
    pos = jnp.where(sel, start + rank, -1.0)
    pos_a, pos_b = _row_slots(pos, sel)
    slot = lax.broadcasted_iota(jnp.int32, (tm, BLOCK_ROWS), 1).astype(F32)
    pick = jnp.where(slot == pos_a, 1.0, jnp.where(slot == pos_b, 1.0, 0.0)).astype(BF16)
    xb = lax.dot_general(pick, h, (((0,), (0,)), ((), ())), preferred_element_type=F32).astype(BF16)
    return xb, pos, count.astype(jnp.int32)


ROUTER_BLOCKS = 4


def _router_kernel(x_ref, g_ref, sh_ref, sc_ref, wr_ref, br_ref, xb_ref, pos_ref, gate_ref, cnt_ref):
    h = _modulate(x_ref[...], g_ref[...], sc_ref[0], sh_ref[0]).astype(BF16)
    gate, sel = _route_rows(h, wr_ref[...].astype(BF16), br_ref[...])
    gate_ref[...] = gate
    for k in range(ROUTER_BLOCKS):
        rows = slice(k * MOE_BLOCK, (k + 1) * MOE_BLOCK)
        xb, pos, count = _sort_block(h[rows, :], sel[rows, :])
        xb_ref[k * BLOCK_ROWS:(k + 1) * BLOCK_ROWS, :] = xb
        pos_ref[rows, :] = pos
        cnt_ref[k] = count


def _router(x, g, mods, layer, wr, br):
    tm = ROUTER_BLOCKS * MOE_BLOCK
    return pl.pallas_call(
        _router_kernel,
        grid=(N_TOK // tm,),
        in_specs=[
            pl.BlockSpec((tm, D_MODEL), lambda i: (i, 0)),
            pl.BlockSpec((1, D_MODEL), lambda i: (0, 0)),
            pl.BlockSpec((1, 1, D_MODEL), _mod_index(layer, 3, tm)),
            pl.BlockSpec((1, 1, D_MODEL), _mod_index(layer, 4, tm)),
            pl.BlockSpec((D_MODEL, LANES), lambda i: (0, 0)),
            pl.BlockSpec((1, LANES), lambda i: (0, 0)),
        ],
        out_specs=[
            pl.BlockSpec((ROUTER_BLOCKS * BLOCK_ROWS, D_MODEL), lambda i: (i, 0)),
            pl.BlockSpec((tm, LANES), lambda i: (i, 0)),
            pl.BlockSpec((tm, LANES), lambda i: (i, 0)),
            pl.BlockSpec((ROUTER_BLOCKS, 1, LANES), lambda i: (i, 0, 0)),
        ],
        out_shape=[
            jax.ShapeDtypeStruct((N_BLOCKS * BLOCK_ROWS, D_MODEL), BF16),
            jax.ShapeDtypeStruct((N_TOK, LANES), F32),
            jax.ShapeDtypeStruct((N_TOK, LANES), F32),
            jax.ShapeDtypeStruct((N_BLOCKS, 1, LANES), jnp.int32),
        ],
        compiler_params=_cparams(("arbitrary",)),
        name="moe_router",
    )(x, g.reshape(1, D_MODEL), mods, mods, wr, br)


TABLE_ROWS = -(-(N_TILES * TILE_CHUNKS) // LANES)


def _tables_kernel(cnt_ref, tend_ref, disp_ref, comb_ref):
    ne = N_EXPERTS
    cnt = jnp.concatenate([cnt_ref[...].astype(F32), jnp.zeros((LANES - N_BLOCKS, LANES), F32)], axis=0)
    m = jnp.ceil(cnt * (1.0 / CHUNK))
    r = lax.broadcasted_iota(jnp.int32, (LANES, LANES), 0)
    c = lax.broadcasted_iota(jnp.int32, (LANES, LANES), 1)
    mb = m.astype(BF16)
    bstart = _dot(mb, jnp.where(r < c, 1.0, 0.0).astype(BF16))
    cm = _dot(jnp.where(c < r, 1.0, 0.0).astype(BF16), mb)
    total = jnp.sum(m, axis=0, keepdims=True)
    tiles = jnp.ceil(total * (1.0 / TILE_CHUNKS))
    tile_end = _dot(jnp.broadcast_to(tiles, (SUBLANES, LANES)).astype(BF16),
                    jnp.where(r <= c, 1.0, 0.0).astype(BF16))[0:1, :]
    choff = (tile_end - tiles) * TILE_CHUNKS
    tend_ref[...] = tile_end.astype(jnp.int32)

    start_t = (choff + cm).T
    m_t = m.T
    bstart_t = bstart.T
    lane = lax.broadcasted_iota(jnp.int32, (ne, LANES), 1).astype(F32)
    acc = [jnp.zeros((ne, LANES), F32) for _ in range(TABLE_ROWS)]
    for b in range(N_BLOCKS):
        lo = start_t[0:ne, b:b + 1]
        hi = lo + m_t[0:ne, b:b + 1]
        bs = bstart_t[0:ne, b:b + 1]
        fwd = (b * BLOCK_CHUNKS) + bs - lo
        for j in range(TABLE_ROWS):
            g = lane + float(j * LANES)
            acc[j] = acc[j] + jnp.where(g >= lo, jnp.where(g < hi, fwd + g, 0.0), 0.0)
        back = jnp.where(lane >= bs, jnp.where(lane < hi - lo + bs, (lo - bs + lane) * CHUNK + 1.0, 0.0), 0.0)
        comb_ref[b:b + 1, :] = (jnp.sum(back, axis=0, keepdims=True) - 1.0).astype(jnp.int32)
    for j in range(TABLE_ROWS):
        disp_ref[j:j + 1, :] = (jnp.sum(acc[j], axis=0, keepdims=True) * CHUNK).astype(jnp.int32)


def _chunk_tables(counts):
    return pl.pallas_call(
        _tables_kernel,
        out_shape=[
            jax.ShapeDtypeStruct((1, LANES), jnp.int32),
            jax.ShapeDtypeStruct((TABLE_ROWS, LANES), jnp.int32),
            jax.ShapeDtypeStruct((N_BLOCKS, LANES), jnp.int32),
        ],
        name="moe_chunk_tables",
    )(counts.reshape(N_BLOCKS, LANES))


def _chunk_copies(read_row, n_chunks, src_hbm, dst, sem, wait, skip_negative=False, priority=0):
    for c in range(n_chunks):
        row = read_row(c)

        def copy(row=row, c=c):
            cp = pltpu.make_async_copy(src_hbm.at[pl.ds(pl.multiple_of(row, CHUNK), CHUNK)],
                                       dst.at[pl.ds(c * CHUNK, CHUNK)], sem)
            if wait:
                cp.wait()
            else:
                cp.start(priority=priority)

        if skip_negative:
            pl.when(row >= 0)(copy)
        else:
            copy()


IN_DEPTH = 4


def _expert_kernel(tend_ref, src_ref, xb_hbm, w1_ref, w3_ref, w2_ref, y_hbm,
                   xbuf, ybuf, zbuf, w13_scr, w2_scr, in_sem, out_sem, zero_sem):
    e = pl.program_id(0)
    first = jnp.where(e == 0, 0, tend_ref[0, jnp.maximum(e - 1, 0)])
    last = tend_ref[0, e]
    n_used = tend_ref[0, N_EXPERTS - 1]
    tiles_per_row = LANES // TILE_CHUNKS

    def fetch(tile, s, wait):
        row = tile // tiles_per_row
        col = (tile % tiles_per_row) * TILE_CHUNKS
        _chunk_copies(lambda c: src_ref[row, col + c], TILE_CHUNKS, xb_hbm, xbuf.at[s], in_sem.at[s], wait,
                      priority=1)

    def store(tile, s):
        rows = pl.ds(pl.multiple_of(tile * EXPERT_TILE, EXPERT_TILE), EXPERT_TILE)
        return pltpu.make_async_copy(ybuf.at[s], y_hbm.at[rows], out_sem.at[s])

    def zero_fill(tile):
        rows = pl.ds(pl.multiple_of(tile * EXPERT_TILE, EXPERT_TILE), EXPERT_TILE)
        return pltpu.make_async_copy(zbuf, y_hbm.at[rows], zero_sem.at[0])

    @pl.when(e == 0)
    def _():
        zbuf[...] = jnp.zeros_like(zbuf)

        def start(t, carry):
            zero_fill(t).start()
            return carry

        lax.fori_loop(n_used, N_TILES, start, 0)

    lead = IN_DEPTH - 1

    @pl.when(last > first)
    def _():
        @pl.when(first == 0)
        def _():
            for k in range(lead):
                fetch(jnp.minimum(k, n_used - 1), k, False)

        w13_scr[:, 0:D_EXPERT] = w1_ref[0, 0].astype(BF16)
        w13_scr[:, D_EXPERT:2 * D_EXPERT] = w3_ref[0, 0].astype(BF16)
        w2_scr[...] = w2_ref[0, 0].astype(BF16)

    def tile_body(t, carry):
        s = t % 2
        fetch(t, t % IN_DEPTH, True)
        ab = _dot(xbuf[t % IN_DEPTH], w13_scr[...])
        fetch(jnp.minimum(t + lead, n_used - 1), (t + lead) % IN_DEPTH, False)
        a = ab[:, 0:D_EXPERT]
        hid = (a * jax.nn.sigmoid(a)) * ab[:, D_EXPERT:2 * D_EXPERT]
        y = _dot(hid.astype(BF16), w2_scr[...]).astype(BF16)

        @pl.when(t >= 2)
        def _():
            store(t, s).wait()

        ybuf[s] = y
        store(t, s).start()
        return carry

    lax.fori_loop(first, last, tile_body, 0)

    @pl.when(e == N_EXPERTS - 1)
    def _():
        for k in range(lead):
            fetch(n_used - 1, (n_used + k) % IN_DEPTH, True)

        @pl.when(n_used >= 1)
        def _():
            store(0, (n_used - 1) % 2).wait()

        @pl.when(n_used >= 2)
        def _():
            store(0, n_used % 2).wait()

        def drain(t, carry):
            zero_fill(t).wait()
            return carry

        lax.fori_loop(n_used, N_TILES, drain, 0)


def _experts(tile_end, disp_src, xb, w1, w3, w2, layer):
    wsel = lambda e, tend, src: (layer, e, 0, 0)
    return pl.pallas_call(
        _expert_kernel,
        grid_spec=pltpu.PrefetchScalarGridSpec(
            num_scalar_prefetch=2,
            grid=(N_EXPERTS,),
            in_specs=[
                pl.BlockSpec(memory_space=pl.ANY),
                pl.BlockSpec((1, 1, D_MODEL, D_EXPERT), wsel),
                pl.BlockSpec((1, 1, D_MODEL, D_EXPERT), wsel),
                pl.BlockSpec((1, 1, D_EXPERT, D_MODEL), wsel),
            ],
            out_specs=pl.BlockSpec(memory_space=pl.ANY),
            scratch_shapes=[pltpu.VMEM((IN_DEPTH, EXPERT_TILE, D_MODEL), BF16),
                            pltpu.VMEM((2, EXPERT_TILE, D_MODEL), BF16),
                            pltpu.VMEM((EXPERT_TILE, D_MODEL), BF16),
                            pltpu.VMEM((D_MODEL, 2 * D_EXPERT), BF16),
                            pltpu.VMEM((D_EXPERT, D_MODEL), BF16),
                            pltpu.SemaphoreType.DMA((IN_DEPTH,)),
                            pltpu.SemaphoreType.DMA((2,)),
                            pltpu.SemaphoreType.DMA((1,))],
        ),
        out_shape=jax.ShapeDtypeStruct((N_TILES * EXPERT_TILE, D_MODEL), BF16),
        compiler_params=_cparams(("arbitrary",)),
        name="moe_experts",
    )(tile_end, disp_src, xb, w1, w3, w2)


def _combine_kernel(src_ref, y_hbm, x_ref, pos_ref, gate_ref, mg_ref, *rest, tm, final):
    if final:
        fg_ref, op_ref, os_ref, ybuf, sem = rest
    else:
        o_ref, ybuf, sem = rest
    i = pl.program_id(0)
    slot = i % 2

    def fetch(blk, s, wait):
        _chunk_copies(lambda c: src_ref[blk, c], BLOCK_CHUNKS, y_hbm, ybuf.at[s], sem.at[s], wait,
                      skip_negative=True)

    @pl.when(i == 0)
    def _():
        ybuf[...] = jnp.zeros_like(ybuf)
        fetch(0, 0, False)

    @pl.when(i + 1 < N_BLOCKS)
    def _():
        fetch(i + 1, 1 - slot, False)

    fetch(i, slot, True)
    pos = pos_ref[...]
    sel = pos >= 0.0
    gate = gate_ref[...]
    pos_a, pos_b = _row_slots(pos, sel)
    gate_a = jnp.sum(jnp.where(pos == pos_a, gate, 0.0), axis=-1, keepdims=True)
    gate_b = jnp.sum(jnp.where(jnp.logical_and(sel, pos == pos_b), gate, 0.0), axis=-1, keepdims=True)
    row = lax.broadcasted_iota(jnp.int32, (tm, BLOCK_ROWS), 1).astype(F32)
    weights = jnp.where(row == pos_a, gate_a, jnp.where(row == pos_b, gate_b, 0.0))
    moe = _dot(weights.astype(BF16), ybuf[slot])
    out = x_ref[...] + mg_ref[0] * moe
    if not final:
        o_ref[...] = out
        return

    ms = jnp.mean(out * out, axis=-1, keepdims=True)
    out = (out * lax.rsqrt(ms + EPS)) * fg_ref[...]

    @pl.when(i < N_PROMPT // tm)
    def _():
        op_ref[...] = out

    @pl.when(i >= N_PROMPT // tm)
    def _():
        os_ref[...] = out


def _combine(comb_src, y, x, pos, gate, mods, layer, final_g=None):
    tm = MOE_BLOCK
    final = final_g is not None
    np_blocks = N_PROMPT // tm

    def gate_idx(i, s):
        return ((layer * MOD_ROWS + _mod_row(i, tm)) * 6 + 5, 0, 0)

    in_specs = [
        pl.BlockSpec(memory_space=pl.ANY),
        pl.BlockSpec((tm, D_MODEL), lambda i, s: (i, 0)),
        pl.BlockSpec((tm, LANES), lambda i, s: (i, 0)),
        pl.BlockSpec((tm, LANES), lambda i, s: (i, 0)),
        pl.BlockSpec((1, 1, D_MODEL), gate_idx),
    ]
    args = [comb_src, y, x, pos, gate, mods]
    if final:
        in_specs.append(pl.BlockSpec((1, D_MODEL), lambda i, s: (0, 0)))
        args.append(final_g.reshape(1, D_MODEL))
        out_specs = [pl.BlockSpec((tm, D_MODEL), lambda i, s: (jnp.minimum(i, np_blocks - 1), 0)),
                     pl.BlockSpec((tm, D_MODEL), lambda i, s: (jnp.maximum(i - np_blocks, 0), 0))]
        out_shape = [jax.ShapeDtypeStruct((N_PROMPT, D_MODEL), F32),
                     jax.ShapeDtypeStruct((N_SAMPLE, D_MODEL), F32)]
    else:
        out_specs = pl.BlockSpec((tm, D_MODEL), lambda i, s: (i, 0))
        out_shape = jax.ShapeDtypeStruct((N_TOK, D_MODEL), F32)
    return pl.pallas_call(
        functools.partial(_combine_kernel, tm=tm, final=final),
        grid_spec=pltpu.PrefetchScalarGridSpec(
            num_scalar_prefetch=1,
            grid=(N_BLOCKS,),
            in_specs=in_specs,
            out_specs=out_specs,
            scratch_shapes=[pltpu.VMEM((2, BLOCK_ROWS, D_MODEL), BF16),
                            pltpu.SemaphoreType.DMA((2,))],
        ),
        out_shape=out_shape,
        compiler_params=_cparams(("arbitrary",)),
        name="moe_combine_final" if final else "moe_combine",
    )(*args)


def _moe(x, g, mods, layer, w_group, b_group, w_expert, b_expert, w1, w3, w2, final_g=None):
    pad = LANES - N_EXPERTS - N_GROUPS
    wr = jnp.concatenate([w_expert, w_group, jnp.zeros((D_MODEL, pad), F32)], axis=1)
    br = jnp.concatenate([b_expert, b_group, jnp.zeros((pad,), F32)]).reshape(1, LANES)
    xb, pos, gate, counts = _router(x, g, mods, layer, wr, br)
    tile_end, disp_src, comb_src = _chunk_tables(counts)
    y = _experts(tile_end, disp_src, xb, w1, w3, w2, layer)
    return _combine(comb_src, y, x, pos, gate, mods, layer, final_g)


def _cache_kernel(*refs):
    n = (len(refs) - 2) // 2
    k_refs, v_refs, ok_ref, ov_ref = refs[:n], refs[n:2 * n], refs[2 * n], refs[2 * n + 1]
    j = pl.program_id(1)
    for src_refs, o_ref in ((k_refs, ok_ref), (v_refs, ov_ref)):
        x = src_refs[0][...]
        for a in range(1, n):
            x = jnp.where(j == a, src_refs[a][...], x)
        x = x.astype(F32)
        for h in range(N_HEADS):
            o_ref[0, 0, :, h, :] = x[:, h * HEAD_W:(h + 1) * HEAD_W]


def _new_caches(qkvs):
    n = len(qkvs)
    k_spec = pl.BlockSpec((SEQ, D_MODEL), lambda b, j: (b, 1))
    v_spec = pl.BlockSpec((SEQ, D_MODEL), lambda b, j: (b, 2))
    out_spec = pl.BlockSpec((1, 1, SEQ, N_HEADS, HEAD_W), lambda b, j: (b, j, 0, 0, 0))
    shape = jax.ShapeDtypeStruct((BATCH, n, SEQ, N_HEADS, HEAD_W), F32)
    return pl.pallas_call(
        _cache_kernel,
        grid=(BATCH, n),
        in_specs=[k_spec] * n + [v_spec] * n,
        out_specs=[out_spec, out_spec],
        out_shape=[shape, shape],
        compiler_params=_cparams(("arbitrary", "arbitrary")),
        name="new_caches",
    )(*qkvs, *qkvs)


def kernel(x_prompt, x_sample, cache_k, cache_v, c, c_ctx, ada_w, ada_b, norm1_g, norm2_g, final_g, hy_w_in, hy_conv_w, hy_conv_b, hy_f_w1, hy_f_b1, hy_f_w2, hy_f_b2, hy_f_w3, hy_f_b3, hy_f_freq, hy_bias, hy_w_out, at_w_qkv, at_lam_q1, at_lam_k1, at_lam_q2, at_lam_k2, at_subln_g, at_w_o, moe_w_group, moe_b_group, moe_w_expert, moe_b_expert, moe_w1, moe_w3, moe_w2):
    x = (x_prompt.reshape(N_PROMPT, D_MODEL), x_sample.reshape(N_SAMPLE, D_MODEL))
    cond = jnp.concatenate([c_ctx[None, :], c, jnp.zeros((MOD_ROWS - 1 - DEC_BATCH, D_MODEL), F32)], axis=0)
    mods = _ada_table(cond, ada_w, ada_b).reshape(DEPTH * MOD_ROWS * 6, 1, D_MODEL)
    ctx_k = cache_k.reshape(DEC_BATCH, DEPTH // 2, PAST_LEN, D_MODEL)
    ctx_v = cache_v.reshape(DEC_BATCH, DEPTH // 2, PAST_LEN, D_MODEL)

    dft = {}
    for L in (SEQ, DEC_SEQ):
        fwd, inv = _dft_mats(L)
        dft[L] = (jnp.asarray(fwd).astype(BF16), jnp.asarray(inv).astype(BF16))

    qkvs = []
    for l in range(DEPTH):
        j = l // 2
        if l % 2 == 0:
            z = _norm_matmul(x, norm1_g[l], mods, l, 0, 1, hy_w_in, j)
            ys = []
            for L, rb0, n_seq, dt in ((SEQ, 0, BATCH, 1024), (DEC_SEQ, N_PROMPT // DEC_SEQ, DEC_BATCH, 512)):
                f_bf, finv = dft[L]
                kc = _filter_spectra(L, hy_f_w1[j], hy_f_b1[j], hy_f_w2[j], hy_f_b2[j], hy_f_w3[j],
                                     hy_f_b3[j], hy_f_freq[j], f_bf)
                ys.append(_hyena_conv(z, hy_conv_w[j], hy_conv_b[j], f_bf, finv, kc, hy_bias[j],
                                      L=L, row_block0=rb0, n_seq=n_seq, dt=dt))
            x = _matmul_residual((ys[0], ys[1]), hy_w_out, j, x, mods, l, 2)
        else:
            lam_init = 0.8 - 0.6 * math.exp(-0.3 * l)
            qkv = _norm_matmul(x, norm1_g[l], mods, l, 0, 1, at_w_qkv, j)
            lam_params = (at_lam_q1[j], at_lam_k1[j], at_lam_q2[j], at_lam_k2[j])
            op = _attention(qkv, lam_params, at_subln_g[j], lam_init, L=SEQ, row_block0=0,
                            n_seq=BATCH, tq=SEQ)
            os_ = _attention(qkv, lam_params, at_subln_g[j], lam_init, L=DEC_SEQ,
                             row_block0=N_PROMPT // DEC_SEQ, n_seq=DEC_BATCH, tq=256,
                             ctx=(ctx_k, ctx_v), j=j)
            qkvs.append(qkv)
            x = _matmul_residual((op, os_), at_w_o, j, x, mods, l, 2)
        x = _moe(x, norm2_g[l], mods, l, moe_w_group[l], moe_b_group[l], moe_w_expert[l],
                 moe_b_expert[l], moe_w1, moe_w3, moe_w2, final_g if l == DEPTH - 1 else None)
    y_prompt = x[0].reshape(BATCH, SEQ, D_MODEL)
    y_sample = x[1].reshape(DEC_BATCH, DEC_SEQ, D_MODEL)
    new_cache_k, new_cache_v = _new_caches(qkvs)
    return (y_prompt, y_sample, new_cache_k, new_cache_v)
```

```python
import functools
import math

import numpy as np
import jax
import jax.numpy as jnp
from jax import lax
from jax.experimental import pallas as pl
from jax.experimental.pallas import tpu as pltpu

F32 = jnp.float32
BF16 = jnp.bfloat16

D_MODEL = 1024
BATCH = 16
SEQ = 256
DEPTH = 4
DEC_BATCH = 4
DEC_SEQ = 1024
PAST_LEN = 256
GRID_W = 64
EPS = 1e-6
HY_BANDS = 16
HY_EMB = 1 + 2 * HY_BANDS
HY_FILTER_HIDDEN = 64
HY_FAST_DECAY = 0.3
HY_SLOW_DECAY = 1.5
HY_TARGET = 1e-2
N_HEADS = 8
HEAD_DIM = 64
ROT_AXIS = HEAD_DIM // 2
ROPE_BASE = 10000.0
N_GROUPS = 4
EXPERTS_PER_GROUP = 8
N_EXPERTS = N_GROUPS * EXPERTS_PER_GROUP
D_EXPERT = D_MODEL // 4

N_PROMPT = BATCH * SEQ
N_SAMPLE = DEC_BATCH * DEC_SEQ
N_TOK = N_PROMPT + N_SAMPLE
MOD_ROWS = 8
LANES = 128
SUBLANES = 8
HEAD_W = 2 * HEAD_DIM
VMEM_LIMIT = 56 * 1024 * 1024

MOE_BLOCK = 256
N_BLOCKS = N_TOK // MOE_BLOCK
CHUNK = 2 * SUBLANES
BLOCK_ROWS = 2 * MOE_BLOCK + N_EXPERTS * CHUNK
BLOCK_CHUNKS = BLOCK_ROWS // CHUNK
EXPERT_TILE = 256
TILE_CHUNKS = EXPERT_TILE // CHUNK
N_TILES = (2 * N_TOK + N_BLOCKS * N_EXPERTS * (CHUNK - 1)) // EXPERT_TILE + N_EXPERTS


def _cparams(sem):
    return pltpu.CompilerParams(dimension_semantics=sem, vmem_limit_bytes=VMEM_LIMIT)


def _mod_row(blk, tm):
    start = blk * tm
    return jnp.where(start < N_PROMPT, 0, 1 + (start - N_PROMPT) // DEC_SEQ)


def _mod_index(layer, which, tm):
    def index_map(i, *_):
        return ((layer * MOD_ROWS + _mod_row(i, tm)) * 6 + which, 0, 0)
    return index_map


@functools.lru_cache(maxsize=None)
def _dft_mats(L):
    k = np.arange(L, dtype=np.float64)[:, None]
    j = np.arange(L, dtype=np.float64)[None, :]
    ang = np.pi * ((k * j) % (2 * L)) / L
    c = np.cos(ang)
    s = np.sin(ang)
    s[0, :] = np.where(np.arange(L) % 2 == 0, 1.0, -1.0)
    fwd = np.concatenate([c, s], axis=0)
    scale = np.full((2 * L,), 1.0 / L)
    scale[0] = scale[L] = 0.5 / L
    inv = fwd.T * scale[None, :]
    return fwd.astype(np.float32), inv.astype(np.float32)


@functools.lru_cache(maxsize=None)
def _filter_feats(L):
    pos = np.arange(L, dtype=np.float64)
    t = pos / (L - 1)
    bands = np.linspace(1e-4, HY_BANDS - 1, HY_BANDS)
    ang = (2.0 * math.pi / L) * pos[:, None] * bands[None, :]
    feats = np.concatenate([t[:, None], np.cos(ang), -np.sin(ang)], axis=-1)
    feats = np.pad(feats, ((0, 0), (0, LANES - HY_EMB)))
    deltas = np.abs(np.linspace(math.log(HY_TARGET) / HY_SLOW_DECAY,
                                math.log(HY_TARGET) / HY_FAST_DECAY, D_MODEL))
    window = np.exp(-t[:, None] * deltas[None, :])
    alt = np.where(np.arange(L) % 2 == 0, 1.0, -1.0)[:, None] * np.ones((1, LANES))
    return feats.astype(np.float32), window.astype(np.float32), alt.astype(np.float32)


@functools.lru_cache(maxsize=None)
def _rope_tables():
    pos = np.arange(DEC_SEQ)
    row = (pos // GRID_W).astype(np.float64)
    col = (pos % GRID_W).astype(np.float64)
    lane = np.arange(HEAD_W)
    d = lane % HEAD_DIM
    axis = d // ROT_AXIS
    n = d % ROT_AXIS
    half = n // (ROT_AXIS // 2)
    f = n % (ROT_AXIS // 2)
    inv = ROPE_BASE ** (-f.astype(np.float64) / (ROT_AXIS // 2))
    p = np.where(axis[None, :] == 0, row[:, None], col[:, None])
    ang = p * inv[None, :]
    sign = np.where(half == 0, -1.0, 1.0)[None, :]
    return np.cos(ang).astype(np.float32), (np.sin(ang) * sign).astype(np.float32)


def _split_bf16(a):
    hi = a.astype(BF16)
    lo = (a - hi.astype(F32)).astype(BF16)
    return hi, lo


def _dot(a, b):
    return jnp.dot(a, b, preferred_element_type=F32)


def _dot3(a_hi, a_lo, b_hi, b_lo):
    return _dot(a_hi, b_hi) + (_dot(a_hi, b_lo) + _dot(a_lo, b_hi))


def _ada_kernel(c_ref, w_ref, b_ref, o_ref):
    c = c_ref[...]
    s = c * jax.nn.sigmoid(c)
    o_ref[0] = _dot(s.astype(BF16), w_ref[0].astype(BF16)) + b_ref[0]


def _ada_table(cond, ada_w, ada_b):
    tn = 1536
    n = 6 * D_MODEL
    return pl.pallas_call(
        _ada_kernel,
        grid=(DEPTH, n // tn),
        in_specs=[
            pl.BlockSpec((MOD_ROWS, D_MODEL), lambda l, j: (0, 0)),
            pl.BlockSpec((1, D_MODEL, tn), lambda l, j: (l, 0, j)),
            pl.BlockSpec((1, 1, tn), lambda l, j: (l, 0, j)),
        ],
        out_specs=pl.BlockSpec((1, MOD_ROWS, tn), lambda l, j: (l, 0, j)),
        out_shape=jax.ShapeDtypeStruct((DEPTH, MOD_ROWS, n), F32),
        compiler_params=_cparams(("arbitrary", "arbitrary")),
        name="ada_table",
    )(cond, ada_w, ada_b.reshape(DEPTH, 1, n))


def _modulate(x, g, scale, shift):
    ms = jnp.mean(x * x, axis=-1, keepdims=True)
    y = (x * lax.rsqrt(ms + EPS)) * g
    return y * (1.0 + scale) + shift


MM_COLS = 512


def _row_group_specs(x, tm, width):
    np_blocks = N_PROMPT // tm
    if isinstance(x, tuple):
        specs = [pl.BlockSpec((tm, width), lambda i, *_: (jnp.minimum(i, np_blocks - 1), 0)),
                 pl.BlockSpec((tm, width), lambda i, *_: (jnp.maximum(i - np_blocks, 0), 0))]
        return specs, list(x)
    specs = [pl.BlockSpec((tm, width), lambda i, *_: (jnp.minimum(i, np_blocks - 1), 0)),
             pl.BlockSpec((tm, width), lambda i, *_: (jnp.maximum(i, np_blocks), 0))]
    return specs, [x, x]


def _pick_rows(i, tm, p_ref, s_ref):
    return jnp.where(i < N_PROMPT // tm, p_ref[...], s_ref[...])


def _normmm_kernel(xp_ref, xs_ref, g_ref, sh_ref, sc_ref, w_ref, o_ref, w_scr, *, tm):
    i = pl.program_id(0)

    @pl.when(i == 0)
    def _():
        w_scr[...] = w_ref[0].astype(BF16)

    x = _pick_rows(i, tm, xp_ref, xs_ref)
    h = _modulate(x, g_ref[...], sc_ref[0], sh_ref[0]).astype(BF16)
    for j in range(w_scr.shape[1] // MM_COLS):
        cols = slice(j * MM_COLS, (j + 1) * MM_COLS)
        o_ref[:, cols] = _dot(h, w_scr[:, cols]).astype(o_ref.dtype)


def _norm_matmul(x, g, mods, layer, which_shift, which_scale, w, widx, *, tm=512):
    n = w.shape[2]
    x_specs, x_args = _row_group_specs(x, tm, D_MODEL)
    return pl.pallas_call(
        functools.partial(_normmm_kernel, tm=tm),
        grid=(N_TOK // tm,),
        in_specs=x_specs + [
            pl.BlockSpec((1, D_MODEL), lambda i: (0, 0)),
            pl.BlockSpec((1, 1, D_MODEL), _mod_index(layer, which_shift, tm)),
            pl.BlockSpec((1, 1, D_MODEL), _mod_index(layer, which_scale, tm)),
            pl.BlockSpec((1, D_MODEL, n), lambda i: (widx, 0, 0), pipeline_mode=pl.Buffered(1)),
        ],
        out_specs=pl.BlockSpec((tm, n), lambda i: (i, 0)),
        out_shape=jax.ShapeDtypeStruct((N_TOK, n), BF16),
        scratch_shapes=[pltpu.VMEM((D_MODEL, n), BF16)],
        compiler_params=_cparams(("arbitrary",)),
        name="norm_matmul",
    )(*x_args, g.reshape(1, D_MODEL), mods, mods, w)


def _mmres_kernel(ap_ref, as_ref, xp_ref, xs_ref, w_ref, gate_ref, o_ref, w_scr, *, tm):
    i = pl.program_id(0)

    @pl.when(i == 0)
    def _():
        w_scr[...] = w_ref[0].astype(BF16)

    a = _pick_rows(i, tm, ap_ref, as_ref)
    x = _pick_rows(i, tm, xp_ref, xs_ref)
    gate = gate_ref[0]
    for j in range(w_scr.shape[1] // MM_COLS):
        cols = slice(j * MM_COLS, (j + 1) * MM_COLS)
        o_ref[:, cols] = x[:, cols] + gate[:, cols] * _dot(a, w_scr[:, cols])


def _matmul_residual(a, w, widx, x, mods, layer, which_gate, *, tm=512):
    n = w.shape[2]
    a_specs, a_args = _row_group_specs(a, tm, D_MODEL)
    x_specs, x_args = _row_group_specs(x, tm, n)
    return pl.pallas_call(
        functools.partial(_mmres_kernel, tm=tm),
        grid=(N_TOK // tm,),
        in_specs=a_specs + x_specs + [
            pl.BlockSpec((1, D_MODEL, n), lambda i: (widx, 0, 0), pipeline_mode=pl.Buffered(1)),
            pl.BlockSpec((1, 1, n), _mod_index(layer, which_gate, tm)),
        ],
        out_specs=pl.BlockSpec((tm, n), lambda i: (i, 0)),
        out_shape=jax.ShapeDtypeStruct((N_TOK, n), F32),
        scratch_shapes=[pltpu.VMEM((D_MODEL, n), BF16)],
        compiler_params=_cparams(("arbitrary",)),
        name="matmul_residual",
    )(*a_args, *x_args, w, mods)


def _filter_kernel(feats_ref, win_ref, alt_ref, w1_ref, b1_ref, w2_ref, b2_ref, fq_ref,
                   w3f_ref, b3f_ref, w3b_ref, b3b_ref, f_ref, o_ref, h_scr, *, L):
    def dense(a, w_ref, b_ref):
        a_hi, a_lo = _split_bf16(a)
        w_hi, w_lo = _split_bf16(w_ref[...])
        return _dot3(a_hi, a_lo, w_hi, w_lo) + b_ref[...]

    @pl.when(jnp.logical_and(pl.program_id(0) == 0, pl.program_id(1) == 0))
    def _():
        fq = fq_ref[...]
        h1 = jnp.sin(fq[0:1, :] * dense(feats_ref[...], w1_ref, b1_ref))
        h_scr[...] = jnp.sin(fq[1:2, :] * dense(h1, w2_ref, b2_ref))

    h = h_scr[...]
    win = win_ref[...]
    hf = dense(h, w3f_ref, b3f_ref) * win
    hb = dense(h, w3b_ref, b3b_ref) * win
    row = lax.broadcasted_iota(jnp.int32, hf.shape, 0)
    hb = jnp.where(row == 0, 0.0, hb)
    kr = _dot(f_ref[0:L, :], (hf + hb).astype(BF16))
    ks = _dot(f_ref[L:2 * L, :], (hf - hb).astype(BF16))
    k_nyq = jnp.sum((hf + hb) * alt_ref[:, 0:1], axis=0, keepdims=True)
    o_ref[0, 0] = kr
    o_ref[0, 1] = jnp.where(row == 0, k_nyq, kr)
    o_ref[0, 2] = jnp.where(row == 0, 0.0, ks)


def _filter_spectra(L, w1, b1, w2, b2, w3, b3, freq, f_bf, *, dt=512):
    feats, window, alt = _filter_feats(L)
    fh = HY_FILTER_HIDDEN
    nd = D_MODEL // dt
    w1p = jnp.pad(w1, ((0, LANES - HY_EMB), (0, 0)))
    const = lambda shape: pl.BlockSpec(shape, lambda o, c: tuple(0 for _ in shape))
    return pl.pallas_call(
        functools.partial(_filter_kernel, L=L),
        grid=(2, nd),
        in_specs=[
            const((L, LANES)),
            pl.BlockSpec((L, dt), lambda o, c: (0, c)),
            const((L, LANES)),
            const((LANES, fh)), const((1, fh)), const((fh, fh)), const((1, fh)), const((2, fh)),
            pl.BlockSpec((fh, dt), lambda o, c: (0, (2 * o) * nd + c)),
            pl.BlockSpec((1, dt), lambda o, c: (0, (2 * o) * nd + c)),
            pl.BlockSpec((fh, dt), lambda o, c: (0, (2 * o + 1) * nd + c)),
            pl.BlockSpec((1, dt), lambda o, c: (0, (2 * o + 1) * nd + c)),
            const((2 * L, L)),
        ],
        out_specs=pl.BlockSpec((1, 3, L, dt), lambda o, c: (o, 0, 0, c)),
        out_shape=jax.ShapeDtypeStruct((2, 3, L, D_MODEL), F32),
        scratch_shapes=[pltpu.VMEM((L, fh), F32)],
        compiler_params=_cparams(("arbitrary", "arbitrary")),
        name=f"hyena_filter_spectra_{L}",
    )(jnp.asarray(feats), jnp.asarray(window), jnp.asarray(alt), w1p, b1.reshape(1, fh), w2,
      b2.reshape(1, fh), freq, w3, b3.reshape(1, -1), w3, b3.reshape(1, -1), f_bf)


def _conv_kernel(v_ref, x1_ref, x2_ref, cwv_ref, cw1_ref, cw2_ref, cbv_ref, cb1_ref, cb2_ref,
                 f_ref, finv_ref, kc_ref, bias_ref, o_ref, *, L):
    row = lax.broadcasted_iota(jnp.int32, v_ref.shape, 0)

    def short_conv(u_ref, w_ref, b_ref):
        u = u_ref[...].astype(F32)
        prev = jnp.where(row == 0, 0.0, pltpu.roll(u, 1, 0))
        nxt = jnp.where(row == L - 1, 0.0, pltpu.roll(u, L - 1, 0))
        w = w_ref[...]
        return prev * w[0:1, :] + u * w[1:2, :] + nxt * w[2:3, :] + b_ref[...]

    def long_conv(u, order):
        spec = _dot(f_ref[...], u.astype(BF16))
        a = spec[0:L, :]
        b = spec[L:2 * L, :]
        kra = kc_ref[order, 0]
        krb = kc_ref[order, 1]
        ks = kc_ref[order, 2]
        prod = jnp.concatenate([a * kra - b * ks, a * ks + b * krb], axis=0)
        return _dot(finv_ref[...], prod.astype(BF16)) + u * bias_ref[order:order + 1, :]

    v = short_conv(v_ref, cwv_ref, cbv_ref)
    y = short_conv(x1_ref, cw1_ref, cb1_ref) * long_conv(v, 0)
    y = short_conv(x2_ref, cw2_ref, cb2_ref) * long_conv(y, 1)
    o_ref[...] = y.astype(o_ref.dtype)


def _hyena_conv(z, conv_w, conv_b, f_bf, finv_bf, kc, bias, *, L, row_block0, n_seq, dt):
    nd = D_MODEL // dt
    cb = conv_b.reshape(1, 3 * D_MODEL)
    seg = lambda s: pl.BlockSpec((L, dt), lambda c, b: (row_block0 + b, s * nd + c))
    cw = lambda s: pl.BlockSpec((3, dt), lambda c, b: (0, s * nd + c))
    cbs = lambda s: pl.BlockSpec((1, dt), lambda c, b: (0, s * nd + c))
    return pl.pallas_call(
        functools.partial(_conv_kernel, L=L),
        grid=(nd, n_seq),
        in_specs=[
            seg(0), seg(1), seg(2), cw(0), cw(1), cw(2), cbs(0), cbs(1), cbs(2),
            pl.BlockSpec((2 * L, L), lambda c, b: (0, 0), pipeline_mode=pl.Buffered(1)),
            pl.BlockSpec((L, 2 * L), lambda c, b: (0, 0), pipeline_mode=pl.Buffered(1)),
            pl.BlockSpec((2, 3, L, dt), lambda c, b: (0, 0, 0, c), pipeline_mode=pl.Buffered(1)),
            pl.BlockSpec((2, dt), lambda c, b: (0, c)),
        ],
        out_specs=pl.BlockSpec((L, dt), lambda c, b: (b, c)),
        out_shape=jax.ShapeDtypeStruct((n_seq * L, D_MODEL), BF16),
        compiler_params=_cparams(("arbitrary", "arbitrary")),
        name=f"hyena_conv_{L}",
    )(z, z, z, conv_w, conv_w, conv_w, cb, cb, cb, f_bf, finv_bf, kc, bias)


def _attn_kernel(lq1_ref, lk1_ref, lq2_ref, lk2_ref, g_ref, q_ref, k_ref, v_ref, *rest,
                 lam_init, has_ctx, tq, nq):
    if has_ctx:
        ck_ref, cv_ref, cos_ref, sin_ref, o_ref, k_scr = rest
    else:
        (o_ref,) = rest
    lam = (jnp.exp(jnp.sum(lq1_ref[...] * lk1_ref[...], axis=-1, keepdims=True))
           - jnp.exp(jnp.sum(lq2_ref[...] * lk2_ref[...], axis=-1, keepdims=True)) + lam_init)
    lane = lax.broadcasted_iota(jnp.int32, (tq, HEAD_W), 1)
    first_map = lane < HEAD_DIM
    first_half = (lane % ROT_AXIS) < (ROT_AXIS // 2)
    nt = (((1,), (1,)), ((), ()))

    def rotary(x, rows):
        x = x.astype(F32)
        partner = jnp.where(first_half, pltpu.roll(x, HEAD_W - ROT_AXIS // 2, 1),
                            pltpu.roll(x, ROT_AXIS // 2, 1))
        return (x * cos_ref[rows, :] + partner * sin_ref[rows, :]).astype(BF16)

    if has_ctx:
        for r in range(k_ref.shape[0] // tq):
            rows = slice(r * tq, (r + 1) * tq)
            for h in range(N_HEADS):
                cols = slice(h * HEAD_W, (h + 1) * HEAD_W)
                k_scr[rows, cols] = rotary(k_ref[rows, cols], rows)
        keys = k_scr
    else:
        keys = k_ref

    def q_block(rows):
        for h in range(N_HEADS):
            cols = slice(h * HEAD_W, (h + 1) * HEAD_W)
            q = q_ref[rows, cols]
            if has_ctx:
                q = rotary(q, rows)
            q = q * (HEAD_DIM ** -0.5)
            zero = jnp.zeros_like(q)
            q2 = jnp.concatenate([jnp.where(first_map, q, zero), jnp.where(first_map, zero, q)], axis=0)
            s = lax.dot_general(q2, keys[:, cols], nt, preferred_element_type=F32)
            m = jnp.max(s, axis=-1, keepdims=True)
            if has_ctx:
                sc = lax.dot_general(q2, ck_ref[0, 0, :, cols].astype(BF16), nt,
                                     preferred_element_type=F32)
                m = jnp.maximum(m, jnp.max(sc, axis=-1, keepdims=True))
                ec = jnp.exp(sc - m)
            e = jnp.exp(s - m)
            den = jnp.sum(e, axis=-1, keepdims=True)
            if has_ctx:
                den = den + jnp.sum(ec, axis=-1, keepdims=True)
            ov = _dot(e.astype(BF16), v_ref[:, cols])
            if has_ctx:
                ov = ov + _dot(ec.astype(BF16), cv_ref[0, 0, :, cols].astype(BF16))
            inv = 1.0 / den
            o = ov[0:tq] * inv[0:tq] - ov[tq:2 * tq] * (lam * inv[tq:2 * tq])
            ms = jnp.mean(o * o, axis=-1, keepdims=True)
            o = (o * lax.rsqrt(ms + EPS)) * g_ref[...] * (1.0 - lam_init)
            o_ref[rows, cols] = o.astype(o_ref.dtype)

    if nq == 1:
        q_block(slice(0, tq))
    else:
        def body(qi, carry):
            q_block(pl.ds(pl.multiple_of(qi * tq, tq), tq))
            return carry
        lax.fori_loop(0, nq, body, 0)


def _attention(qkv, lam_params, subln_g, lam_init, *, L, row_block0, n_seq, tq, ctx=None, j=0):
    small = pl.BlockSpec((1, HEAD_DIM), lambda b: (0, 0))
    in_specs = [small, small, small, small,
                pl.BlockSpec((1, HEAD_W), lambda b: (0, 0)),
                pl.BlockSpec((L, D_MODEL), lambda b: (row_block0 + b, 0)),
                pl.BlockSpec((L, D_MODEL), lambda b: (row_block0 + b, 1)),
                pl.BlockSpec((L, D_MODEL), lambda b: (row_block0 + b, 2))]
    args = [p.reshape(1, HEAD_DIM) for p in lam_params] + [subln_g.reshape(1, HEAD_W), qkv, qkv, qkv]
    scratch = []
    if ctx is not None:
        ctx_spec = pl.BlockSpec((1, 1, PAST_LEN, D_MODEL), lambda b: (b, j, 0, 0))
        rope_spec = pl.BlockSpec((L, HEAD_W), lambda b: (0, 0))
        cos, sin = _rope_tables()
        in_specs += [ctx_spec, ctx_spec, rope_spec, rope_spec]
        args += list(ctx) + [jnp.asarray(cos), jnp.asarray(sin)]
        scratch = [pltpu.VMEM((L, D_MODEL), BF16)]
    return pl.pallas_call(
        functools.partial(_attn_kernel, lam_init=lam_init, has_ctx=ctx is not None, tq=tq, nq=L // tq),
        grid=(n_seq,),
        in_specs=in_specs,
        out_specs=pl.BlockSpec((L, D_MODEL), lambda b: (b, 0)),
        out_shape=jax.ShapeDtypeStruct((n_seq * L, D_MODEL), BF16),
        scratch_shapes=scratch,
        compiler_params=_cparams(("arbitrary",)),
        name="diff_attention_ctx" if ctx is not None else "diff_attention",
    )(*args)


def _row_slots(pos, sel):
    big = float(BLOCK_ROWS)
    pos_a = jnp.max(pos, axis=-1, keepdims=True)
    pos_b = jnp.min(jnp.where(sel, pos, big), axis=-1, keepdims=True)
    return pos_a, pos_b


def _route_rows(h, wr, br):
    tm = h.shape[0]
    logits = _dot(h, wr) + br
    lane = lax.broadcasted_iota(jnp.int32, (tm, LANES), 1)
    neg = -jnp.inf
    is_grp = jnp.logical_and(lane >= N_EXPERTS, lane < N_EXPERTS + N_GROUPS)
    lg = jnp.where(is_grp, logits, neg)
    mg = jnp.max(lg, axis=-1, keepdims=True)
    g_gate = 1.0 / jnp.sum(jnp.exp(lg - mg), axis=-1, keepdims=True)
    g_idx = jnp.min(jnp.where(lg == mg, lane - N_EXPERTS, N_GROUPS), axis=-1, keepdims=True)
    in_grp = jnp.logical_and(lane < N_EXPERTS, lane // EXPERTS_PER_GROUP == g_idx)
    le = jnp.where(in_grp, logits, neg)
    m1 = jnp.max(le, axis=-1, keepdims=True)
    i1 = jnp.min(jnp.where(le == m1, lane, LANES), axis=-1, keepdims=True)
    le2 = jnp.where(lane == i1, neg, le)
    m2 = jnp.max(le2, axis=-1, keepdims=True)
    i2 = jnp.min(jnp.where(le2 == m2, lane, LANES), axis=-1, keepdims=True)
    e2 = jnp.exp(m2 - m1)
    inv = 1.0 / (1.0 + e2)
    gate = jnp.where(lane == i1, g_gate * inv, jnp.where(lane == i2, g_gate * (e2 * inv), 0.0))
    return gate, jnp.where(jnp.logical_or(lane == i1, lane == i2), 1.0, 0.0)


def _sort_block(h, onehot):
    tm = MOE_BLOCK
    sel = onehot > 0.5
    r = lax.broadcasted_iota(jnp.int32, (tm, tm), 0)
    c = lax.broadcasted_iota(jnp.int32, (tm, tm), 1)
    rank = _dot(jnp.where(c < r, 1.0, 0.0).astype(BF16), onehot.astype(BF16))
    count = jnp.sum(onehot, axis=0, keepdims=True)
    padded = jnp.ceil(count * (1.0 / CHUNK)) * CHUNK
    er = lax.broadcasted_iota(jnp.int32, (LANES, LANES), 0)
    ec = lax.broadcasted_iota(jnp.int32, (LANES, LANES), 1)
    before = jnp.where(er < ec, 1.0, 0.0).astype(BF16)
    start = _dot(jnp.broadcast_to(padded, (SUBLANES, LANES)).astype(BF16), before)[0:1, :]
    pos = jnp.where(sel, start + rank, -1.0)
    pos_a, pos_b = _row_slots(pos, sel)
    slot = lax.broadcasted_iota(jnp.int32, (tm, BLOCK_ROWS), 1).astype(F32)
    pick = jnp.where(slot == pos_a, 1.0, jnp.where(slot == pos_b, 1.0, 0.0)).astype(BF16)
    xb = lax.dot_general(pick, h, (((0,), (0,)), ((), ())), preferred_element_type=F32).astype(BF16)
    return xb, pos, count.astype(jnp.int32)


ROUTER_BLOCKS = 4


def _router_kernel(x_ref, g_ref, sh_ref, sc_ref, wr_ref, br_ref, xb_ref, pos_ref, gate_ref, cnt_ref):
    h = _modulate(x_ref[...], g_ref[...], sc_ref[0], sh_ref[0]).astype(BF16)
    gate, sel = _route_rows(h, wr_ref[...].astype(BF16), br_ref[...])
    gate_ref[...] = gate
    for k in range(ROUTER_BLOCKS):
        rows = slice(k * MOE_BLOCK, (k + 1) * MOE_BLOCK)
        xb, pos, count = _sort_block(h[rows, :], sel[rows, :])
        xb_ref[k * BLOCK_ROWS:(k + 1) * BLOCK_ROWS, :] = xb
        pos_ref[rows, :] = pos
        cnt_ref[k] = count


def _router(x, g, mods, layer, wr, br):
    tm = ROUTER_BLOCKS * MOE_BLOCK
    return pl.pallas_call(
        _router_kernel,
        grid=(N_TOK // tm,),
        in_specs=[
            pl.BlockSpec((tm, D_MODEL), lambda i: (i, 0)),
            pl.BlockSpec((1, D_MODEL), lambda i: (0, 0)),
            pl.BlockSpec((1, 1, D_MODEL), _mod_index(layer, 3, tm)),
            pl.BlockSpec((1, 1, D_MODEL), _mod_index(layer, 4, tm)),
            pl.BlockSpec((D_MODEL, LANES), lambda i: (0, 0)),
            pl.BlockSpec((1, LANES), lambda i: (0, 0)),
        ],
        out_specs=[
            pl.BlockSpec((ROUTER_BLOCKS * BLOCK_ROWS, D_MODEL), lambda i: (i, 0)),
            pl.BlockSpec((tm, LANES), lambda i: (i, 0)),
            pl.BlockSpec((tm, LANES), lambda i: (i, 0)),
            pl.BlockSpec((ROUTER_BLOCKS, 1, LANES), lambda i: (i, 0, 0)),
        ],
        out_shape=[
            jax.ShapeDtypeStruct((N_BLOCKS * BLOCK_ROWS, D_MODEL), BF16),
            jax.ShapeDtypeStruct((N_TOK, LANES), F32),
            jax.ShapeDtypeStruct((N_TOK, LANES), F32),
            jax.ShapeDtypeStruct((N_BLOCKS, 1, LANES), jnp.int32),
        ],
        compiler_params=_cparams(("arbitrary",)),
        name="moe_router",
    )(x, g.reshape(1, D_MODEL), mods, mods, wr, br)


TABLE_ROWS = -(-(N_TILES * TILE_CHUNKS) // LANES)


def _tables_kernel(cnt_ref, tend_ref, disp_ref, comb_ref):
    ne = N_EXPERTS
    cnt = jnp.concatenate([cnt_ref[...].astype(F32), jnp.zeros((LANES - N_BLOCKS, LANES), F32)], axis=0)
    m = jnp.ceil(cnt * (1.0 / CHUNK))
    r = lax.broadcasted_iota(jnp.int32, (LANES, LANES), 0)
    c = lax.broadcasted_iota(jnp.int32, (LANES, LANES), 1)
    mb = m.astype(BF16)
    bstart = _dot(mb, jnp.where(r < c, 1.0, 0.0).astype(BF16))
    cm = _dot(jnp.where(c < r, 1.0, 0.0).astype(BF16), mb)
    total = jnp.sum(m, axis=0, keepdims=True)
    tiles = jnp.ceil(total * (1.0 / TILE_CHUNKS))
    tile_end = _dot(jnp.broadcast_to(tiles, (SUBLANES, LANES)).astype(BF16),
                    jnp.where(r <= c, 1.0, 0.0).astype(BF16))[0:1, :]
    choff = (tile_end - tiles) * TILE_CHUNKS
    tend_ref[...] = tile_end.astype(jnp.int32)

    start_t = (choff + cm).T
    m_t = m.T
    bstart_t = bstart.T
    lane = lax.broadcasted_iota(jnp.int32, (ne, LANES), 1).astype(F32)
    acc = [jnp.zeros((ne, LANES), F32) for _ in range(TABLE_ROWS)]
    for b in range(N_BLOCKS):
        lo = start_t[0:ne, b:b + 1]
        hi = lo + m_t[0:ne, b:b + 1]
        bs = bstart_t[0:ne, b:b + 1]
        fwd = (b * BLOCK_CHUNKS) + bs - lo
        for j in range(TABLE_ROWS):
            g = lane + float(j * LANES)
            acc[j] = acc[j] + jnp.where(g >= lo, jnp.where(g < hi, fwd + g, 0.0), 0.0)
        back = jnp.where(lane >= bs, jnp.where(lane < hi - lo + bs, (lo - bs + lane) * CHUNK + 1.0, 0.0), 0.0)
        comb_ref[b:b + 1, :] = (jnp.sum(back, axis=0, keepdims=True) - 1.0).astype(jnp.int32)
    for j in range(TABLE_ROWS):
        disp_ref[j:j + 1, :] = (jnp.sum(acc[j], axis=0, keepdims=True) * CHUNK).astype(jnp.int32)


def _chunk_tables(counts):
    return pl.pallas_call(
        _tables_kernel,
        out_shape=[
            jax.ShapeDtypeStruct((1, LANES), jnp.int32),
            jax.ShapeDtypeStruct((TABLE_ROWS, LANES), jnp.int32),
            jax.ShapeDtypeStruct((N_BLOCKS, LANES), jnp.int32),
        ],
        name="moe_chunk_tables",
    )(counts.reshape(N_BLOCKS, LANES))


def _chunk_copies(read_row, n_chunks, src_hbm, dst, sem, wait, skip_negative=False, priority=0):
    for c in range(n_chunks):
        row = read_row(c)

        def copy(row=row, c=c):
            cp = pltpu.make_async_copy(src_hbm.at[pl.ds(pl.multiple_of(row, CHUNK), CHUNK)],
                                       dst.at[pl.ds(c * CHUNK, CHUNK)], sem)
            if wait:
                cp.wait()
            else:
                cp.start(priority=priority)

        if skip_negative:
            pl.when(row >= 0)(copy)
        else:
            copy()


IN_DEPTH = 6


def _expert_kernel(tend_ref, src_ref, xb_hbm, w1_ref, w3_ref, w2_ref, y_hbm,
                   xbuf, ybuf, zbuf, w13_scr, w2_scr, in_sem, out_sem, zero_sem):
    e = pl.program_id(0)
    first = jnp.where(e == 0, 0, tend_ref[0, jnp.maximum(e - 1, 0)])
    last = tend_ref[0, e]
    n_used = tend_ref[0, N_EXPERTS - 1]
    tiles_per_row = LANES // TILE_CHUNKS

    def fetch(tile, s, wait):
        row = tile // tiles_per_row
        col = (tile % tiles_per_row) * TILE_CHUNKS
        _chunk_copies(lambda c: src_ref[row, col + c], TILE_CHUNKS, xb_hbm, xbuf.at[s], in_sem.at[s], wait,
                      priority=1)

    def store(tile, s):
        rows = pl.ds(pl.multiple_of(tile * EXPERT_TILE, EXPERT_TILE), EXPERT_TILE)
        return pltpu.make_async_copy(ybuf.at[s], y_hbm.at[rows], out_sem.at[s])

    def zero_fill(tile):
        rows = pl.ds(pl.multiple_of(tile * EXPERT_TILE, EXPERT_TILE), EXPERT_TILE)
        return pltpu.make_async_copy(zbuf, y_hbm.at[rows], zero_sem.at[0])

    @pl.when(e == 0)
    def _():
        zbuf[...] = jnp.zeros_like(zbuf)

        def start(t, carry):
            zero_fill(t).start()
            return carry

        lax.fori_loop(n_used, N_TILES, start, 0)

    lead = IN_DEPTH - 1

    @pl.when(last > first)
    def _():
        @pl.when(first == 0)
        def _():
            for k in range(lead):
                fetch(jnp.minimum(k, n_used - 1), k, False)

        w13_scr[:, 0:D_EXPERT] = w1_ref[0, 0].astype(BF16)
        w13_scr[:, D_EXPERT:2 * D_EXPERT] = w3_ref[0, 0].astype(BF16)
        w2_scr[...] = w2_ref[0, 0].astype(BF16)

    def tile_body(t, carry):
        s = t % 2
        fetch(t, t % IN_DEPTH, True)
        ab = _dot(xbuf[t % IN_DEPTH], w13_scr[...])
        fetch(jnp.minimum(t + lead, n_used - 1), (t + lead) % IN_DEPTH, False)
        a = ab[:, 0:D_EXPERT]
        hid = (a * jax.nn.sigmoid(a)) * ab[:, D_EXPERT:2 * D_EXPERT]
        y = _dot(hid.astype(BF16), w2_scr[...]).astype(BF16)

        @pl.when(t >= 2)
        def _():
            store(t, s).wait()

        ybuf[s] = y
        store(t, s).start()
        return carry

    lax.fori_loop(first, last, tile_body, 0)

    @pl.when(e == N_EXPERTS - 1)
    def _():
        for k in range(lead):
            fetch(n_used - 1, (n_used + k) % IN_DEPTH, True)

        @pl.when(n_used >= 1)
        def _():
            store(0, (n_used - 1) % 2).wait()

        @pl.when(n_used >= 2)
        def _():
            store(0, n_used % 2).wait()

        def drain(t, carry):
            zero_fill(t).wait()
            return carry

        lax.fori_loop(n_used, N_TILES, drain, 0)


def _experts(tile_end, disp_src, xb, w1, w3, w2, layer):
    wsel = lambda e, tend, src: (layer, e, 0, 0)
    return pl.pallas_call(
        _expert_kernel,
        grid_spec=pltpu.PrefetchScalarGridSpec(
            num_scalar_prefetch=2,
            grid=(N_EXPERTS,),
            in_specs=[
                pl.BlockSpec(memory_space=pl.ANY),
                pl.BlockSpec((1, 1, D_MODEL, D_EXPERT), wsel),
                pl.BlockSpec((1, 1, D_MODEL, D_EXPERT), wsel),
                pl.BlockSpec((1, 1, D_EXPERT, D_MODEL), wsel),
            ],
            out_specs=pl.BlockSpec(memory_space=pl.ANY),
            scratch_shapes=[pltpu.VMEM((IN_DEPTH, EXPERT_TILE, D_MODEL), BF16),
                            pltpu.VMEM((2, EXPERT_TILE, D_MODEL), BF16),
                            pltpu.VMEM((EXPERT_TILE, D_MODEL), BF16),
                            pltpu.VMEM((D_MODEL, 2 * D_EXPERT), BF16),
                            pltpu.VMEM((D_EXPERT, D_MODEL), BF16),
                            pltpu.SemaphoreType.DMA((IN_DEPTH,)),
                            pltpu.SemaphoreType.DMA((2,)),
                            pltpu.SemaphoreType.DMA((1,))],
        ),
        out_shape=jax.ShapeDtypeStruct((N_TILES * EXPERT_TILE, D_MODEL), BF16),
        compiler_params=_cparams(("arbitrary",)),
        name="moe_experts",
    )(tile_end, disp_src, xb, w1, w3, w2)


def _combine_kernel(src_ref, y_hbm, x_ref, pos_ref, gate_ref, mg_ref, *rest, tm, final):
    if final:
        fg_ref, op_ref, os_ref, ybuf, sem = rest
    else:
        o_ref, ybuf, sem = rest
    i = pl.program_id(0)
    slot = i % 2

    def fetch(blk, s, wait):
        _chunk_copies(lambda c: src_ref[blk, c], BLOCK_CHUNKS, y_hbm, ybuf.at[s], sem.at[s], wait,
                      skip_negative=True)

    @pl.when(i == 0)
    def _():
        ybuf[...] = jnp.zeros_like(ybuf)
        fetch(0, 0, False)

    @pl.when(i + 1 < N_BLOCKS)
    def _():
        fetch(i + 1, 1 - slot, False)

    fetch(i, slot, True)
    pos = pos_ref[...]
    sel = pos >= 0.0
    gate = gate_ref[...]
    pos_a, pos_b = _row_slots(pos, sel)
    gate_a = jnp.sum(jnp.where(pos == pos_a, gate, 0.0), axis=-1, keepdims=True)
    gate_b = jnp.sum(jnp.where(jnp.logical_and(sel, pos == pos_b), gate, 0.0), axis=-1, keepdims=True)
    row = lax.broadcasted_iota(jnp.int32, (tm, BLOCK_ROWS), 1).astype(F32)
    weights = jnp.where(row == pos_a, gate_a, jnp.where(row == pos_b, gate_b, 0.0))
    moe = _dot(weights.astype(BF16), ybuf[slot])
    out = x_ref[...] + mg_ref[0] * moe
    if not final:
        o_ref[...] = out
        return

    ms = jnp.mean(out * out, axis=-1, keepdims=True)
    out = (out * lax.rsqrt(ms + EPS)) * fg_ref[...]

    @pl.when(i < N_PROMPT // tm)
    def _():
        op_ref[...] = out

    @pl.when(i >= N_PROMPT // tm)
    def _():
        os_ref[...] = out


def _combine(comb_src, y, x, pos, gate, mods, layer, final_g=None):
    tm = MOE_BLOCK
    final = final_g is not None
    np_blocks = N_PROMPT // tm

    def gate_idx(i, s):
        return ((layer * MOD_ROWS + _mod_row(i, tm)) * 6 + 5, 0, 0)

    in_specs = [
        pl.BlockSpec(memory_space=pl.ANY),
        pl.BlockSpec((tm, D_MODEL), lambda i, s: (i, 0)),
        pl.BlockSpec((tm, LANES), lambda i, s: (i, 0)),
        pl.BlockSpec((tm, LANES), lambda i, s: (i, 0)),
        pl.BlockSpec((1, 1, D_MODEL), gate_idx),
    ]
    args = [comb_src, y, x, pos, gate, mods]
    if final:
        in_specs.append(pl.BlockSpec((1, D_MODEL), lambda i, s: (0, 0)))
        args.append(final_g.reshape(1, D_MODEL))
        out_specs = [pl.BlockSpec((tm, D_MODEL), lambda i, s: (jnp.minimum(i, np_blocks - 1), 0)),
                     pl.BlockSpec((tm, D_MODEL), lambda i, s: (jnp.maximum(i - np_blocks, 0), 0))]
        out_shape = [jax.ShapeDtypeStruct((N_PROMPT, D_MODEL), F32),
                     jax.ShapeDtypeStruct((N_SAMPLE, D_MODEL), F32)]
    else:
        out_specs = pl.BlockSpec((tm, D_MODEL), lambda i, s: (i, 0))
        out_shape = jax.ShapeDtypeStruct((N_TOK, D_MODEL), F32)
    return pl.pallas_call(
        functools.partial(_combine_kernel, tm=tm, final=final),
        grid_spec=pltpu.PrefetchScalarGridSpec(
            num_scalar_prefetch=1,
            grid=(N_BLOCKS,),
            in_specs=in_specs,
            out_specs=out_specs,
            scratch_shapes=[pltpu.VMEM((2, BLOCK_ROWS, D_MODEL), BF16),
                            pltpu.SemaphoreType.DMA((2,))],
        ),
        out_shape=out_shape,
        compiler_params=_cparams(("arbitrary",)),
        name="moe_combine_final" if final else "moe_combine",
    )(*args)


def _moe(x, g, mods, layer, w_group, b_group, w_expert, b_expert, w1, w3, w2, final_g=None):
    pad = LANES - N_EXPERTS - N_GROUPS
    wr = jnp.concatenate([w_expert, w_group, jnp.zeros((D_MODEL, pad), F32)], axis=1)
    br = jnp.concatenate([b_expert, b_group, jnp.zeros((pad,), F32)]).reshape(1, LANES)
    xb, pos, gate, counts = _router(x, g, mods, layer, wr, br)
    tile_end, disp_src, comb_src = _chunk_tables(counts)
    y = _experts(tile_end, disp_src, xb, w1, w3, w2, layer)
    return _combine(comb_src, y, x, pos, gate, mods, layer, final_g)


def _cache_kernel(*refs):
    n = (len(refs) - 2) // 2
    k_refs, v_refs, ok_ref, ov_ref = refs[:n], refs[n:2 * n], refs[2 * n], refs[2 * n + 1]
    j = pl.program_id(1)
    for src_refs, o_ref in ((k_refs, ok_ref), (v_refs, ov_ref)):
        x = src_refs[0][...]
        for a in range(1, n):
            x = jnp.where(j == a, src_refs[a][...], x)
        x = x.astype(F32)
        for h in range(N_HEADS):
            o_ref[0, 0, :, h, :] = x[:, h * HEAD_W:(h + 1) * HEAD_W]


def _new_caches(qkvs):
    n = len(qkvs)
    k_spec = pl.BlockSpec((SEQ, D_MODEL), lambda b, j: (b, 1))
    v_spec = pl.BlockSpec((SEQ, D_MODEL), lambda b, j: (b, 2))
    out_spec = pl.BlockSpec((1, 1, SEQ, N_HEADS, HEAD_W), lambda b, j: (b, j, 0, 0, 0))
    shape = jax.ShapeDtypeStruct((BATCH, n, SEQ, N_HEADS, HEAD_W), F32)
    return pl.pallas_call(
        _cache_kernel,
        grid=(BATCH, n),
        in_specs=[k_spec] * n + [v_spec] * n,
        out_specs=[out_spec, out_spec],
        out_shape=[shape, shape],
        compiler_params=_cparams(("arbitrary", "arbitrary")),
        name="new_caches",
    )(*qkvs, *qkvs)


def kernel(x_prompt, x_sample, cache_k, cache_v, c, c_ctx, ada_w, ada_b, norm1_g, norm2_g, final_g, hy_w_in, hy_conv_w, hy_conv_b, hy_f_w1, hy_f_b1, hy_f_w2, hy_f_b2, hy_f_w3, hy_f_b3, hy_f_freq, hy_bias, hy_w_out, at_w_qkv, at_lam_q1, at_lam_k1, at_lam_q2, at_lam_k2, at_subln_g, at_w_o, moe_w_group, moe_b_group, moe_w_expert, moe_b_expert, moe_w1, moe_w3, moe_w2):
    x = (x_prompt.reshape(N_PROMPT, D_MODEL), x_sample.reshape(N_SAMPLE, D_MODEL))
    cond = jnp.concatenate([c_ctx[None, :], c, jnp.zeros((MOD_ROWS - 1 - DEC_BATCH, D_MODEL), F32)], axis=0)
    mods = _ada_table(cond, ada_w, ada_b).reshape(DEPTH * MOD_ROWS * 6, 1, D_MODEL)
    ctx_k = cache_k.reshape(DEC_BATCH, DEPTH // 2, PAST_LEN, D_MODEL)
    ctx_v = cache_v.reshape(DEC_BATCH, DEPTH // 2, PAST_LEN, D_MODEL)

    dft = {}
    for L in (SEQ, DEC_SEQ):
        fwd, inv = _dft_mats(L)
        dft[L] = (jnp.asarray(fwd).astype(BF16), jnp.asarray(inv).astype(BF16))

    qkvs = []
    for l in range(DEPTH):
        j = l // 2
        if l % 2 == 0:
            z = _norm_matmul(x, norm1_g[l], mods, l, 0, 1, hy_w_in, j)
            ys = []
            for L, rb0, n_seq, dt in ((SEQ, 0, BATCH, 1024), (DEC_SEQ, N_PROMPT // DEC_SEQ, DEC_BATCH, 512)):
                f_bf, finv = dft[L]
                kc = _filter_spectra(L, hy_f_w1[j], hy_f_b1[j], hy_f_w2[j], hy_f_b2[j], hy_f_w3[j],
                                     hy_f_b3[j], hy_f_freq[j], f_bf)
                ys.append(_hyena_conv(z, hy_conv_w[j], hy_conv_b[j], f_bf, finv, kc, hy_bias[j],
                                      L=L, row_block0=rb0, n_seq=n_seq, dt=dt))
            x = _matmul_residual((ys[0], ys[1]), hy_w_out, j, x, mods, l, 2)
        else:
            lam_init = 0.8 - 0.6 * math.exp(-0.3 * l)
            qkv = _norm_matmul(x, norm1_g[l], mods, l, 0, 1, at_w_qkv, j)
            lam_params = (at_lam_q1[j], at_lam_k1[j], at_lam_q2[j], at_lam_k2[j])
            op = _attention(qkv, lam_params, at_subln_g[j], lam_init, L=SEQ, row_block0=0,
                            n_seq=BATCH, tq=SEQ)
            os_ = _attention(qkv, lam_params, at_subln_g[j], lam_init, L=DEC_SEQ,
                             row_block0=N_PROMPT // DEC_SEQ, n_seq=DEC_BATCH, tq=256,
                             ctx=(ctx_k, ctx_v), j=j)
            qkvs.append(qkv)
            x = _matmul_residual((op, os_), at_w_o, j, x, mods, l, 2)
        x = _moe(x, norm2_g[l], mods, l, moe_w_group[l], moe_b_group[l], moe_w_expert[l],
                 moe_b_expert[l], moe_w1, moe_w3, moe_w2, final_g if l == DEPTH - 1 else None)
    y_prompt = x[0].reshape(BATCH, SEQ, D_MODEL)
    y_sample = x[1].reshape(DEC_BATCH, DEC_SEQ, D_MODEL)
    new_cache_k, new_cache_v = _new_caches(qkvs)
    return (y_prompt, y_sample, new_cache_k, new_cache_v)
```

```python
import functools
import math

import numpy as np
import jax
import jax.numpy as jnp
from jax import lax
from jax.experimental import pallas as pl
from jax.experimental.pallas import tpu as pltpu

F32 = jnp.float32
BF16 = jnp.bfloat16

D_MODEL = 1024
BATCH = 16
SEQ = 256
DEPTH = 4
DEC_BATCH = 4
DEC_SEQ = 1024
PAST_LEN = 256
GRID_W = 64
EPS = 1e-6
HY_BANDS = 16
HY_EMB = 1 + 2 * HY_BANDS
HY_FILTER_HIDDEN = 64
HY_FAST_DECAY = 0.3
HY_SLOW_DECAY = 1.5
HY_TARGET = 1e-2
N_HEADS = 8
HEAD_DIM = 64
ROT_AXIS = HEAD_DIM // 2
ROPE_BASE = 10000.0
N_GROUPS = 4
EXPERTS_PER_GROUP = 8
N_EXPERTS = N_GROUPS * EXPERTS_PER_GROUP
D_EXPERT = D_MODEL // 4

N_PROMPT = BATCH * SEQ
N_SAMPLE = DEC_BATCH * DEC_SEQ
N_TOK = N_PROMPT + N_SAMPLE
MOD_ROWS = 8
LANES = 128
SUBLANES = 8
HEAD_W = 2 * HEAD_DIM
VMEM_LIMIT = 56 * 1024 * 1024

MOE_BLOCK = 256
N_BLOCKS = N_TOK // MOE_BLOCK
CHUNK = 2 * SUBLANES
BLOCK_ROWS = 2 * MOE_BLOCK + N_EXPERTS * CHUNK
BLOCK_CHUNKS = BLOCK_ROWS // CHUNK
EXPERT_TILE = 256
TILE_CHUNKS = EXPERT_TILE // CHUNK
N_TILES = (2 * N_TOK + N_BLOCKS * N_EXPERTS * (CHUNK - 1)) // EXPERT_TILE + N_EXPERTS


def _cparams(sem):
    return pltpu.CompilerParams(dimension_semantics=sem, vmem_limit_bytes=VMEM_LIMIT)


def _mod_row(blk, tm):
    start = blk * tm
    return jnp.where(start < N_PROMPT, 0, 1 + (start - N_PROMPT) // DEC_SEQ)


def _mod_index(layer, which, tm):
    def index_map(i, *_):
        return ((layer * MOD_ROWS + _mod_row(i, tm)) * 6 + which, 0, 0)
    return index_map


@functools.lru_cache(maxsize=None)
def _dft_mats(L):
    k = np.arange(L, dtype=np.float64)[:, None]
    j = np.arange(L, dtype=np.float64)[None, :]
    ang = np.pi * ((k * j) % (2 * L)) / L
    c = np.cos(ang)
    s = np.sin(ang)
    s[0, :] = np.where(np.arange(L) % 2 == 0, 1.0, -1.0)
    fwd = np.concatenate([c, s], axis=0)
    scale = np.full((2 * L,), 1.0 / L)
    scale[0] = scale[L] = 0.5 / L
    inv = fwd.T * scale[None, :]
    return fwd.astype(np.float32), inv.astype(np.float32)


@functools.lru_cache(maxsize=None)
def _filter_feats(L):
    pos = np.arange(L, dtype=np.float64)
    t = pos / (L - 1)
    bands = np.linspace(1e-4, HY_BANDS - 1, HY_BANDS)
    ang = (2.0 * math.pi / L) * pos[:, None] * bands[None, :]
    feats = np.concatenate([t[:, None], np.cos(ang), -np.sin(ang)], axis=-1)
    feats = np.pad(feats, ((0, 0), (0, LANES - HY_EMB)))
    deltas = np.abs(np.linspace(math.log(HY_TARGET) / HY_SLOW_DECAY,
                                math.log(HY_TARGET) / HY_FAST_DECAY, D_MODEL))
    window = np.exp(-t[:, None] * deltas[None, :])
    alt = np.where(np.arange(L) % 2 == 0, 1.0, -1.0)[:, None] * np.ones((1, LANES))
    return feats.astype(np.float32), window.astype(np.float32), alt.astype(np.float32)


@functools.lru_cache(maxsize=None)
def _rope_tables():
    pos = np.arange(DEC_SEQ)
    row = (pos // GRID_W).astype(np.float64)
    col = (pos % GRID_W).astype(np.float64)
    lane = np.arange(HEAD_W)
    d = lane % HEAD_DIM
    axis = d // ROT_AXIS
    n = d % ROT_AXIS
    half = n // (ROT_AXIS // 2)
    f = n % (ROT_AXIS // 2)
    inv = ROPE_BASE ** (-f.astype(np.float64) / (ROT_AXIS // 2))
    p = np.where(axis[None, :] == 0, row[:, None], col[:, None])
    ang = p * inv[None, :]
    sign = np.where(half == 0, -1.0, 1.0)[None, :]
    return np.cos(ang).astype(np.float32), (np.sin(ang) * sign).astype(np.float32)


def _split_bf16(a):
    hi = a.astype(BF16)
    lo = (a - hi.astype(F32)).astype(BF16)
    return hi, lo


def _dot(a, b):
    return jnp.dot(a, b, preferred_element_type=F32)


def _dot3(a_hi, a_lo, b_hi, b_lo):
    return _dot(a_hi, b_hi) + (_dot(a_hi, b_lo) + _dot(a_lo, b_hi))


def _ada_kernel(c_ref, w_ref, b_ref, o_ref):
    c = c_ref[...]
    s = c * jax.nn.sigmoid(c)
    o_ref[0] = _dot(s.astype(BF16), w_ref[0].astype(BF16)) + b_ref[0]


def _ada_table(cond, ada_w, ada_b):
    tn = 1536
    n = 6 * D_MODEL
    return pl.pallas_call(
        _ada_kernel,
        grid=(DEPTH, n // tn),
        in_specs=[
            pl.BlockSpec((MOD_ROWS, D_MODEL), lambda l, j: (0, 0)),
            pl.BlockSpec((1, D_MODEL, tn), lambda l, j: (l, 0, j)),
            pl.BlockSpec((1, 1, tn), lambda l, j: (l, 0, j)),
        ],
        out_specs=pl.BlockSpec((1, MOD_ROWS, tn), lambda l, j: (l, 0, j)),
        out_shape=jax.ShapeDtypeStruct((DEPTH, MOD_ROWS, n), F32),
        compiler_params=_cparams(("arbitrary", "arbitrary")),
        name="ada_table",
    )(cond, ada_w, ada_b.reshape(DEPTH, 1, n))


def _modulate(x, g, scale, shift):
    ms = jnp.mean(x * x, axis=-1, keepdims=True)
    y = (x * lax.rsqrt(ms + EPS)) * g
    return y * (1.0 + scale) + shift


MM_COLS = 512


def _row_group_specs(x, tm, width):
    np_blocks = N_PROMPT // tm
    if isinstance(x, tuple):
        specs = [pl.BlockSpec((tm, width), lambda i, *_: (jnp.minimum(i, np_blocks - 1), 0)),
                 pl.BlockSpec((tm, width), lambda i, *_: (jnp.maximum(i - np_blocks, 0), 0))]
        return specs, list(x)
    specs = [pl.BlockSpec((tm, width), lambda i, *_: (jnp.minimum(i, np_blocks - 1), 0)),
             pl.BlockSpec((tm, width), lambda i, *_: (jnp.maximum(i, np_blocks), 0))]
    return specs, [x, x]


def _pick_rows(i, tm, p_ref, s_ref):
    return jnp.where(i < N_PROMPT // tm, p_ref[...], s_ref[...])


def _normmm_kernel(xp_ref, xs_ref, g_ref, sh_ref, sc_ref, w_ref, o_ref, w_scr, *, tm):
    i = pl.program_id(0)

    @pl.when(i == 0)
    def _():
        w_scr[...] = w_ref[0].astype(BF16)

    x = _pick_rows(i, tm, xp_ref, xs_ref)
    h = _modulate(x, g_ref[...], sc_ref[0], sh_ref[0]).astype(BF16)
    for j in range(w_scr.shape[1] // MM_COLS):
        cols = slice(j * MM_COLS, (j + 1) * MM_COLS)
        o_ref[:, cols] = _dot(h, w_scr[:, cols]).astype(o_ref.dtype)


def _norm_matmul(x, g, mods, layer, which_shift, which_scale, w, widx, *, tm=512):
    n = w.shape[2]
    x_specs, x_args = _row_group_specs(x, tm, D_MODEL)
    return pl.pallas_call(
        functools.partial(_normmm_kernel, tm=tm),
        grid=(N_TOK // tm,),
        in_specs=x_specs + [
            pl.BlockSpec((1, D_MODEL), lambda i: (0, 0)),
            pl.BlockSpec((1, 1, D_MODEL), _mod_index(layer, which_shift, tm)),
            pl.BlockSpec((1, 1, D_MODEL), _mod_index(layer, which_scale, tm)),
            pl.BlockSpec((1, D_MODEL, n), lambda i: (widx, 0, 0), pipeline_mode=pl.Buffered(1)),
        ],
        out_specs=pl.BlockSpec((tm, n), lambda i: (i, 0)),
        out_shape=jax.ShapeDtypeStruct((N_TOK, n), BF16),
        scratch_shapes=[pltpu.VMEM((D_MODEL, n), BF16)],
        compiler_params=_cparams(("arbitrary",)),
        name="norm_matmul",
    )(*x_args, g.reshape(1, D_MODEL), mods, mods, w)


def _mmres_kernel(ap_ref, as_ref, xp_ref, xs_ref, w_ref, gate_ref, o_ref, w_scr, *, tm):
    i = pl.program_id(0)

    @pl.when(i == 0)
    def _():
        w_scr[...] = w_ref[0].astype(BF16)

    a = _pick_rows(i, tm, ap_ref, as_ref)
    x = _pick_rows(i, tm, xp_ref, xs_ref)
    gate = gate_ref[0]
    for j in range(w_scr.shape[1] // MM_COLS):
        cols = slice(j * MM_COLS, (j + 1) * MM_COLS)
        o_ref[:, cols] = x[:, cols] + gate[:, cols] * _dot(a, w_scr[:, cols])


def _matmul_residual(a, w, widx, x, mods, layer, which_gate, *, tm=512):
    n = w.shape[2]
    a_specs, a_args = _row_group_specs(a, tm, D_MODEL)
    x_specs, x_args = _row_group_specs(x, tm, n)
    return pl.pallas_call(
        functools.partial(_mmres_kernel, tm=tm),
        grid=(N_TOK // tm,),
        in_specs=a_specs + x_specs + [
            pl.BlockSpec((1, D_MODEL, n), lambda i: (widx, 0, 0), pipeline_mode=pl.Buffered(1)),
            pl.BlockSpec((1, 1, n), _mod_index(layer, which_gate, tm)),
        ],
        out_specs=pl.BlockSpec((tm, n), lambda i: (i, 0)),
        out_shape=jax.ShapeDtypeStruct((N_TOK, n), F32),
        scratch_shapes=[pltpu.VMEM((D_MODEL, n), BF16)],
        compiler_params=_cparams(("arbitrary",)),
        name="matmul_residual",
    )(*a_args, *x_args, w, mods)


def _filter_kernel(feats_ref, win_ref, alt_ref, w1_ref, b1_ref, w2_ref, b2_ref, fq_ref,
                   w3f_ref, b3f_ref, w3b_ref, b3b_ref, f_ref, o_ref, h_scr, *, L):
    def dense(a, w_ref, b_ref):
        a_hi, a_lo = _split_bf16(a)
        w_hi, w_lo = _split_bf16(w_ref[...])
        return _dot3(a_hi, a_lo, w_hi, w_lo) + b_ref[...]

    @pl.when(jnp.logical_and(pl.program_id(0) == 0, pl.program_id(1) == 0))
    def _():
        fq = fq_ref[...]
        h1 = jnp.sin(fq[0:1, :] * dense(feats_ref[...], w1_ref, b1_ref))
        h_scr[...] = jnp.sin(fq[1:2, :] * dense(h1, w2_ref, b2_ref))

    h = h_scr[...]
    win = win_ref[...]
    hf = dense(h, w3f_ref, b3f_ref) * win
    hb = dense(h, w3b_ref, b3b_ref) * win
    row = lax.broadcasted_iota(jnp.int32, hf.shape, 0)
    hb = jnp.where(row == 0, 0.0, hb)
    kr = _dot(f_ref[0:L, :], (hf + hb).astype(BF16))
    ks = _dot(f_ref[L:2 * L, :], (hf - hb).astype(BF16))
    k_nyq = jnp.sum((hf + hb) * alt_ref[:, 0:1], axis=0, keepdims=True)
    o_ref[0, 0] = kr
    o_ref[0, 1] = jnp.where(row == 0, k_nyq, kr)
    o_ref[0, 2] = jnp.where(row == 0, 0.0, ks)


def _filter_spectra(L, w1, b1, w2, b2, w3, b3, freq, f_bf, *, dt=512):
    feats, window, alt = _filter_feats(L)
    fh = HY_FILTER_HIDDEN
    nd = D_MODEL // dt
    w1p = jnp.pad(w1, ((0, LANES - HY_EMB), (0, 0)))
    const = lambda shape: pl.BlockSpec(shape, lambda o, c: tuple(0 for _ in shape))
    return pl.pallas_call(
        functools.partial(_filter_kernel, L=L),
        grid=(2, nd),
        in_specs=[
            const((L, LANES)),
            pl.BlockSpec((L, dt), lambda o, c: (0, c)),
            const((L, LANES)),
            const((LANES, fh)), const((1, fh)), const((fh, fh)), const((1, fh)), const((2, fh)),
            pl.BlockSpec((fh, dt), lambda o, c: (0, (2 * o) * nd + c)),
            pl.BlockSpec((1, dt), lambda o, c: (0, (2 * o) * nd + c)),
            pl.BlockSpec((fh, dt), lambda o, c: (0, (2 * o + 1) * nd + c)),
            pl.BlockSpec((1, dt), lambda o, c: (0, (2 * o + 1) * nd + c)),
            const((2 * L, L)),
        ],
        out_specs=pl.BlockSpec((1, 3, L, dt), lambda o, c: (o, 0, 0, c)),
        out_shape=jax.ShapeDtypeStruct((2, 3, L, D_MODEL), F32),
        scratch_shapes=[pltpu.VMEM((L, fh), F32)],
        compiler_params=_cparams(("arbitrary", "arbitrary")),
        name=f"hyena_filter_spectra_{L}",
    )(jnp.asarray(feats), jnp.asarray(window), jnp.asarray(alt), w1p, b1.reshape(1, fh), w2,
      b2.reshape(1, fh), freq, w3, b3.reshape(1, -1), w3, b3.reshape(1, -1), f_bf)


def _conv_kernel(v_ref, x1_ref, x2_ref, cwv_ref, cw1_ref, cw2_ref, cbv_ref, cb1_ref, cb2_ref,
                 f_ref, finv_ref, kc_ref, bias_ref, o_ref, *, L):
    row = lax.broadcasted_iota(jnp.int32, v_ref.shape, 0)

    def short_conv(u_ref, w_ref, b_ref):
        u = u_ref[...].astype(F32)
        prev = jnp.where(row == 0, 0.0, pltpu.roll(u, 1, 0))
        nxt = jnp.where(row == L - 1, 0.0, pltpu.roll(u, L - 1, 0))
        w = w_ref[...]
        return prev * w[0:1, :] + u * w[1:2, :] + nxt * w[2:3, :] + b_ref[...]

    def long_conv(u, order):
        spec = _dot(f_ref[...], u.astype(BF16))
        a = spec[0:L, :]
        b = spec[L:2 * L, :]
        kra = kc_ref[order, 0]
        krb = kc_ref[order, 1]
        ks = kc_ref[order, 2]
        prod = jnp.concatenate([a * kra - b * ks, a * ks + b * krb], axis=0)
        return _dot(finv_ref[...], prod.astype(BF16)) + u * bias_ref[order:order + 1, :]

    v = short_conv(v_ref, cwv_ref, cbv_ref)
    y = short_conv(x1_ref, cw1_ref, cb1_ref) * long_conv(v, 0)
    y = short_conv(x2_ref, cw2_ref, cb2_ref) * long_conv(y, 1)
    o_ref[...] = y.astype(o_ref.dtype)


def _hyena_conv(z, conv_w, conv_b, f_bf, finv_bf, kc, bias, *, L, row_block0, n_seq, dt):
    nd = D_MODEL // dt
    cb = conv_b.reshape(1, 3 * D_MODEL)
    seg = lambda s: pl.BlockSpec((L, dt), lambda c, b: (row_block0 + b, s * nd + c))
    cw = lambda s: pl.BlockSpec((3, dt), lambda c, b: (0, s * nd + c))
    cbs = lambda s: pl.BlockSpec((1, dt), lambda c, b: (0, s * nd + c))
    return pl.pallas_call(
        functools.partial(_conv_kernel, L=L),
        grid=(nd, n_seq),
        in_specs=[
            seg(0), seg(1), seg(2), cw(0), cw(1), cw(2), cbs(0), cbs(1), cbs(2),
            pl.BlockSpec((2 * L, L), lambda c, b: (0, 0), pipeline_mode=pl.Buffered(1)),
            pl.BlockSpec((L, 2 * L), lambda c, b: (0, 0), pipeline_mode=pl.Buffered(1)),
            pl.BlockSpec((2, 3, L, dt), lambda c, b: (0, 0, 0, c), pipeline_mode=pl.Buffered(1)),
            pl.BlockSpec((2, dt), lambda c, b: (0, c)),
        ],
        out_specs=pl.BlockSpec((L, dt), lambda c, b: (b, c)),
        out_shape=jax.ShapeDtypeStruct((n_seq * L, D_MODEL), BF16),
        compiler_params=_cparams(("arbitrary", "arbitrary")),
        name=f"hyena_conv_{L}",
    )(z, z, z, conv_w, conv_w, conv_w, cb, cb, cb, f_bf, finv_bf, kc, bias)


def _attn_kernel(lq1_ref, lk1_ref, lq2_ref, lk2_ref, g_ref, q_ref, k_ref, v_ref, *rest,
                 lam_init, has_ctx, tq, nq):
    if has_ctx:
        ck_ref, cv_ref, cos_ref, sin_ref, o_ref, k_scr = rest
    else:
        (o_ref,) = rest
    lam = (jnp.exp(jnp.sum(lq1_ref[...] * lk1_ref[...], axis=-1, keepdims=True))
           - jnp.exp(jnp.sum(lq2_ref[...] * lk2_ref[...], axis=-1, keepdims=True)) + lam_init)
    lane = lax.broadcasted_iota(jnp.int32, (tq, HEAD_W), 1)
    first_map = lane < HEAD_DIM
    first_half = (lane % ROT_AXIS) < (ROT_AXIS // 2)
    nt = (((1,), (1,)), ((), ()))

    def rotary(x, rows):
        x = x.astype(F32)
        partner = jnp.where(first_half, pltpu.roll(x, HEAD_W - ROT_AXIS // 2, 1),
                            pltpu.roll(x, ROT_AXIS // 2, 1))
        return (x * cos_ref[rows, :] + partner * sin_ref[rows, :]).astype(BF16)

    if has_ctx:
        for r in range(k_ref.shape[0] // tq):
            rows = slice(r * tq, (r + 1) * tq)
            for h in range(N_HEADS):
                cols = slice(h * HEAD_W, (h + 1) * HEAD_W)
                k_scr[rows, cols] = rotary(k_ref[rows, cols], rows)
        keys = k_scr
    else:
        keys = k_ref

    def q_block(rows):
        for h in range(N_HEADS):
            cols = slice(h * HEAD_W, (h + 1) * HEAD_W)
            q = q_ref[rows, cols]
            if has_ctx:
                q = rotary(q, rows)
            q = q * (HEAD_DIM ** -0.5)
            zero = jnp.zeros_like(q)
            q2 = jnp.concatenate([jnp.where(first_map, q, zero), jnp.where(first_map, zero, q)], axis=0)
            s = lax.dot_general(q2, keys[:, cols], nt, preferred_element_type=F32)
            m = jnp.max(s, axis=-1, keepdims=True)
            if has_ctx:
                sc = lax.dot_general(q2, ck_ref[0, 0, :, cols].astype(BF16), nt,
                                     preferred_element_type=F32)
                m = jnp.maximum(m, jnp.max(sc, axis=-1, keepdims=True))
                ec = jnp.exp(sc - m)
            e = jnp.exp(s - m)
            den = jnp.sum(e, axis=-1, keepdims=True)
            if has_ctx:
                den = den + jnp.sum(ec, axis=-1, keepdims=True)
            ov = _dot(e.astype(BF16), v_ref[:, cols])
            if has_ctx:
                ov = ov + _dot(ec.astype(BF16), cv_ref[0, 0, :, cols].astype(BF16))
            inv = 1.0 / den
            o = ov[0:tq] * inv[0:tq] - ov[tq:2 * tq] * (lam * inv[tq:2 * tq])
            ms = jnp.mean(o * o, axis=-1, keepdims=True)
            o = (o * lax.rsqrt(ms + EPS)) * g_ref[...] * (1.0 - lam_init)
            o_ref[rows, cols] = o.astype(o_ref.dtype)

    if nq == 1:
        q_block(slice(0, tq))
    else:
        def body(qi, carry):
            q_block(pl.ds(pl.multiple_of(qi * tq, tq), tq))
            return carry
        lax.fori_loop(0, nq, body, 0)


def _attention(qkv, lam_params, subln_g, lam_init, *, L, row_block0, n_seq, tq, ctx=None, j=0):
    small = pl.BlockSpec((1, HEAD_DIM), lambda b: (0, 0))
    in_specs = [small, small, small, small,
                pl.BlockSpec((1, HEAD_W), lambda b: (0, 0)),
                pl.BlockSpec((L, D_MODEL), lambda b: (row_block0 + b, 0)),
                pl.BlockSpec((L, D_MODEL), lambda b: (row_block0 + b, 1)),
                pl.BlockSpec((L, D_MODEL), lambda b: (row_block0 + b, 2))]
    args = [p.reshape(1, HEAD_DIM) for p in lam_params] + [subln_g.reshape(1, HEAD_W), qkv, qkv, qkv]
    scratch = []
    if ctx is not None:
        ctx_spec = pl.BlockSpec((1, 1, PAST_LEN, D_MODEL), lambda b: (b, j, 0, 0))
        rope_spec = pl.BlockSpec((L, HEAD_W), lambda b: (0, 0))
        cos, sin = _rope_tables()
        in_specs += [ctx_spec, ctx_spec, rope_spec, rope_spec]
        args += list(ctx) + [jnp.asarray(cos), jnp.asarray(sin)]
        scratch = [pltpu.VMEM((L, D_MODEL), BF16)]
    return pl.pallas_call(
        functools.partial(_attn_kernel, lam_init=lam_init, has_ctx=ctx is not None, tq=tq, nq=L // tq),
        grid=(n_seq,),
        in_specs=in_specs,
        out_specs=pl.BlockSpec((L, D_MODEL), lambda b: (b, 0)),
        out_shape=jax.ShapeDtypeStruct((n_seq * L, D_MODEL), BF16),
        scratch_shapes=scratch,
        compiler_params=_cparams(("arbitrary",)),
        name="diff_attention_ctx" if ctx is not None else "diff_attention",
    )(*args)


def _row_slots(pos, sel):
    big = float(BLOCK_ROWS)
    pos_a = jnp.max(pos, axis=-1, keepdims=True)
    pos_b = jnp.min(jnp.where(sel, pos, big), axis=-1, keepdims=True)
    return pos_a, pos_b


def _route_rows(h, wr, br):
    tm = h.shape[0]
    logits = _dot(h, wr) + br
    lane = lax.broadcasted_iota(jnp.int32, (tm, LANES), 1)
    neg = -jnp.inf
    is_grp = jnp.logical_and(lane >= N_EXPERTS, lane < N_EXPERTS + N_GROUPS)
    lg = jnp.where(is_grp, logits, neg)
    mg = jnp.max(lg, axis=-1, keepdims=True)
    g_gate = 1.0 / jnp.sum(jnp.exp(lg - mg), axis=-1, keepdims=True)
    g_idx = jnp.min(jnp.where(lg == mg, lane - N_EXPERTS, N_GROUPS), axis=-1, keepdims=True)
    in_grp = jnp.logical_and(lane < N_EXPERTS, lane // EXPERTS_PER_GROUP == g_idx)
    le = jnp.where(in_grp, logits, neg)
    m1 = jnp.max(le, axis=-1, keepdims=True)
    i1 = jnp.min(jnp.where(le == m1, lane, LANES), axis=-1, keepdims=True)
    le2 = jnp.where(lane == i1, neg, le)
    m2 = jnp.max(le2, axis=-1, keepdims=True)
    i2 = jnp.min(jnp.where(le2 == m2, lane, LANES), axis=-1, keepdims=True)
    e2 = jnp.exp(m2 - m1)
    inv = 1.0 / (1.0 + e2)
    gate = jnp.where(lane == i1, g_gate * inv, jnp.where(lane == i2, g_gate * (e2 * inv), 0.0))
    return gate, jnp.where(jnp.logical_or(lane == i1, lane == i2), 1.0, 0.0)


def _sort_block(h, onehot):
    tm = MOE_BLOCK
    sel = onehot > 0.5
    r = lax.broadcasted_iota(jnp.int32, (tm, tm), 0)
    c = lax.broadcasted_iota(jnp.int32, (tm, tm), 1)
    rank = _dot(jnp.where(c < r, 1.0, 0.0).astype(BF16), onehot.astype(BF16))
    count = jnp.sum(onehot, axis=0, keepdims=True)
    padded = jnp.ceil(count * (1.0 / CHUNK)) * CHUNK
    er = lax.broadcasted_iota(jnp.int32, (LANES, LANES), 0)
    ec = lax.broadcasted_iota(jnp.int32, (LANES, LANES), 1)
    before = jnp.where(er < ec, 1.0, 0.0).astype(BF16)
    start = _dot(jnp.broadcast_to(padded, (SUBLANES, LANES)).astype(BF16), before)[0:1, :]
    pos = jnp.where(sel, start + rank, -1.0)
    pos_a, pos_b = _row_slots(pos, sel)
    slot = lax.broadcasted_iota(jnp.int32, (tm, BLOCK_ROWS), 1).astype(F32)
    pick = jnp.where(slot == pos_a, 1.0, jnp.where(slot == pos_b, 1.0, 0.0)).astype(BF16)
    xb = lax.dot_general(pick, h, (((0,), (0,)), ((), ())), preferred_element_type=F32).astype(BF16)
    return xb, pos, count.astype(jnp.int32)


ROUTER_BLOCKS = 4


def _router_kernel(x_ref, g_ref, sh_ref, sc_ref, wr_ref, br_ref, xb_ref, pos_ref, gate_ref, cnt_ref):
    h = _modulate(x_ref[...], g_ref[...], sc_ref[0], sh_ref[0]).astype(BF16)
    gate, sel = _route_rows(h, wr_ref[...].astype(BF16), br_ref[...])
    gate_ref[...] = gate
    for k in range(ROUTER_BLOCKS):
        rows = slice(k * MOE_BLOCK, (k + 1) * MOE_BLOCK)
        xb, pos, count = _sort_block(h[rows, :], sel[rows, :])
        xb_ref[k * BLOCK_ROWS:(k + 1) * BLOCK_ROWS, :] = xb
        pos_ref[rows, :] = pos
        cnt_ref[k] = count


def _router(x, g, mods, layer, wr, br):
    tm = ROUTER_BLOCKS * MOE_BLOCK
    return pl.pallas_call(
        _router_kernel,
        grid=(N_TOK // tm,),
        in_specs=[
            pl.BlockSpec((tm, D_MODEL), lambda i: (i, 0)),
            pl.BlockSpec((1, D_MODEL), lambda i: (0, 0)),
            pl.BlockSpec((1, 1, D_MODEL), _mod_index(layer, 3, tm)),
            pl.BlockSpec((1, 1, D_MODEL), _mod_index(layer, 4, tm)),
            pl.BlockSpec((D_MODEL, LANES), lambda i: (0, 0)),
            pl.BlockSpec((1, LANES), lambda i: (0, 0)),
        ],
        out_specs=[
            pl.BlockSpec((ROUTER_BLOCKS * BLOCK_ROWS, D_MODEL), lambda i: (i, 0)),
            pl.BlockSpec((tm, LANES), lambda i: (i, 0)),
            pl.BlockSpec((tm, LANES), lambda i: (i, 0)),
            pl.BlockSpec((ROUTER_BLOCKS, 1, LANES), lambda i: (i, 0, 0)),
        ],
        out_shape=[
            jax.ShapeDtypeStruct((N_BLOCKS * BLOCK_ROWS, D_MODEL), BF16),
            jax.ShapeDtypeStruct((N_TOK, LANES), F32),
            jax.ShapeDtypeStruct((N_TOK, LANES), F32),
            jax.ShapeDtypeStruct((N_BLOCKS, 1, LANES), jnp.int32),
        ],
        compiler_params=_cparams(("arbitrary",)),
        name="moe_router",
    )(x, g.reshape(1, D_MODEL), mods, mods, wr, br)


TABLE_ROWS = -(-(N_TILES * TILE_CHUNKS) // LANES)


def _tables_kernel(cnt_ref, tend_ref, disp_ref, comb_ref):
    ne = N_EXPERTS
    cnt = jnp.concatenate([cnt_ref[...].astype(F32), jnp.zeros((LANES - N_BLOCKS, LANES), F32)], axis=0)
    m = jnp.ceil(cnt * (1.0 / CHUNK))
    r = lax.broadcasted_iota(jnp.int32, (LANES, LANES), 0)
    c = lax.broadcasted_iota(jnp.int32, (LANES, LANES), 1)
    mb = m.astype(BF16)
    bstart = _dot(mb, jnp.where(r < c, 1.0, 0.0).astype(BF16))
    cm = _dot(jnp.where(c < r, 1.0, 0.0).astype(BF16), mb)
    total = jnp.sum(m, axis=0, keepdims=True)
    tiles = jnp.ceil(total * (1.0 / TILE_CHUNKS))
    tile_end = _dot(jnp.broadcast_to(tiles, (SUBLANES, LANES)).astype(BF16),
                    jnp.where(r <= c, 1.0, 0.0).astype(BF16))[0:1, :]
    choff = (tile_end - tiles) * TILE_CHUNKS
    tend_ref[...] = tile_end.astype(jnp.int32)

    start_t = (choff + cm).T
    m_t = m.T
    bstart_t = bstart.T
    lane = lax.broadcasted_iota(jnp.int32, (ne, LANES), 1).astype(F32)
    acc = [jnp.zeros((ne, LANES), F32) for _ in range(TABLE_ROWS)]
    for b in range(N_BLOCKS):
        lo = start_t[0:ne, b:b + 1]
        hi = lo + m_t[0:ne, b:b + 1]
        bs = bstart_t[0:ne, b:b + 1]
        fwd = (b * BLOCK_CHUNKS) + bs - lo
        for j in range(TABLE_ROWS):
            g = lane + float(j * LANES)
            acc[j] = acc[j] + jnp.where(g >= lo, jnp.where(g < hi, fwd + g, 0.0), 0.0)
        back = jnp.where(lane >= bs, jnp.where(lane < hi - lo + bs, (lo - bs + lane) * CHUNK + 1.0, 0.0), 0.0)
        comb_ref[b:b + 1, :] = (jnp.sum(back, axis=0, keepdims=True) - 1.0).astype(jnp.int32)
    for j in range(TABLE_ROWS):
        disp_ref[j:j + 1, :] = (jnp.sum(acc[j], axis=0, keepdims=True) * CHUNK).astype(jnp.int32)


def _chunk_tables(counts):
    return pl.pallas_call(
        _tables_kernel,
        out_shape=[
            jax.ShapeDtypeStruct((1, LANES), jnp.int32),
            jax.ShapeDtypeStruct((TABLE_ROWS, LANES), jnp.int32),
            jax.ShapeDtypeStruct((N_BLOCKS, LANES), jnp.int32),
        ],
        name="moe_chunk_tables",
    )(counts.reshape(N_BLOCKS, LANES))


def _chunk_copies(read_row, n_chunks, src_hbm, dst, sem, wait, skip_negative=False, priority=0):
    for c in range(n_chunks):
        row = read_row(c)

        def copy(row=row, c=c):
            cp = pltpu.make_async_copy(src_hbm.at[pl.ds(pl.multiple_of(row, CHUNK), CHUNK)],
                                       dst.at[pl.ds(c * CHUNK, CHUNK)], sem)
            if wait:
                cp.wait()
            else:
                cp.start(priority=priority)

        if skip_negative:
            pl.when(row >= 0)(copy)
        else:
            copy()


IN_DEPTH = 6


def _expert_kernel(tend_ref, src_ref, xb_hbm, w1_ref, w3_ref, w2_ref, y_hbm,
                   xbuf, ybuf, zbuf, w13_scr, w2_scr, in_sem, out_sem, zero_sem):
    e = pl.program_id(0)
    first = jnp.where(e == 0, 0, tend_ref[0, jnp.maximum(e - 1, 0)])
    last = tend_ref[0, e]
    n_used = tend_ref[0, N_EXPERTS - 1]
    tiles_per_row = LANES // TILE_CHUNKS

    def fetch(tile, s, wait):
        row = tile // tiles_per_row
        col = (tile % tiles_per_row) * TILE_CHUNKS
        _chunk_copies(lambda c: src_ref[row, col + c], TILE_CHUNKS, xb_hbm, xbuf.at[s], in_sem.at[s], wait,
                      priority=1)

    def store(tile, s):
        rows = pl.ds(pl.multiple_of(tile * EXPERT_TILE, EXPERT_TILE), EXPERT_TILE)
        return pltpu.make_async_copy(ybuf.at[s], y_hbm.at[rows], out_sem.at[s])

    def zero_fill(tile):
        rows = pl.ds(pl.multiple_of(tile * EXPERT_TILE, EXPERT_TILE), EXPERT_TILE)
        return pltpu.make_async_copy(zbuf, y_hbm.at[rows], zero_sem.at[0])

    @pl.when(e == 0)
    def _():
        zbuf[...] = jnp.zeros_like(zbuf)

        def start(t, carry):
            zero_fill(t).start()
            return carry

        lax.fori_loop(n_used, N_TILES, start, 0)

    lead = IN_DEPTH - 1

    @pl.when(last > first)
    def _():
        @pl.when(first == 0)
        def _():
            for k in range(lead):
                fetch(jnp.minimum(k, n_used - 1), k, False)

        w13_scr[:, 0:D_EXPERT] = w1_ref[0, 0].astype(BF16)
        w13_scr[:, D_EXPERT:2 * D_EXPERT] = w3_ref[0, 0].astype(BF16)
        w2_scr[...] = w2_ref[0, 0].astype(BF16)

    def tile_body(t, carry):
        s = t % 2
        fetch(t, t % IN_DEPTH, True)
        ab = _dot(xbuf[t % IN_DEPTH], w13_scr[...])
        fetch(jnp.minimum(t + lead, n_used - 1), (t + lead) % IN_DEPTH, False)
        a = ab[:, 0:D_EXPERT]
        hid = (a * jax.nn.sigmoid(a)) * ab[:, D_EXPERT:2 * D_EXPERT]
        y = _dot(hid.astype(BF16), w2_scr[...]).astype(BF16)

        @pl.when(t >= 2)
        def _():
            store(t, s).wait()

        ybuf[s] = y
        store(t, s).start()
        return carry

    lax.fori_loop(first, last, tile_body, 0)

    @pl.when(e == N_EXPERTS - 1)
    def _():
        for k in range(lead):
            fetch(n_used - 1, (n_used + k) % IN_DEPTH, True)

        @pl.when(n_used >= 1)
        def _():
            store(0, (n_used - 1) % 2).wait()

        @pl.when(n_used >= 2)
        def _():
            store(0, n_used % 2).wait()

        def drain(t, carry):
            zero_fill(t).wait()
            return carry

        lax.fori_loop(n_used, N_TILES, drain, 0)


def _experts(tile_end, disp_src, xb, w1, w3, w2, layer):
    wsel = lambda e, tend, src: (layer, e, 0, 0)
    return pl.pallas_call(
        _expert_kernel,
        grid_spec=pltpu.PrefetchScalarGridSpec(
            num_scalar_prefetch=2,
            grid=(N_EXPERTS,),
            in_specs=[
                pl.BlockSpec(memory_space=pl.ANY),
                pl.BlockSpec((1, 1, D_MODEL, D_EXPERT), wsel),
                pl.BlockSpec((1, 1, D_MODEL, D_EXPERT), wsel),
                pl.BlockSpec((1, 1, D_EXPERT, D_MODEL), wsel),
            ],
            out_specs=pl.BlockSpec(memory_space=pl.ANY),
            scratch_shapes=[pltpu.VMEM((IN_DEPTH, EXPERT_TILE, D_MODEL), BF16),
                            pltpu.VMEM((2, EXPERT_TILE, D_MODEL), BF16),
                            pltpu.VMEM((EXPERT_TILE, D_MODEL), BF16),
                            pltpu.VMEM((D_MODEL, 2 * D_EXPERT), BF16),
                            pltpu.VMEM((D_EXPERT, D_MODEL), BF16),
                            pltpu.SemaphoreType.DMA((IN_DEPTH,)),
                            pltpu.SemaphoreType.DMA((2,)),
                            pltpu.SemaphoreType.DMA((1,))],
        ),
        out_shape=jax.ShapeDtypeStruct((N_TILES * EXPERT_TILE, D_MODEL), BF16),
        compiler_params=_cparams(("arbitrary",)),
        name="moe_experts",
    )(tile_end, disp_src, xb, w1, w3, w2)


OUT_DEPTH = 3


def _combine_kernel(src_ref, y_hbm, x_ref, pos_ref, gate_ref, mg_ref, *rest, tm, final):
    if final:
        fg_ref, op_ref, os_ref, ybuf, sem = rest
    else:
        o_ref, ybuf, sem = rest
    i = pl.program_id(0)
    slot = i % OUT_DEPTH

    def fetch(blk, s, wait):
        _chunk_copies(lambda c: src_ref[blk, c], BLOCK_CHUNKS, y_hbm, ybuf.at[s], sem.at[s], wait,
                      skip_negative=True)

    @pl.when(i == 0)
    def _():
        ybuf[...] = jnp.zeros_like(ybuf)
        for k in range(OUT_DEPTH - 1):
            fetch(k, k, False)

    @pl.when(i + (OUT_DEPTH - 1) < N_BLOCKS)
    def _():
        fetch(i + (OUT_DEPTH - 1), (i + (OUT_DEPTH - 1)) % OUT_DEPTH, False)

    fetch(i, slot, True)
    pos = pos_ref[...]
    sel = pos >= 0.0
    gate = gate_ref[...]
    pos_a, pos_b = _row_slots(pos, sel)
    gate_a = jnp.sum(jnp.where(pos == pos_a, gate, 0.0), axis=-1, keepdims=True)
    gate_b = jnp.sum(jnp.where(jnp.logical_and(sel, pos == pos_b), gate, 0.0), axis=-1, keepdims=True)
    row = lax.broadcasted_iota(jnp.int32, (tm, BLOCK_ROWS), 1).astype(F32)
    weights = jnp.where(row == pos_a, gate_a, jnp.where(row == pos_b, gate_b, 0.0))
    moe = _dot(weights.astype(BF16), ybuf[slot])
    out = x_ref[...] + mg_ref[0] * moe
    if not final:
        o_ref[...] = out
        return

    ms = jnp.mean(out * out, axis=-1, keepdims=True)
    out = (out * lax.rsqrt(ms + EPS)) * fg_ref[...]

    @pl.when(i < N_PROMPT // tm)
    def _():
        op_ref[...] = out

    @pl.when(i >= N_PROMPT // tm)
    def _():
        os_ref[...] = out


def _combine(comb_src, y, x, pos, gate, mods, layer, final_g=None):
    tm = MOE_BLOCK
    final = final_g is not None
    np_blocks = N_PROMPT // tm

    def gate_idx(i, s):
        return ((layer * MOD_ROWS + _mod_row(i, tm)) * 6 + 5, 0, 0)

    in_specs = [
        pl.BlockSpec(memory_space=pl.ANY),
        pl.BlockSpec((tm, D_MODEL), lambda i, s: (i, 0)),
        pl.BlockSpec((tm, LANES), lambda i, s: (i, 0)),
        pl.BlockSpec((tm, LANES), lambda i, s: (i, 0)),
        pl.BlockSpec((1, 1, D_MODEL), gate_idx),
    ]
    args = [comb_src, y, x, pos, gate, mods]
    if final:
        in_specs.append(pl.BlockSpec((1, D_MODEL), lambda i, s: (0, 0)))
        args.append(final_g.reshape(1, D_MODEL))
        out_specs = [pl.BlockSpec((tm, D_MODEL), lambda i, s: (jnp.minimum(i, np_blocks - 1), 0)),
                     pl.BlockSpec((tm, D_MODEL), lambda i, s: (jnp.maximum(i - np_blocks, 0), 0))]
        out_shape = [jax.ShapeDtypeStruct((N_PROMPT, D_MODEL), F32),
                     jax.ShapeDtypeStruct((N_SAMPLE, D_MODEL), F32)]
    else:
        out_specs = pl.BlockSpec((tm, D_MODEL), lambda i, s: (i, 0))
        out_shape = jax.ShapeDtypeStruct((N_TOK, D_MODEL), F32)
    return pl.pallas_call(
        functools.partial(_combine_kernel, tm=tm, final=final),
        grid_spec=pltpu.PrefetchScalarGridSpec(
            num_scalar_prefetch=1,
            grid=(N_BLOCKS,),
            in_specs=in_specs,
            out_specs=out_specs,
            scratch_shapes=[pltpu.VMEM((OUT_DEPTH, BLOCK_ROWS, D_MODEL), BF16),
                            pltpu.SemaphoreType.DMA((OUT_DEPTH,))],
        ),
        out_shape=out_shape,
        compiler_params=_cparams(("arbitrary",)),
        name="moe_combine_final" if final else "moe_combine",
    )(*args)


def _moe(x, g, mods, layer, w_group, b_group, w_expert, b_expert, w1, w3, w2, final_g=None):
    pad = LANES - N_EXPERTS - N_GROUPS
    wr = jnp.concatenate([w_expert, w_group, jnp.zeros((D_MODEL, pad), F32)], axis=1)
    br = jnp.concatenate([b_expert, b_group, jnp.zeros((pad,), F32)]).reshape(1, LANES)
    xb, pos, gate, counts = _router(x, g, mods, layer, wr, br)
    tile_end, disp_src, comb_src = _chunk_tables(counts)
    y = _experts(tile_end, disp_src, xb, w1, w3, w2, layer)
    return _combine(comb_src, y, x, pos, gate, mods, layer, final_g)


def _cache_kernel(*refs):
    n = (len(refs) - 2) // 2
    k_refs, v_refs, ok_ref, ov_ref = refs[:n], refs[n:2 * n], refs[2 * n], refs[2 * n + 1]
    j = pl.program_id(1)
    for src_refs, o_ref in ((k_refs, ok_ref), (v_refs, ov_ref)):
        x = src_refs[0][...]
        for a in range(1, n):
            x = jnp.where(j == a, src_refs[a][...], x)
        x = x.astype(F32)
        for h in range(N_HEADS):
            o_ref[0, 0, :, h, :] = x[:, h * HEAD_W:(h + 1) * HEAD_W]


def _new_caches(qkvs):
    n = len(qkvs)
    k_spec = pl.BlockSpec((SEQ, D_MODEL), lambda b, j: (b, 1))
    v_spec = pl.BlockSpec((SEQ, D_MODEL), lambda b, j: (b, 2))
    out_spec = pl.BlockSpec((1, 1, SEQ, N_HEADS, HEAD_W), lambda b, j: (b, j, 0, 0, 0))
    shape = jax.ShapeDtypeStruct((BATCH, n, SEQ, N_HEADS, HEAD_W), F32)
    return pl.pallas_call(
        _cache_kernel,
        grid=(BATCH, n),
        in_specs=[k_spec] * n + [v_spec] * n,
        out_specs=[out_spec, out_spec],
        out_shape=[shape, shape],
        compiler_params=_cparams(("arbitrary", "arbitrary")),
        name="new_caches",
    )(*qkvs, *qkvs)


def kernel(x_prompt, x_sample, cache_k, cache_v, c, c_ctx, ada_w, ada_b, norm1_g, norm2_g, final_g, hy_w_in, hy_conv_w, hy_conv_b, hy_f_w1, hy_f_b1, hy_f_w2, hy_f_b2, hy_f_w3, hy_f_b3, hy_f_freq, hy_bias, hy_w_out, at_w_qkv, at_lam_q1, at_lam_k1, at_lam_q2, at_lam_k2, at_subln_g, at_w_o, moe_w_group, moe_b_group, moe_w_expert, moe_b_expert, moe_w1, moe_w3, moe_w2):
    x = (x_prompt.reshape(N_PROMPT, D_MODEL), x_sample.reshape(N_SAMPLE, D_MODEL))
    cond = jnp.concatenate([c_ctx[None, :], c, jnp.zeros((MOD_ROWS - 1 - DEC_BATCH, D_MODEL), F32)], axis=0)
    mods = _ada_table(cond, ada_w, ada_b).reshape(DEPTH * MOD_ROWS * 6, 1, D_MODEL)
    ctx_k = cache_k.reshape(DEC_BATCH, DEPTH // 2, PAST_LEN, D_MODEL)
    ctx_v = cache_v.reshape(DEC_BATCH, DEPTH // 2, PAST_LEN, D_MODEL)

    dft = {}
    for L in (SEQ, DEC_SEQ):
        fwd, inv = _dft_mats(L)
        dft[L] = (jnp.asarray(fwd).astype(BF16), jnp.asarray(inv).astype(BF16))

    qkvs = []
    for l in range(DEPTH):
        j = l // 2
        if l % 2 == 0:
            z = _norm_matmul(x, norm1_g[l], mods, l, 0, 1, hy_w_in, j)
            ys = []
            for L, rb0, n_seq, dt in ((SEQ, 0, BATCH, 1024), (DEC_SEQ, N_PROMPT // DEC_SEQ, DEC_BATCH, 512)):
                f_bf, finv = dft[L]
                kc = _filter_spectra(L, hy_f_w1[j], hy_f_b1[j], hy_f_w2[j], hy_f_b2[j], hy_f_w3[j],
                                     hy_f_b3[j], hy_f_freq[j], f_bf)
                ys.append(_hyena_conv(z, hy_conv_w[j], hy_conv_b[j], f_bf, finv, kc, hy_bias[j],
                                      L=L, row_block0=rb0, n_seq=n_seq, dt=dt))
            x = _matmul_residual((ys[0], ys[1]), hy_w_out, j, x, mods, l, 2)
        else:
            lam_init = 0.8 - 0.6 * math.exp(-0.3 * l)
            qkv = _norm_matmul(x, norm1_g[l], mods, l, 0, 1, at_w_qkv, j)
            lam_params = (at_lam_q1[j], at_lam_k1[j], at_lam_q2[j], at_lam_k2[j])
            op = _attention(qkv, lam_params, at_subln_g[j], lam_init, L=SEQ, row_block0=0,
                            n_seq=BATCH, tq=SEQ)
            os_ = _attention(qkv, lam_params, at_subln_g[j], lam_init, L=DEC_SEQ,
                             row_block0=N_PROMPT // DEC_SEQ, n_seq=DEC_BATCH, tq=256,
                             ctx=(ctx_k, ctx_v), j=j)
            qkvs.append(qkv)
            x = _matmul_residual((op, os_), at_w_o, j, x, mods, l, 2)
        x = _moe(x, norm2_g[l], mods, l, moe_w_group[l], moe_b_group[l], moe_w_expert[l],
                 moe_b_expert[l], moe_w1, moe_w3, moe_w2, final_g if l == DEPTH - 1 else None)
    y_prompt = x[0].reshape(BATCH, SEQ, D_MODEL)
    y_sample = x[1].reshape(DEC_BATCH, DEC_SEQ, D_MODEL)
    new_cache_k, new_cache_v = _new_caches(qkvs)
    return (y_prompt, y_sample, new_cache_k, new_cache_v)
```
